```python
import jax, jax.numpy as jnp
from jax import lax
import numpy as np

D_MODEL = 2048
BATCH = 4
SEQ = 2048
DEPTH = 1
DEC_BATCH = 128
DEC_SEQ = 4
PAST_LEN = 16384
PAGE_SIZE = 128

HEAD_DIM = 128
MLSTM_HEADS = 8
GDN_HEADS = 8
MLSTM_WIDTH = MLSTM_HEADS * HEAD_DIM
GDN_WIDTH = GDN_HEADS * HEAD_DIM
MIX_WIDTH = MLSTM_WIDTH + GDN_WIDTH
CONV_W = 4
N_META = 16
CHUNK = 64
D_FF = -(-8 * D_MODEL // (3 * 256)) * 256
EPS = 1e-6
IN_SIZES = (MLSTM_WIDTH,) * 4 + (MLSTM_HEADS,) * 2 + (GDN_WIDTH,) * 4 + (GDN_HEADS,) * 2
IN_COLS = sum(IN_SIZES)

kernel_name = "hymba_mlstm_gdn_decode_step"


def _split_cols(x, sizes):
    out, start = [], 0
    for s in sizes:
        out.append(x[..., start:start + s])
        start += s
    return out


def rmsnorm(x, w):
    xf = x.astype(jnp.float32)
    y = xf * lax.rsqrt(jnp.mean(xf * xf, axis=-1, keepdims=True) + EPS)
    return (y * w.astype(jnp.float32)).astype(x.dtype)


def l2norm(x):
    return x * lax.rsqrt(jnp.sum(x * x, axis=-1, keepdims=True) + EPS)


def causal_conv(x_pre, buf, w):
    T = x_pre.shape[1]
    xp = jnp.concatenate([buf, x_pre], axis=1)
    out = xp[:, 0:T] * w[0]
    for i in range(1, CONV_W):
        out = out + xp[:, i:i + T] * w[i]
    return jax.nn.silu(out), xp[:, -(CONV_W - 1):]


def mlstm_chunk(state, inp):
    C, n, m = state
    q, k, v, li, lf = inp
    L = q.shape[1]
    b = jnp.cumsum(lf, axis=1)
    a = b + m[:, None, :]
    mask = jnp.tril(jnp.ones((L, L), dtype=bool))[None, :, :, None]
    Dm = jnp.where(mask, b[:, :, None, :] - b[:, None, :, :] + li[:, None, :, :], -jnp.inf)
    m_t = jnp.maximum(a, jnp.max(Dm, axis=2))
    inter = jnp.exp(a - m_t)
    W = jnp.exp(Dm - m_t[:, :, None, :])
    s = jnp.einsum('bthd,bjhd->btjh', q, k) * W
    num = jnp.einsum('btjh,bjhe->bthe', s, v) + inter[..., None] * jnp.einsum('bthd,bhde->bthe', q, C)
    den = jnp.sum(s, axis=2) + inter * jnp.einsum('bthd,bhd->bth', q, n)
    h = num / jnp.maximum(jnp.abs(den), jnp.exp(-m_t))[..., None]
    mL = m_t[:, -1]
    wL = jnp.exp(b[:, -1:] - b + li - mL[:, None])
    dC = jnp.exp(b[:, -1] + m - mL)
    C_new = dC[..., None, None] * C + jnp.einsum('bjh,bjhd,bjhe->bhde', wL, k, v)
    n_new = dC[..., None] * n + jnp.einsum('bjh,bjhd->bhd', wL, k)
    return (C_new, n_new, mL), h


def gdn_chunk(S, inp):
    q, k, v, lg, beta = inp
    L = q.shape[1]
    G = jnp.cumsum(lg, axis=1)
    causal = jnp.tril(jnp.ones((L, L), dtype=bool))[None, :, :, None]
    strict = jnp.tril(jnp.ones((L, L), dtype=bool), k=-1)[None, :, :, None]
    diff = G[:, :, None, :] - G[:, None, :, :]
    decay = jnp.where(causal, jnp.exp(jnp.where(causal, diff, 0.0)), 0.0)
    kk = jnp.einsum('bthd,bjhd->btjh', k, k)
    A = jnp.where(strict, beta[:, :, None, :] * kk * decay, 0.0)
    eG = jnp.exp(G)
    rhs = beta[..., None] * (v - eG[..., None] * jnp.einsum('bthd,bhde->bthe', k, S))
    Amat = jnp.transpose(A, (0, 3, 1, 2)) + jnp.eye(L, dtype=A.dtype)
    Wv = lax.linalg.triangular_solve(Amat, jnp.transpose(rhs, (0, 2, 1, 3)),
                                     left_side=True, lower=True, unit_diagonal=True)
    Wv = jnp.transpose(Wv, (0, 2, 1, 3))
    qk = jnp.einsum('bthd,bjhd->btjh', q, k) * decay
    o = eG[..., None] * jnp.einsum('bthd,bhde->bthe', q, S) + jnp.einsum('btjh,bjhe->bthe', qk, Wv)
    GL = G[:, -1]
    S_new = jnp.exp(GL)[..., None, None] * S + jnp.einsum('bjh,bjhd,bjhe->bhde', jnp.exp(GL[:, None] - G), k, Wv)
    return S_new, o


def chunked_scan(fn, state, xs, lead, chunk):
    outs = []
    if lead > 0:
        state, o = fn(state, tuple(t[:, :lead] for t in xs))
        outs.append(o)
        xs = tuple(t[:, lead:] for t in xs)
    B, T = xs[0].shape[:2]
    nc = T // chunk
    blocks = tuple(jnp.moveaxis(t.reshape((B, nc, chunk) + t.shape[2:]), 1, 0) for t in xs)
    state, o = lax.scan(fn, state, blocks)
    o = jnp.moveaxis(o, 0, 1).reshape((B, T) + o.shape[3:])
    outs.append(o)
    return state, jnp.concatenate(outs, axis=1)


def mixer(u, st, w_in, b_i, b_f, m_norm, conv_w, A_log, dt_bias, g_norm, w_out, lead, chunk):
    B, T, _ = u.shape
    f32 = jnp.float32
    C0, n0, m0, S0, buf0 = st
    proj = jnp.einsum('btd,dc->btc', u, w_in).astype(f32)
    mq, mk, mv, mo, mi, mf, gq, gk, gv, gz, ga, gb = _split_cols(proj, IN_SIZES)

    def heads(t):
        return t.reshape(B, T, t.shape[-1] // HEAD_DIM, HEAD_DIM)

    qm = heads(mq) * (HEAD_DIM ** -0.5)
    li = mi + b_i.astype(f32)
    lf = jax.nn.log_sigmoid(mf + b_f.astype(f32))
    (C1, n1, m1), hm = chunked_scan(mlstm_chunk, (C0, n0, m0), (qm, heads(mk), heads(mv), li, lf), lead, chunk)
    hm = rmsnorm(hm, m_norm) * jax.nn.sigmoid(heads(mo))

    qkv, buf1 = causal_conv(jnp.concatenate([gq, gk, gv], axis=-1), buf0, conv_w.astype(f32))
    cq, ck, cv = _split_cols(qkv, (GDN_WIDTH,) * 3)
    qg = l2norm(heads(cq)) * (HEAD_DIM ** -0.5)
    kg = l2norm(heads(ck))
    lg = -jnp.exp(A_log.astype(f32)) * jax.nn.softplus(ga + dt_bias.astype(f32))
    beta = jax.nn.sigmoid(gb)
    S1, og = chunked_scan(gdn_chunk, S0, (qg, kg, heads(cv), lg, beta), lead, chunk)
    og = rmsnorm(og, g_norm) * jax.nn.silu(heads(gz))

    mix = jnp.concatenate([hm.reshape(B, T, MLSTM_WIDTH), og.reshape(B, T, GDN_WIDTH)], axis=-1).astype(u.dtype)
    out = jnp.einsum('btc,cd->btd', mix, w_out)
    return out, (C1, n1, m1, S1, buf1)


def decoder_layer(x, st, norm_pre_mix, norm_post_mix, norm_pre_ffn, norm_post_ffn, w_in, b_i, b_f,
                  m_norm, conv_w, A_log, dt_bias, g_norm, w_out, w_gate_up, w_down, lead, chunk):
    u = rmsnorm(x, norm_pre_mix)
    mo, new_st = mixer(u, st, w_in, b_i, b_f, m_norm, conv_w, A_log, dt_bias, g_norm, w_out, lead, chunk)
    x = x + rmsnorm(mo, norm_post_mix)
    u = rmsnorm(x, norm_pre_ffn)
    g, up = _split_cols(jnp.einsum('btd,df->btf', u, w_gate_up), (D_FF, D_FF))
    f = jnp.einsum('btf,fd->btd', jax.nn.silu(g) * up, w_down)
    x = x + rmsnorm(f, norm_post_ffn)
    return x, new_st


def setup_inputs(seed: int = 0) -> dict:
    key = jax.random.key(seed)
    ks = jax.random.split(key, 32)
    f32 = jnp.float32
    nrm = lambda k, shape, s: jax.random.normal(k, shape, f32) * s
    gain = lambda k, shape: 1.0 + 0.05 * jax.random.normal(k, shape, f32)
    dt = jnp.exp(jax.random.uniform(ks[20], (DEPTH, GDN_HEADS), f32, np.log(1e-3), np.log(1e-1)))
    return {
        "x_prompt": nrm(ks[0], (BATCH, SEQ, D_MODEL), 1.0),
        "x_sample": nrm(ks[1], (DEC_BATCH, DEC_SEQ, D_MODEL), 1.0),
        "state_mlstm_C": nrm(ks[2], (DEPTH, DEC_BATCH, MLSTM_HEADS, HEAD_DIM, HEAD_DIM), 0.1),
        "state_mlstm_n": nrm(ks[3], (DEPTH, DEC_BATCH, MLSTM_HEADS, HEAD_DIM), 0.5),
        "state_mlstm_m": 2.0 + nrm(ks[4], (DEPTH, DEC_BATCH, MLSTM_HEADS), 1.0),
        "state_gdn_S": nrm(ks[5], (DEPTH, DEC_BATCH, GDN_HEADS, HEAD_DIM, HEAD_DIM), 0.05),
        "state_gdn_conv": nrm(ks[6], (DEPTH, DEC_BATCH, CONV_W - 1, 3 * GDN_WIDTH), 0.5),
        "meta_tokens": nrm(ks[7], (N_META, D_MODEL), 1.0),
        "norm_pre_mix": gain(ks[8], (DEPTH, D_MODEL)),
        "norm_post_mix": gain(ks[9], (DEPTH, D_MODEL)),
        "norm_pre_ffn": gain(ks[10], (DEPTH, D_MODEL)),
        "norm_post_ffn": gain(ks[11], (DEPTH, D_MODEL)),
        "w_in": nrm(ks[12], (DEPTH, D_MODEL, IN_COLS), D_MODEL ** -0.5),
        "mlstm_b_i": nrm(ks[13], (DEPTH, MLSTM_HEADS), 0.1),
        "mlstm_b_f": 3.0 + nrm(ks[14], (DEPTH, MLSTM_HEADS), 0.5),
        "mlstm_norm": gain(ks[15], (DEPTH, MLSTM_HEADS, HEAD_DIM)),
        "gdn_conv_w": nrm(ks[16], (DEPTH, CONV_W, 3 * GDN_WIDTH), CONV_W ** -0.5),
        "gdn_A_log": jnp.log(jax.random.uniform(ks[17], (DEPTH, GDN_HEADS), f32, 1.0, 16.0)),
        "gdn_dt_bias": jnp.log(jnp.expm1(dt)),
        "gdn_norm": gain(ks[18], (DEPTH, GDN_HEADS, HEAD_DIM)),
        "w_out": nrm(ks[19], (DEPTH, MIX_WIDTH, D_MODEL), MIX_WIDTH ** -0.5),
        "w_gate_up": nrm(ks[21], (DEPTH, D_MODEL, 2 * D_FF), D_MODEL ** -0.5),
        "w_down": nrm(ks[22], (DEPTH, D_FF, D_MODEL), D_FF ** -0.5),
    }


def reference(x_prompt, x_sample, state_mlstm_C, state_mlstm_n, state_mlstm_m, state_gdn_S, state_gdn_conv,
              meta_tokens, norm_pre_mix, norm_post_mix, norm_pre_ffn, norm_post_ffn, w_in, mlstm_b_i, mlstm_b_f,
              mlstm_norm, gdn_conv_w, gdn_A_log, gdn_dt_bias, gdn_norm, w_out, w_gate_up, w_down):
    f32 = jnp.float32
    B = x_prompt.shape[0]
    Bs = x_sample.shape[0]
    xp = jnp.concatenate([jnp.broadcast_to(meta_tokens.astype(x_prompt.dtype)[None], (B, N_META, D_MODEL)), x_prompt], axis=1)
    xs = x_sample
    p_states, s_states = [], []
    for l in range(DEPTH):
        w = (norm_pre_mix[l], norm_post_mix[l], norm_pre_ffn[l], norm_post_ffn[l], w_in[l], mlstm_b_i[l],
             mlstm_b_f[l], mlstm_norm[l], gdn_conv_w[l], gdn_A_log[l], gdn_dt_bias[l], gdn_norm[l], w_out[l],
             w_gate_up[l], w_down[l])
        st_p = (jnp.zeros((B, MLSTM_HEADS, HEAD_DIM, HEAD_DIM), f32),
                jnp.zeros((B, MLSTM_HEADS, HEAD_DIM), f32),
                jnp.zeros((B, MLSTM_HEADS), f32),
                jnp.zeros((B, GDN_HEADS, HEAD_DIM, HEAD_DIM), f32),
                jnp.zeros((B, CONV_W - 1, 3 * GDN_WIDTH), f32))
        xp, new_p = decoder_layer(xp, st_p, *w, lead=N_META, chunk=CHUNK)
        st_s = (state_mlstm_C[l].astype(f32), state_mlstm_n[l].astype(f32), state_mlstm_m[l].astype(f32),
                state_gdn_S[l].astype(f32), state_gdn_conv[l].astype(f32))
        xs, new_s = decoder_layer(xs, st_s, *w, lead=0, chunk=xs.shape[1])
        p_states.append(new_p)
        s_states.append(new_s)
    y_prompt = xp[:, N_META:]
    y_sample = xs
    pC, pn, pm, pS, pconv = [jnp.stack([s[i] for s in p_states]) for i in range(5)]
    sC, sn, sm, sS, sconv = [jnp.stack([s[i] for s in s_states]) for i in range(5)]
    return (y_prompt, y_sample, pC, pn, pm, pS, pconv, sC, sn, sm, sS, sconv)
```

```python
import functools

import jax
import jax.numpy as jnp
from jax import lax
from jax.experimental import pallas as pl
from jax.experimental.pallas import tpu as pltpu

F32 = jnp.float32
BF16 = jnp.bfloat16

D_MODEL = 2048
HEAD_DIM = 128
N_HEADS = 8
WIDTH = N_HEADS * HEAD_DIM
CONV_W = 4
N_META = 16
D_FF = 5632
EPS = 1e-6
Q_SCALE = HEAD_DIM ** -0.5

COL_MQ, COL_MK, COL_MV, COL_MO = 0, 8, 16, 24
COL_GQ, COL_GK, COL_GV, COL_GZ = 32, 40, 48, 56
COL_GATES = 64
PROJ_COLS = 65 * 128
LANE_MI, LANE_MF, LANE_GA, LANE_GB = 0, 8, 16, 24

MLSTM_CHUNK = 256
GDN_CHUNK = 64
VMEM_LIMIT = 56 * 1024 * 1024


def _sigmoid(x):
    return 1.0 / (1.0 + jnp.exp(-x))


def _softplus(x):
    return jnp.maximum(x, 0.0) + jnp.log1p(jnp.exp(-jnp.abs(x)))


def _log_sigmoid(x):
    return -_softplus(-x)


def _rms_scale(x):
    return lax.rsqrt(jnp.mean(x * x, axis=-1, keepdims=True) + EPS)


def _dot(a, b):
    return jnp.dot(a, b, preferred_element_type=F32)


def _dot_nt(a, b):
    return lax.dot_general(a, b, (((1,), (1,)), ((), ())), preferred_element_type=F32)


def _dot_tn(a, b):
    return lax.dot_general(a, b, (((0,), (0,)), ((), ())), preferred_element_type=F32)


def _dot_f32(a, b):
    return jnp.dot(a, b, precision=lax.Precision.HIGHEST, preferred_element_type=F32)


def _select_lane(x, lane_idx, target):
    return jnp.sum(jnp.where(lane_idx == target, x, 0.0), axis=1, keepdims=True)


def _norm_matmul_kernel(x_ref, g_ref, w_ref, o_ref, u_ref):
    @pl.when(pl.program_id(1) == 0)
    def _():
        x = x_ref[...]
        u_ref[...] = (x * _rms_scale(x) * g_ref[...]).astype(BF16)

    o_ref[...] = _dot(u_ref[...], w_ref[...])


def _norm_matmul(x, gain, w, *, tm, tn):
    m, d = x.shape
    n = w.shape[1]
    return pl.pallas_call(
        _norm_matmul_kernel,
        grid=(m // tm, n // tn),
        in_specs=[pl.BlockSpec((tm, d), lambda i, j: (i, 0)),
                  pl.BlockSpec((1, d), lambda i, j: (0, 0)),
                  pl.BlockSpec((d, tn), lambda i, j: (0, j))],
        out_specs=pl.BlockSpec((tm, tn), lambda i, j: (i, j)),
        out_shape=jax.ShapeDtypeStruct((m, n), F32),
        scratch_shapes=[pltpu.VMEM((tm, d), BF16)],
        compiler_params=pltpu.CompilerParams(dimension_semantics=("parallel", "arbitrary"),
                                             vmem_limit_bytes=VMEM_LIMIT),
        name="norm_matmul",
    )(x, gain, w)


def _out_proj_kernel(ma_ref, mb_ref, wa_ref, wb_ref, x_ref, gpost_ref, gpre_ref, x1_ref, u2_ref):
    mo = _dot(ma_ref[...].astype(BF16), wa_ref[...]) + _dot(mb_ref[...].astype(BF16), wb_ref[...])
    x1 = x_ref[...] + mo * _rms_scale(mo) * gpost_ref[...]
    x1_ref[...] = x1
    u2_ref[...] = (x1 * _rms_scale(x1) * gpre_ref[...]).astype(BF16)


def _out_proj(mix_a, col_a, mix_b, col_b, w_out, x, g_post, g_pre, *, tm):
    m, d = x.shape
    return pl.pallas_call(
        _out_proj_kernel,
        grid=(m // tm,),
        in_specs=[pl.BlockSpec((tm, WIDTH), lambda i: (i, col_a)),
                  pl.BlockSpec((tm, WIDTH), lambda i: (i, col_b)),
                  pl.BlockSpec((WIDTH, d), lambda i: (0, 0)),
                  pl.BlockSpec((WIDTH, d), lambda i: (1, 0)),
                  pl.BlockSpec((tm, d), lambda i: (i, 0)),
                  pl.BlockSpec((1, d), lambda i: (0, 0)),
                  pl.BlockSpec((1, d), lambda i: (0, 0))],
        out_specs=[pl.BlockSpec((tm, d), lambda i: (i, 0)),
                   pl.BlockSpec((tm, d), lambda i: (i, 0))],
        out_shape=[jax.ShapeDtypeStruct((m, d), F32), jax.ShapeDtypeStruct((m, d), BF16)],
        compiler_params=pltpu.CompilerParams(dimension_semantics=("parallel",),
                                             vmem_limit_bytes=VMEM_LIMIT),
        name="out_proj",
    )(mix_a, mix_b, w_out, w_out, x, g_post, g_pre)


def _ffn_kernel(u_ref, wg_ref, wu_ref, wd_ref, x1_ref, gain_ref, o_ref, acc_ref):
    f = pl.program_id(1)
    u = u_ref[...]
    g = _dot(u, wg_ref[...])
    up = _dot(u, wu_ref[...])
    hidden = (g * _sigmoid(g) * up).astype(BF16)
    part = _dot(hidden, wd_ref[...])

    @pl.when(f == 0)
    def _():
        acc_ref[...] = part

    @pl.when(f > 0)
    def _():
        acc_ref[...] += part

    @pl.when(f == pl.num_programs(1) - 1)
    def _():
        y = acc_ref[...]
        o_ref[...] = x1_ref[...] + y * _rms_scale(y) * gain_ref[...]


def _ffn(u2, w_gate_up, w_down, x1, gain, *, tm, tf):
    m, d = x1.shape
    nf = D_FF // tf
    return pl.pallas_call(
        _ffn_kernel,
        grid=(m // tm, nf),
        in_specs=[pl.BlockSpec((tm, d), lambda i, f: (i, 0)),
                  pl.BlockSpec((d, tf), lambda i, f: (0, f)),
                  pl.BlockSpec((d, tf), lambda i, f: (0, nf + f)),
                  pl.BlockSpec((tf, d), lambda i, f: (f, 0)),
                  pl.BlockSpec((tm, d), lambda i, f: (i, 0)),
                  pl.BlockSpec((1, d), lambda i, f: (0, 0))],
        out_specs=pl.BlockSpec((tm, d), lambda i, f: (i, 0)),
        out_shape=jax.ShapeDtypeStruct((m, d), F32),
        scratch_shapes=[pltpu.VMEM((tm, d), F32)],
        compiler_params=pltpu.CompilerParams(dimension_semantics=("parallel", "arbitrary"),
                                             vmem_limit_bytes=VMEM_LIMIT),
        name="ffn",
    )(u2, w_gate_up, w_gate_up, w_down, x1, gain)


def _mlstm_chunk_kernel(q_ref, k_ref, v_ref, og_ref, gates_ref, gbias_ref, mnorm_ref, c0_ref, n0_ref, m0_ref,
                        h_ref, c_out_ref, n_out_ref, m_out_ref, c_s, n_s, m_s, *, chunk):
    L = chunk
    head = pl.program_id(1)
    c = pl.program_id(2)

    @pl.when(c == 0)
    def _():
        c_s[...] = c0_ref[0, 0]
        n_s[...] = n0_ref[0, 0]
        m_s[...] = m0_ref[0, 0]

    gates = gates_ref[...] + gbias_ref[...]
    lane = lax.broadcasted_iota(jnp.int32, (L, 128), 1)
    li_col = _select_lane(gates, lane, head + LANE_MI)
    lf_col = _select_lane(_log_sigmoid(gates), lane, head + LANE_MF)

    rows = lax.broadcasted_iota(jnp.int32, (L, L), 0)
    cols = lax.broadcasted_iota(jnp.int32, (L, L), 1)
    tril = rows >= cols
    eye = rows == cols
    lf_row = jnp.sum(jnp.where(eye, lf_col, 0.0), axis=0, keepdims=True)
    li_row = jnp.sum(jnp.where(eye, li_col, 0.0), axis=0, keepdims=True)
    b_col = jnp.sum(jnp.where(tril, lf_row, 0.0), axis=1, keepdims=True)
    b_row = jnp.sum(jnp.where(rows <= cols, lf_col, 0.0), axis=0, keepdims=True)

    m_prev = m_s[:, 0:1]
    a_col = b_col + m_prev
    dm = jnp.where(tril, b_col - b_row + li_row, -jnp.inf)
    m_t = jnp.maximum(a_col, jnp.max(dm, axis=1, keepdims=True))
    inter = jnp.exp(a_col - m_t)
    wmat = jnp.exp(dm - m_t)

    q = q_ref[...] * Q_SCALE
    k = k_ref[...]
    qb = q.astype(BF16)
    kb = k.astype(BF16)
    vb = v_ref[...].astype(BF16)
    c_prev = c_s[...]
    n_prev = n_s[...]

    s = _dot_nt(qb, kb) * wmat
    num = _dot(s.astype(BF16), vb) + inter * _dot(qb, c_prev.astype(BF16))
    den = jnp.sum(s, axis=1, keepdims=True) + inter * jnp.sum(q * n_prev, axis=1, keepdims=True)
    hh = num / jnp.maximum(jnp.abs(den), jnp.exp(-m_t))
    out = hh * _rms_scale(hh) * mnorm_ref[0] * _sigmoid(og_ref[...])
    h_ref[...] = out.astype(h_ref.dtype)

    m_last = m_t[L - 1:L, :]
    b_last = b_col[L - 1:L, :]
    w_last = jnp.exp(b_last - b_col + li_col - m_last)
    d_c = jnp.exp(b_last + m_prev - m_last)
    kw = w_last * k
    c_new = d_c * c_prev + _dot_tn(kw.astype(BF16), vb)
    n_new = d_c * n_prev + jnp.sum(kw, axis=0, keepdims=True)
    m_new = jnp.broadcast_to(m_last, (1, 128))
    c_s[...] = c_new
    n_s[...] = n_new
    m_s[...] = m_new

    @pl.when(c == pl.num_programs(2) - 1)
    def _():
        c_out_ref[0, 0] = c_new
        n_out_ref[0, 0] = n_new
        m_out_ref[0, 0] = m_new


def _mlstm_chunked(proj, gbias, mnorm, c0, n0, m0, *, batch, seq, chunk):
    nc = seq // chunk
    shared = c0.shape[0] == 1

    def col(base):
        return pl.BlockSpec((chunk, 128), lambda b, h, c: (b * nc + c, base + h))

    def state(shape):
        return pl.BlockSpec((1, 1) + shape, lambda b, h, c: (0 if shared else b, h, 0, 0))

    def state_out(shape):
        return pl.BlockSpec((1, 1) + shape, lambda b, h, c: (b, h, 0, 0))

    return pl.pallas_call(
        functools.partial(_mlstm_chunk_kernel, chunk=chunk),
        grid=(batch, N_HEADS, nc),
        in_specs=[col(COL_MQ), col(COL_MK), col(COL_MV), col(COL_MO),
                  pl.BlockSpec((chunk, 128), lambda b, h, c: (b * nc + c, COL_GATES)),
                  pl.BlockSpec((1, 128), lambda b, h, c: (0, 0)),
                  pl.BlockSpec((1, 1, 128), lambda b, h, c: (h, 0, 0)),
                  state((128, 128)), state((1, 128)), state((1, 128))],
        out_specs=[pl.BlockSpec((chunk, 128), lambda b, h, c: (b * nc + c, h)),
                   state_out((128, 128)), state_out((1, 128)), state_out((1, 128))],
        out_shape=[jax.ShapeDtypeStruct((batch * seq, WIDTH), BF16),
                   jax.ShapeDtypeStruct((batch, N_HEADS, 128, 128), F32),
                   jax.ShapeDtypeStruct((batch, N_HEADS, 1, 128), F32),
                   jax.ShapeDtypeStruct((batch, N_HEADS, 1, 128), F32)],
        scratch_shapes=[pltpu.VMEM((128, 128), F32), pltpu.VMEM((1, 128), F32), pltpu.VMEM((1, 128), F32)],
        compiler_params=pltpu.CompilerParams(dimension_semantics=("parallel", "parallel", "arbitrary"),
                                             vmem_limit_bytes=VMEM_LIMIT),
        name="mlstm_chunk",
    )(proj, proj, proj, proj, proj, gbias, mnorm, c0, n0, m0)


def _unit_lower_inverse(a, rows, cols, size):
    def same_block(shift):
        return (rows >> shift) == (cols >> shift)

    ident = jnp.where(rows == cols, 1.0, 0.0)
    d = jnp.where(same_block(3), a, 0.0)
    d2 = _dot_f32(d, d)
    d4 = _dot_f32(d2, d2)
    t = ident - d
    t = t + _dot_f32(t, d2)
    t = t + _dot_f32(t, d4)
    shift = 3
    while (1 << shift) < size:
        e = jnp.where(jnp.logical_and(same_block(shift + 1), jnp.logical_not(same_block(shift))), a, 0.0)
        t = t - _dot_f32(_dot_f32(t, e), t)
        shift += 1
    return t


def _gdn_prep_kernel(q_ref, k_ref, v_ref, pq_ref, pk_ref, pv_ref, iq_ref, ik_ref, iv_ref, wq_ref, wk_ref, wv_ref,
                     gates_ref, gbias_ref, arow_ref,
                     u_ref, w_ref, qe_ref, kd_ref, attn_ref, glast_ref, ext, *, chunk):
    L = chunk
    head = pl.program_id(1)
    first = pl.program_id(2) == 0

    def conv(x_ref, prev_ref, init_ref, cw_ref):
        prev = jnp.where(first, init_ref[0], prev_ref[...])
        ext[0:8, :] = prev
        ext[8:8 + L, :] = x_ref[...]
        base = 8 - (CONV_W - 1)
        acc = ext[base:base + L, :] * cw_ref[0:1, :]
        for i in range(1, CONV_W):
            acc = acc + ext[base + i:base + i + L, :] * cw_ref[i:i + 1, :]
        return acc * _sigmoid(acc)

    q = conv(q_ref, pq_ref, iq_ref, wq_ref)
    k = conv(k_ref, pk_ref, ik_ref, wk_ref)
    v = conv(v_ref, pv_ref, iv_ref, wv_ref)
    qn = q * lax.rsqrt(jnp.sum(q * q, axis=-1, keepdims=True) + EPS) * Q_SCALE
    kn = k * lax.rsqrt(jnp.sum(k * k, axis=-1, keepdims=True) + EPS)

    gates = gates_ref[...] + gbias_ref[...]
    lane = lax.broadcasted_iota(jnp.int32, (L, 128), 1)
    lg_col = _select_lane(-jnp.exp(arow_ref[...]) * _softplus(gates), lane, head + LANE_GA)
    beta = _select_lane(_sigmoid(gates), lane, head + LANE_GB)

    rows = lax.broadcasted_iota(jnp.int32, (L, L), 0)
    cols = lax.broadcasted_iota(jnp.int32, (L, L), 1)
    causal = rows >= cols
    lg_row = jnp.sum(jnp.where(rows == cols, lg_col, 0.0), axis=0, keepdims=True)
    g_col = jnp.sum(jnp.where(causal, lg_row, 0.0), axis=1, keepdims=True)
    g_row = jnp.sum(jnp.where(rows <= cols, lg_col, 0.0), axis=0, keepdims=True)
    decay = jnp.where(causal, jnp.exp(jnp.where(causal, g_col - g_row, 0.0)), 0.0)

    qb = qn.astype(BF16)
    kb = kn.astype(BF16)
    kk = _dot_nt(kb, kb)
    qk = _dot_nt(qb, kb)
    a = jnp.where(rows > cols, beta * kk * decay, 0.0)
    t = _unit_lower_inverse(a, rows, cols, L)

    e_g = jnp.exp(g_col)
    u_ref[0, 0] = _dot_f32(t, beta * v)
    w_ref[0, 0] = _dot_f32(t, (beta * e_g) * kn).astype(BF16)
    qe_ref[0, 0] = (qn * e_g).astype(BF16)
    g_last = g_col[L - 1:L, :]
    kd_ref[0, 0] = (kn * jnp.exp(g_last - g_col)).astype(BF16)
    attn_ref[0, 0, 0] = (qk * decay).astype(BF16)
    glast_ref[0, 0, 0] = jnp.broadcast_to(jnp.exp(g_last), (1, 128))


def _gdn_prep(proj, conv_init, conv_w, gbias, arow, *, batch, seq, chunk):
    nc = seq // chunk
    rows8 = seq // 8

    def col(base):
        return pl.BlockSpec((chunk, 128), lambda b, h, c: (b * nc + c, base + h))

    def prev(base):
        return pl.BlockSpec((8, 128), lambda b, h, c: (jnp.maximum(b * rows8 + c * (chunk // 8) - 1, 0), base + h))

    def init(base):
        return pl.BlockSpec((1, 8, 128), lambda b, h, c: (0, 0, base + h))

    def cw(base):
        return pl.BlockSpec((CONV_W, 128), lambda b, h, c: (0, base + h))

    def per_token(dtype):
        return (pl.BlockSpec((1, 1, chunk, 128), lambda b, h, c: (b, h, c, 0)),
                jax.ShapeDtypeStruct((batch, N_HEADS, seq, 128), dtype))

    outs = [per_token(F32), per_token(BF16), per_token(BF16), per_token(BF16),
            (pl.BlockSpec((1, 1, 1, chunk, chunk), lambda b, h, c: (b, h, c, 0, 0)),
             jax.ShapeDtypeStruct((batch, N_HEADS, nc, chunk, chunk), BF16)),
            (pl.BlockSpec((1, 1, 1, 1, 128), lambda b, h, c: (b, h, c, 0, 0)),
             jax.ShapeDtypeStruct((batch, N_HEADS, nc, 1, 128), F32))]
    return pl.pallas_call(
        functools.partial(_gdn_prep_kernel, chunk=chunk),
        grid=(batch, N_HEADS, nc),
        in_specs=[col(COL_GQ), col(COL_GK), col(COL_GV), prev(COL_GQ), prev(COL_GK), prev(COL_GV),
                  init(0), init(8), init(16), cw(0), cw(8), cw(16),
                  pl.BlockSpec((chunk, 128), lambda b, h, c: (b * nc + c, COL_GATES)),
                  pl.BlockSpec((1, 128), lambda b, h, c: (0, 0)),
                  pl.BlockSpec((1, 128), lambda b, h, c: (0, 0))],
        out_specs=[o[0] for o in outs],
        out_shape=[o[1] for o in outs],
        scratch_shapes=[pltpu.VMEM((8 + chunk, 128), F32)],
        compiler_params=pltpu.CompilerParams(dimension_semantics=("parallel", "parallel", "parallel"),
                                             vmem_limit_bytes=VMEM_LIMIT),
        name="gdn_prep",
    )(proj, proj, proj, proj, proj, proj, conv_init, conv_init, conv_init, conv_w, conv_w, conv_w,
      proj, gbias, arow)


def _gdn_scan_kernel(u_ref, w_ref, qe_ref, kd_ref, attn_ref, glast_ref, gz_ref, gnorm_ref, s0_ref,
                     o_ref, s_out_ref, s_s):
    c = pl.program_id(2)

    @pl.when(c == 0)
    def _():
        s_s[...] = s0_ref[0, 0]

    s_prev = s_s[...]
    sb = s_prev.astype(BF16)
    v_new = u_ref[0, 0] - _dot(w_ref[0, 0], sb)
    vb = v_new.astype(BF16)
    o = _dot(qe_ref[0, 0], sb) + _dot(attn_ref[0, 0, 0], vb)
    s_new = glast_ref[0, 0, 0][:, 0:1] * s_prev + _dot_tn(kd_ref[0, 0], vb)
    s_s[...] = s_new
    gz = gz_ref[...]
    o_ref[...] = (o * _rms_scale(o) * gnorm_ref[0] * (gz * _sigmoid(gz))).astype(o_ref.dtype)

    @pl.when(c == pl.num_programs(2) - 1)
    def _():
        s_out_ref[0, 0] = s_new


def _gdn_scan(prep, proj, gnorm, s0, *, batch, seq, chunk):
    nc = seq // chunk
    shared = s0.shape[0] == 1
    u, w, qe, kd, attn, glast = prep
    tok = pl.BlockSpec((1, 1, chunk, 128), lambda b, h, c: (b, h, c, 0))
    return pl.pallas_call(
        _gdn_scan_kernel,
        grid=(batch, N_HEADS, nc),
        in_specs=[tok, tok, tok, tok,
                  pl.BlockSpec((1, 1, 1, chunk, chunk), lambda b, h, c: (b, h, c, 0, 0)),
                  pl.BlockSpec((1, 1, 1, 1, 128), lambda b, h, c: (b, h, c, 0, 0)),
                  pl.BlockSpec((chunk, 128), lambda b, h, c: (b * nc + c, COL_GZ + h)),
                  pl.BlockSpec((1, 1, 128), lambda b, h, c: (h, 0, 0)),
                  pl.BlockSpec((1, 1, 128, 128), lambda b, h, c: (0 if shared else b, h, 0, 0))],
        out_specs=[pl.BlockSpec((chunk, 128), lambda b, h, c: (b * nc + c, h)),
                   pl.BlockSpec((1, 1, 128, 128), lambda b, h, c: (b, h, 0, 0))],
        out_shape=[jax.ShapeDtypeStruct((batch * seq, WIDTH), BF16),
                   jax.ShapeDtypeStruct((batch, N_HEADS, 128, 128), F32)],
        scratch_shapes=[pltpu.VMEM((128, 128), F32)],
        compiler_params=pltpu.CompilerParams(dimension_semantics=("parallel", "parallel", "arbitrary"),
                                             vmem_limit_bytes=VMEM_LIMIT),
        name="gdn_scan",
    )(u, w, qe, kd, attn, glast, proj, gnorm, s0)


def _sample_kernel(p_ref, c_ref, n_ref, m_ref, s_ref, cb_ref, gbias_ref, arow_ref, cw_ref, mnorm_ref, gnorm_ref,
                   mix_ref, c_out_ref, n_out_ref, m_out_ref, s_out_ref, ext, xt, hraw, *, steps):
    T = steps
    H = N_HEADS

    def pcol(block, h, r0=0, r1=None):
        r1 = T if r1 is None else r1
        return p_ref[0, r0:r1, (block + h) * 128:(block + h + 1) * 128]

    nbuf = CONV_W - 1
    ext[0:nbuf, :] = cb_ref[0]
    ext[nbuf:nbuf + T, :] = p_ref[0, :, COL_GQ * 128:COL_GZ * 128]
    acc = ext[0:T, :] * cw_ref[0:1, :]
    for i in range(1, CONV_W):
        acc = acc + ext[i:i + T, :] * cw_ref[i:i + 1, :]
    xc = acc * _sigmoid(acc)

    gates = p_ref[0, :, COL_GATES * 128:(COL_GATES + 1) * 128] + gbias_ref[...]
    lf_all = _log_sigmoid(gates)
    lg_all = -jnp.exp(arow_ref[...]) * _softplus(gates)
    beta_all = _sigmoid(gates)

    for h in range(H):
        xt[2 * T * h:2 * T * h + T, :] = pcol(COL_MQ, h) * Q_SCALE
        xt[2 * T * h + T:2 * T * (h + 1), :] = pcol(COL_MK, h)
        cq = xc[:, h * 128:(h + 1) * 128]
        ck = xc[:, WIDTH + h * 128:WIDTH + (h + 1) * 128]
        base = 2 * T * (H + h)
        xt[base:base + T, :] = cq * lax.rsqrt(jnp.sum(cq * cq, axis=-1, keepdims=True) + EPS) * Q_SCALE
        xt[base + T:base + 2 * T, :] = ck * lax.rsqrt(jnp.sum(ck * ck, axis=-1, keepdims=True) + EPS)
    xtt = xt[...].T

    lane1 = lax.broadcasted_iota(jnp.int32, (1, 128), 1)
    m_acc = jnp.zeros((1, 128), F32)
    for h in range(H):
        cm = c_ref[0, h]
        n = n_ref[0, h:h + 1, :]
        m = m_ref[0, :, h:h + 1]
        for t in range(T):
            li = gates[t:t + 1, LANE_MI + h:LANE_MI + h + 1]
            lf = lf_all[t:t + 1, LANE_MF + h:LANE_MF + h + 1]
            m_new = jnp.maximum(lf + m, li)
            fp = jnp.exp(lf + m - m_new)
            ip = jnp.exp(li - m_new)
            m = m_new
            qcol = xtt[:, 2 * T * h + t:2 * T * h + t + 1]
            kcol = xtt[:, 2 * T * h + T + t:2 * T * h + T + t + 1]
            cm = fp * cm + (ip * kcol) * pcol(COL_MV, h, t, t + 1)
            n = fp * n + ip * pcol(COL_MK, h, t, t + 1)
            num = jnp.sum(qcol * cm, axis=0, keepdims=True)
            qn = jnp.sum(pcol(COL_MQ, h, t, t + 1) * Q_SCALE * n, axis=1, keepdims=True)
            hraw[t:t + 1, h * 128:(h + 1) * 128] = num / jnp.maximum(jnp.abs(qn), jnp.exp(-m))
        c_out_ref[0, h] = cm
        n_out_ref[0, h:h + 1, :] = n
        m_acc = jnp.where(lane1 == h, m, m_acc)

        sm = s_ref[0, h]
        base = 2 * T * (H + h)
        for t in range(T):
            a = jnp.exp(lg_all[t:t + 1, LANE_GA + h:LANE_GA + h + 1])
            bt = beta_all[t:t + 1, LANE_GB + h:LANE_GB + h + 1]
            qcol = xtt[:, base + t:base + t + 1]
            kcol = xtt[:, base + T + t:base + T + t + 1]
            vrow = xc[t:t + 1, 2 * WIDTH + h * 128:2 * WIDTH + (h + 1) * 128]
            k_s = jnp.sum(kcol * sm, axis=0, keepdims=True)
            v_new = bt * (vrow - a * k_s)
            sm = a * sm + kcol * v_new
            hraw[t:t + 1, WIDTH + h * 128:WIDTH + (h + 1) * 128] = jnp.sum(qcol * sm, axis=0, keepdims=True)
        s_out_ref[0, h] = sm
    m_out_ref[0] = m_acc

    for h in range(H):
        hm = hraw[:, h * 128:(h + 1) * 128]
        mix_ref[0, :, h * 128:(h + 1) * 128] = hm * _rms_scale(hm) * mnorm_ref[h:h + 1, :] * _sigmoid(pcol(COL_MO, h))
        og = hraw[:, WIDTH + h * 128:WIDTH + (h + 1) * 128]
        gz = pcol(COL_GZ, h)
        mix_ref[0, :, WIDTH + h * 128:WIDTH + (h + 1) * 128] = (
            og * _rms_scale(og) * gnorm_ref[h:h + 1, :] * (gz * _sigmoid(gz)))


def _sample_step(proj, c, n, m, s, conv_buf, gbias, arow, conv_w, mnorm, gnorm):
    bs, steps, _ = proj.shape
    assert 2 * steps * 2 * N_HEADS == 128 and steps >= CONV_W - 1

    def per_seq(*shape):
        return pl.BlockSpec((1,) + shape, lambda b: (b,) + (0,) * len(shape))

    def const(*shape):
        return pl.BlockSpec(shape, lambda b: (0,) * len(shape))

    return pl.pallas_call(
        functools.partial(_sample_kernel, steps=steps),
        grid=(bs,),
        in_specs=[per_seq(steps, PROJ_COLS), per_seq(N_HEADS, 128, 128), per_seq(N_HEADS, 128), per_seq(1, 128),
                  per_seq(N_HEADS, 128, 128), per_seq(CONV_W - 1, 3 * WIDTH),
                  const(1, 128), const(1, 128), const(CONV_W, 3 * WIDTH), const(N_HEADS, 128), const(N_HEADS, 128)],
        out_specs=[per_seq(steps, 2 * WIDTH), per_seq(N_HEADS, 128, 128), per_seq(N_HEADS, 128), per_seq(1, 128),
                   per_seq(N_HEADS, 128, 128)],
        out_shape=[jax.ShapeDtypeStruct((bs, steps, 2 * WIDTH), F32),
                   jax.ShapeDtypeStruct((bs, N_HEADS, 128, 128), F32),
                   jax.ShapeDtypeStruct((bs, N_HEADS, 128), F32),
                   jax.ShapeDtypeStruct((bs, 1, 128), F32),
                   jax.ShapeDtypeStruct((bs, N_HEADS, 128, 128), F32)],
        scratch_shapes=[pltpu.VMEM((8, 3 * WIDTH), F32), pltpu.VMEM((128, 128), F32),
                        pltpu.VMEM((steps, 2 * WIDTH), F32)],
        compiler_params=pltpu.CompilerParams(dimension_semantics=("parallel",),
                                             vmem_limit_bytes=VMEM_LIMIT),
        name="sample_step",
    )(proj, c, n, m, s, conv_buf, gbias, arow, conv_w, mnorm, gnorm)


def kernel(x_prompt, x_sample, state_mlstm_C, state_mlstm_n, state_mlstm_m, state_gdn_S, state_gdn_conv, meta_tokens, norm_pre_mix, norm_post_mix, norm_pre_ffn, norm_post_ffn, w_in, mlstm_b_i, mlstm_b_f, mlstm_norm, gdn_conv_w, gdn_A_log, gdn_dt_bias, gdn_norm, w_out, w_gate_up, w_down):
    assert w_in.shape[0] == 1, "single-layer trunk"
    B, S, D = x_prompt.shape
    Bs, Ts, _ = x_sample.shape
    H = N_HEADS

    wi = w_in[0]
    o_mi = 4 * WIDTH
    o_gq = o_mi + 2 * H
    o_ga = o_gq + 4 * WIDTH
    w_proj = jnp.concatenate(
        [wi[:, :o_mi], wi[:, o_gq:o_ga], wi[:, o_mi:o_gq], wi[:, o_ga:o_ga + 2 * H],
         jnp.zeros((D, 128 - 4 * H), wi.dtype)], axis=1).astype(BF16)
    w_out_b = w_out[0].astype(BF16)
    w_gu_b = w_gate_up[0].astype(BF16)
    w_dn_b = w_down[0].astype(BF16)
    zeros_h = jnp.zeros((H,), F32)
    gbias = jnp.concatenate([mlstm_b_i[0], mlstm_b_f[0], gdn_dt_bias[0], zeros_h,
                             jnp.zeros((128 - 4 * H,), F32)]).reshape(1, 128)
    arow = jnp.concatenate([zeros_h, zeros_h, gdn_A_log[0], zeros_h,
                            jnp.zeros((128 - 4 * H,), F32)]).reshape(1, 128)
    g_pre_mix = norm_pre_mix[0].reshape(1, D)
    g_post_mix = norm_post_mix[0].reshape(1, D)
    g_pre_ffn = norm_pre_ffn[0].reshape(1, D)
    g_post_ffn = norm_post_ffn[0].reshape(1, D)
    mnorm = mlstm_norm[0]
    gnorm = gdn_norm[0]
    conv_w = gdn_conv_w[0]

    xp = x_prompt.reshape(B * S, D)
    xs = x_sample.reshape(Bs * Ts, D)

    proj_p = _norm_matmul(xp, g_pre_mix, w_proj, tm=1024, tn=640)
    proj_s = _norm_matmul(xs, g_pre_mix, w_proj, tm=Bs * Ts, tn=640)
    proj_m = _norm_matmul(meta_tokens, g_pre_mix, w_proj, tm=N_META, tn=640)

    c0, n0, m0 = (jnp.zeros((1, H, 128, 128), F32), jnp.zeros((1, H, 1, 128), F32), jnp.zeros((1, H, 1, 128), F32))
    _, c_l, n_l, m_l = _mlstm_chunked(proj_m, gbias, mnorm.reshape(H, 1, 128), c0, n0, m0,
                                      batch=1, seq=N_META, chunk=N_META)
    prep_l = _gdn_prep(proj_m, jnp.zeros((1, 8, 3 * WIDTH), F32), conv_w, gbias, arow,
                       batch=1, seq=N_META, chunk=N_META)
    _, s_l = _gdn_scan(prep_l, proj_m, gnorm.reshape(H, 1, 128), c0, batch=1, seq=N_META, chunk=N_META)

    mix_a, p_c, p_n, p_m = _mlstm_chunked(proj_p, gbias, mnorm.reshape(H, 1, 128), c_l, n_l, m_l,
                                          batch=B, seq=S, chunk=MLSTM_CHUNK)
    conv_init = proj_m[N_META - 8:, COL_GQ * 128:COL_GZ * 128].reshape(1, 8, 3 * WIDTH)
    prep_p = _gdn_prep(proj_p, conv_init, conv_w, gbias, arow, batch=B, seq=S, chunk=GDN_CHUNK)
    mix_b, p_s = _gdn_scan(prep_p, proj_p, gnorm.reshape(H, 1, 128), s_l, batch=B, seq=S, chunk=GDN_CHUNK)
    p_conv = proj_p.reshape(B, S, PROJ_COLS)[:, S - (CONV_W - 1):, COL_GQ * 128:COL_GZ * 128]

    x1_p, u2_p = _out_proj(mix_a, 0, mix_b, 0, w_out_b, xp, g_post_mix, g_pre_ffn, tm=256)
    y_p = _ffn(u2_p, w_gu_b, w_dn_b, x1_p, g_post_ffn, tm=512, tf=512)

    proj_s3 = proj_s.reshape(Bs, Ts, PROJ_COLS)
    m_in = jnp.pad(state_mlstm_m[0], ((0, 0), (0, 128 - H))).reshape(Bs, 1, 128)
    mix_s, s_c, s_n, s_m, s_s = _sample_step(proj_s3, state_mlstm_C[0], state_mlstm_n[0], m_in, state_gdn_S[0],
                                             state_gdn_conv[0], gbias, arow, conv_w, mnorm, gnorm)
    mix_s2 = mix_s.reshape(Bs * Ts, 2 * WIDTH)
    s_conv = proj_s3[:, Ts - (CONV_W - 1):, COL_GQ * 128:COL_GZ * 128]
    x1_s, u2_s = _out_proj(mix_s2, 0, mix_s2, 1, w_out_b, xs, g_post_mix, g_pre_ffn, tm=256)
    y_s = _ffn(u2_s, w_gu_b, w_dn_b, x1_s, g_post_ffn, tm=512, tf=512)

    return (y_p.reshape(B, S, D), y_s.reshape(Bs, Ts, D),
            p_c[None], p_n[:, :, 0, :][None], p_m[:, :, 0, 0][None], p_s[None], p_conv[None],
            s_c[None], s_n[None], s_m[:, 0, :H][None], s_s[None], s_conv[None])
```

```python
import functools

import jax
import jax.numpy as jnp
from jax import lax
from jax.experimental import pallas as pl
from jax.experimental.pallas import tpu as pltpu

F32 = jnp.float32
BF16 = jnp.bfloat16

D_MODEL = 2048
HEAD_DIM = 128
N_HEADS = 8
WIDTH = N_HEADS * HEAD_DIM
CONV_W = 4
N_META = 16
D_FF = 5632
EPS = 1e-6
Q_SCALE = HEAD_DIM ** -0.5

COL_MQ, COL_MK, COL_MV, COL_MO = 0, 8, 16, 24
COL_GQ, COL_GK, COL_GV, COL_GZ = 32, 40, 48, 56
COL_GATES = 64
PROJ_COLS = 65 * 128
LANE_MI, LANE_MF, LANE_GA, LANE_GB = 0, 8, 16, 24

MLSTM_CHUNK = 256
GDN_CHUNK = 64
GDN_PREP_CHUNKS = 8
SAMPLE_SEQS = 2
VMEM_LIMIT = 56 * 1024 * 1024


def _sigmoid(x):
    return 1.0 / (1.0 + jnp.exp(-x))


def _softplus(x):
    return jnp.maximum(x, 0.0) + jnp.log1p(jnp.exp(-jnp.abs(x)))


def _log_sigmoid(x):
    return -_softplus(-x)


def _rms_scale(x):
    return lax.rsqrt(jnp.mean(x * x, axis=-1, keepdims=True) + EPS)


def _l2_scale(x):
    return lax.rsqrt(jnp.sum(x * x, axis=-1, keepdims=True) + EPS)


def _dot(a, b):
    return jnp.dot(a, b, preferred_element_type=F32)


def _dot_nt(a, b):
    return lax.dot_general(a, b, (((1,), (1,)), ((), ())), preferred_element_type=F32)


def _dot_tn(a, b):
    return lax.dot_general(a, b, (((0,), (0,)), ((), ())), preferred_element_type=F32)


def _dot_f32(a, b):
    return jnp.dot(a, b, precision=lax.Precision.HIGHEST, preferred_element_type=F32)


def _select_lane(x, lane_idx, target):
    return jnp.sum(jnp.where(lane_idx == target, x, 0.0), axis=1, keepdims=True)


def _norm_matmul_kernel(x_ref, g_ref, w_ref, o_ref, u_ref):
    @pl.when(pl.program_id(1) == 0)
    def _():
        x = x_ref[...]
        u_ref[...] = (x * _rms_scale(x) * g_ref[...]).astype(BF16)

    o_ref[...] = _dot(u_ref[...], w_ref[...])


def _norm_matmul(x, gain, w, *, tm, tn):
    m, d = x.shape
    n = w.shape[1]
    return pl.pallas_call(
        _norm_matmul_kernel,
        grid=(m // tm, n // tn),
        in_specs=[pl.BlockSpec((tm, d), lambda i, j: (i, 0)),
                  pl.BlockSpec((1, d), lambda i, j: (0, 0)),
                  pl.BlockSpec((d, tn), lambda i, j: (0, j))],
        out_specs=pl.BlockSpec((tm, tn), lambda i, j: (i, j)),
        out_shape=jax.ShapeDtypeStruct((m, n), F32),
        scratch_shapes=[pltpu.VMEM((tm, d), BF16)],
        compiler_params=pltpu.CompilerParams(dimension_semantics=("parallel", "arbitrary"),
                                             vmem_limit_bytes=VMEM_LIMIT),
        name="norm_matmul",
    )(x, gain, w)


def _out_proj_kernel(ma_ref, mb_ref, wa_ref, wb_ref, x_ref, gpost_ref, gpre_ref, x1_ref, u2_ref):
    mo = _dot(ma_ref[...].astype(BF16), wa_ref[...]) + _dot(mb_ref[...].astype(BF16), wb_ref[...])
    x1 = x_ref[...] + mo * _rms_scale(mo) * gpost_ref[...]
    x1_ref[...] = x1
    u2_ref[...] = (x1 * _rms_scale(x1) * gpre_ref[...]).astype(BF16)


def _out_proj(mix_a, col_a, mix_b, col_b, w_out, x, g_post, g_pre, *, tm):
    m, d = x.shape
    return pl.pallas_call(
        _out_proj_kernel,
        grid=(m // tm,),
        in_specs=[pl.BlockSpec((tm, WIDTH), lambda i: (i, col_a)),
                  pl.BlockSpec((tm, WIDTH), lambda i: (i, col_b)),
                  pl.BlockSpec((WIDTH, d), lambda i: (0, 0)),
                  pl.BlockSpec((WIDTH, d), lambda i: (1, 0)),
                  pl.BlockSpec((tm, d), lambda i: (i, 0)),
                  pl.BlockSpec((1, d), lambda i: (0, 0)),
                  pl.BlockSpec((1, d), lambda i: (0, 0))],
        out_specs=[pl.BlockSpec((tm, d), lambda i: (i, 0)),
                   pl.BlockSpec((tm, d), lambda i: (i, 0))],
        out_shape=[jax.ShapeDtypeStruct((m, d), F32), jax.ShapeDtypeStruct((m, d), BF16)],
        compiler_params=pltpu.CompilerParams(dimension_semantics=("parallel",),
                                             vmem_limit_bytes=VMEM_LIMIT),
        name="out_proj",
    )(mix_a, mix_b, w_out, w_out, x, g_post, g_pre)


def _ffn_kernel(u_ref, wg_ref, wu_ref, wd_ref, x1_ref, gain_ref, o_ref, acc_ref):
    f = pl.program_id(1)
    u = u_ref[...]
    g = _dot(u, wg_ref[...])
    up = _dot(u, wu_ref[...])
    hidden = (g * _sigmoid(g) * up).astype(BF16)
    part = _dot(hidden, wd_ref[...])

    @pl.when(f == 0)
    def _():
        acc_ref[...] = part

    @pl.when(f > 0)
    def _():
        acc_ref[...] += part

    @pl.when(f == pl.num_programs(1) - 1)
    def _():
        y = acc_ref[...]
        o_ref[...] = x1_ref[...] + y * _rms_scale(y) * gain_ref[...]


def _ffn(u2, w_gate_up, w_down, x1, gain, *, tm, tf):
    m, d = x1.shape
    nf = D_FF // tf
    return pl.pallas_call(
        _ffn_kernel,
        grid=(m // tm, nf),
        in_specs=[pl.BlockSpec((tm, d), lambda i, f: (i, 0)),
                  pl.BlockSpec((d, tf), lambda i, f: (0, f)),
                  pl.BlockSpec((d, tf), lambda i, f: (0, nf + f)),
                  pl.BlockSpec((tf, d), lambda i, f: (f, 0)),
                  pl.BlockSpec((tm, d), lambda i, f: (i, 0)),
                  pl.BlockSpec((1, d), lambda i, f: (0, 0))],
        out_specs=pl.BlockSpec((tm, d), lambda i, f: (i, 0)),
        out_shape=jax.ShapeDtypeStruct((m, d), F32),
        scratch_shapes=[pltpu.VMEM((tm, d), F32)],
        compiler_params=pltpu.CompilerParams(dimension_semantics=("parallel", "arbitrary"),
                                             vmem_limit_bytes=VMEM_LIMIT),
        name="ffn",
    )(u2, w_gate_up, w_gate_up, w_down, x1, gain)


def _mlstm_chunk_kernel(q_ref, k_ref, v_ref, og_ref, gates_ref, gbias_ref, mnorm_ref, c0_ref, n0_ref, m0_ref,
                        h_ref, c_out_ref, n_out_ref, m_out_ref, c_s, n_s, m_s, *, chunk):
    L = chunk
    nb = q_ref.shape[0]
    shared = c0_ref.shape[0] == 1
    head = pl.program_id(0)
    c = pl.program_id(1)

    @pl.when(c == 0)
    def _():
        for b in range(nb):
            b0 = 0 if shared else b
            c_s[b] = c0_ref[b0, 0]
            n_s[b] = n0_ref[b0, 0]
            m_s[b] = m0_ref[b0, 0]

    lane = lax.broadcasted_iota(jnp.int32, (L, 128), 1)
    rows = lax.broadcasted_iota(jnp.int32, (L, L), 0)
    cols = lax.broadcasted_iota(jnp.int32, (L, L), 1)
    tril = rows >= cols
    triu = rows <= cols
    eye = rows == cols
    last = c == pl.num_programs(1) - 1

    for b in range(nb):
        gates = gates_ref[b] + gbias_ref[...]
        li_col = _select_lane(gates, lane, head + LANE_MI)
        lf_col = _select_lane(_log_sigmoid(gates), lane, head + LANE_MF)
        lf_row = jnp.sum(jnp.where(eye, lf_col, 0.0), axis=0, keepdims=True)
        li_row = jnp.sum(jnp.where(eye, li_col, 0.0), axis=0, keepdims=True)
        b_col = jnp.sum(jnp.where(tril, lf_row, 0.0), axis=1, keepdims=True)
        b_row = jnp.sum(jnp.where(triu, lf_col, 0.0), axis=0, keepdims=True)

        m_prev = m_s[b, :, 0:1]
        a_col = b_col + m_prev
        dm = jnp.where(tril, b_col - b_row + li_row, -jnp.inf)
        m_t = jnp.maximum(a_col, jnp.max(dm, axis=1, keepdims=True))
        inter = jnp.exp(a_col - m_t)
        wmat = jnp.exp(dm - m_t)

        q = q_ref[b] * Q_SCALE
        k = k_ref[b]
        qb = q.astype(BF16)
        kb = k.astype(BF16)
        vb = v_ref[b].astype(BF16)
        c_prev = c_s[b]
        n_prev = n_s[b]

        s = _dot_nt(qb, kb) * wmat
        num = _dot(s.astype(BF16), vb) + inter * _dot(qb, c_prev.astype(BF16))
        den = jnp.sum(s, axis=1, keepdims=True) + inter * jnp.sum(q * n_prev, axis=1, keepdims=True)
        hh = num / jnp.maximum(jnp.abs(den), jnp.exp(-m_t))
        out = hh * _rms_scale(hh) * mnorm_ref[0] * _sigmoid(og_ref[b])
        h_ref[b] = out.astype(h_ref.dtype)

        m_last = m_t[L - 1:L, :]
        b_last = b_col[L - 1:L, :]
        w_last = jnp.exp(b_last - b_col + li_col - m_last)
        d_c = jnp.exp(b_last + m_prev - m_last)
        kw = w_last * k
        c_s[b] = d_c * c_prev + _dot_tn(kw.astype(BF16), vb)
        n_s[b] = d_c * n_prev + jnp.sum(kw, axis=0, keepdims=True)
        m_s[b] = jnp.broadcast_to(m_last, (1, 128))

    @pl.when(last)
    def _():
        for b in range(nb):
            c_out_ref[b, 0] = c_s[b]
            n_out_ref[b, 0] = n_s[b]
            m_out_ref[b, 0] = m_s[b]


def _mlstm_chunked(proj, gbias, mnorm, c0, n0, m0, *, chunk):
    batch, seq, _ = proj.shape
    nb0 = c0.shape[0]

    def col(base):
        return pl.BlockSpec((batch, chunk, 128), lambda h, c: (0, c, base + h))

    def state(nb, shape):
        return pl.BlockSpec((nb, 1) + shape, lambda h, c: (0, h, 0, 0))

    return pl.pallas_call(
        functools.partial(_mlstm_chunk_kernel, chunk=chunk),
        grid=(N_HEADS, seq // chunk),
        in_specs=[col(COL_MQ), col(COL_MK), col(COL_MV), col(COL_MO),
                  pl.BlockSpec((batch, chunk, 128), lambda h, c: (0, c, COL_GATES)),
                  pl.BlockSpec((1, 128), lambda h, c: (0, 0)),
                  pl.BlockSpec((1, 1, 128), lambda h, c: (h, 0, 0)),
                  state(nb0, (128, 128)), state(nb0, (1, 128)), state(nb0, (1, 128))],
        out_specs=[pl.BlockSpec((batch, chunk, 128), lambda h, c: (0, c, h)),
                   state(batch, (128, 128)), state(batch, (1, 128)), state(batch, (1, 128))],
        out_shape=[jax.ShapeDtypeStruct((batch, seq, WIDTH), BF16),
                   jax.ShapeDtypeStruct((batch, N_HEADS, 128, 128), F32),
                   jax.ShapeDtypeStruct((batch, N_HEADS, 1, 128), F32),
                   jax.ShapeDtypeStruct((batch, N_HEADS, 1, 128), F32)],
        scratch_shapes=[pltpu.VMEM((batch, 128, 128), F32), pltpu.VMEM((batch, 1, 128), F32),
                        pltpu.VMEM((batch, 1, 128), F32)],
        compiler_params=pltpu.CompilerParams(dimension_semantics=("parallel", "arbitrary"),
                                             vmem_limit_bytes=VMEM_LIMIT),
        name="mlstm_chunk",
    )(proj, proj, proj, proj, proj, gbias, mnorm, c0, n0, m0)


def _split(x):
    hi = x.astype(BF16)
    return hi, (x - hi.astype(F32)).astype(BF16)


def _dot_split(a, b):
    a_hi, a_lo = a
    b_hi, b_lo = b
    if a_hi.shape[1] % 128 == 0:
        main = _dot(jnp.concatenate([a_hi, a_lo], axis=1), jnp.concatenate([b_hi, b_hi], axis=0))
    else:
        main = _dot(a_hi, b_hi) + _dot(a_lo, b_hi)
    return main + _dot(a_hi, b_lo)


def _unit_lower_inverse(mats, rows, cols, size):
    def same_block(shift):
        return (rows >> shift) == (cols >> shift)

    ident = jnp.where(rows == cols, 1.0, 0.0)
    d = [jnp.where(same_block(3), a, 0.0) for a in mats]
    d_s = [_split(x) for x in d]
    d2_s = [_split(_dot_split(x, x)) for x in d_s]
    d4_s = [_split(_dot_split(x, x)) for x in d2_s]
    t = [ident - x for x in d]
    t = [x + _dot_split(_split(x), y) for x, y in zip(t, d2_s)]
    t = [x + _dot_split(_split(x), y) for x, y in zip(t, d4_s)]
    shift = 3
    while (1 << shift) < size:
        off = jnp.logical_and(same_block(shift + 1), jnp.logical_not(same_block(shift)))
        e_s = [_split(jnp.where(off, a, 0.0)) for a in mats]
        t_s = [_split(x) for x in t]
        te_s = [_split(_dot_split(x, y)) for x, y in zip(t_s, e_s)]
        t = [x - _dot_split(y, z) for x, y, z in zip(t, te_s, t_s)]
        shift += 1
    return t


def _gdn_prep_kernel(q_ref, k_ref, v_ref, pq_ref, pk_ref, pv_ref, iq_ref, ik_ref, iv_ref, wq_ref, wk_ref, wv_ref,
                     gates_ref, gbias_ref, arow_ref,
                     u_ref, w_ref, qe_ref, kd_ref, attn_ref, glast_ref, ext, *, chunk, nsub):
    L = chunk
    R = nsub * L
    head = pl.program_id(1)
    first = pl.program_id(2) == 0

    def conv(x_ref, prev_ref, init_ref, cw_ref):
        prev = jnp.where(first, init_ref[0], prev_ref[0])
        ext[0:8, :] = prev
        ext[8:8 + R, :] = x_ref[0]
        base = 8 - (CONV_W - 1)
        acc = ext[base:base + R, :] * cw_ref[0:1, :]
        for i in range(1, CONV_W):
            acc = acc + ext[base + i:base + i + R, :] * cw_ref[i:i + 1, :]
        return acc * _sigmoid(acc)

    q = conv(q_ref, pq_ref, iq_ref, wq_ref)
    k = conv(k_ref, pk_ref, ik_ref, wk_ref)
    v_all = conv(v_ref, pv_ref, iv_ref, wv_ref)
    qn_all = q * _l2_scale(q) * Q_SCALE
    kn_all = k * _l2_scale(k)

    gates = gates_ref[0] + gbias_ref[...]
    lane = lax.broadcasted_iota(jnp.int32, (R, 128), 1)
    lg_all = _select_lane(-jnp.exp(arow_ref[...]) * _softplus(gates), lane, head + LANE_GA)
    beta_all = _select_lane(_sigmoid(gates), lane, head + LANE_GB)

    U = min(R, 128)
    per_unit = U // L
    rows = lax.broadcasted_iota(jnp.int32, (U, U), 0)
    cols = lax.broadcasted_iota(jnp.int32, (U, U), 1)
    shift = L.bit_length() - 1
    same_chunk = (rows >> shift) == (cols >> shift)
    causal = jnp.logical_and(rows >= cols, same_chunk)
    strict = jnp.logical_and(rows > cols, same_chunk)
    upper = jnp.logical_and(rows <= cols, same_chunk)
    eye = rows == cols
    row_id = lax.broadcasted_iota(jnp.int32, (U, 1), 0)

    units = range(R // U)
    sls = [slice(j * U, (j + 1) * U) for j in units]
    g_cols, decays, a_mats, qks = [], [], [], []
    for sl in sls:
        lg_col = lg_all[sl]
        lg_row = jnp.sum(jnp.where(eye, lg_col, 0.0), axis=0, keepdims=True)
        g_col = jnp.sum(jnp.where(causal, lg_row, 0.0), axis=1, keepdims=True)
        g_row = jnp.sum(jnp.where(upper, lg_col, 0.0), axis=0, keepdims=True)
        decay = jnp.where(causal, jnp.exp(jnp.where(causal, g_col - g_row, 0.0)), 0.0)
        kb = kn_all[sl].astype(BF16)
        kk = _dot_nt(kb, kb)
        qks.append(_dot_nt(qn_all[sl].astype(BF16), kb))
        a_mats.append(jnp.where(strict, beta_all[sl] * kk * decay, 0.0))
        g_cols.append(g_col)
        decays.append(decay)

    t_mats = _unit_lower_inverse(a_mats, rows, cols, L)

    e_gs = [jnp.exp(g) for g in g_cols]
    rhs_s = [_split(jnp.concatenate([beta_all[sl] * v_all[sl], (beta_all[sl] * e_g) * kn_all[sl]], axis=1))
             for sl, e_g in zip(sls, e_gs)]
    uws = [_dot_split(_split(t), r) for t, r in zip(t_mats, rhs_s)]

    for j in units:
        sl, g_col = sls[j], g_cols[j]
        u_ref[0, 0, sl, :] = uws[j][:, 0:128]
        w_ref[0, 0, sl, :] = uws[j][:, 128:256].astype(BF16)
        qe_ref[0, 0, sl, :] = (qn_all[sl] * e_gs[j]).astype(BF16)
        attn = qks[j] * decays[j]
        g_last_col = jnp.zeros((U, 1), F32)
        for i in range(per_unit):
            g_last = g_col[(i + 1) * L - 1:(i + 1) * L, :]
            g_last_col = jnp.where((row_id >> shift) == i, g_last, g_last_col)
            glast_ref[0, 0, j * per_unit + i] = jnp.broadcast_to(jnp.exp(g_last), (1, 128))
            diag = attn[i * L:(i + 1) * L, :]
            if i > 0:
                diag = pltpu.roll(diag, U - i * L, axis=1)
            attn_ref[0, 0, j * per_unit + i] = diag[:, 0:L].astype(BF16)
        kd_ref[0, 0, sl, :] = (kn_all[sl] * jnp.exp(g_last_col - g_col)).astype(BF16)


def _gdn_prep(proj, conv_init, conv_w, gbias, arow, *, chunk, nsub):
    batch, seq, _ = proj.shape
    nc = seq // chunk
    R = nsub * chunk

    def col(base):
        return pl.BlockSpec((1, R, 128), lambda b, h, c: (b, c, base + h))

    def prev(base):
        return pl.BlockSpec((1, 8, 128), lambda b, h, c: (b, jnp.maximum(c * (R // 8) - 1, 0), base + h))

    def init(base):
        return pl.BlockSpec((1, 8, 128), lambda b, h, c: (0, 0, base + h))

    def cw(base):
        return pl.BlockSpec((CONV_W, 128), lambda b, h, c: (0, base + h))

    def per_token(dtype):
        return (pl.BlockSpec((1, 1, R, 128), lambda b, h, c: (b, h, c, 0)),
                jax.ShapeDtypeStruct((batch, N_HEADS, seq, 128), dtype))

    outs = [per_token(F32), per_token(BF16), per_token(BF16), per_token(BF16),
            (pl.BlockSpec((1, 1, nsub, chunk, chunk), lambda b, h, c: (b, h, c, 0, 0)),
             jax.ShapeDtypeStruct((batch, N_HEADS, nc, chunk, chunk), BF16)),
            (pl.BlockSpec((1, 1, nsub, 1, 128), lambda b, h, c: (b, h, c, 0, 0)),
             jax.ShapeDtypeStruct((batch, N_HEADS, nc, 1, 128), F32))]
    return pl.pallas_call(
        functools.partial(_gdn_prep_kernel, chunk=chunk, nsub=nsub),
        grid=(batch, N_HEADS, nc // nsub),
        in_specs=[col(COL_GQ), col(COL_GK), col(COL_GV), prev(COL_GQ), prev(COL_GK), prev(COL_GV),
                  init(0), init(8), init(16), cw(0), cw(8), cw(16),
                  pl.BlockSpec((1, R, 128), lambda b, h, c: (b, c, COL_GATES)),
                  pl.BlockSpec((1, 128), lambda b, h, c: (0, 0)),
                  pl.BlockSpec((1, 128), lambda b, h, c: (0, 0))],
        out_specs=[o[0] for o in outs],
        out_shape=[o[1] for o in outs],
        scratch_shapes=[pltpu.VMEM((8 + R, 128), F32)],
        compiler_params=pltpu.CompilerParams(dimension_semantics=("parallel", "parallel", "parallel"),
                                             vmem_limit_bytes=VMEM_LIMIT),
        name="gdn_prep",
    )(proj, proj, proj, proj, proj, proj, conv_init, conv_init, conv_init, conv_w, conv_w, conv_w,
      proj, gbias, arow)


def _gdn_scan_kernel(u_ref, w_ref, qe_ref, kd_ref, attn_ref, glast_ref, gz_ref, gnorm_ref, s0_ref,
                     o_ref, s_out_ref, s_s):
    nb = u_ref.shape[0]
    shared = s0_ref.shape[0] == 1
    c = pl.program_id(1)
    last = c == pl.num_programs(1) - 1

    @pl.when(c == 0)
    def _():
        for b in range(nb):
            s_s[b] = s0_ref[0 if shared else b, 0]

    L = u_ref.shape[2]
    seqs = range(nb)
    s_prev = [s_s[b] for b in seqs]
    ws = [_dot(jnp.concatenate([w_ref[b, 0], qe_ref[b, 0]], axis=0), s_prev[b].astype(BF16)) for b in seqs]
    vb = [(u_ref[b, 0] - ws[b][0:L]).astype(BF16) for b in seqs]
    intra = [_dot(attn_ref[b, 0, 0], vb[b]) for b in seqs]
    upd = [_dot_tn(kd_ref[b, 0], vb[b]) for b in seqs]
    for b in seqs:
        s_s[b] = glast_ref[b, 0, 0][:, 0:1] * s_prev[b] + upd[b]
        o = ws[b][L:2 * L] + intra[b]
        gz = gz_ref[b]
        o_ref[b] = (o * _rms_scale(o) * gnorm_ref[0] * (gz * _sigmoid(gz))).astype(o_ref.dtype)

    @pl.when(last)
    def _():
        for b in range(nb):
            s_out_ref[b, 0] = s_s[b]


def _gdn_scan(prep, proj, gnorm, s0, *, chunk):
    batch, seq, _ = proj.shape
    u, w, qe, kd, attn, glast = prep
    tok = pl.BlockSpec((batch, 1, chunk, 128), lambda h, c: (0, h, c, 0))
    return pl.pallas_call(
        _gdn_scan_kernel,
        grid=(N_HEADS, seq // chunk),
        in_specs=[tok, tok, tok, tok,
                  pl.BlockSpec((batch, 1, 1, chunk, chunk), lambda h, c: (0, h, c, 0, 0)),
                  pl.BlockSpec((batch, 1, 1, 1, 128), lambda h, c: (0, h, c, 0, 0)),
                  pl.BlockSpec((batch, chunk, 128), lambda h, c: (0, c, COL_GZ + h)),
                  pl.BlockSpec((1, 1, 128), lambda h, c: (h, 0, 0)),
                  pl.BlockSpec((s0.shape[0], 1, 128, 128), lambda h, c: (0, h, 0, 0))],
        out_specs=[pl.BlockSpec((batch, chunk, 128), lambda h, c: (0, c, h)),
                   pl.BlockSpec((batch, 1, 128, 128), lambda h, c: (0, h, 0, 0))],
        out_shape=[jax.ShapeDtypeStruct((batch, seq, WIDTH), BF16),
                   jax.ShapeDtypeStruct((batch, N_HEADS, 128, 128), F32)],
        scratch_shapes=[pltpu.VMEM((batch, 128, 128), F32)],
        compiler_params=pltpu.CompilerParams(dimension_semantics=("parallel", "arbitrary"),
                                             vmem_limit_bytes=VMEM_LIMIT),
        name="gdn_scan",
    )(u, w, qe, kd, attn, glast, proj, gnorm, s0)


def _sample_kernel(p_ref, c_ref, n_ref, m_ref, s_ref, cb_ref, gbias_ref, arow_ref, cw_ref, mnorm_ref, gnorm_ref,
                   mix_ref, c_out_ref, n_out_ref, m_out_ref, s_out_ref, ext, xc_s, *, steps):
    T = steps
    NS = SAMPLE_SEQS
    R = NS * T
    H = N_HEADS
    nbuf = CONV_W - 1

    for s in range(NS):
        ext[8 * s:8 * s + nbuf, :] = cb_ref[0, s]
        ext[8 * s + nbuf:8 * s + nbuf + T, :] = p_ref[0, s * T:(s + 1) * T, COL_GQ * 128:COL_GZ * 128]
        acc = ext[8 * s:8 * s + T, :] * cw_ref[0:1, :]
        for i in range(1, CONV_W):
            acc = acc + ext[8 * s + i:8 * s + i + T, :] * cw_ref[i:i + 1, :]
        xc_s[s * T:(s + 1) * T, :] = acc * _sigmoid(acc)

    rowi = lax.broadcasted_iota(jnp.int32, (R, 1), 0)
    tpos = rowi & (T - 1)
    second = rowi >= T
    seq_rows = [jnp.logical_not(second), second]

    def pick(x, j):
        return jnp.where(second, x[T + j:T + j + 1, :], x[j:j + 1, :])

    def cumsum_t(x):
        acc = jnp.where(tpos >= 0, pick(x, 0), 0.0)
        for j in range(1, T):
            acc = acc + jnp.where(tpos >= j, pick(x, j), 0.0)
        return acc

    def pcol(block, h):
        return p_ref[0, :, (block + h) * 128:(block + h + 1) * 128]

    gates = p_ref[0, :, COL_GATES * 128:(COL_GATES + 1) * 128] + gbias_ref[...]
    lane = lax.broadcasted_iota(jnp.int32, (R, 128), 1)
    lf_all = _log_sigmoid(gates)
    lg_all = -jnp.exp(arow_ref[...]) * _softplus(gates)
    beta_all = _sigmoid(gates)
    m_in = m_ref[0]
    lane_s = lax.broadcasted_iota(jnp.int32, (NS, 128), 1)
    row_s = lax.broadcasted_iota(jnp.int32, (NS, 128), 0)
    m_acc = jnp.zeros((NS, 128), F32)

    for h in range(H):
        li = _select_lane(gates, lane, LANE_MI + h)
        lf = _select_lane(lf_all, lane, LANE_MF + h)
        mh = _select_lane(m_in, lane_s, h)
        m0 = jnp.where(second, mh[1:2, :], mh[0:1, :])
        b = cumsum_t(lf)
        a = b + m0
        dcol = [b - pick(b, j) + pick(li, j) for j in range(T)]
        m_t = a
        for j in range(T):
            m_t = jnp.maximum(m_t, jnp.where(tpos >= j, dcol[j], -jnp.inf))
        inter = jnp.exp(a - m_t)

        q = pcol(COL_MQ, h) * Q_SCALE
        k = pcol(COL_MK, h)
        v = pcol(COL_MV, h)
        num = jnp.zeros((R, 128), F32)
        den = jnp.zeros((R, 1), F32)
        for j in range(T):
            w_j = jnp.where(tpos >= j, jnp.exp(jnp.where(tpos >= j, dcol[j] - m_t, 0.0)), 0.0)
            s_j = jnp.sum(q * pick(k, j), axis=1, keepdims=True) * w_j
            num = num + s_j * pick(v, j)
            den = den + s_j
        qb = q.astype(BF16)
        qc = jnp.where(second, _dot(qb, c_ref[0, 1, h].astype(BF16)), _dot(qb, c_ref[0, 0, h].astype(BF16)))
        n_sel = jnp.where(second, n_ref[0, 1, h:h + 1, :], n_ref[0, 0, h:h + 1, :])
        num = num + inter * qc
        den = den + inter * jnp.sum(q * n_sel, axis=1, keepdims=True)
        hh = num / jnp.maximum(jnp.abs(den), jnp.exp(-m_t))
        mix_ref[0, :, h * 128:(h + 1) * 128] = hh * _rms_scale(hh) * mnorm_ref[h:h + 1, :] * _sigmoid(pcol(COL_MO, h))

        m_l = pick(m_t, T - 1)
        b_l = pick(b, T - 1)
        kw = jnp.exp(b_l - b + li - m_l) * k
        d_c = jnp.exp(b_l + m0 - m_l)
        vb = v.astype(BF16)
        for s in range(NS):
            kws = jnp.where(seq_rows[s], kw, 0.0)
            dcs = d_c[s * T:s * T + 1, :]
            c_out_ref[0, s, h] = dcs * c_ref[0, s, h] + _dot_tn(kws.astype(BF16), vb)
            n_out_ref[0, s, h:h + 1, :] = dcs * n_ref[0, s, h:h + 1, :] + jnp.sum(kws, axis=0, keepdims=True)
            m_acc = jnp.where(jnp.logical_and(lane_s == h, row_s == s), m_l[s * T:s * T + 1, :], m_acc)

        lg = _select_lane(lg_all, lane, LANE_GA + h)
        beta = _select_lane(beta_all, lane, LANE_GB + h)
        g = cumsum_t(lg)
        e_g = jnp.exp(g)
        cq = xc_s[:, h * 128:(h + 1) * 128]
        ck = xc_s[:, WIDTH + h * 128:WIDTH + (h + 1) * 128]
        cv = xc_s[:, 2 * WIDTH + h * 128:2 * WIDTH + (h + 1) * 128]
        qn = cq * _l2_scale(cq) * Q_SCALE
        kn = ck * _l2_scale(ck)
        kq = jnp.concatenate([kn, qn], axis=0).astype(BF16)
        r0 = _dot(kq, s_ref[0, 0, h].astype(BF16))
        r1 = _dot(kq, s_ref[0, 1, h].astype(BF16))
        k_s = jnp.where(second, r1[0:R], r0[0:R])
        q_s = jnp.where(second, r1[R:2 * R], r0[R:2 * R])
        x = beta * (cv - e_g * k_s)
        rel = [jnp.exp(jnp.where(tpos >= j, g - pick(g, j), 0.0)) for j in range(T)]
        for j in range(T - 1):
            kk_j = jnp.sum(kn * pick(kn, j), axis=1, keepdims=True)
            a_j = jnp.where(tpos > j, beta * kk_j * rel[j], 0.0)
            x = x - a_j * pick(x, j)
        o = e_g * q_s
        for j in range(T):
            qk_j = jnp.sum(qn * pick(kn, j), axis=1, keepdims=True)
            o = o + jnp.where(tpos >= j, qk_j * rel[j], 0.0) * pick(x, j)
        gz = pcol(COL_GZ, h)
        mix_ref[0, :, WIDTH + h * 128:WIDTH + (h + 1) * 128] = (
            o * _rms_scale(o) * gnorm_ref[h:h + 1, :] * (gz * _sigmoid(gz)))

        g_l = pick(g, T - 1)
        kd = kn * jnp.exp(g_l - g)
        e_l = jnp.exp(g_l)
        xb = x.astype(BF16)
        for s in range(NS):
            kds = jnp.where(seq_rows[s], kd, 0.0)
            s_out_ref[0, s, h] = e_l[s * T:s * T + 1, :] * s_ref[0, s, h] + _dot_tn(kds.astype(BF16), xb)
    m_out_ref[0] = m_acc


def _sample_step(proj, c, n, m, s, conv_buf, gbias, arow, conv_w, mnorm, gnorm, *, steps):
    groups = proj.shape[0]
    ns = SAMPLE_SEQS
    assert ns == 2 and ns * steps == 8 and steps >= CONV_W - 1 and steps & (steps - 1) == 0

    def per_group(*shape):
        return pl.BlockSpec((1,) + shape, lambda b: (b,) + (0,) * len(shape))

    def const(*shape):
        return pl.BlockSpec(shape, lambda b: (0,) * len(shape))

    state = (ns, N_HEADS, 128, 128)
    return pl.pallas_call(
        functools.partial(_sample_kernel, steps=steps),
        grid=(groups,),
        in_specs=[per_group(ns * steps, PROJ_COLS), per_group(*state), per_group(ns, N_HEADS, 128),
                  per_group(ns, 128), per_group(*state), per_group(ns, CONV_W - 1, 3 * WIDTH),
                  const(1, 128), const(1, 128), const(CONV_W, 3 * WIDTH), const(N_HEADS, 128), const(N_HEADS, 128)],
        out_specs=[per_group(ns * steps, 2 * WIDTH), per_group(*state), per_group(ns, N_HEADS, 128),
                   per_group(ns, 128), per_group(*state)],
        out_shape=[jax.ShapeDtypeStruct((groups, ns * steps, 2 * WIDTH), F32),
                   jax.ShapeDtypeStruct((groups,) + state, F32),
                   jax.ShapeDtypeStruct((groups, ns, N_HEADS, 128), F32),
                   jax.ShapeDtypeStruct((groups, ns, 128), F32),
                   jax.ShapeDtypeStruct((groups,) + state, F32)],
        scratch_shapes=[pltpu.VMEM((8 * ns, 3 * WIDTH), F32), pltpu.VMEM((ns * steps, 3 * WIDTH), F32)],
        compiler_params=pltpu.CompilerParams(dimension_semantics=("parallel",),
                                             vmem_limit_bytes=VMEM_LIMIT),
        name="sample_step",
    )(proj, c, n, m, s, conv_buf, gbias, arow, conv_w, mnorm, gnorm)


def kernel(x_prompt, x_sample, state_mlstm_C, state_mlstm_n, state_mlstm_m, state_gdn_S, state_gdn_conv, meta_tokens, norm_pre_mix, norm_post_mix, norm_pre_ffn, norm_post_ffn, w_in, mlstm_b_i, mlstm_b_f, mlstm_norm, gdn_conv_w, gdn_A_log, gdn_dt_bias, gdn_norm, w_out, w_gate_up, w_down):
    assert w_in.shape[0] == 1, "single-layer trunk"
    B, S, D = x_prompt.shape
    Bs, Ts, _ = x_sample.shape
    H = N_HEADS

    wi = w_in[0]
    o_mi = 4 * WIDTH
    o_gq = o_mi + 2 * H
    o_ga = o_gq + 4 * WIDTH
    w_proj = jnp.concatenate(
        [wi[:, :o_mi], wi[:, o_gq:o_ga], wi[:, o_mi:o_gq], wi[:, o_ga:o_ga + 2 * H],
         jnp.zeros((D, 128 - 4 * H), wi.dtype)], axis=1).astype(BF16)
    w_out_b = w_out[0].astype(BF16)
    w_gu_b = w_gate_up[0].astype(BF16)
    w_dn_b = w_down[0].astype(BF16)
    zeros_h = jnp.zeros((H,), F32)
    gbias = jnp.concatenate([mlstm_b_i[0], mlstm_b_f[0], gdn_dt_bias[0], zeros_h,
                             jnp.zeros((128 - 4 * H,), F32)]).reshape(1, 128)
    arow = jnp.concatenate([zeros_h, zeros_h, gdn_A_log[0], zeros_h,
                            jnp.zeros((128 - 4 * H,), F32)]).reshape(1, 128)
    g_pre_mix = norm_pre_mix[0].reshape(1, D)
    g_post_mix = norm_post_mix[0].reshape(1, D)
    g_pre_ffn = norm_pre_ffn[0].reshape(1, D)
    g_post_ffn = norm_post_ffn[0].reshape(1, D)
    mnorm = mlstm_norm[0]
    gnorm = gdn_norm[0]
    mnorm3 = mnorm.reshape(H, 1, 128)
    gnorm3 = gnorm.reshape(H, 1, 128)
    conv_w = gdn_conv_w[0]

    xp = x_prompt.reshape(B * S, D)
    xs = x_sample.reshape(Bs * Ts, D)

    proj_p = _norm_matmul(xp, g_pre_mix, w_proj, tm=1024, tn=640).reshape(B, S, PROJ_COLS)
    proj_s = _norm_matmul(xs, g_pre_mix, w_proj, tm=Bs * Ts, tn=640)
    proj_m = _norm_matmul(meta_tokens, g_pre_mix, w_proj, tm=N_META, tn=640).reshape(1, N_META, PROJ_COLS)

    c0, n0, m0 = (jnp.zeros((1, H, 128, 128), F32), jnp.zeros((1, H, 1, 128), F32), jnp.zeros((1, H, 1, 128), F32))
    _, c_l, n_l, m_l = _mlstm_chunked(proj_m, gbias, mnorm3, c0, n0, m0, chunk=N_META)
    prep_l = _gdn_prep(proj_m, jnp.zeros((1, 8, 3 * WIDTH), F32), conv_w, gbias, arow, chunk=N_META, nsub=1)
    _, s_l = _gdn_scan(prep_l, proj_m, gnorm3, c0, chunk=N_META)

    mix_a, p_c, p_n, p_m = _mlstm_chunked(proj_p, gbias, mnorm3, c_l, n_l, m_l, chunk=MLSTM_CHUNK)
    conv_init = proj_m[:, N_META - 8:, COL_GQ * 128:COL_GZ * 128]
    prep_p = _gdn_prep(proj_p, conv_init, conv_w, gbias, arow, chunk=GDN_CHUNK, nsub=GDN_PREP_CHUNKS)
    mix_b, p_s = _gdn_scan(prep_p, proj_p, gnorm3, s_l, chunk=GDN_CHUNK)
    p_conv = proj_p[:, S - (CONV_W - 1):, COL_GQ * 128:COL_GZ * 128]

    x1_p, u2_p = _out_proj(mix_a.reshape(B * S, WIDTH), 0, mix_b.reshape(B * S, WIDTH), 0, w_out_b, xp,
                           g_post_mix, g_pre_ffn, tm=256)
    y_p = _ffn(u2_p, w_gu_b, w_dn_b, x1_p, g_post_ffn, tm=512, tf=512)

    ns = SAMPLE_SEQS
    grp = Bs // ns
    m_in = jnp.pad(state_mlstm_m[0], ((0, 0), (0, 128 - H))).reshape(grp, ns, 128)
    mix_s, s_c, s_n, s_m, s_s = _sample_step(
        proj_s.reshape(grp, ns * Ts, PROJ_COLS), state_mlstm_C[0].reshape(grp, ns, H, 128, 128),
        state_mlstm_n[0].reshape(grp, ns, H, 128), m_in, state_gdn_S[0].reshape(grp, ns, H, 128, 128),
        state_gdn_conv[0].reshape(grp, ns, CONV_W - 1, 3 * WIDTH), gbias, arow, conv_w, mnorm, gnorm, steps=Ts)
    mix_s2 = mix_s.reshape(Bs * Ts, 2 * WIDTH)
    s_conv = proj_s.reshape(Bs, Ts, PROJ_COLS)[:, Ts - (CONV_W - 1):, COL_GQ * 128:COL_GZ * 128]
    x1_s, u2_s = _out_proj(mix_s2, 0, mix_s2, 1, w_out_b, xs, g_post_mix, g_pre_ffn, tm=256)
    y_s = _ffn(u2_s, w_gu_b, w_dn_b, x1_s, g_post_ffn, tm=512, tf=512)

    return (y_p.reshape(B, S, D), y_s.reshape(Bs, Ts, D),
            p_c[None], p_n[:, :, 0, :][None], p_m[:, :, 0, 0][None], p_s[None], p_conv[None],
            s_c.reshape(Bs, H, 128, 128)[None], s_n.reshape(Bs, H, 128)[None], s_m.reshape(Bs, 128)[:, :H][None],
            s_s.reshape(Bs, H, 128, 128)[None], s_conv[None])
```

```python
import functools

import jax
import jax.numpy as jnp
from jax import lax
from jax.experimental import pallas as pl
from jax.experimental.pallas import tpu as pltpu

F32 = jnp.float32
BF16 = jnp.bfloat16

D_MODEL = 2048
HEAD_DIM = 128
N_HEADS = 8
WIDTH = N_HEADS * HEAD_DIM
CONV_W = 4
N_META = 16
D_FF = 5632
EPS = 1e-6
Q_SCALE = HEAD_DIM ** -0.5

COL_MQ, COL_MK, COL_MV, COL_MO = 0, 8, 16, 24
COL_GQ, COL_GK, COL_GV, COL_GZ = 32, 40, 48, 56
COL_GATES = 64
PROJ_COLS = 66 * 128
LANE_MI, LANE_MF, LANE_GA, LANE_GB = 0, 8, 16, 24

MLSTM_CHUNK = 256
GDN_CHUNK = 64
GDN_PREP_CHUNKS = 8
SAMPLE_SEQS = 2
VMEM_LIMIT = 56 * 1024 * 1024


def _sigmoid(x):
    return 1.0 / (1.0 + jnp.exp(-x))


def _softplus(x):
    return jnp.maximum(x, 0.0) + jnp.log1p(jnp.exp(-jnp.abs(x)))


def _log_sigmoid(x):
    return -_softplus(-x)


def _rms_scale(x):
    return lax.rsqrt(jnp.mean(x * x, axis=-1, keepdims=True) + EPS)


def _l2_scale(x):
    return lax.rsqrt(jnp.sum(x * x, axis=-1, keepdims=True) + EPS)


def _dot(a, b):
    return jnp.dot(a, b, preferred_element_type=F32)


def _dot_nt(a, b):
    return lax.dot_general(a, b, (((1,), (1,)), ((), ())), preferred_element_type=F32)


def _dot_tn(a, b):
    return lax.dot_general(a, b, (((0,), (0,)), ((), ())), preferred_element_type=F32)


def _dot_f32(a, b):
    return jnp.dot(a, b, precision=lax.Precision.HIGHEST, preferred_element_type=F32)


def _select_lane(x, lane_idx, target):
    return jnp.sum(jnp.where(lane_idx == target, x, 0.0), axis=1, keepdims=True)


def _regroup_kernel(w_ref, o_ref):
    wide = 4 * WIDTH
    narrow = 2 * N_HEADS
    rows = w_ref.shape[0]
    o_ref[:, 0:wide] = w_ref[:, 0:wide].astype(BF16)
    o_ref[:, wide:2 * wide] = w_ref[:, wide + narrow:2 * wide + narrow].astype(BF16)
    gates = jnp.concatenate([w_ref[:, wide:wide + narrow], w_ref[:, 2 * wide + narrow:2 * (wide + narrow)],
                             jnp.zeros((rows, PROJ_COLS - 2 * wide - 2 * narrow), F32)], axis=1)
    o_ref[:, 2 * wide:PROJ_COLS] = gates.astype(BF16)


def _regroup_weights(w, *, tk=128):
    d, n = w.shape
    return pl.pallas_call(
        _regroup_kernel,
        grid=(d // tk,),
        in_specs=[pl.BlockSpec((tk, n), lambda i: (i, 0))],
        out_specs=pl.BlockSpec((tk, PROJ_COLS), lambda i: (i, 0)),
        out_shape=jax.ShapeDtypeStruct((d, PROJ_COLS), BF16),
        compiler_params=pltpu.CompilerParams(dimension_semantics=("parallel",), vmem_limit_bytes=VMEM_LIMIT),
        name="regroup_weights",
    )(w)


def _norm_matmul_kernel(x_ref, g_ref, w_ref, o_ref, u_ref):
    @pl.when(pl.program_id(1) == 0)
    def _():
        x = x_ref[...]
        u_ref[...] = (x * _rms_scale(x) * g_ref[...]).astype(BF16)

    res = _dot(u_ref[...], w_ref[...])
    for blk in range(o_ref.shape[0]):
        o_ref[blk] = res[:, blk * 128:(blk + 1) * 128]


def _norm_matmul(x, gain, w, *, tm, tn):
    m, d = x.shape
    n = w.shape[1]
    return pl.pallas_call(
        _norm_matmul_kernel,
        grid=(m // tm, n // tn),
        in_specs=[pl.BlockSpec((tm, d), lambda i, j: (i, 0)),
                  pl.BlockSpec((1, d), lambda i, j: (0, 0)),
                  pl.BlockSpec((d, tn), lambda i, j: (0, j))],
        out_specs=pl.BlockSpec((tn // 128, tm, 128), lambda i, j: (j, i, 0)),
        out_shape=jax.ShapeDtypeStruct((n // 128, m, 128), F32),
        scratch_shapes=[pltpu.VMEM((tm, d), BF16)],
        compiler_params=pltpu.CompilerParams(dimension_semantics=("parallel", "arbitrary"),
                                             vmem_limit_bytes=VMEM_LIMIT),
        name="norm_matmul",
    )(x, gain, w)


def _out_proj_kernel(ma_ref, mb_ref, wa_ref, wb_ref, x_ref, gpost_ref, gpre_ref, x1_ref, u2_ref):
    def rows(ref):
        if len(ref.shape) == 2:
            return ref[...].astype(BF16)
        return jnp.concatenate([ref[h] for h in range(ref.shape[0])], axis=1).astype(BF16)

    mo = _dot(rows(ma_ref), wa_ref[...]) + _dot(rows(mb_ref), wb_ref[...])
    x1 = x_ref[...] + mo * _rms_scale(mo) * gpost_ref[...]
    x1_ref[...] = x1
    u2_ref[...] = (x1 * _rms_scale(x1) * gpre_ref[...]).astype(BF16)


def _out_proj(mix_a, col_a, mix_b, col_b, w_out, x, g_post, g_pre, *, tm):
    m, d = x.shape

    def mix_spec(mix, col):
        if mix.ndim == 2:
            return pl.BlockSpec((tm, WIDTH), lambda i: (i, col))
        return pl.BlockSpec((N_HEADS, tm, 128), lambda i: (0, i, 0))

    return pl.pallas_call(
        _out_proj_kernel,
        grid=(m // tm,),
        in_specs=[mix_spec(mix_a, col_a),
                  mix_spec(mix_b, col_b),
                  pl.BlockSpec((WIDTH, d), lambda i: (0, 0)),
                  pl.BlockSpec((WIDTH, d), lambda i: (1, 0)),
                  pl.BlockSpec((tm, d), lambda i: (i, 0)),
                  pl.BlockSpec((1, d), lambda i: (0, 0)),
                  pl.BlockSpec((1, d), lambda i: (0, 0))],
        out_specs=[pl.BlockSpec((tm, d), lambda i: (i, 0)),
                   pl.BlockSpec((tm, d), lambda i: (i, 0))],
        out_shape=[jax.ShapeDtypeStruct((m, d), F32), jax.ShapeDtypeStruct((m, d), BF16)],
        compiler_params=pltpu.CompilerParams(dimension_semantics=("parallel",),
                                             vmem_limit_bytes=VMEM_LIMIT),
        name="out_proj",
    )(mix_a, mix_b, w_out, w_out, x, g_post, g_pre)


def _ffn_kernel(u_ref, wg_ref, wu_ref, wd_ref, x1_ref, gain_ref, o_ref, acc_ref):
    f = pl.program_id(1)
    u = u_ref[...]
    g = _dot(u, wg_ref[...])
    up = _dot(u, wu_ref[...])
    hidden = (g * _sigmoid(g) * up).astype(BF16)
    part = _dot(hidden, wd_ref[...])

    @pl.when(f == 0)
    def _():
        acc_ref[...] = part

    @pl.when(f > 0)
    def _():
        acc_ref[...] += part

    @pl.when(f == pl.num_programs(1) - 1)
    def _():
        y = acc_ref[...]
        o_ref[...] = x1_ref[...] + y * _rms_scale(y) * gain_ref[...]


def _ffn(u2, w_gate_up, w_down, x1, gain, *, tm, tf):
    m, d = x1.shape
    nf = D_FF // tf
    return pl.pallas_call(
        _ffn_kernel,
        grid=(m // tm, nf),
        in_specs=[pl.BlockSpec((tm, d), lambda i, f: (i, 0)),
                  pl.BlockSpec((d, tf), lambda i, f: (0, f)),
                  pl.BlockSpec((d, tf), lambda i, f: (0, nf + f)),
                  pl.BlockSpec((tf, d), lambda i, f: (f, 0)),
                  pl.BlockSpec((tm, d), lambda i, f: (i, 0)),
                  pl.BlockSpec((1, d), lambda i, f: (0, 0))],
        out_specs=pl.BlockSpec((tm, d), lambda i, f: (i, 0)),
        out_shape=jax.ShapeDtypeStruct((m, d), F32),
        scratch_shapes=[pltpu.VMEM((tm, d), F32)],
        compiler_params=pltpu.CompilerParams(dimension_semantics=("parallel", "arbitrary"),
                                             vmem_limit_bytes=VMEM_LIMIT),
        name="ffn",
    )(u2, w_gate_up, w_gate_up, w_down, x1, gain)


def _mlstm_chunk_kernel(q_ref, k_ref, v_ref, og_ref, gates_ref, gbias_ref, mnorm_ref, c0_ref, n0_ref, m0_ref,
                        h_ref, c_out_ref, n_out_ref, m_out_ref, c_s, n_s, m_s, *, chunk):
    L = chunk
    nb = q_ref.shape[1]
    shared = c0_ref.shape[0] == 1
    head = pl.program_id(0)
    c = pl.program_id(1)

    @pl.when(c == 0)
    def _():
        for b in range(nb):
            b0 = 0 if shared else b
            c_s[b] = c0_ref[b0, 0]
            n_s[b] = n0_ref[b0, 0]
            m_s[b] = m0_ref[b0, 0]

    lane = lax.broadcasted_iota(jnp.int32, (L, 128), 1)
    rows = lax.broadcasted_iota(jnp.int32, (L, L), 0)
    cols = lax.broadcasted_iota(jnp.int32, (L, L), 1)
    tril = rows >= cols
    triu = rows <= cols
    eye = rows == cols
    last = c == pl.num_programs(1) - 1

    for b in range(nb):
        gates = gates_ref[0, b] + gbias_ref[...]
        li_col = _select_lane(gates, lane, head + LANE_MI)
        lf_col = _select_lane(_log_sigmoid(gates), lane, head + LANE_MF)
        lf_row = jnp.sum(jnp.where(eye, lf_col, 0.0), axis=0, keepdims=True)
        li_row = jnp.sum(jnp.where(eye, li_col, 0.0), axis=0, keepdims=True)
        b_col = jnp.sum(jnp.where(tril, lf_row, 0.0), axis=1, keepdims=True)
        b_row = jnp.sum(jnp.where(triu, lf_col, 0.0), axis=0, keepdims=True)

        m_prev = m_s[b, :, 0:1]
        a_col = b_col + m_prev
        dm = jnp.where(tril, b_col - b_row + li_row, -jnp.inf)
        m_t = jnp.maximum(a_col, jnp.max(dm, axis=1, keepdims=True))
        inter = jnp.exp(a_col - m_t)
        wmat = jnp.exp(dm - m_t)

        q = q_ref[0, b] * Q_SCALE
        k = k_ref[0, b]
        qb = q.astype(BF16)
        kb = k.astype(BF16)
        vb = v_ref[0, b].astype(BF16)
        c_prev = c_s[b]
        n_prev = n_s[b]

        s = _dot_nt(qb, kb) * wmat
        num = _dot(s.astype(BF16), vb) + inter * _dot(qb, c_prev.astype(BF16))
        den = jnp.sum(s, axis=1, keepdims=True) + inter * jnp.sum(q * n_prev, axis=1, keepdims=True)
        hh = num / jnp.maximum(jnp.abs(den), jnp.exp(-m_t))
        out = hh * _rms_scale(hh) * mnorm_ref[0] * _sigmoid(og_ref[0, b])
        h_ref[0, b] = out.astype(h_ref.dtype)

        m_last = m_t[L - 1:L, :]
        b_last = b_col[L - 1:L, :]
        w_last = jnp.exp(b_last - b_col + li_col - m_last)
        d_c = jnp.exp(b_last + m_prev - m_last)
        kw = w_last * k
        c_s[b] = d_c * c_prev + _dot_tn(kw.astype(BF16), vb)
        n_s[b] = d_c * n_prev + jnp.sum(kw, axis=0, keepdims=True)
        m_s[b] = jnp.broadcast_to(m_last, (1, 128))

    @pl.when(last)
    def _():
        for b in range(nb):
            c_out_ref[b, 0] = c_s[b]
            n_out_ref[b, 0] = n_s[b]
            m_out_ref[b, 0] = m_s[b]


def _mlstm_chunked(proj, gbias, mnorm, c0, n0, m0, *, chunk):
    _, batch, seq, _ = proj.shape
    nb0 = c0.shape[0]

    def col(base):
        return pl.BlockSpec((1, batch, chunk, 128), lambda h, c: (base + h, 0, c, 0))

    def state(nb, shape):
        return pl.BlockSpec((nb, 1) + shape, lambda h, c: (0, h, 0, 0))

    return pl.pallas_call(
        functools.partial(_mlstm_chunk_kernel, chunk=chunk),
        grid=(N_HEADS, seq // chunk),
        in_specs=[col(COL_MQ), col(COL_MK), col(COL_MV), col(COL_MO),
                  pl.BlockSpec((1, batch, chunk, 128), lambda h, c: (COL_GATES, 0, c, 0)),
                  pl.BlockSpec((1, 128), lambda h, c: (0, 0)),
                  pl.BlockSpec((1, 1, 128), lambda h, c: (h, 0, 0)),
                  state(nb0, (128, 128)), state(nb0, (1, 128)), state(nb0, (1, 128))],
        out_specs=[pl.BlockSpec((1, batch, chunk, 128), lambda h, c: (h, 0, c, 0)),
                   state(batch, (128, 128)), state(batch, (1, 128)), state(batch, (1, 128))],
        out_shape=[jax.ShapeDtypeStruct((N_HEADS, batch, seq, 128), BF16),
                   jax.ShapeDtypeStruct((batch, N_HEADS, 128, 128), F32),
                   jax.ShapeDtypeStruct((batch, N_HEADS, 1, 128), F32),
                   jax.ShapeDtypeStruct((batch, N_HEADS, 1, 128), F32)],
        scratch_shapes=[pltpu.VMEM((batch, 128, 128), F32), pltpu.VMEM((batch, 1, 128), F32),
                        pltpu.VMEM((batch, 1, 128), F32)],
        compiler_params=pltpu.CompilerParams(dimension_semantics=("parallel", "arbitrary"),
                                             vmem_limit_bytes=VMEM_LIMIT),
        name="mlstm_chunk",
    )(proj, proj, proj, proj, proj, gbias, mnorm, c0, n0, m0)


def _split(x):
    hi = x.astype(BF16)
    return hi, (x - hi.astype(F32)).astype(BF16)


def _dot_split(a, b):
    a_hi, a_lo = a
    b_hi, b_lo = b
    if a_hi.shape[1] % 128 == 0:
        main = _dot(jnp.concatenate([a_hi, a_lo], axis=1), jnp.concatenate([b_hi, b_hi], axis=0))
    else:
        main = _dot(a_hi, b_hi) + _dot(a_lo, b_hi)
    return main + _dot(a_hi, b_lo)


def _unit_lower_inverse(mats, rows, cols, size):
    def same_block(shift):
        return (rows >> shift) == (cols >> shift)

    ident = jnp.where(rows == cols, 1.0, 0.0)
    d = [jnp.where(same_block(3), a, 0.0) for a in mats]
    d_s = [_split(x) for x in d]
    d2_s = [_split(_dot_split(x, x)) for x in d_s]
    d4_s = [_split(_dot_split(x, x)) for x in d2_s]
    t = [ident - x for x in d]
    t = [x + _dot_split(_split(x), y) for x, y in zip(t, d2_s)]
    t = [x + _dot_split(_split(x), y) for x, y in zip(t, d4_s)]
    shift = 3
    while (1 << shift) < size:
        off = jnp.logical_and(same_block(shift + 1), jnp.logical_not(same_block(shift)))
        e_s = [_split(jnp.where(off, a, 0.0)) for a in mats]
        t_s = [_split(x) for x in t]
        te_s = [_split(_dot_split(x, y)) for x, y in zip(t_s, e_s)]
        t = [x - _dot_split(y, z) for x, y, z in zip(t, te_s, t_s)]
        shift += 1
    return t


def _gdn_prep_kernel(q_ref, k_ref, v_ref, pq_ref, pk_ref, pv_ref, iq_ref, ik_ref, iv_ref, wq_ref, wk_ref, wv_ref,
                     gates_ref, gbias_ref, arow_ref,
                     u_ref, w_ref, qe_ref, kd_ref, attn_ref, glast_ref, ext, *, chunk, nsub):
    L = chunk
    R = nsub * L
    head = pl.program_id(1)
    first = pl.program_id(2) == 0

    def conv(x_ref, prev_ref, init_ref, cw_ref):
        prev = jnp.where(first, init_ref[0], prev_ref[0, 0])
        ext[0:8, :] = prev
        ext[8:8 + R, :] = x_ref[0, 0]
        base = 8 - (CONV_W - 1)
        acc = ext[base:base + R, :] * cw_ref[0:1, :]
        for i in range(1, CONV_W):
            acc = acc + ext[base + i:base + i + R, :] * cw_ref[i:i + 1, :]
        return acc * _sigmoid(acc)

    q = conv(q_ref, pq_ref, iq_ref, wq_ref)
    k = conv(k_ref, pk_ref, ik_ref, wk_ref)
    v_all = conv(v_ref, pv_ref, iv_ref, wv_ref)
    qn_all = q * _l2_scale(q) * Q_SCALE
    kn_all = k * _l2_scale(k)

    gates = gates_ref[0, 0] + gbias_ref[...]
    lane = lax.broadcasted_iota(jnp.int32, (R, 128), 1)
    lg_all = _select_lane(-jnp.exp(arow_ref[...]) * _softplus(gates), lane, head + LANE_GA)
    beta_all = _select_lane(_sigmoid(gates), lane, head + LANE_GB)

    U = min(R, 128)
    per_unit = U // L
    rows = lax.broadcasted_iota(jnp.int32, (U, U), 0)
    cols = lax.broadcasted_iota(jnp.int32, (U, U), 1)
    shift = L.bit_length() - 1
    same_chunk = (rows >> shift) == (cols >> shift)
    causal = jnp.logical_and(rows >= cols, same_chunk)
    strict = jnp.logical_and(rows > cols, same_chunk)
    upper = jnp.logical_and(rows <= cols, same_chunk)
    eye = rows == cols
    row_id = lax.broadcasted_iota(jnp.int32, (U, 1), 0)

    units = range(R // U)
    sls = [slice(j * U, (j + 1) * U) for j in units]
    g_cols, decays, a_mats, qks = [], [], [], []
    for sl in sls:
        lg_col = lg_all[sl]
        lg_row = jnp.sum(jnp.where(eye, lg_col, 0.0), axis=0, keepdims=True)
        g_col = jnp.sum(jnp.where(causal, lg_row, 0.0), axis=1, keepdims=True)
        g_row = jnp.sum(jnp.where(upper, lg_col, 0.0), axis=0, keepdims=True)
        decay = jnp.where(causal, jnp.exp(jnp.where(causal, g_col - g_row, 0.0)), 0.0)
        kb = kn_all[sl].astype(BF16)
        kk = _dot_nt(kb, kb)
        qks.append(_dot_nt(qn_all[sl].astype(BF16), kb))
        a_mats.append(jnp.where(strict, beta_all[sl] * kk * decay, 0.0))
        g_cols.append(g_col)
        decays.append(decay)

    t_mats = _unit_lower_inverse(a_mats, rows, cols, L)

    e_gs = [jnp.exp(g) for g in g_cols]
    rhs_s = [_split(jnp.concatenate([beta_all[sl] * v_all[sl], (beta_all[sl] * e_g) * kn_all[sl]], axis=1))
             for sl, e_g in zip(sls, e_gs)]
    uws = [_dot_split(_split(t), r) for t, r in zip(t_mats, rhs_s)]

    for j in units:
        sl, g_col = sls[j], g_cols[j]
        u_ref[0, 0, sl, :] = uws[j][:, 0:128]
        w_ref[0, 0, sl, :] = uws[j][:, 128:256].astype(BF16)
        qe_ref[0, 0, sl, :] = (qn_all[sl] * e_gs[j]).astype(BF16)
        attn = qks[j] * decays[j]
        g_last_col = jnp.zeros((U, 1), F32)
        for i in range(per_unit):
            g_last = g_col[(i + 1) * L - 1:(i + 1) * L, :]
            g_last_col = jnp.where((row_id >> shift) == i, g_last, g_last_col)
            glast_ref[0, 0, j * per_unit + i] = jnp.broadcast_to(jnp.exp(g_last), (1, 128))
            diag = attn[i * L:(i + 1) * L, :]
            if i > 0:
                diag = pltpu.roll(diag, U - i * L, axis=1)
            attn_ref[0, 0, j * per_unit + i] = diag[:, 0:L].astype(BF16)
        kd_ref[0, 0, sl, :] = (kn_all[sl] * jnp.exp(g_last_col - g_col)).astype(BF16)


def _gdn_prep(proj, conv_init, conv_w, gbias, arow, *, chunk, nsub):
    _, batch, seq, _ = proj.shape
    nc = seq // chunk
    R = nsub * chunk

    def col(base):
        return pl.BlockSpec((1, 1, R, 128), lambda b, h, c: (base + h, b, c, 0))

    def prev(base):
        return pl.BlockSpec((1, 1, 8, 128), lambda b, h, c: (base + h, b, jnp.maximum(c * (R // 8) - 1, 0), 0))

    def init(base):
        return pl.BlockSpec((1, 8, 128), lambda b, h, c: (base + h, 0, 0))

    def cw(base):
        return pl.BlockSpec((CONV_W, 128), lambda b, h, c: (0, base + h))

    def per_token(dtype):
        return (pl.BlockSpec((1, 1, R, 128), lambda b, h, c: (b, h, c, 0)),
                jax.ShapeDtypeStruct((batch, N_HEADS, seq, 128), dtype))

    outs = [per_token(F32), per_token(BF16), per_token(BF16), per_token(BF16),
            (pl.BlockSpec((1, 1, nsub, chunk, chunk), lambda b, h, c: (b, h, c, 0, 0)),
             jax.ShapeDtypeStruct((batch, N_HEADS, nc, chunk, chunk), BF16)),
            (pl.BlockSpec((1, 1, nsub, 1, 128), lambda b, h, c: (b, h, c, 0, 0)),
             jax.ShapeDtypeStruct((batch, N_HEADS, nc, 1, 128), F32))]
    return pl.pallas_call(
        functools.partial(_gdn_prep_kernel, chunk=chunk, nsub=nsub),
        grid=(batch, N_HEADS, nc // nsub),
        in_specs=[col(COL_GQ), col(COL_GK), col(COL_GV), prev(COL_GQ), prev(COL_GK), prev(COL_GV),
                  init(0), init(8), init(16), cw(0), cw(8), cw(16),
                  pl.BlockSpec((1, 1, R, 128), lambda b, h, c: (COL_GATES, b, c, 0)),
                  pl.BlockSpec((1, 128), lambda b, h, c: (0, 0)),
                  pl.BlockSpec((1, 128), lambda b, h, c: (0, 0))],
        out_specs=[o[0] for o in outs],
        out_shape=[o[1] for o in outs],
        scratch_shapes=[pltpu.VMEM((8 + R, 128), F32)],
        compiler_params=pltpu.CompilerParams(dimension_semantics=("parallel", "parallel", "parallel"),
                                             vmem_limit_bytes=VMEM_LIMIT),
        name="gdn_prep",
    )(proj, proj, proj, proj, proj, proj, conv_init, conv_init, conv_init, conv_w, conv_w, conv_w,
      proj, gbias, arow)


def _gdn_scan_kernel(u_ref, w_ref, qe_ref, kd_ref, attn_ref, glast_ref, gz_ref, gnorm_ref, s0_ref,
                     o_ref, s_out_ref, s_s):
    nb = u_ref.shape[0]
    shared = s0_ref.shape[0] == 1
    c = pl.program_id(1)
    last = c == pl.num_programs(1) - 1

    @pl.when(c == 0)
    def _():
        for b in range(nb):
            s_s[b] = s0_ref[0 if shared else b, 0]

    L = u_ref.shape[2]
    seqs = range(nb)
    s_prev = [s_s[b] for b in seqs]
    ws = [_dot(jnp.concatenate([w_ref[b, 0], qe_ref[b, 0]], axis=0), s_prev[b].astype(BF16)) for b in seqs]
    vb = [(u_ref[b, 0] - ws[b][0:L]).astype(BF16) for b in seqs]
    intra = [_dot(attn_ref[b, 0, 0], vb[b]) for b in seqs]
    upd = [_dot_tn(kd_ref[b, 0], vb[b]) for b in seqs]
    for b in seqs:
        s_s[b] = glast_ref[b, 0, 0][:, 0:1] * s_prev[b] + upd[b]
        o = ws[b][L:2 * L] + intra[b]
        gz = gz_ref[0, b]
        o_ref[0, b] = (o * _rms_scale(o) * gnorm_ref[0] * (gz * _sigmoid(gz))).astype(o_ref.dtype)

    @pl.when(last)
    def _():
        for b in range(nb):
            s_out_ref[b, 0] = s_s[b]


def _gdn_scan(prep, proj, gnorm, s0, *, chunk):
    _, batch, seq, _ = proj.shape
    u, w, qe, kd, attn, glast = prep
    tok = pl.BlockSpec((batch, 1, chunk, 128), lambda h, c: (0, h, c, 0))
    return pl.pallas_call(
        _gdn_scan_kernel,
        grid=(N_HEADS, seq // chunk),
        in_specs=[tok, tok, tok, tok,
                  pl.BlockSpec((batch, 1, 1, chunk, chunk), lambda h, c: (0, h, c, 0, 0)),
                  pl.BlockSpec((batch, 1, 1, 1, 128), lambda h, c: (0, h, c, 0, 0)),
                  pl.BlockSpec((1, batch, chunk, 128), lambda h, c: (COL_GZ + h, 0, c, 0)),
                  pl.BlockSpec((1, 1, 128), lambda h, c: (h, 0, 0)),
                  pl.BlockSpec((s0.shape[0], 1, 128, 128), lambda h, c: (0, h, 0, 0))],
        out_specs=[pl.BlockSpec((1, batch, chunk, 128), lambda h, c: (h, 0, c, 0)),
                   pl.BlockSpec((batch, 1, 128, 128), lambda h, c: (0, h, 0, 0))],
        out_shape=[jax.ShapeDtypeStruct((N_HEADS, batch, seq, 128), BF16),
                   jax.ShapeDtypeStruct((batch, N_HEADS, 128, 128), F32)],
        scratch_shapes=[pltpu.VMEM((batch, 128, 128), F32)],
        compiler_params=pltpu.CompilerParams(dimension_semantics=("parallel", "arbitrary"),
                                             vmem_limit_bytes=VMEM_LIMIT),
        name="gdn_scan",
    )(u, w, qe, kd, attn, glast, proj, gnorm, s0)


def _sample_kernel(p_ref, c_ref, n_ref, m_ref, s_ref, cb_ref, gbias_ref, arow_ref, cw_ref, mnorm_ref, gnorm_ref,
                   mix_ref, c_out_ref, n_out_ref, m_out_ref, s_out_ref, ext, xc_s, *, steps):
    T = steps
    NS = SAMPLE_SEQS
    R = NS * T
    H = N_HEADS
    nbuf = CONV_W - 1

    for s in range(NS):
        ext[8 * s:8 * s + nbuf, :] = cb_ref[0, s]
        for blk in range(COL_GZ - COL_GQ):
            ext[8 * s + nbuf:8 * s + nbuf + T, blk * 128:(blk + 1) * 128] = p_ref[COL_GQ + blk, 0, s * T:(s + 1) * T, :]
        acc = ext[8 * s:8 * s + T, :] * cw_ref[0:1, :]
        for i in range(1, CONV_W):
            acc = acc + ext[8 * s + i:8 * s + i + T, :] * cw_ref[i:i + 1, :]
        xc_s[s * T:(s + 1) * T, :] = acc * _sigmoid(acc)

    rowi = lax.broadcasted_iota(jnp.int32, (R, 1), 0)
    tpos = rowi & (T - 1)
    second = rowi >= T
    seq_rows = [jnp.logical_not(second), second]

    def pick(x, j):
        return jnp.where(second, x[T + j:T + j + 1, :], x[j:j + 1, :])

    def cumsum_t(x):
        acc = jnp.where(tpos >= 0, pick(x, 0), 0.0)
        for j in range(1, T):
            acc = acc + jnp.where(tpos >= j, pick(x, j), 0.0)
        return acc

    def pcol(block, h):
        return p_ref[block + h, 0]

    gates = p_ref[COL_GATES, 0] + gbias_ref[...]
    lane = lax.broadcasted_iota(jnp.int32, (R, 128), 1)
    lf_all = _log_sigmoid(gates)
    lg_all = -jnp.exp(arow_ref[...]) * _softplus(gates)
    beta_all = _sigmoid(gates)
    m_in = m_ref[0]
    lane_s = lax.broadcasted_iota(jnp.int32, (NS, 128), 1)
    row_s = lax.broadcasted_iota(jnp.int32, (NS, 128), 0)
    m_acc = jnp.zeros((NS, 128), F32)

    for h in range(H):
        li = _select_lane(gates, lane, LANE_MI + h)
        lf = _select_lane(lf_all, lane, LANE_MF + h)
        mh = _select_lane(m_in, lane_s, h)
        m0 = jnp.where(second, mh[1:2, :], mh[0:1, :])
        b = cumsum_t(lf)
        a = b + m0
        dcol = [b - pick(b, j) + pick(li, j) for j in range(T)]
        m_t = a
        for j in range(T):
            m_t = jnp.maximum(m_t, jnp.where(tpos >= j, dcol[j], -jnp.inf))
        inter = jnp.exp(a - m_t)

        q = pcol(COL_MQ, h) * Q_SCALE
        k = pcol(COL_MK, h)
        v = pcol(COL_MV, h)
        num = jnp.zeros((R, 128), F32)
        den = jnp.zeros((R, 1), F32)
        for j in range(T):
            w_j = jnp.where(tpos >= j, jnp.exp(jnp.where(tpos >= j, dcol[j] - m_t, 0.0)), 0.0)
            s_j = jnp.sum(q * pick(k, j), axis=1, keepdims=True) * w_j
            num = num + s_j * pick(v, j)
            den = den + s_j
        qb = q.astype(BF16)
        qc = jnp.where(second, _dot(qb, c_ref[0, 1, h].astype(BF16)), _dot(qb, c_ref[0, 0, h].astype(BF16)))
        n_sel = jnp.where(second, n_ref[0, 1, h:h + 1, :], n_ref[0, 0, h:h + 1, :])
        num = num + inter * qc
        den = den + inter * jnp.sum(q * n_sel, axis=1, keepdims=True)
        hh = num / jnp.maximum(jnp.abs(den), jnp.exp(-m_t))
        mix_ref[0, :, h * 128:(h + 1) * 128] = hh * _rms_scale(hh) * mnorm_ref[h:h + 1, :] * _sigmoid(pcol(COL_MO, h))

        m_l = pick(m_t, T - 1)
        b_l = pick(b, T - 1)
        kw = jnp.exp(b_l - b + li - m_l) * k
        d_c = jnp.exp(b_l + m0 - m_l)
        vb = v.astype(BF16)
        for s in range(NS):
            kws = jnp.where(seq_rows[s], kw, 0.0)
            dcs = d_c[s * T:s * T + 1, :]
            c_out_ref[0, s, h] = dcs * c_ref[0, s, h] + _dot_tn(kws.astype(BF16), vb)
            n_out_ref[0, s, h:h + 1, :] = dcs * n_ref[0, s, h:h + 1, :] + jnp.sum(kws, axis=0, keepdims=True)
            m_acc = jnp.where(jnp.logical_and(lane_s == h, row_s == s), m_l[s * T:s * T + 1, :], m_acc)

        lg = _select_lane(lg_all, lane, LANE_GA + h)
        beta = _select_lane(beta_all, lane, LANE_GB + h)
        g = cumsum_t(lg)
        e_g = jnp.exp(g)
        cq = xc_s[:, h * 128:(h + 1) * 128]
        ck = xc_s[:, WIDTH + h * 128:WIDTH + (h + 1) * 128]
        cv = xc_s[:, 2 * WIDTH + h * 128:2 * WIDTH + (h + 1) * 128]
        qn = cq * _l2_scale(cq) * Q_SCALE
        kn = ck * _l2_scale(ck)
        kq = jnp.concatenate([kn, qn], axis=0).astype(BF16)
        r0 = _dot(kq, s_ref[0, 0, h].astype(BF16))
        r1 = _dot(kq, s_ref[0, 1, h].astype(BF16))
        k_s = jnp.where(second, r1[0:R], r0[0:R])
        q_s = jnp.where(second, r1[R:2 * R], r0[R:2 * R])
        x = beta * (cv - e_g * k_s)
        rel = [jnp.exp(jnp.where(tpos >= j, g - pick(g, j), 0.0)) for j in range(T)]
        for j in range(T - 1):
            kk_j = jnp.sum(kn * pick(kn, j), axis=1, keepdims=True)
            a_j = jnp.where(tpos > j, beta * kk_j * rel[j], 0.0)
            x = x - a_j * pick(x, j)
        o = e_g * q_s
        for j in range(T):
            qk_j = jnp.sum(qn * pick(kn, j), axis=1, keepdims=True)
            o = o + jnp.where(tpos >= j, qk_j * rel[j], 0.0) * pick(x, j)
        gz = pcol(COL_GZ, h)
        mix_ref[0, :, WIDTH + h * 128:WIDTH + (h + 1) * 128] = (
            o * _rms_scale(o) * gnorm_ref[h:h + 1, :] * (gz * _sigmoid(gz)))

        g_l = pick(g, T - 1)
        kd = kn * jnp.exp(g_l - g)
        e_l = jnp.exp(g_l)
        xb = x.astype(BF16)
        for s in range(NS):
            kds = jnp.where(seq_rows[s], kd, 0.0)
            s_out_ref[0, s, h] = e_l[s * T:s * T + 1, :] * s_ref[0, s, h] + _dot_tn(kds.astype(BF16), xb)
    m_out_ref[0] = m_acc


def _sample_step(proj, c, n, m, s, conv_buf, gbias, arow, conv_w, mnorm, gnorm, *, steps):
    groups = c.shape[0]
    ns = SAMPLE_SEQS
    assert ns == 2 and ns * steps == 8 and steps >= CONV_W - 1 and steps & (steps - 1) == 0

    def per_group(*shape):
        return pl.BlockSpec((1,) + shape, lambda b: (b,) + (0,) * len(shape))

    def const(*shape):
        return pl.BlockSpec(shape, lambda b: (0,) * len(shape))

    state = (ns, N_HEADS, 128, 128)
    return pl.pallas_call(
        functools.partial(_sample_kernel, steps=steps),
        grid=(groups,),
        in_specs=[pl.BlockSpec((PROJ_COLS // 128, 1, ns * steps, 128), lambda b: (0, b, 0, 0)),
                  per_group(*state), per_group(ns, N_HEADS, 128),
                  per_group(ns, 128), per_group(*state), per_group(ns, CONV_W - 1, 3 * WIDTH),
                  const(1, 128), const(1, 128), const(CONV_W, 3 * WIDTH), const(N_HEADS, 128), const(N_HEADS, 128)],
        out_specs=[per_group(ns * steps, 2 * WIDTH), per_group(*state), per_group(ns, N_HEADS, 128),
                   per_group(ns, 128), per_group(*state)],
        out_shape=[jax.ShapeDtypeStruct((groups, ns * steps, 2 * WIDTH), F32),
                   jax.ShapeDtypeStruct((groups,) + state, F32),
                   jax.ShapeDtypeStruct((groups, ns, N_HEADS, 128), F32),
                   jax.ShapeDtypeStruct((groups, ns, 128), F32),
                   jax.ShapeDtypeStruct((groups,) + state, F32)],
        scratch_shapes=[pltpu.VMEM((8 * ns, 3 * WIDTH), F32), pltpu.VMEM((ns * steps, 3 * WIDTH), F32)],
        compiler_params=pltpu.CompilerParams(dimension_semantics=("parallel",),
                                             vmem_limit_bytes=VMEM_LIMIT),
        name="sample_step",
    )(proj, c, n, m, s, conv_buf, gbias, arow, conv_w, mnorm, gnorm)


def kernel(x_prompt, x_sample, state_mlstm_C, state_mlstm_n, state_mlstm_m, state_gdn_S, state_gdn_conv, meta_tokens, norm_pre_mix, norm_post_mix, norm_pre_ffn, norm_post_ffn, w_in, mlstm_b_i, mlstm_b_f, mlstm_norm, gdn_conv_w, gdn_A_log, gdn_dt_bias, gdn_norm, w_out, w_gate_up, w_down):
    assert w_in.shape[0] == 1, "single-layer trunk"
    B, S, D = x_prompt.shape
    Bs, Ts, _ = x_sample.shape
    H = N_HEADS

    w_proj = _regroup_weights(w_in[0])
    w_out_b = w_out[0].astype(BF16)
    w_gu_b = w_gate_up[0].astype(BF16)
    w_dn_b = w_down[0].astype(BF16)
    zeros_h = jnp.zeros((H,), F32)
    gbias = jnp.concatenate([mlstm_b_i[0], mlstm_b_f[0], gdn_dt_bias[0], zeros_h,
                             jnp.zeros((128 - 4 * H,), F32)]).reshape(1, 128)
    arow = jnp.concatenate([zeros_h, zeros_h, gdn_A_log[0], zeros_h,
                            jnp.zeros((128 - 4 * H,), F32)]).reshape(1, 128)
    g_pre_mix = norm_pre_mix[0].reshape(1, D)
    g_post_mix = norm_post_mix[0].reshape(1, D)
    g_pre_ffn = norm_pre_ffn[0].reshape(1, D)
    g_post_ffn = norm_post_ffn[0].reshape(1, D)
    mnorm = mlstm_norm[0]
    gnorm = gdn_norm[0]
    mnorm3 = mnorm.reshape(H, 1, 128)
    gnorm3 = gnorm.reshape(H, 1, 128)
    conv_w = gdn_conv_w[0]

    xp = x_prompt.reshape(B * S, D)
    xs = x_sample.reshape(Bs * Ts, D)

    nblk = PROJ_COLS // 128
    proj_p = _norm_matmul(xp, g_pre_mix, w_proj, tm=1024, tn=768).reshape(nblk, B, S, 128)
    n_sm = Bs * Ts + N_META
    proj_sm = _norm_matmul(jnp.concatenate([xs, meta_tokens], axis=0), g_pre_mix, w_proj, tm=n_sm, tn=768)
    proj_m = proj_sm[:, Bs * Ts:].reshape(nblk, 1, N_META, 128)

    def conv_rows(blocks):
        return jnp.moveaxis(blocks, 0, -2).reshape(blocks.shape[1:-1] + (3 * WIDTH,))

    c0, n0, m0 = (jnp.zeros((1, H, 128, 128), F32), jnp.zeros((1, H, 1, 128), F32), jnp.zeros((1, H, 1, 128), F32))
    _, c_l, n_l, m_l = _mlstm_chunked(proj_m, gbias, mnorm3, c0, n0, m0, chunk=N_META)
    prep_l = _gdn_prep(proj_m, jnp.zeros((3 * H, 8, 128), F32), conv_w, gbias, arow, chunk=N_META, nsub=1)
    _, s_l = _gdn_scan(prep_l, proj_m, gnorm3, c0, chunk=N_META)

    mix_a, p_c, p_n, p_m = _mlstm_chunked(proj_p, gbias, mnorm3, c_l, n_l, m_l, chunk=MLSTM_CHUNK)
    conv_init = proj_m[COL_GQ:COL_GZ, 0, N_META - 8:, :]
    prep_p = _gdn_prep(proj_p, conv_init, conv_w, gbias, arow, chunk=GDN_CHUNK, nsub=GDN_PREP_CHUNKS)
    mix_b, p_s = _gdn_scan(prep_p, proj_p, gnorm3, s_l, chunk=GDN_CHUNK)
    p_conv = conv_rows(proj_p[COL_GQ:COL_GZ, :, S - (CONV_W - 1):, :])

    x1_p, u2_p = _out_proj(mix_a.reshape(H, B * S, 128), 0, mix_b.reshape(H, B * S, 128), 0, w_out_b, xp,
                           g_post_mix, g_pre_ffn, tm=256)
    y_p = _ffn(u2_p, w_gu_b, w_dn_b, x1_p, g_post_ffn, tm=512, tf=512)

    ns = SAMPLE_SEQS
    grp = Bs // ns
    m_in = jnp.pad(state_mlstm_m[0], ((0, 0), (0, 128 - H))).reshape(grp, ns, 128)
    mix_s, s_c, s_n, s_m, s_s = _sample_step(
        proj_sm.reshape(nblk, n_sm // (ns * Ts), ns * Ts, 128), state_mlstm_C[0].reshape(grp, ns, H, 128, 128),
        state_mlstm_n[0].reshape(grp, ns, H, 128), m_in, state_gdn_S[0].reshape(grp, ns, H, 128, 128),
        state_gdn_conv[0].reshape(grp, ns, CONV_W - 1, 3 * WIDTH), gbias, arow, conv_w, mnorm, gnorm, steps=Ts)
    mix_s2 = mix_s.reshape(Bs * Ts, 2 * WIDTH)
    s_conv = conv_rows(proj_sm[COL_GQ:COL_GZ, :Bs * Ts].reshape(3 * H, Bs, Ts, 128)[:, :, Ts - (CONV_W - 1):, :])
    x1_s, u2_s = _out_proj(mix_s2, 0, mix_s2, 1, w_out_b, xs, g_post_mix, g_pre_ffn, tm=256)
    y_s = _ffn(u2_s, w_gu_b, w_dn_b, x1_s, g_post_ffn, tm=512, tf=512)

    return (y_p.reshape(B, S, D), y_s.reshape(Bs, Ts, D),
            p_c[None], p_n[:, :, 0, :][None], p_m[:, :, 0, 0][None], p_s[None], p_conv[None],
            s_c.reshape(Bs, H, 128, 128)[None], s_n.reshape(Bs, H, 128)[None], s_m.reshape(Bs, 128)[:, :H][None],
            s_s.reshape(Bs, H, 128, 128)[None], s_conv[None])
```

```python
import functools

import jax
import jax.numpy as jnp
from jax import lax
from jax.experimental import pallas as pl
from jax.experimental.pallas import tpu as pltpu

F32 = jnp.float32
BF16 = jnp.bfloat16

D_MODEL = 2048
HEAD_DIM = 128
N_HEADS = 8
WIDTH = N_HEADS * HEAD_DIM
CONV_W = 4
N_META = 16
D_FF = 5632
EPS = 1e-6
Q_SCALE = HEAD_DIM ** -0.5

COL_MQ, COL_MK, COL_MV, COL_MO = 0, 8, 16, 24
COL_GQ, COL_GK, COL_GV, COL_GZ = 32, 40, 48, 56
COL_GATES = 64
PROJ_COLS = 66 * 128
LANE_MI, LANE_MF, LANE_GA, LANE_GB = 0, 8, 16, 24

MLSTM_CHUNK = 256
GDN_CHUNK = 64
GDN_PREP_CHUNKS = 8
GDN_SCAN_HEADS = 8
SAMPLE_SEQS = 2
VMEM_LIMIT = 56 * 1024 * 1024


def _sigmoid(x):
    return 0.5 * jnp.tanh(0.5 * x) + 0.5


def _softplus(x):
    return jnp.maximum(x, 0.0) + jnp.log1p(jnp.exp(-jnp.abs(x)))


def _log_sigmoid(x):
    return -_softplus(-x)


def _rms_scale(x):
    return lax.rsqrt(jnp.mean(x * x, axis=-1, keepdims=True) + EPS)


def _l2_scale(x):
    return lax.rsqrt(jnp.sum(x * x, axis=-1, keepdims=True) + EPS)


def _dot(a, b):
    return jnp.dot(a, b, preferred_element_type=F32)


def _dot_nt(a, b):
    return lax.dot_general(a, b, (((1,), (1,)), ((), ())), preferred_element_type=F32)


def _dot_tn(a, b):
    return lax.dot_general(a, b, (((0,), (0,)), ((), ())), preferred_element_type=F32)


def _dot_f32(a, b):
    return jnp.dot(a, b, precision=lax.Precision.HIGHEST, preferred_element_type=F32)


def _select_lane(x, lane_idx, target):
    return jnp.sum(jnp.where(lane_idx == target, x, 0.0), axis=1, keepdims=True)


def _regroup_kernel(w_ref, o_ref):
    wide = 4 * WIDTH
    narrow = 2 * N_HEADS
    rows = w_ref.shape[0]
    o_ref[:, 0:wide] = w_ref[:, 0:wide].astype(BF16)
    o_ref[:, wide:2 * wide] = w_ref[:, wide + narrow:2 * wide + narrow].astype(BF16)
    gates = jnp.concatenate([w_ref[:, wide:wide + narrow], w_ref[:, 2 * wide + narrow:2 * (wide + narrow)],
                             jnp.zeros((rows, PROJ_COLS - 2 * wide - 2 * narrow), F32)], axis=1)
    o_ref[:, 2 * wide:PROJ_COLS] = gates.astype(BF16)


def _regroup_weights(w, *, tk=128):
    d, n = w.shape
    return pl.pallas_call(
        _regroup_kernel,
        grid=(d // tk,),
        in_specs=[pl.BlockSpec((tk, n), lambda i: (i, 0))],
        out_specs=pl.BlockSpec((tk, PROJ_COLS), lambda i: (i, 0)),
        out_shape=jax.ShapeDtypeStruct((d, PROJ_COLS), BF16),
        compiler_params=pltpu.CompilerParams(dimension_semantics=("parallel",), vmem_limit_bytes=VMEM_LIMIT),
        name="regroup_weights",
    )(w)


def _norm_matmul_kernel(x_ref, g_ref, w_ref, o_ref, u_ref):
    @pl.when(pl.program_id(1) == 0)
    def _():
        x = x_ref[...]
        u_ref[...] = (x * _rms_scale(x) * g_ref[...]).astype(BF16)

    res = _dot(u_ref[...], w_ref[...])
    for blk in range(o_ref.shape[0]):
        o_ref[blk] = res[:, blk * 128:(blk + 1) * 128]


def _norm_matmul(x, gain, w, *, tm, tn):
    m, d = x.shape
    n = w.shape[1]
    return pl.pallas_call(
        _norm_matmul_kernel,
        grid=(m // tm, n // tn),
        in_specs=[pl.BlockSpec((tm, d), lambda i, j: (i, 0)),
                  pl.BlockSpec((1, d), lambda i, j: (0, 0)),
                  pl.BlockSpec((d, tn), lambda i, j: (0, j))],
        out_specs=pl.BlockSpec((tn // 128, tm, 128), lambda i, j: (j, i, 0)),
        out_shape=jax.ShapeDtypeStruct((n // 128, m, 128), F32),
        scratch_shapes=[pltpu.VMEM((tm, d), BF16)],
        compiler_params=pltpu.CompilerParams(dimension_semantics=("parallel", "arbitrary"),
                                             vmem_limit_bytes=VMEM_LIMIT),
        name="norm_matmul",
    )(x, gain, w)


def _out_proj_kernel(ma_ref, mb_ref, wa_ref, wb_ref, x_ref, gpost_ref, gpre_ref, x1_ref, u2_ref):
    def rows(ref):
        if len(ref.shape) == 2:
            return ref[...].astype(BF16)
        return jnp.concatenate([ref[h] for h in range(ref.shape[0])], axis=1).astype(BF16)

    mo = _dot(rows(ma_ref), wa_ref[...]) + _dot(rows(mb_ref), wb_ref[...])
    x1 = x_ref[...] + mo * _rms_scale(mo) * gpost_ref[...]
    x1_ref[...] = x1
    u2_ref[...] = (x1 * _rms_scale(x1) * gpre_ref[...]).astype(BF16)


def _out_proj(mix_a, col_a, mix_b, col_b, w_out, x, g_post, g_pre, *, tm):
    m, d = x.shape

    def mix_spec(mix, col):
        if mix.ndim == 2:
            return pl.BlockSpec((tm, WIDTH), lambda i: (i, col))
        return pl.BlockSpec((N_HEADS, tm, 128), lambda i: (0, i, 0))

    return pl.pallas_call(
        _out_proj_kernel,
        grid=(m // tm,),
        in_specs=[mix_spec(mix_a, col_a),
                  mix_spec(mix_b, col_b),
                  pl.BlockSpec((WIDTH, d), lambda i: (0, 0)),
                  pl.BlockSpec((WIDTH, d), lambda i: (1, 0)),
                  pl.BlockSpec((tm, d), lambda i: (i, 0)),
                  pl.BlockSpec((1, d), lambda i: (0, 0)),
                  pl.BlockSpec((1, d), lambda i: (0, 0))],
        out_specs=[pl.BlockSpec((tm, d), lambda i: (i, 0)),
                   pl.BlockSpec((tm, d), lambda i: (i, 0))],
        out_shape=[jax.ShapeDtypeStruct((m, d), F32), jax.ShapeDtypeStruct((m, d), BF16)],
        compiler_params=pltpu.CompilerParams(dimension_semantics=("parallel",),
                                             vmem_limit_bytes=VMEM_LIMIT),
        name="out_proj",
    )(mix_a, mix_b, w_out, w_out, x, g_post, g_pre)


def _ffn_kernel(u_ref, wg_ref, wu_ref, wd_ref, x1_ref, gain_ref, o_ref, acc_ref):
    f = pl.program_id(1)
    u = u_ref[...]
    g = _dot(u, wg_ref[...])
    up = _dot(u, wu_ref[...])
    hidden = (g * _sigmoid(g) * up).astype(BF16)
    part = _dot(hidden, wd_ref[...])

    @pl.when(f == 0)
    def _():
        acc_ref[...] = part

    @pl.when(f > 0)
    def _():
        acc_ref[...] += part

    @pl.when(f == pl.num_programs(1) - 1)
    def _():
        y = acc_ref[...]
        o_ref[...] = x1_ref[...] + y * _rms_scale(y) * gain_ref[...]


def _ffn(u2, w_gate_up, w_down, x1, gain, *, tm, tf):
    m, d = x1.shape
    nf = D_FF // tf
    return pl.pallas_call(
        _ffn_kernel,
        grid=(m // tm, nf),
        in_specs=[pl.BlockSpec((tm, d), lambda i, f: (i, 0)),
                  pl.BlockSpec((d, tf), lambda i, f: (0, f)),
                  pl.BlockSpec((d, tf), lambda i, f: (0, nf + f)),
                  pl.BlockSpec((tf, d), lambda i, f: (f, 0)),
                  pl.BlockSpec((tm, d), lambda i, f: (i, 0)),
                  pl.BlockSpec((1, d), lambda i, f: (0, 0))],
        out_specs=pl.BlockSpec((tm, d), lambda i, f: (i, 0)),
        out_shape=jax.ShapeDtypeStruct((m, d), F32),
        scratch_shapes=[pltpu.VMEM((tm, d), F32)],
        compiler_params=pltpu.CompilerParams(dimension_semantics=("parallel", "arbitrary"),
                                             vmem_limit_bytes=VMEM_LIMIT),
        name="ffn",
    )(u2, w_gate_up, w_gate_up, w_down, x1, gain)


def _mlstm_chunk_kernel(q_ref, k_ref, v_ref, og_ref, gates_ref, gbias_ref, mnorm_ref, c0_ref, n0_ref, m0_ref,
                        h_ref, c_out_ref, n_out_ref, m_out_ref, c_s, n_s, m_s, *, chunk):
    L = chunk
    nb = q_ref.shape[1]
    shared = c0_ref.shape[0] == 1
    head = pl.program_id(0)
    c = pl.program_id(1)

    @pl.when(c == 0)
    def _():
        for b in range(nb):
            b0 = 0 if shared else b
            c_s[b] = c0_ref[b0, 0]
            n_s[b] = n0_ref[b0, 0]
            m_s[b] = m0_ref[b0, 0]

    lane = lax.broadcasted_iota(jnp.int32, (L, 128), 1)
    rows = lax.broadcasted_iota(jnp.int32, (L, L), 0)
    cols = lax.broadcasted_iota(jnp.int32, (L, L), 1)
    tril = rows >= cols
    triu = rows <= cols
    eye = rows == cols
    last = c == pl.num_programs(1) - 1

    for b in range(nb):
        gates = gates_ref[0, b] + gbias_ref[...]
        li_col = _select_lane(gates, lane, head + LANE_MI)
        lf_col = _select_lane(_log_sigmoid(gates), lane, head + LANE_MF)
        lf_row = jnp.sum(jnp.where(eye, lf_col, 0.0), axis=0, keepdims=True)
        li_row = jnp.sum(jnp.where(eye, li_col, 0.0), axis=0, keepdims=True)
        b_col = jnp.sum(jnp.where(tril, lf_row, 0.0), axis=1, keepdims=True)
        b_row = jnp.sum(jnp.where(triu, lf_col, 0.0), axis=0, keepdims=True)

        m_prev = m_s[b, :, 0:1]
        a_col = b_col + m_prev
        dm = jnp.where(tril, b_col - b_row + li_row, -jnp.inf)
        m_t = jnp.maximum(a_col, jnp.max(dm, axis=1, keepdims=True))
        inter = jnp.exp(a_col - m_t)
        wmat = jnp.exp(dm - m_t)

        q = q_ref[0, b] * Q_SCALE
        k = k_ref[0, b]
        qb = q.astype(BF16)
        kb = k.astype(BF16)
        vb = v_ref[0, b].astype(BF16)
        c_prev = c_s[b]
        n_prev = n_s[b]

        s = _dot_nt(qb, kb) * wmat
        num = _dot(s.astype(BF16), vb) + inter * _dot(qb, c_prev.astype(BF16))
        den = jnp.sum(s, axis=1, keepdims=True) + inter * jnp.sum(q * n_prev, axis=1, keepdims=True)
        hh = num / jnp.maximum(jnp.abs(den), jnp.exp(-m_t))
        out = hh * _rms_scale(hh) * mnorm_ref[0] * _sigmoid(og_ref[0, b])
        h_ref[0, b] = out.astype(h_ref.dtype)

        m_last = m_t[L - 1:L, :]
        b_last = b_col[L - 1:L, :]
        w_last = jnp.exp(b_last - b_col + li_col - m_last)
        d_c = jnp.exp(b_last + m_prev - m_last)
        kw = w_last * k
        c_s[b] = d_c * c_prev + _dot_tn(kw.astype(BF16), vb)
        n_s[b] = d_c * n_prev + jnp.sum(kw, axis=0, keepdims=True)
        m_s[b] = jnp.broadcast_to(m_last, (1, 128))

    @pl.when(last)
    def _():
        for b in range(nb):
            c_out_ref[b, 0] = c_s[b]
            n_out_ref[b, 0] = n_s[b]
            m_out_ref[b, 0] = m_s[b]


def _mlstm_chunked(proj, gbias, mnorm, c0, n0, m0, *, chunk):
    _, batch, seq, _ = proj.shape
    nb0 = c0.shape[0]

    def col(base):
        return pl.BlockSpec((1, batch, chunk, 128), lambda h, c: (base + h, 0, c, 0))

    def state(nb, shape):
        return pl.BlockSpec((nb, 1) + shape, lambda h, c: (0, h, 0, 0))

    return pl.pallas_call(
        functools.partial(_mlstm_chunk_kernel, chunk=chunk),
        grid=(N_HEADS, seq // chunk),
        in_specs=[col(COL_MQ), col(COL_MK), col(COL_MV), col(COL_MO),
                  pl.BlockSpec((1, batch, chunk, 128), lambda h, c: (COL_GATES, 0, c, 0)),
                  pl.BlockSpec((1, 128), lambda h, c: (0, 0)),
                  pl.BlockSpec((1, 1, 128), lambda h, c: (h, 0, 0)),
                  state(nb0, (128, 128)), state(nb0, (1, 128)), state(nb0, (1, 128))],
        out_specs=[pl.BlockSpec((1, batch, chunk, 128), lambda h, c: (h, 0, c, 0)),
                   state(batch, (128, 128)), state(batch, (1, 128)), state(batch, (1, 128))],
        out_shape=[jax.ShapeDtypeStruct((N_HEADS, batch, seq, 128), BF16),
                   jax.ShapeDtypeStruct((batch, N_HEADS, 128, 128), F32),
                   jax.ShapeDtypeStruct((batch, N_HEADS, 1, 128), F32),
                   jax.ShapeDtypeStruct((batch, N_HEADS, 1, 128), F32)],
        scratch_shapes=[pltpu.VMEM((batch, 128, 128), F32), pltpu.VMEM((batch, 1, 128), F32),
                        pltpu.VMEM((batch, 1, 128), F32)],
        compiler_params=pltpu.CompilerParams(dimension_semantics=("parallel", "arbitrary"),
                                             vmem_limit_bytes=VMEM_LIMIT),
        name="mlstm_chunk",
    )(proj, proj, proj, proj, proj, gbias, mnorm, c0, n0, m0)


def _split(x):
    hi = x.astype(BF16)
    return hi, (x - hi.astype(F32)).astype(BF16)


def _dot_split(a, b):
    a_hi, a_lo = a
    b_hi, b_lo = b
    if a_hi.shape[1] % 128 == 0:
        main = _dot(jnp.concatenate([a_hi, a_lo], axis=1), jnp.concatenate([b_hi, b_hi], axis=0))
    else:
        main = _dot(a_hi, b_hi) + _dot(a_lo, b_hi)
    return main + _dot(a_hi, b_lo)


def _unit_lower_inverse(mats, rows, cols, size):
    def same_block(shift):
        return (rows >> shift) == (cols >> shift)

    ident = jnp.where(rows == cols, 1.0, 0.0)
    d = [jnp.where(same_block(3), a, 0.0) for a in mats]
    d_s = [_split(x) for x in d]
    d2_s = [_split(_dot_split(x, x)) for x in d_s]
    d4_s = [_split(_dot_split(x, x)) for x in d2_s]
    t = [ident - x for x in d]
    t = [x + _dot_split(_split(x), y) for x, y in zip(t, d2_s)]
    t = [x + _dot_split(_split(x), y) for x, y in zip(t, d4_s)]
    shift = 3
    while (1 << shift) < size:
        off = jnp.logical_and(same_block(shift + 1), jnp.logical_not(same_block(shift)))
        e_s = [_split(jnp.where(off, a, 0.0)) for a in mats]
        t_s = [_split(x) for x in t]
        te_s = [_split(_dot_split(x, y)) for x, y in zip(t_s, e_s)]
        t = [x - _dot_split(y, z) for x, y, z in zip(t, te_s, t_s)]
        shift += 1
    return t


def _gdn_prep_kernel(q_ref, k_ref, v_ref, pq_ref, pk_ref, pv_ref, iq_ref, ik_ref, iv_ref, wq_ref, wk_ref, wv_ref,
                     gates_ref, gbias_ref, arow_ref,
                     u_ref, w_ref, qe_ref, kd_ref, attn_ref, glast_ref, ext, *, chunk, nsub):
    L = chunk
    R = nsub * L
    head = pl.program_id(1)
    first = pl.program_id(2) == 0

    def conv(x_ref, prev_ref, init_ref, cw_ref):
        prev = jnp.where(first, init_ref[0], prev_ref[0, 0])
        ext[0:8, :] = prev
        ext[8:8 + R, :] = x_ref[0, 0]
        base = 8 - (CONV_W - 1)
        acc = ext[base:base + R, :] * cw_ref[0:1, :]
        for i in range(1, CONV_W):
            acc = acc + ext[base + i:base + i + R, :] * cw_ref[i:i + 1, :]
        return acc * _sigmoid(acc)

    q = conv(q_ref, pq_ref, iq_ref, wq_ref)
    k = conv(k_ref, pk_ref, ik_ref, wk_ref)
    v_all = conv(v_ref, pv_ref, iv_ref, wv_ref)
    qn_all = q * _l2_scale(q) * Q_SCALE
    kn_all = k * _l2_scale(k)

    gates = gates_ref[0, 0] + gbias_ref[...]
    lane = lax.broadcasted_iota(jnp.int32, (R, 128), 1)
    lg_all = _select_lane(-jnp.exp(arow_ref[...]) * _softplus(gates), lane, head + LANE_GA)
    beta_all = _select_lane(_sigmoid(gates), lane, head + LANE_GB)

    U = min(R, 128)
    per_unit = U // L
    rows = lax.broadcasted_iota(jnp.int32, (U, U), 0)
    cols = lax.broadcasted_iota(jnp.int32, (U, U), 1)
    shift = L.bit_length() - 1
    same_chunk = (rows >> shift) == (cols >> shift)
    causal = jnp.logical_and(rows >= cols, same_chunk)
    strict = jnp.logical_and(rows > cols, same_chunk)
    upper = jnp.logical_and(rows <= cols, same_chunk)
    eye = rows == cols
    row_id = lax.broadcasted_iota(jnp.int32, (U, 1), 0)

    units = range(R // U)
    sls = [slice(j * U, (j + 1) * U) for j in units]
    g_cols, decays, a_mats, qks = [], [], [], []
    for sl in sls:
        lg_col = lg_all[sl]
        lg_row = jnp.sum(jnp.where(eye, lg_col, 0.0), axis=0, keepdims=True)
        g_col = jnp.sum(jnp.where(causal, lg_row, 0.0), axis=1, keepdims=True)
        g_row = jnp.sum(jnp.where(upper, lg_col, 0.0), axis=0, keepdims=True)
        decay = jnp.where(causal, jnp.exp(jnp.where(causal, g_col - g_row, 0.0)), 0.0)
        kb = kn_all[sl].astype(BF16)
        kk = _dot_nt(kb, kb)
        qks.append(_dot_nt(qn_all[sl].astype(BF16), kb))
        a_mats.append(jnp.where(strict, beta_all[sl] * kk * decay, 0.0))
        g_cols.append(g_col)
        decays.append(decay)

    t_mats = _unit_lower_inverse(a_mats, rows, cols, L)

    e_gs = [jnp.exp(g) for g in g_cols]
    rhs_s = [_split(jnp.concatenate([beta_all[sl] * v_all[sl], (beta_all[sl] * e_g) * kn_all[sl]], axis=1))
             for sl, e_g in zip(sls, e_gs)]
    uws = [_dot_split(_split(t), r) for t, r in zip(t_mats, rhs_s)]

    for j in units:
        sl, g_col = sls[j], g_cols[j]
        u_ref[0, 0, sl, :] = uws[j][:, 0:128]
        w_ref[0, 0, sl, :] = uws[j][:, 128:256].astype(BF16)
        qe_ref[0, 0, sl, :] = (qn_all[sl] * e_gs[j]).astype(BF16)
        attn = qks[j] * decays[j]
        g_last_col = jnp.zeros((U, 1), F32)
        for i in range(per_unit):
            g_last = g_col[(i + 1) * L - 1:(i + 1) * L, :]
            g_last_col = jnp.where((row_id >> shift) == i, g_last, g_last_col)
            glast_ref[0, 0, j * per_unit + i] = jnp.broadcast_to(jnp.exp(g_last), (1, 128))
            diag = attn[i * L:(i + 1) * L, :]
            if i > 0:
                diag = pltpu.roll(diag, U - i * L, axis=1)
            attn_ref[0, 0, j * per_unit + i] = diag[:, 0:L].astype(BF16)
        kd_ref[0, 0, sl, :] = (kn_all[sl] * jnp.exp(g_last_col - g_col)).astype(BF16)


def _gdn_prep(proj, conv_init, conv_w, gbias, arow, *, chunk, nsub):
    _, batch, seq, _ = proj.shape
    nc = seq // chunk
    R = nsub * chunk

    def col(base):
        return pl.BlockSpec((1, 1, R, 128), lambda b, h, c: (base + h, b, c, 0))

    def prev(base):
        return pl.BlockSpec((1, 1, 8, 128), lambda b, h, c: (base + h, b, jnp.maximum(c * (R // 8) - 1, 0), 0))

    def init(base):
        return pl.BlockSpec((1, 8, 128), lambda b, h, c: (base + h, 0, 0))

    def cw(base):
        return pl.BlockSpec((CONV_W, 128), lambda b, h, c: (0, base + h))

    def per_token(dtype):
        return (pl.BlockSpec((1, 1, R, 128), lambda b, h, c: (b, h, c, 0)),
                jax.ShapeDtypeStruct((batch, N_HEADS, seq, 128), dtype))

    outs = [per_token(F32), per_token(BF16), per_token(BF16), per_token(BF16),
            (pl.BlockSpec((1, 1, nsub, chunk, chunk), lambda b, h, c: (b, h, c, 0, 0)),
             jax.ShapeDtypeStruct((batch, N_HEADS, nc, chunk, chunk), BF16)),
            (pl.BlockSpec((1, 1, nsub, 1, 128), lambda b, h, c: (b, h, c, 0, 0)),
             jax.ShapeDtypeStruct((batch, N_HEADS, nc, 1, 128), F32))]
    return pl.pallas_call(
        functools.partial(_gdn_prep_kernel, chunk=chunk, nsub=nsub),
        grid=(batch, N_HEADS, nc // nsub),
        in_specs=[col(COL_GQ), col(COL_GK), col(COL_GV), prev(COL_GQ), prev(COL_GK), prev(COL_GV),
                  init(0), init(8), init(16), cw(0), cw(8), cw(16),
                  pl.BlockSpec((1, 1, R, 128), lambda b, h, c: (COL_GATES, b, c, 0)),
                  pl.BlockSpec((1, 128), lambda b, h, c: (0, 0)),
                  pl.BlockSpec((1, 128), lambda b, h, c: (0, 0))],
        out_specs=[o[0] for o in outs],
        out_shape=[o[1] for o in outs],
        scratch_shapes=[pltpu.VMEM((8 + R, 128), F32)],
        compiler_params=pltpu.CompilerParams(dimension_semantics=("parallel", "parallel", "parallel"),
                                             vmem_limit_bytes=VMEM_LIMIT),
        name="gdn_prep",
    )(proj, proj, proj, proj, proj, proj, conv_init, conv_init, conv_init, conv_w, conv_w, conv_w,
      proj, gbias, arow)


def _gdn_scan_kernel(u_ref, w_ref, qe_ref, kd_ref, attn_ref, glast_ref, gz_ref, gnorm_ref, s0_ref,
                     o_ref, s_out_ref, s_s):
    nb, nh = u_ref.shape[0], u_ref.shape[1]
    shared = s0_ref.shape[0] == 1
    c = pl.program_id(1)
    last = c == pl.num_programs(1) - 1
    chains = [(b, h) for b in range(nb) for h in range(nh)]

    @pl.when(c == 0)
    def _():
        for b, h in chains:
            s_s[b, h] = s0_ref[0 if shared else b, h]

    L = u_ref.shape[2]
    s_prev = [s_s[b, h] for b, h in chains]
    ws = [_dot(jnp.concatenate([w_ref[b, h], qe_ref[b, h]], axis=0), sp.astype(BF16))
          for (b, h), sp in zip(chains, s_prev)]
    vb = [(u_ref[b, h] - r[0:L]).astype(BF16) for (b, h), r in zip(chains, ws)]
    intra = [_dot(attn_ref[b, h, 0], v) for (b, h), v in zip(chains, vb)]
    upd = [_dot_tn(kd_ref[b, h], v) for (b, h), v in zip(chains, vb)]
    for i, (b, h) in enumerate(chains):
        s_s[b, h] = glast_ref[b, h, 0][:, 0:1] * s_prev[i] + upd[i]
        o = ws[i][L:2 * L] + intra[i]
        gz = gz_ref[h, b]
        o_ref[h, b] = (o * _rms_scale(o) * gnorm_ref[h] * (gz * _sigmoid(gz))).astype(o_ref.dtype)

    @pl.when(last)
    def _():
        for b, h in chains:
            s_out_ref[b, h] = s_s[b, h]


def _gdn_scan(prep, proj, gnorm, s0, *, chunk):
    _, batch, seq, _ = proj.shape
    u, w, qe, kd, attn, glast = prep
    hb = GDN_SCAN_HEADS
    assert COL_GZ % hb == 0 and N_HEADS % hb == 0
    tok = pl.BlockSpec((batch, hb, chunk, 128), lambda h, c: (0, h, c, 0))
    return pl.pallas_call(
        _gdn_scan_kernel,
        grid=(N_HEADS // hb, seq // chunk),
        in_specs=[tok, tok, tok, tok,
                  pl.BlockSpec((batch, hb, 1, chunk, chunk), lambda h, c: (0, h, c, 0, 0)),
                  pl.BlockSpec((batch, hb, 1, 1, 128), lambda h, c: (0, h, c, 0, 0)),
                  pl.BlockSpec((hb, batch, chunk, 128), lambda h, c: (COL_GZ // hb + h, 0, c, 0)),
                  pl.BlockSpec((hb, 1, 128), lambda h, c: (h, 0, 0)),
                  pl.BlockSpec((s0.shape[0], hb, 128, 128), lambda h, c: (0, h, 0, 0))],
        out_specs=[pl.BlockSpec((hb, batch, chunk, 128), lambda h, c: (h, 0, c, 0)),
                   pl.BlockSpec((batch, hb, 128, 128), lambda h, c: (0, h, 0, 0))],
        out_shape=[jax.ShapeDtypeStruct((N_HEADS, batch, seq, 128), BF16),
                   jax.ShapeDtypeStruct((batch, N_HEADS, 128, 128), F32)],
        scratch_shapes=[pltpu.VMEM((batch, hb, 128, 128), F32)],
        compiler_params=pltpu.CompilerParams(dimension_semantics=("parallel", "arbitrary"),
                                             vmem_limit_bytes=VMEM_LIMIT),
        name="gdn_scan",
    )(u, w, qe, kd, attn, glast, proj, gnorm, s0)


def _sample_kernel(p_ref, c_ref, n_ref, m_ref, s_ref, cb_ref, gbias_ref, arow_ref, cw_ref, mnorm_ref, gnorm_ref,
                   mix_ref, c_out_ref, n_out_ref, m_out_ref, s_out_ref, ext, xc_s, *, steps):
    T = steps
    NS = SAMPLE_SEQS
    R = NS * T
    H = N_HEADS
    nbuf = CONV_W - 1

    for s in range(NS):
        ext[8 * s:8 * s + nbuf, :] = cb_ref[0, s]
        for blk in range(COL_GZ - COL_GQ):
            ext[8 * s + nbuf:8 * s + nbuf + T, blk * 128:(blk + 1) * 128] = p_ref[COL_GQ + blk, 0, s * T:(s + 1) * T, :]
        acc = ext[8 * s:8 * s + T, :] * cw_ref[0:1, :]
        for i in range(1, CONV_W):
            acc = acc + ext[8 * s + i:8 * s + i + T, :] * cw_ref[i:i + 1, :]
        xc_s[s * T:(s + 1) * T, :] = acc * _sigmoid(acc)

    rowi = lax.broadcasted_iota(jnp.int32, (R, 1), 0)
    tpos = rowi & (T - 1)
    second = rowi >= T
    seq_rows = [jnp.logical_not(second), second]

    def pick(x, j):
        return jnp.where(second, x[T + j:T + j + 1, :], x[j:j + 1, :])

    def cumsum_t(x):
        acc = jnp.where(tpos >= 0, pick(x, 0), 0.0)
        for j in range(1, T):
            acc = acc + jnp.where(tpos >= j, pick(x, j), 0.0)
        return acc

    def pcol(block, h):
        return p_ref[block + h, 0]

    gates = p_ref[COL_GATES, 0] + gbias_ref[...]
    lane = lax.broadcasted_iota(jnp.int32, (R, 128), 1)
    lf_all = _log_sigmoid(gates)
    lg_all = -jnp.exp(arow_ref[...]) * _softplus(gates)
    beta_all = _sigmoid(gates)
    m_in = m_ref[0]
    lane_s = lax.broadcasted_iota(jnp.int32, (NS, 128), 1)
    row_s = lax.broadcasted_iota(jnp.int32, (NS, 128), 0)
    m_acc = jnp.zeros((NS, 128), F32)

    qc_all, gdn_in = [], []
    for h in range(H):
        qb = (pcol(COL_MQ, h) * Q_SCALE).astype(BF16)
        qc_all.append(jnp.where(second, _dot(qb, c_ref[0, 1, h].astype(BF16)),
                                _dot(qb, c_ref[0, 0, h].astype(BF16))))
        cq = xc_s[:, h * 128:(h + 1) * 128]
        ck = xc_s[:, WIDTH + h * 128:WIDTH + (h + 1) * 128]
        qn = cq * _l2_scale(cq) * Q_SCALE
        kn = ck * _l2_scale(ck)
        kq = jnp.concatenate([kn, qn], axis=0).astype(BF16)
        r0 = _dot(kq, s_ref[0, 0, h].astype(BF16))
        r1 = _dot(kq, s_ref[0, 1, h].astype(BF16))
        gdn_in.append((qn, kn, jnp.where(second, r1[0:R], r0[0:R]), jnp.where(second, r1[R:2 * R], r0[R:2 * R])))

    c_upd, s_upd = [], []
    for h in range(H):
        li = _select_lane(gates, lane, LANE_MI + h)
        lf = _select_lane(lf_all, lane, LANE_MF + h)
        mh = _select_lane(m_in, lane_s, h)
        m0 = jnp.where(second, mh[1:2, :], mh[0:1, :])
        b = cumsum_t(lf)
        a = b + m0
        dcol = [b - pick(b, j) + pick(li, j) for j in range(T)]
        m_t = a
        for j in range(T):
            m_t = jnp.maximum(m_t, jnp.where(tpos >= j, dcol[j], -jnp.inf))
        inter = jnp.exp(a - m_t)

        q = pcol(COL_MQ, h) * Q_SCALE
        k = pcol(COL_MK, h)
        v = pcol(COL_MV, h)
        num = jnp.zeros((R, 128), F32)
        den = jnp.zeros((R, 1), F32)
        for j in range(T):
            w_j = jnp.where(tpos >= j, jnp.exp(jnp.where(tpos >= j, dcol[j] - m_t, 0.0)), 0.0)
            s_j = jnp.sum(q * pick(k, j), axis=1, keepdims=True) * w_j
            num = num + s_j * pick(v, j)
            den = den + s_j
        qc = qc_all[h]
        n_sel =jnp.where(second, n_ref[0, 1, h:h + 1, :], n_ref[0, 0, h:h + 1, :])
        num = num + inter * qc
        den = den + inter * jnp.sum(q * n_sel, axis=1, keepdims=True)
        hh = num / jnp.maximum(jnp.abs(den), jnp.exp(-m_t))
        mix_ref[0, :, h * 128:(h + 1) * 128] = hh * _rms_scale(hh) * mnorm_ref[h:h + 1, :] * _sigmoid(pcol(COL_MO, h))

        m_l = pick(m_t, T - 1)
        b_l = pick(b, T - 1)
        kw = jnp.exp(b_l - b + li - m_l) * k
        d_c = jnp.exp(b_l + m0 - m_l)
        c_upd.append((kw, d_c, v.astype(BF16)))
        for s in range(NS):
            kws = jnp.where(seq_rows[s], kw, 0.0)
            n_out_ref[0, s, h:h + 1, :] = (d_c[s * T:s * T + 1, :] * n_ref[0, s, h:h + 1, :]
                                           + jnp.sum(kws, axis=0, keepdims=True))
            m_acc = jnp.where(jnp.logical_and(lane_s == h, row_s == s), m_l[s * T:s * T + 1, :], m_acc)

        lg = _select_lane(lg_all, lane, LANE_GA + h)
        beta = _select_lane(beta_all, lane, LANE_GB + h)
        g = cumsum_t(lg)
        e_g = jnp.exp(g)
        cv = xc_s[:, 2 * WIDTH + h * 128:2 * WIDTH + (h + 1) * 128]
        qn, kn, k_s, q_s = gdn_in[h]
        x = beta * (cv - e_g * k_s)
        rel = [jnp.exp(jnp.where(tpos >= j, g - pick(g, j), 0.0)) for j in range(T)]
        for j in range(T - 1):
            kk_j = jnp.sum(kn * pick(kn, j), axis=1, keepdims=True)
            a_j = jnp.where(tpos > j, beta * kk_j * rel[j], 0.0)
            x = x - a_j * pick(x, j)
        o = e_g * q_s
        for j in range(T):
            qk_j = jnp.sum(qn * pick(kn, j), axis=1, keepdims=True)
            o = o + jnp.where(tpos >= j, qk_j * rel[j], 0.0) * pick(x, j)
        gz = pcol(COL_GZ, h)
        mix_ref[0, :, WIDTH + h * 128:WIDTH + (h + 1) * 128] = (
            o * _rms_scale(o) * gnorm_ref[h:h + 1, :] * (gz * _sigmoid(gz)))

        g_l = pick(g, T - 1)
        s_upd.append((kn * jnp.exp(g_l - g), jnp.exp(g_l), x.astype(BF16)))
    m_out_ref[0] = m_acc

    for h in range(H):
        for (lhs, scale, rhs), st_ref, out_ref in ((c_upd[h], c_ref, c_out_ref), (s_upd[h], s_ref, s_out_ref)):
            for s in range(NS):
                rows_s = jnp.where(seq_rows[s], lhs, 0.0).astype(BF16)
                out_ref[0, s, h] = scale[s * T:s * T + 1, :] * st_ref[0, s, h] + _dot_tn(rows_s, rhs)


def _sample_step(proj, c, n, m, s, conv_buf, gbias, arow, conv_w, mnorm, gnorm, *, steps):
    groups = c.shape[0]
    ns = SAMPLE_SEQS
    assert ns == 2 and ns * steps == 8 and steps >= CONV_W - 1 and steps & (steps - 1) == 0

    def per_group(*shape):
        return pl.BlockSpec((1,) + shape, lambda b: (b,) + (0,) * len(shape))

    def const(*shape):
        return pl.BlockSpec(shape, lambda b: (0,) * len(shape))

    state = (ns, N_HEADS, 128, 128)
    return pl.pallas_call(
        functools.partial(_sample_kernel, steps=steps),
        grid=(groups,),
        in_specs=[pl.BlockSpec((PROJ_COLS // 128, 1, ns * steps, 128), lambda b: (0, b, 0, 0)),
                  per_group(*state), per_group(ns, N_HEADS, 128),
                  per_group(ns, 128), per_group(*state), per_group(ns, CONV_W - 1, 3 * WIDTH),
                  const(1, 128), const(1, 128), const(CONV_W, 3 * WIDTH), const(N_HEADS, 128), const(N_HEADS, 128)],
        out_specs=[per_group(ns * steps, 2 * WIDTH), per_group(*state), per_group(ns, N_HEADS, 128),
                   per_group(ns, 128), per_group(*state)],
        out_shape=[jax.ShapeDtypeStruct((groups, ns * steps, 2 * WIDTH), F32),
                   jax.ShapeDtypeStruct((groups,) + state, F32),
                   jax.ShapeDtypeStruct((groups, ns, N_HEADS, 128), F32),
                   jax.ShapeDtypeStruct((groups, ns, 128), F32),
                   jax.ShapeDtypeStruct((groups,) + state, F32)],
        scratch_shapes=[pltpu.VMEM((8 * ns, 3 * WIDTH), F32), pltpu.VMEM((ns * steps, 3 * WIDTH), F32)],
        compiler_params=pltpu.CompilerParams(dimension_semantics=("parallel",),
                                             vmem_limit_bytes=VMEM_LIMIT),
        name="sample_step",
    )(proj, c, n, m, s, conv_buf, gbias, arow, conv_w, mnorm, gnorm)


def kernel(x_prompt, x_sample, state_mlstm_C, state_mlstm_n, state_mlstm_m, state_gdn_S, state_gdn_conv, meta_tokens, norm_pre_mix, norm_post_mix, norm_pre_ffn, norm_post_ffn, w_in, mlstm_b_i, mlstm_b_f, mlstm_norm, gdn_conv_w, gdn_A_log, gdn_dt_bias, gdn_norm, w_out, w_gate_up, w_down):
    assert w_in.shape[0] == 1, "single-layer trunk"
    B, S, D = x_prompt.shape
    Bs, Ts, _ = x_sample.shape
    H = N_HEADS

    w_proj = _regroup_weights(w_in[0])
    w_out_b = w_out[0].astype(BF16)
    w_gu_b = w_gate_up[0].astype(BF16)
    w_dn_b = w_down[0].astype(BF16)
    zeros_h = jnp.zeros((H,), F32)
    gbias = jnp.concatenate([mlstm_b_i[0], mlstm_b_f[0], gdn_dt_bias[0], zeros_h,
                             jnp.zeros((128 - 4 * H,), F32)]).reshape(1, 128)
    arow = jnp.concatenate([zeros_h, zeros_h, gdn_A_log[0], zeros_h,
                            jnp.zeros((128 - 4 * H,), F32)]).reshape(1, 128)
    g_pre_mix = norm_pre_mix[0].reshape(1, D)
    g_post_mix = norm_post_mix[0].reshape(1, D)
    g_pre_ffn = norm_pre_ffn[0].reshape(1, D)
    g_post_ffn = norm_post_ffn[0].reshape(1, D)
    mnorm = mlstm_norm[0]
    gnorm = gdn_norm[0]
    mnorm3 = mnorm.reshape(H, 1, 128)
    gnorm3 = gnorm.reshape(H, 1, 128)
    conv_w = gdn_conv_w[0]

    xp = x_prompt.reshape(B * S, D)
    xs = x_sample.reshape(Bs * Ts, D)

    nblk = PROJ_COLS // 128
    proj_p = _norm_matmul(xp, g_pre_mix, w_proj, tm=1024, tn=768).reshape(nblk, B, S, 128)
    n_sm = Bs * Ts + N_META
    proj_sm = _norm_matmul(jnp.concatenate([xs, meta_tokens], axis=0), g_pre_mix, w_proj, tm=n_sm, tn=768)
    proj_m = proj_sm[:, Bs * Ts:].reshape(nblk, 1, N_META, 128)

    def conv_rows(blocks):
        return jnp.moveaxis(blocks, 0, -2).reshape(blocks.shape[1:-1] + (3 * WIDTH,))

    c0, n0, m0 = (jnp.zeros((1, H, 128, 128), F32), jnp.zeros((1, H, 1, 128), F32), jnp.zeros((1, H, 1, 128), F32))
    _, c_l, n_l, m_l = _mlstm_chunked(proj_m, gbias, mnorm3, c0, n0, m0, chunk=N_META)
    prep_l = _gdn_prep(proj_m, jnp.zeros((3 * H, 8, 128), F32), conv_w, gbias, arow, chunk=N_META, nsub=1)
    _, s_l = _gdn_scan(prep_l, proj_m, gnorm3, c0, chunk=N_META)

    mix_a, p_c, p_n, p_m = _mlstm_chunked(proj_p, gbias, mnorm3, c_l, n_l, m_l, chunk=MLSTM_CHUNK)
    conv_init = proj_m[COL_GQ:COL_GZ, 0, N_META - 8:, :]
    prep_p = _gdn_prep(proj_p, conv_init, conv_w, gbias, arow, chunk=GDN_CHUNK, nsub=GDN_PREP_CHUNKS)
    mix_b, p_s = _gdn_scan(prep_p, proj_p, gnorm3, s_l, chunk=GDN_CHUNK)
    p_conv = conv_rows(proj_p[COL_GQ:COL_GZ, :, S - (CONV_W - 1):, :])

    x1_p, u2_p = _out_proj(mix_a.reshape(H, B * S, 128), 0, mix_b.reshape(H, B * S, 128), 0, w_out_b, xp,
                           g_post_mix, g_pre_ffn, tm=256)
    y_p = _ffn(u2_p, w_gu_b, w_dn_b, x1_p, g_post_ffn, tm=512, tf=512)

    ns = SAMPLE_SEQS
    grp = Bs // ns
    m_in = jnp.pad(state_mlstm_m[0], ((0, 0), (0, 128 - H))).reshape(grp, ns, 128)
    mix_s, s_c, s_n, s_m, s_s = _sample_step(
        proj_sm.reshape(nblk, n_sm // (ns * Ts), ns * Ts, 128), state_mlstm_C[0].reshape(grp, ns, H, 128, 128),
        state_mlstm_n[0].reshape(grp, ns, H, 128), m_in, state_gdn_S[0].reshape(grp, ns, H, 128, 128),
        state_gdn_conv[0].reshape(grp, ns, CONV_W - 1, 3 * WIDTH), gbias, arow, conv_w, mnorm, gnorm, steps=Ts)
    mix_s2 = mix_s.reshape(Bs * Ts, 2 * WIDTH)
    s_conv = conv_rows(proj_sm[COL_GQ:COL_GZ, :Bs * Ts].reshape(3 * H, Bs, Ts, 128)[:, :, Ts - (CONV_W - 1):, :])
    x1_s, u2_s = _out_proj(mix_s2, 0, mix_s2, 1, w_out_b, xs, g_post_mix, g_pre_ffn, tm=256)
    y_s = _ffn(u2_s, w_gu_b, w_dn_b, x1_s, g_post_ffn, tm=512, tf=512)

    return (y_p.reshape(B, S, D), y_s.reshape(Bs, Ts, D),
            p_c[None], p_n[:, :, 0, :][None], p_m[:, :, 0, 0][None], p_s[None], p_conv[None],
            s_c.reshape(Bs, H, 128, 128)[None], s_n.reshape(Bs, H, 128)[None], s_m.reshape(Bs, 128)[:, :H][None],
            s_s.reshape(Bs, H, 128, 128)[None], s_conv[None])
```

```python
import functools

import jax
import jax.numpy as jnp
from jax import lax
from jax.experimental import pallas as pl
from jax.experimental.pallas import tpu as pltpu

F32 = jnp.float32
BF16 = jnp.bfloat16

D_MODEL = 2048
HEAD_DIM = 128
N_HEADS = 8
WIDTH = N_HEADS * HEAD_DIM
CONV_W = 4
N_META = 16
D_FF = 5632
EPS = 1e-6
Q_SCALE = HEAD_DIM ** -0.5

COL_MQ, COL_MK, COL_MV, COL_MO = 0, 8, 16, 24
COL_GQ, COL_GK, COL_GV, COL_GZ = 32, 40, 48, 56
COL_GATES = 64
PROJ_COLS = 66 * 128
LANE_MI, LANE_MF, LANE_GA, LANE_GB = 0, 8, 16, 24

MLSTM_CHUNK = 256
GDN_CHUNK = 64
GDN_PREP_CHUNKS = 16
GDN_SCAN_HEADS = 8
SAMPLE_SEQS = 2
VMEM_LIMIT = 56 * 1024 * 1024


def _sigmoid(x):
    return 0.5 * jnp.tanh(0.5 * x) + 0.5


def _softplus(x):
    return jnp.maximum(x, 0.0) + jnp.log1p(jnp.exp(-jnp.abs(x)))


def _log_sigmoid(x):
    return -_softplus(-x)


def _rms_scale(x):
    return lax.rsqrt(jnp.mean(x * x, axis=-1, keepdims=True) + EPS)


def _l2_scale(x):
    return lax.rsqrt(jnp.sum(x * x, axis=-1, keepdims=True) + EPS)


def _dot(a, b):
    return jnp.dot(a, b, preferred_element_type=F32)


def _dot_nt(a, b):
    return lax.dot_general(a, b, (((1,), (1,)), ((), ())), preferred_element_type=F32)


def _dot_tn(a, b):
    return lax.dot_general(a, b, (((0,), (0,)), ((), ())), preferred_element_type=F32)


def _dot_f32(a, b):
    return jnp.dot(a, b, precision=lax.Precision.HIGHEST, preferred_element_type=F32)


def _select_lane(x, lane_idx, target):
    return jnp.sum(jnp.where(lane_idx == target, x, 0.0), axis=1, keepdims=True)


def _regroup_kernel(w_ref, o_ref):
    wide = 4 * WIDTH
    narrow = 2 * N_HEADS
    rows = w_ref.shape[0]
    o_ref[:, 0:wide] = w_ref[:, 0:wide].astype(BF16)
    o_ref[:, wide:2 * wide] = w_ref[:, wide + narrow:2 * wide + narrow].astype(BF16)
    gates = jnp.concatenate([w_ref[:, wide:wide + narrow], w_ref[:, 2 * wide + narrow:2 * (wide + narrow)],
                             jnp.zeros((rows, PROJ_COLS - 2 * wide - 2 * narrow), F32)], axis=1)
    o_ref[:, 2 * wide:PROJ_COLS] = gates.astype(BF16)


def _regroup_weights(w, *, tk=128):
    d, n = w.shape
    return pl.pallas_call(
        _regroup_kernel,
        grid=(d // tk,),
        in_specs=[pl.BlockSpec((tk, n), lambda i: (i, 0))],
        out_specs=pl.BlockSpec((tk, PROJ_COLS), lambda i: (i, 0)),
        out_shape=jax.ShapeDtypeStruct((d, PROJ_COLS), BF16),
        compiler_params=pltpu.CompilerParams(dimension_semantics=("parallel",), vmem_limit_bytes=VMEM_LIMIT),
        name="regroup_weights",
    )(w)


def _norm_matmul_kernel(x_ref, g_ref, w_ref, o_ref, u_ref):
    @pl.when(pl.program_id(1) == 0)
    def _():
        x = x_ref[...]
        u_ref[...] = (x * _rms_scale(x) * g_ref[...]).astype(BF16)

    res = _dot(u_ref[...], w_ref[...])
    for blk in range(o_ref.shape[0]):
        o_ref[blk] = res[:, blk * 128:(blk + 1) * 128]


def _norm_matmul(x, gain, w, *, tm, tn):
    m, d = x.shape
    n = w.shape[1]
    return pl.pallas_call(
        _norm_matmul_kernel,
        grid=(m // tm, n // tn),
        in_specs=[pl.BlockSpec((tm, d), lambda i, j: (i, 0)),
                  pl.BlockSpec((1, d), lambda i, j: (0, 0)),
                  pl.BlockSpec((d, tn), lambda i, j: (0, j))],
        out_specs=pl.BlockSpec((tn // 128, tm, 128), lambda i, j: (j, i, 0)),
        out_shape=jax.ShapeDtypeStruct((n // 128, m, 128), F32),
        scratch_shapes=[pltpu.VMEM((tm, d), BF16)],
        compiler_params=pltpu.CompilerParams(dimension_semantics=("parallel", "arbitrary"),
                                             vmem_limit_bytes=VMEM_LIMIT),
        name="norm_matmul",
    )(x, gain, w)


def _out_proj_kernel(ma_ref, mb_ref, wa_ref, wb_ref, x_ref, gpost_ref, gpre_ref, x1_ref, u2_ref):
    def rows(ref):
        if len(ref.shape) == 2:
            return ref[...].astype(BF16)
        return jnp.concatenate([ref[h] for h in range(ref.shape[0])], axis=1).astype(BF16)

    mo = _dot(rows(ma_ref), wa_ref[...]) + _dot(rows(mb_ref), wb_ref[...])
    x1 = x_ref[...] + mo * _rms_scale(mo) * gpost_ref[...]
    x1_ref[...] = x1
    u2_ref[...] = (x1 * _rms_scale(x1) * gpre_ref[...]).astype(BF16)


def _out_proj(mix_a, col_a, mix_b, col_b, w_out, x, g_post, g_pre, *, tm):
    m, d = x.shape

    def mix_spec(mix, col):
        if mix.ndim == 2:
            return pl.BlockSpec((tm, WIDTH), lambda i: (i, col))
        return pl.BlockSpec((N_HEADS, tm, 128), lambda i: (0, i, 0))

    return pl.pallas_call(
        _out_proj_kernel,
        grid=(m // tm,),
        in_specs=[mix_spec(mix_a, col_a),
                  mix_spec(mix_b, col_b),
                  pl.BlockSpec((WIDTH, d), lambda i: (0, 0)),
                  pl.BlockSpec((WIDTH, d), lambda i: (1, 0)),
                  pl.BlockSpec((tm, d), lambda i: (i, 0)),
                  pl.BlockSpec((1, d), lambda i: (0, 0)),
                  pl.BlockSpec((1, d), lambda i: (0, 0))],
        out_specs=[pl.BlockSpec((tm, d), lambda i: (i, 0)),
                   pl.BlockSpec((tm, d), lambda i: (i, 0))],
        out_shape=[jax.ShapeDtypeStruct((m, d), F32), jax.ShapeDtypeStruct((m, d), BF16)],
        compiler_params=pltpu.CompilerParams(dimension_semantics=("parallel",),
                                             vmem_limit_bytes=VMEM_LIMIT),
        name="out_proj",
    )(mix_a, mix_b, w_out, w_out, x, g_post, g_pre)


def _ffn_kernel(u_ref, wg_ref, wu_ref, wd_ref, x1_ref, gain_ref, o_ref, acc_ref):
    f = pl.program_id(1)
    u = u_ref[...]
    g = _dot(u, wg_ref[...])
    up = _dot(u, wu_ref[...])
    hidden = (g * _sigmoid(g) * up).astype(BF16)
    part = _dot(hidden, wd_ref[...])

    @pl.when(f == 0)
    def _():
        acc_ref[...] = part

    @pl.when(f > 0)
    def _():
        acc_ref[...] += part

    @pl.when(f == pl.num_programs(1) - 1)
    def _():
        y = acc_ref[...]
        o_ref[...] = x1_ref[...] + y * _rms_scale(y) * gain_ref[...]


def _ffn(u2, w_gate_up, w_down, x1, gain, *, tm, tf):
    m, d = x1.shape
    nf = D_FF // tf
    return pl.pallas_call(
        _ffn_kernel,
        grid=(m // tm, nf),
        in_specs=[pl.BlockSpec((tm, d), lambda i, f: (i, 0)),
                  pl.BlockSpec((d, tf), lambda i, f: (0, f)),
                  pl.BlockSpec((d, tf), lambda i, f: (0, nf + f)),
                  pl.BlockSpec((tf, d), lambda i, f: (f, 0)),
                  pl.BlockSpec((tm, d), lambda i, f: (i, 0)),
                  pl.BlockSpec((1, d), lambda i, f: (0, 0))],
        out_specs=pl.BlockSpec((tm, d), lambda i, f: (i, 0)),
        out_shape=jax.ShapeDtypeStruct((m, d), F32),
        scratch_shapes=[pltpu.VMEM((tm, d), F32)],
        compiler_params=pltpu.CompilerParams(dimension_semantics=("parallel", "arbitrary"),
                                             vmem_limit_bytes=VMEM_LIMIT),
        name="ffn",
    )(u2, w_gate_up, w_gate_up, w_down, x1, gain)


def _gate_act_kernel(g_ref, gbias_ref, arow_ref, o_ref):
    g = g_ref[0] + gbias_ref[...]
    lane = lax.broadcasted_iota(jnp.int32, g.shape, 1)
    decay = -jnp.exp(arow_ref[...]) * _softplus(g)
    o_ref[...] = jnp.where(lane < LANE_MF, g,
                           jnp.where(lane < LANE_GA, _log_sigmoid(g), jnp.where(lane < LANE_GB, decay, _sigmoid(g))))


def _gate_act(proj, gbias, arow, *, tm):
    m = proj.shape[1]
    return pl.pallas_call(
        _gate_act_kernel,
        grid=(m // tm,),
        in_specs=[pl.BlockSpec((1, tm, 128), lambda i: (COL_GATES, i, 0)),
                  pl.BlockSpec((1, 128), lambda i: (0, 0)),
                  pl.BlockSpec((1, 128), lambda i: (0, 0))],
        out_specs=pl.BlockSpec((tm, 128), lambda i: (i, 0)),
        out_shape=jax.ShapeDtypeStruct((m, 128), F32),
        compiler_params=pltpu.CompilerParams(dimension_semantics=("parallel",), vmem_limit_bytes=VMEM_LIMIT),
        name="gate_act",
    )(proj, gbias, arow)


def _mlstm_chunk_kernel(q_ref, k_ref, v_ref, og_ref, gates_ref, mnorm_ref, c0_ref, n0_ref, m0_ref,
                        h_ref, c_out_ref, n_out_ref, m_out_ref, c_s, n_s, m_s, *, chunk):
    L = chunk
    nb = q_ref.shape[1]
    shared = c0_ref.shape[0] == 1
    head = pl.program_id(0)
    c = pl.program_id(1)

    @pl.when(c == 0)
    def _():
        for b in range(nb):
            b0 = 0 if shared else b
            c_s[b] = c0_ref[b0, 0]
            n_s[b] = n0_ref[b0, 0]
            m_s[b] = m0_ref[b0, 0]

    lane = lax.broadcasted_iota(jnp.int32, (L, 128), 1)
    rows = lax.broadcasted_iota(jnp.int32, (L, L), 0)
    cols = lax.broadcasted_iota(jnp.int32, (L, L), 1)
    tril = rows >= cols
    triu = rows <= cols
    eye = rows == cols
    last = c == pl.num_programs(1) - 1

    seqs = range(nb)
    gates = [gates_ref[b] for b in seqs]
    li_col = [_select_lane(g, lane, head + LANE_MI) for g in gates]
    lf_col = [_select_lane(g, lane, head + LANE_MF) for g in gates]
    lf_row = [jnp.sum(jnp.where(eye, x, 0.0), axis=0, keepdims=True) for x in lf_col]
    li_row = [jnp.sum(jnp.where(eye, x, 0.0), axis=0, keepdims=True) for x in li_col]
    b_col = [jnp.sum(jnp.where(tril, x, 0.0), axis=1, keepdims=True) for x in lf_row]
    b_row = [jnp.sum(jnp.where(triu, x, 0.0), axis=0, keepdims=True) for x in lf_col]

    qb = [(q_ref[0, b] * Q_SCALE).astype(BF16) for b in seqs]
    kb = [k_ref[0, b].astype(BF16) for b in seqs]
    vb = [v_ref[0, b].astype(BF16) for b in seqs]
    qk = [_dot_nt(x, y) for x, y in zip(qb, kb)]
    qc = [_dot(x, c_s[b].astype(BF16)) for b, x in zip(seqs, qb)]

    m_prev = [m_s[b, :, 0:1] for b in seqs]
    a_col = [x + y for x, y in zip(b_col, m_prev)]
    dm = [jnp.where(tril, bc - br + lr, -jnp.inf) for bc, br, lr in zip(b_col, b_row, li_row)]
    m_t = [jnp.maximum(a, jnp.max(d, axis=1, keepdims=True)) for a, d in zip(a_col, dm)]
    inter = [jnp.exp(a - m) for a, m in zip(a_col, m_t)]
    s = [x * jnp.exp(d - m) for x, d, m in zip(qk, dm, m_t)]
    sv = [_dot(x.astype(BF16), v) for x, v in zip(s, vb)]

    m_last = [m[L - 1:L, :] for m in m_t]
    b_last = [x[L - 1:L, :] for x in b_col]
    d_c = [jnp.exp(bl + mp - ml) for bl, mp, ml in zip(b_last, m_prev, m_last)]
    kw = [jnp.exp(bl - bc + li - ml) * k_ref[0, b]
          for b, bl, bc, li, ml in zip(seqs, b_last, b_col, li_col, m_last)]
    c_upd = [_dot_tn(x.astype(BF16), v) for x, v in zip(kw, vb)]

    for b in seqs:
        q = q_ref[0, b] * Q_SCALE
        num = sv[b] + inter[b] * qc[b]
        den = jnp.sum(s[b], axis=1, keepdims=True) + inter[b] * jnp.sum(q * n_s[b], axis=1, keepdims=True)
        hh = num / jnp.maximum(jnp.abs(den), jnp.exp(-m_t[b]))
        out = hh * _rms_scale(hh) * mnorm_ref[0] * _sigmoid(og_ref[0, b])
        h_ref[0, b] = out.astype(h_ref.dtype)
        c_s[b] = d_c[b] * c_s[b] + c_upd[b]
        n_s[b] = d_c[b] * n_s[b] + jnp.sum(kw[b], axis=0, keepdims=True)
        m_s[b] = jnp.broadcast_to(m_last[b], (1, 128))

    @pl.when(last)
    def _():
        for b in range(nb):
            c_out_ref[b, 0] = c_s[b]
            n_out_ref[b, 0] = n_s[b]
            m_out_ref[b, 0] = m_s[b]


def _mlstm_chunked(proj, gates, mnorm, c0, n0, m0, *, chunk):
    _, batch, seq, _ = proj.shape
    nb0 = c0.shape[0]

    def col(base):
        return pl.BlockSpec((1, batch, chunk, 128), lambda h, c: (base + h, 0, c, 0))

    def state(nb, shape):
        return pl.BlockSpec((nb, 1) + shape, lambda h, c: (0, h, 0, 0))

    return pl.pallas_call(
        functools.partial(_mlstm_chunk_kernel, chunk=chunk),
        grid=(N_HEADS, seq // chunk),
        in_specs=[col(COL_MQ), col(COL_MK), col(COL_MV), col(COL_MO),
                  pl.BlockSpec((batch, chunk, 128), lambda h, c: (0, c, 0)),
                  pl.BlockSpec((1, 1, 128), lambda h, c: (h, 0, 0)),
                  state(nb0, (128, 128)), state(nb0, (1, 128)), state(nb0, (1, 128))],
        out_specs=[pl.BlockSpec((1, batch, chunk, 128), lambda h, c: (h, 0, c, 0)),
                   state(batch, (128, 128)), state(batch, (1, 128)), state(batch, (1, 128))],
        out_shape=[jax.ShapeDtypeStruct((N_HEADS, batch, seq, 128), BF16),
                   jax.ShapeDtypeStruct((batch, N_HEADS, 128, 128), F32),
                   jax.ShapeDtypeStruct((batch, N_HEADS, 1, 128), F32),
                   jax.ShapeDtypeStruct((batch, N_HEADS, 1, 128), F32)],
        scratch_shapes=[pltpu.VMEM((batch, 128, 128), F32), pltpu.VMEM((batch, 1, 128), F32),
                        pltpu.VMEM((batch, 1, 128), F32)],
        compiler_params=pltpu.CompilerParams(dimension_semantics=("parallel", "arbitrary"),
                                             vmem_limit_bytes=VMEM_LIMIT),
        name="mlstm_chunk",
    )(proj, proj, proj, proj, gates, mnorm, c0, n0, m0)


def _split(x):
    hi = x.astype(BF16)
    return hi, (x - hi.astype(F32)).astype(BF16)


def _dot_split(a, b):
    a_hi, a_lo = a
    b_hi, b_lo = b
    if a_hi.shape[1] % 128 == 0:
        main = _dot(jnp.concatenate([a_hi, a_lo], axis=1), jnp.concatenate([b_hi, b_hi], axis=0))
    else:
        main = _dot(a_hi, b_hi) + _dot(a_lo, b_hi)
    return main + _dot(a_hi, b_lo)


def _unit_lower_inverse(mats, rows, cols, size):
    def same_block(shift):
        return (rows >> shift) == (cols >> shift)

    ident = jnp.where(rows == cols, 1.0, 0.0)
    d = [jnp.where(same_block(3), a, 0.0) for a in mats]
    d_s = [_split(x) for x in d]
    d2_s = [_split(_dot_split(x, x)) for x in d_s]
    d4_s = [_split(_dot_split(x, x)) for x in d2_s]
    t = [ident - x for x in d]
    t = [x + _dot_split(_split(x), y) for x, y in zip(t, d2_s)]
    t = [x + _dot_split(_split(x), y) for x, y in zip(t, d4_s)]
    shift = 3
    while (1 << shift) < size:
        off = jnp.logical_and(same_block(shift + 1), jnp.logical_not(same_block(shift)))
        e_s = [_split(jnp.where(off, a, 0.0)) for a in mats]
        t_s = [_split(x) for x in t]
        te_s = [_split(_dot_split(x, y)) for x, y in zip(t_s, e_s)]
        t = [x - _dot_split(y, z) for x, y, z in zip(t, te_s, t_s)]
        shift += 1
    return t


def _gdn_prep_kernel(q_ref, k_ref, v_ref, pq_ref, pk_ref, pv_ref, iq_ref, ik_ref, iv_ref, wq_ref, wk_ref, wv_ref,
                     gates_ref,
                     u_ref, w_ref, qe_ref, kd_ref, attn_ref, glast_ref, ext, *, chunk, nsub):
    L = chunk
    R = nsub * L
    head = pl.program_id(1)
    first = pl.program_id(2) == 0

    def conv(x_ref, prev_ref, init_ref, cw_ref):
        prev = jnp.where(first, init_ref[0], prev_ref[0, 0])
        ext[0:8, :] = prev
        ext[8:8 + R, :] = x_ref[0, 0]
        base = 8 - (CONV_W - 1)
        acc = ext[base:base + R, :] * cw_ref[0:1, :]
        for i in range(1, CONV_W):
            acc = acc + ext[base + i:base + i + R, :] * cw_ref[i:i + 1, :]
        return acc * _sigmoid(acc)

    q = conv(q_ref, pq_ref, iq_ref, wq_ref)
    k = conv(k_ref, pk_ref, ik_ref, wk_ref)
    v_all = conv(v_ref, pv_ref, iv_ref, wv_ref)
    qn_all = q * _l2_scale(q) * Q_SCALE
    kn_all = k * _l2_scale(k)

    gates = gates_ref[0]
    lane = lax.broadcasted_iota(jnp.int32, (R, 128), 1)
    lg_all = _select_lane(gates, lane, head + LANE_GA)
    beta_all = _select_lane(gates, lane, head + LANE_GB)

    U = min(R, 128)
    per_unit = U // L
    rows = lax.broadcasted_iota(jnp.int32, (U, U), 0)
    cols = lax.broadcasted_iota(jnp.int32, (U, U), 1)
    shift = L.bit_length() - 1
    same_chunk = (rows >> shift) == (cols >> shift)
    causal = jnp.logical_and(rows >= cols, same_chunk)
    strict = jnp.logical_and(rows > cols, same_chunk)
    upper = jnp.logical_and(rows <= cols, same_chunk)
    eye = rows == cols
    row_id = lax.broadcasted_iota(jnp.int32, (U, 1), 0)

    units = range(R // U)
    sls = [slice(j * U, (j + 1) * U) for j in units]
    g_cols, decays, a_mats, qks = [], [], [], []
    for sl in sls:
        lg_col = lg_all[sl]
        lg_row = jnp.sum(jnp.where(eye, lg_col, 0.0), axis=0, keepdims=True)
        g_col = jnp.sum(jnp.where(causal, lg_row, 0.0), axis=1, keepdims=True)
        g_row = jnp.sum(jnp.where(upper, lg_col, 0.0), axis=0, keepdims=True)
        decay = jnp.where(causal, jnp.exp(jnp.where(causal, g_col - g_row, 0.0)), 0.0)
        kb = kn_all[sl].astype(BF16)
        kk = _dot_nt(kb, kb)
        qks.append(_dot_nt(qn_all[sl].astype(BF16), kb))
        a_mats.append(jnp.where(strict, beta_all[sl] * kk * decay, 0.0))
        g_cols.append(g_col)
        decays.append(decay)

    t_mats = _unit_lower_inverse(a_mats, rows, cols, L)

    e_gs = [jnp.exp(g) for g in g_cols]
    rhs_s = [_split(jnp.concatenate([beta_all[sl] * v_all[sl], (beta_all[sl] * e_g) * kn_all[sl]], axis=1))
             for sl, e_g in zip(sls, e_gs)]
    uws = [_dot_split(_split(t), r) for t, r in zip(t_mats, rhs_s)]

    for j in units:
        sl, g_col = sls[j], g_cols[j]
        u_ref[0, 0, sl, :] = uws[j][:, 0:128]
        w_ref[0, 0, sl, :] = uws[j][:, 128:256].astype(BF16)
        qe_ref[0, 0, sl, :] = (qn_all[sl] * e_gs[j]).astype(BF16)
        attn = qks[j] * decays[j]
        g_last_col = jnp.zeros((U, 1), F32)
        for i in range(per_unit):
            g_last = g_col[(i + 1) * L - 1:(i + 1) * L, :]
            g_last_col = jnp.where((row_id >> shift) == i, g_last, g_last_col)
            glast_ref[0, 0, j * per_unit + i] = jnp.broadcast_to(jnp.exp(g_last), (1, 128))
            diag = attn[i * L:(i + 1) * L, :]
            if i > 0:
                diag = pltpu.roll(diag, U - i * L, axis=1)
            attn_ref[0, 0, j * per_unit + i] = diag[:, 0:L].astype(BF16)
        kd_ref[0, 0, sl, :] = (kn_all[sl] * jnp.exp(g_last_col - g_col)).astype(BF16)


def _gdn_prep(proj, gates, conv_init, conv_w, *, chunk, nsub):
    _, batch, seq, _ = proj.shape
    nc = seq // chunk
    R = nsub * chunk

    def col(base):
        return pl.BlockSpec((1, 1, R, 128), lambda b, h, c: (base + h, b, c, 0))

    def prev(base):
        return pl.BlockSpec((1, 1, 8, 128), lambda b, h, c: (base + h, b, jnp.maximum(c * (R // 8) - 1, 0), 0))

    def init(base):
        return pl.BlockSpec((1, 8, 128), lambda b, h, c: (base + h, 0, 0))

    def cw(base):
        return pl.BlockSpec((CONV_W, 128), lambda b, h, c: (0, base + h))

    def per_token(dtype):
        return (pl.BlockSpec((1, 1, R, 128), lambda b, h, c: (b, h, c, 0)),
                jax.ShapeDtypeStruct((batch, N_HEADS, seq, 128), dtype))

    outs = [per_token(F32), per_token(BF16), per_token(BF16), per_token(BF16),
            (pl.BlockSpec((1, 1, nsub, chunk, chunk), lambda b, h, c: (b, h, c, 0, 0)),
             jax.ShapeDtypeStruct((batch, N_HEADS, nc, chunk, chunk), BF16)),
            (pl.BlockSpec((1, 1, nsub, 1, 128), lambda b, h, c: (b, h, c, 0, 0)),
             jax.ShapeDtypeStruct((batch, N_HEADS, nc, 1, 128), F32))]
    return pl.pallas_call(
        functools.partial(_gdn_prep_kernel, chunk=chunk, nsub=nsub),
        grid=(batch, N_HEADS, nc // nsub),
        in_specs=[col(COL_GQ), col(COL_GK), col(COL_GV), prev(COL_GQ), prev(COL_GK), prev(COL_GV),
                  init(0), init(8), init(16), cw(0), cw(8), cw(16),
                  pl.BlockSpec((1, R, 128), lambda b, h, c: (b, c, 0))],
        out_specs=[o[0] for o in outs],
        out_shape=[o[1] for o in outs],
        scratch_shapes=[pltpu.VMEM((8 + R, 128), F32)],
        compiler_params=pltpu.CompilerParams(dimension_semantics=("parallel", "parallel", "parallel"),
                                             vmem_limit_bytes=VMEM_LIMIT),
        name="gdn_prep",
    )(proj, proj, proj, proj, proj, proj, conv_init, conv_init, conv_init, conv_w, conv_w, conv_w,
      gates)


def _gdn_scan_kernel(u_ref, w_ref, qe_ref, kd_ref, attn_ref, glast_ref, gz_ref, gnorm_ref, s0_ref,
                     o_ref, s_out_ref, s_s):
    nb, nh = u_ref.shape[0], u_ref.shape[1]
    shared = s0_ref.shape[0] == 1
    c = pl.program_id(1)
    last = c == pl.num_programs(1) - 1
    chains = [(b, h) for b in range(nb) for h in range(nh)]

    @pl.when(c == 0)
    def _():
        for b, h in chains:
            s_s[b, h] = s0_ref[0 if shared else b, h]

    L = u_ref.shape[2]
    s_prev = [s_s[b, h] for b, h in chains]
    ws = [_dot(jnp.concatenate([w_ref[b, h], qe_ref[b, h]], axis=0), sp.astype(BF16))
          for (b, h), sp in zip(chains, s_prev)]
    vb = [(u_ref[b, h] - r[0:L]).astype(BF16) for (b, h), r in zip(chains, ws)]
    intra = [_dot(attn_ref[b, h, 0], v) for (b, h), v in zip(chains, vb)]
    upd = [_dot_tn(kd_ref[b, h], v) for (b, h), v in zip(chains, vb)]
    for i, (b, h) in enumerate(chains):
        s_s[b, h] = glast_ref[b, h, 0][:, 0:1] * s_prev[i] + upd[i]
        o = ws[i][L:2 * L] + intra[i]
        gz = gz_ref[h, b]
        o_ref[h, b] = (o * _rms_scale(o) * gnorm_ref[h] * (gz * _sigmoid(gz))).astype(o_ref.dtype)

    @pl.when(last)
    def _():
        for b, h in chains:
            s_out_ref[b, h] = s_s[b, h]


def _gdn_scan(prep, proj, gnorm, s0, *, chunk):
    _, batch, seq, _ = proj.shape
    u, w, qe, kd, attn, glast = prep
    hb = GDN_SCAN_HEADS
    assert COL_GZ % hb == 0 and N_HEADS % hb == 0
    tok = pl.BlockSpec((batch, hb, chunk, 128), lambda h, c: (0, h, c, 0))
    return pl.pallas_call(
        _gdn_scan_kernel,
        grid=(N_HEADS // hb, seq // chunk),
        in_specs=[tok, tok, tok, tok,
                  pl.BlockSpec((batch, hb, 1, chunk, chunk), lambda h, c: (0, h, c, 0, 0)),
                  pl.BlockSpec((batch, hb, 1, 1, 128), lambda h, c: (0, h, c, 0, 0)),
                  pl.BlockSpec((hb, batch, chunk, 128), lambda h, c: (COL_GZ // hb + h, 0, c, 0)),
                  pl.BlockSpec((hb, 1, 128), lambda h, c: (h, 0, 0)),
                  pl.BlockSpec((s0.shape[0], hb, 128, 128), lambda h, c: (0, h, 0, 0))],
        out_specs=[pl.BlockSpec((hb, batch, chunk, 128), lambda h, c: (h, 0, c, 0)),
                   pl.BlockSpec((batch, hb, 128, 128), lambda h, c: (0, h, 0, 0))],
        out_shape=[jax.ShapeDtypeStruct((N_HEADS, batch, seq, 128), BF16),
                   jax.ShapeDtypeStruct((batch, N_HEADS, 128, 128), F32)],
        scratch_shapes=[pltpu.VMEM((batch, hb, 128, 128), F32)],
        compiler_params=pltpu.CompilerParams(dimension_semantics=("parallel", "arbitrary"),
                                             vmem_limit_bytes=VMEM_LIMIT),
        name="gdn_scan",
    )(u, w, qe, kd, attn, glast, proj, gnorm, s0)


def _sample_kernel(p_ref, gates_ref, c_ref, n_ref, m_ref, s_ref, cb_ref, cw_ref, mnorm_ref, gnorm_ref,
                   mix_ref, c_out_ref, n_out_ref, m_out_ref, s_out_ref, ext, xc_s, *, steps):
    T = steps
    NS = SAMPLE_SEQS
    R = NS * T
    H = N_HEADS
    nbuf = CONV_W - 1

    for s in range(NS):
        ext[8 * s:8 * s + nbuf, :] = cb_ref[0, s]
        for blk in range(COL_GZ - COL_GQ):
            ext[8 * s + nbuf:8 * s + nbuf + T, blk * 128:(blk + 1) * 128] = p_ref[COL_GQ + blk, 0, s * T:(s + 1) * T, :]
        acc = ext[8 * s:8 * s + T, :] * cw_ref[0:1, :]
        for i in range(1, CONV_W):
            acc = acc + ext[8 * s + i:8 * s + i + T, :] * cw_ref[i:i + 1, :]
        xc_s[s * T:(s + 1) * T, :] = acc * _sigmoid(acc)

    rowi = lax.broadcasted_iota(jnp.int32, (R, 1), 0)
    tpos = rowi & (T - 1)
    second = rowi >= T
    seq_rows = [jnp.logical_not(second), second]

    def pick(x, j):
        return jnp.where(second, x[T + j:T + j + 1, :], x[j:j + 1, :])

    def cumsum_t(x):
        acc = jnp.where(tpos >= 0, pick(x, 0), 0.0)
        for j in range(1, T):
            acc = acc + jnp.where(tpos >= j, pick(x, j), 0.0)
        return acc

    def pcol(block, h):
        return p_ref[block + h, 0]

    gates = gates_ref[0]
    lane = lax.broadcasted_iota(jnp.int32, (R, 128), 1)
    lf_all = lg_all = beta_all = gates
    m_in = m_ref[0]
    lane_s = lax.broadcasted_iota(jnp.int32, (NS, 128), 1)
    row_s = lax.broadcasted_iota(jnp.int32, (NS, 128), 0)
    m_acc = jnp.zeros((NS, 128), F32)

    qc_all, gdn_in = [], []
    for h in range(H):
        qb = (pcol(COL_MQ, h) * Q_SCALE).astype(BF16)
        qc_all.append(jnp.where(second, _dot(qb, c_ref[0, 1, h].astype(BF16)),
                                _dot(qb, c_ref[0, 0, h].astype(BF16))))
        cq = xc_s[:, h * 128:(h + 1) * 128]
        ck = xc_s[:, WIDTH + h * 128:WIDTH + (h + 1) * 128]
        qn = cq * _l2_scale(cq) * Q_SCALE
        kn = ck * _l2_scale(ck)
        kq = jnp.concatenate([kn, qn], axis=0).astype(BF16)
        r0 = _dot(kq, s_ref[0, 0, h].astype(BF16))
        r1 = _dot(kq, s_ref[0, 1, h].astype(BF16))
        gdn_in.append((qn, kn, jnp.where(second, r1[0:R], r0[0:R]), jnp.where(second, r1[R:2 * R], r0[R:2 * R])))

    c_upd, s_upd = [], []
    for h in range(H):
        li = _select_lane(gates, lane, LANE_MI + h)
        lf = _select_lane(lf_all, lane, LANE_MF + h)
        mh = _select_lane(m_in, lane_s, h)
        m0 = jnp.where(second, mh[1:2, :], mh[0:1, :])
        b = cumsum_t(lf)
        a = b + m0
        dcol = [b - pick(b, j) + pick(li, j) for j in range(T)]
        m_t = a
        for j in range(T):
            m_t = jnp.maximum(m_t, jnp.where(tpos >= j, dcol[j], -jnp.inf))
        inter = jnp.exp(a - m_t)

        q = pcol(COL_MQ, h) * Q_SCALE
        k = pcol(COL_MK, h)
        v = pcol(COL_MV, h)
        num = jnp.zeros((R, 128), F32)
        den = jnp.zeros((R, 1), F32)
        for j in range(T):
            w_j = jnp.where(tpos >= j, jnp.exp(jnp.where(tpos >= j, dcol[j] - m_t, 0.0)), 0.0)
            s_j = jnp.sum(q * pick(k, j), axis=1, keepdims=True) * w_j
            num = num + s_j * pick(v, j)
            den = den + s_j
        qc = qc_all[h]
        n_sel =jnp.where(second, n_ref[0, 1, h:h + 1, :], n_ref[0, 0, h:h + 1, :])
        num = num + inter * qc
        den = den + inter * jnp.sum(q * n_sel, axis=1, keepdims=True)
        hh = num / jnp.maximum(jnp.abs(den), jnp.exp(-m_t))
        mix_ref[0, :, h * 128:(h + 1) * 128] = hh * _rms_scale(hh) * mnorm_ref[h:h + 1, :] * _sigmoid(pcol(COL_MO, h))

        m_l = pick(m_t, T - 1)
        b_l = pick(b, T - 1)
        kw = jnp.exp(b_l - b + li - m_l) * k
        d_c = jnp.exp(b_l + m0 - m_l)
        c_upd.append((kw, d_c, v.astype(BF16)))
        for s in range(NS):
            kws = jnp.where(seq_rows[s], kw, 0.0)
            n_out_ref[0, s, h:h + 1, :] = (d_c[s * T:s * T + 1, :] * n_ref[0, s, h:h + 1, :]
                                           + jnp.sum(kws, axis=0, keepdims=True))
            m_acc = jnp.where(jnp.logical_and(lane_s == h, row_s == s), m_l[s * T:s * T + 1, :], m_acc)

        lg = _select_lane(lg_all, lane, LANE_GA + h)
        beta = _select_lane(beta_all, lane, LANE_GB + h)
        g = cumsum_t(lg)
        e_g = jnp.exp(g)
        cv = xc_s[:, 2 * WIDTH + h * 128:2 * WIDTH + (h + 1) * 128]
        qn, kn, k_s, q_s = gdn_in[h]
        x = beta * (cv - e_g * k_s)
        rel = [jnp.exp(jnp.where(tpos >= j, g - pick(g, j), 0.0)) for j in range(T)]
        for j in range(T - 1):
            kk_j = jnp.sum(kn * pick(kn, j), axis=1, keepdims=True)
            a_j = jnp.where(tpos > j, beta * kk_j * rel[j], 0.0)
            x = x - a_j * pick(x, j)
        o = e_g * q_s
        for j in range(T):
            qk_j = jnp.sum(qn * pick(kn, j), axis=1, keepdims=True)
            o = o + jnp.where(tpos >= j, qk_j * rel[j], 0.0) * pick(x, j)
        gz = pcol(COL_GZ, h)
        mix_ref[0, :, WIDTH + h * 128:WIDTH + (h + 1) * 128] = (
            o * _rms_scale(o) * gnorm_ref[h:h + 1, :] * (gz * _sigmoid(gz)))

        g_l = pick(g, T - 1)
        s_upd.append((kn * jnp.exp(g_l - g), jnp.exp(g_l), x.astype(BF16)))
    m_out_ref[0] = m_acc

    for h in range(H):
        for (lhs, scale, rhs), st_ref, out_ref in ((c_upd[h], c_ref, c_out_ref), (s_upd[h], s_ref, s_out_ref)):
            for s in range(NS):
                rows_s = jnp.where(seq_rows[s], lhs, 0.0).astype(BF16)
                out_ref[0, s, h] = scale[s * T:s * T + 1, :] * st_ref[0, s, h] + _dot_tn(rows_s, rhs)


def _sample_step(proj, gates, c, n, m, s, conv_buf, conv_w, mnorm, gnorm, *, steps):
    groups = c.shape[0]
    ns = SAMPLE_SEQS
    assert ns == 2 and ns * steps == 8 and steps >= CONV_W - 1 and steps & (steps - 1) == 0

    def per_group(*shape):
        return pl.BlockSpec((1,) + shape, lambda b: (b,) + (0,) * len(shape))

    def const(*shape):
        return pl.BlockSpec(shape, lambda b: (0,) * len(shape))

    state = (ns, N_HEADS, 128, 128)
    return pl.pallas_call(
        functools.partial(_sample_kernel, steps=steps),
        grid=(groups,),
        in_specs=[pl.BlockSpec((PROJ_COLS // 128, 1, ns * steps, 128), lambda b: (0, b, 0, 0)),
                  per_group(ns * steps, 128), per_group(*state), per_group(ns, N_HEADS, 128),
                  per_group(ns, 128), per_group(*state), per_group(ns, CONV_W - 1, 3 * WIDTH),
                  const(CONV_W, 3 * WIDTH), const(N_HEADS, 128), const(N_HEADS, 128)],
        out_specs=[per_group(ns * steps, 2 * WIDTH), per_group(*state), per_group(ns, N_HEADS, 128),
                   per_group(ns, 128), per_group(*state)],
        out_shape=[jax.ShapeDtypeStruct((groups, ns * steps, 2 * WIDTH), F32),
                   jax.ShapeDtypeStruct((groups,) + state, F32),
                   jax.ShapeDtypeStruct((groups, ns, N_HEADS, 128), F32),
                   jax.ShapeDtypeStruct((groups, ns, 128), F32),
                   jax.ShapeDtypeStruct((groups,) + state, F32)],
        scratch_shapes=[pltpu.VMEM((8 * ns, 3 * WIDTH), F32), pltpu.VMEM((ns * steps, 3 * WIDTH), F32)],
        compiler_params=pltpu.CompilerParams(dimension_semantics=("parallel",),
                                             vmem_limit_bytes=VMEM_LIMIT),
        name="sample_step",
    )(proj, gates, c, n, m, s, conv_buf, conv_w, mnorm, gnorm)


def kernel(x_prompt, x_sample, state_mlstm_C, state_mlstm_n, state_mlstm_m, state_gdn_S, state_gdn_conv, meta_tokens, norm_pre_mix, norm_post_mix, norm_pre_ffn, norm_post_ffn, w_in, mlstm_b_i, mlstm_b_f, mlstm_norm, gdn_conv_w, gdn_A_log, gdn_dt_bias, gdn_norm, w_out, w_gate_up, w_down):
    assert w_in.shape[0] == 1, "single-layer trunk"
    B, S, D = x_prompt.shape
    Bs, Ts, _ = x_sample.shape
    H = N_HEADS

    w_proj = _regroup_weights(w_in[0])
    w_out_b = w_out[0].astype(BF16)
    w_gu_b = w_gate_up[0].astype(BF16)
    w_dn_b = w_down[0].astype(BF16)
    zeros_h = jnp.zeros((H,), F32)
    gbias = jnp.concatenate([mlstm_b_i[0], mlstm_b_f[0], gdn_dt_bias[0], zeros_h,
                             jnp.zeros((128 - 4 * H,), F32)]).reshape(1, 128)
    arow = jnp.concatenate([zeros_h, zeros_h, gdn_A_log[0], zeros_h,
                            jnp.zeros((128 - 4 * H,), F32)]).reshape(1, 128)
    g_pre_mix = norm_pre_mix[0].reshape(1, D)
    g_post_mix = norm_post_mix[0].reshape(1, D)
    g_pre_ffn = norm_pre_ffn[0].reshape(1, D)
    g_post_ffn = norm_post_ffn[0].reshape(1, D)
    mnorm = mlstm_norm[0]
    gnorm = gdn_norm[0]
    mnorm3 = mnorm.reshape(H, 1, 128)
    gnorm3 = gnorm.reshape(H, 1, 128)
    conv_w = gdn_conv_w[0]

    xp = x_prompt.reshape(B * S, D)
    xs = x_sample.reshape(Bs * Ts, D)

    nblk = PROJ_COLS // 128
    proj_p = _norm_matmul(xp, g_pre_mix, w_proj, tm=1024, tn=768).reshape(nblk, B, S, 128)
    n_sm = Bs * Ts + N_META
    proj_sm = _norm_matmul(jnp.concatenate([xs, meta_tokens], axis=0), g_pre_mix, w_proj, tm=n_sm, tn=768)
    proj_m = proj_sm[:, Bs * Ts:].reshape(nblk, 1, N_META, 128)

    gates_p = _gate_act(proj_p.reshape(nblk, B * S, 128), gbias, arow, tm=1024).reshape(B, S, 128)
    gates_sm = _gate_act(proj_sm, gbias, arow, tm=n_sm)
    gates_m = gates_sm[Bs * Ts:].reshape(1, N_META, 128)

    def conv_rows(blocks):
        return jnp.moveaxis(blocks, 0, -2).reshape(blocks.shape[1:-1] + (3 * WIDTH,))

    c0, n0, m0 = (jnp.zeros((1, H, 128, 128), F32), jnp.zeros((1, H, 1, 128), F32), jnp.zeros((1, H, 1, 128), F32))
    _, c_l, n_l, m_l = _mlstm_chunked(proj_m, gates_m, mnorm3, c0, n0, m0, chunk=N_META)
    prep_l = _gdn_prep(proj_m, gates_m, jnp.zeros((3 * H, 8, 128), F32), conv_w, chunk=N_META, nsub=1)
    _, s_l = _gdn_scan(prep_l, proj_m, gnorm3, c0, chunk=N_META)

    mix_a, p_c, p_n, p_m = _mlstm_chunked(proj_p, gates_p, mnorm3, c_l, n_l, m_l, chunk=MLSTM_CHUNK)
    conv_init = proj_m[COL_GQ:COL_GZ, 0, N_META - 8:, :]
    prep_p = _gdn_prep(proj_p, gates_p, conv_init, conv_w, chunk=GDN_CHUNK, nsub=GDN_PREP_CHUNKS)
    mix_b, p_s = _gdn_scan(prep_p, proj_p, gnorm3, s_l, chunk=GDN_CHUNK)
    p_conv = conv_rows(proj_p[COL_GQ:COL_GZ, :, S - (CONV_W - 1):, :])

    x1_p, u2_p = _out_proj(mix_a.reshape(H, B * S, 128), 0, mix_b.reshape(H, B * S, 128), 0, w_out_b, xp,
                           g_post_mix, g_pre_ffn, tm=256)
    y_p = _ffn(u2_p, w_gu_b, w_dn_b, x1_p, g_post_ffn, tm=512, tf=512)

    ns = SAMPLE_SEQS
    grp = Bs // ns
    m_in = jnp.pad(state_mlstm_m[0], ((0, 0), (0, 128 - H))).reshape(grp, ns, 128)
    mix_s, s_c, s_n, s_m, s_s = _sample_step(
        proj_sm.reshape(nblk, n_sm // (ns * Ts), ns * Ts, 128), gates_sm.reshape(n_sm // (ns * Ts), ns * Ts, 128),
        state_mlstm_C[0].reshape(grp, ns, H, 128, 128),
        state_mlstm_n[0].reshape(grp, ns, H, 128), m_in, state_gdn_S[0].reshape(grp, ns, H, 128, 128),
        state_gdn_conv[0].reshape(grp, ns, CONV_W - 1, 3 * WIDTH), conv_w, mnorm, gnorm, steps=Ts)
    mix_s2 = mix_s.reshape(Bs * Ts, 2 * WIDTH)
    s_conv = conv_rows(proj_sm[COL_GQ:COL_GZ, :Bs * Ts].reshape(3 * H, Bs, Ts, 128)[:, :, Ts - (CONV_W - 1):, :])
    x1_s, u2_s = _out_proj(mix_s2, 0, mix_s2, 1, w_out_b, xs, g_post_mix, g_pre_ffn, tm=256)
    y_s = _ffn(u2_s, w_gu_b, w_dn_b, x1_s, g_post_ffn, tm=512, tf=512)

    return (y_p.reshape(B, S, D), y_s.reshape(Bs, Ts, D),
            p_c[None], p_n[:, :, 0, :][None], p_m[:, :, 0, 0][None], p_s[None], p_conv[None],
            s_c.reshape(Bs, H, 128, 128)[None], s_n.reshape(Bs, H, 128)[None], s_m.reshape(Bs, 128)[:, :H][None],
            s_s.reshape(Bs, H, 128, 128)[None], s_conv[None])
```

```python
import functools

import jax
import jax.numpy as jnp
from jax import lax
from jax.experimental import pallas as pl
from jax.experimental.pallas import tpu as pltpu

F32 = jnp.float32
BF16 = jnp.bfloat16

D_MODEL = 2048
HEAD_DIM = 128
N_HEADS = 8
WIDTH = N_HEADS * HEAD_DIM
CONV_W = 4
N_META = 16
D_FF = 5632
EPS = 1e-6
Q_SCALE = HEAD_DIM ** -0.5

COL_MQ, COL_MK, COL_MV, COL_MO = 0, 8, 16, 24
COL_GQ, COL_GK, COL_GV, COL_GZ = 32, 40, 48, 56
COL_GATES = 64
PROJ_COLS = 66 * 128
LANE_MI, LANE_MF, LANE_GA, LANE_GB = 0, 8, 16, 24

MLSTM_CHUNK = 256
GDN_CHUNK = 64
GDN_PREP_CHUNKS = 16
GDN_SCAN_HEADS = 8
SAMPLE_SEQS = 2
VMEM_LIMIT = 56 * 1024 * 1024


def _sigmoid(x):
    return 0.5 * jnp.tanh(0.5 * x) + 0.5


def _softplus(x):
    return jnp.maximum(x, 0.0) + jnp.log1p(jnp.exp(-jnp.abs(x)))


def _log_sigmoid(x):
    return -_softplus(-x)


def _rms_scale(x):
    return lax.rsqrt(jnp.mean(x * x, axis=-1, keepdims=True) + EPS)


def _l2_scale(x):
    return lax.rsqrt(jnp.sum(x * x, axis=-1, keepdims=True) + EPS)


def _dot(a, b):
    return jnp.dot(a, b, preferred_element_type=F32)


def _dot_nt(a, b):
    return lax.dot_general(a, b, (((1,), (1,)), ((), ())), preferred_element_type=F32)


def _dot_tn(a, b):
    return lax.dot_general(a, b, (((0,), (0,)), ((), ())), preferred_element_type=F32)


def _dot_f32(a, b):
    return jnp.dot(a, b, precision=lax.Precision.HIGHEST, preferred_element_type=F32)


def _select_lane(x, lane_idx, target):
    return jnp.sum(jnp.where(lane_idx == target, x, 0.0), axis=1, keepdims=True)


REGROUP_TILE = 256


def _regroup_kernel(w_ref, mg_ref, gg_ref, o_ref):
    j = pl.program_id(0)
    wide_tiles = 2 * 4 * WIDTH // REGROUP_TILE

    @pl.when(j < wide_tiles)
    def _():
        o_ref[...] = w_ref[...].T.astype(BF16)

    @pl.when(j == wide_tiles)
    def _():
        pad = jnp.zeros((REGROUP_TILE - 4 * N_HEADS, w_ref.shape[1]), F32)
        o_ref[...] = jnp.concatenate([mg_ref[...], gg_ref[...], pad], axis=0).T.astype(BF16)


def _regroup_weights(w_t):
    n, d = w_t.shape
    wide = 4 * WIDTH
    narrow = 2 * N_HEADS
    assert n == 2 * (wide + narrow) and PROJ_COLS == 2 * wide + REGROUP_TILE
    wide_tiles = 2 * wide // REGROUP_TILE

    def src_row(j):
        j = jnp.minimum(j, wide_tiles - 1)
        return (j * (REGROUP_TILE // narrow) + jnp.where(j >= wide_tiles // 2, 1, 0)) * narrow

    return pl.pallas_call(
        _regroup_kernel,
        grid=(wide_tiles + 1,),
        in_specs=[pl.BlockSpec((pl.Element(REGROUP_TILE), pl.Element(d)), lambda j: (src_row(j), 0)),
                  pl.BlockSpec((narrow, d), lambda j: (wide // narrow, 0)),
                  pl.BlockSpec((narrow, d), lambda j: ((2 * wide + narrow) // narrow, 0))],
        out_specs=pl.BlockSpec((d, REGROUP_TILE), lambda j: (0, j)),
        out_shape=jax.ShapeDtypeStruct((d, PROJ_COLS), BF16),
        compiler_params=pltpu.CompilerParams(dimension_semantics=("parallel",), vmem_limit_bytes=VMEM_LIMIT),
        name="regroup_weights",
    )(w_t, w_t, w_t)


def _norm_matmul_kernel(x_ref, g_ref, w_ref, o_ref, u_ref):
    @pl.when(pl.program_id(1) == 0)
    def _():
        x = x_ref[...]
        u_ref[...] = (x * _rms_scale(x) * g_ref[...]).astype(BF16)

    res = _dot(u_ref[...], w_ref[...])
    for blk in range(o_ref.shape[0]):
        o_ref[blk] = res[:, blk * 128:(blk + 1) * 128]


def _norm_matmul(x, gain, w, *, tm, tn):
    m, d = x.shape
    n = w.shape[1]
    return pl.pallas_call(
        _norm_matmul_kernel,
        grid=(m // tm, n // tn),
        in_specs=[pl.BlockSpec((tm, d), lambda i, j: (i, 0)),
                  pl.BlockSpec((1, d), lambda i, j: (0, 0)),
                  pl.BlockSpec((d, tn), lambda i, j: (0, j))],
        out_specs=pl.BlockSpec((tn // 128, tm, 128), lambda i, j: (j, i, 0)),
        out_shape=jax.ShapeDtypeStruct((n // 128, m, 128), F32),
        scratch_shapes=[pltpu.VMEM((tm, d), BF16)],
        compiler_params=pltpu.CompilerParams(dimension_semantics=("parallel", "arbitrary"),
                                             vmem_limit_bytes=VMEM_LIMIT),
        name="norm_matmul",
    )(x, gain, w)


def _out_proj_kernel(ma_ref, mb_ref, wa_ref, wb_ref, x_ref, gpost_ref, gpre_ref, x1_ref, u2_ref):
    def rows(ref):
        if len(ref.shape) == 2:
            return ref[...].astype(BF16)
        return jnp.concatenate([ref[h] for h in range(ref.shape[0])], axis=1).astype(BF16)

    mo = _dot(rows(ma_ref), wa_ref[...]) + _dot(rows(mb_ref), wb_ref[...])
    x1 = x_ref[...] + mo * _rms_scale(mo) * gpost_ref[...]
    x1_ref[...] = x1
    u2_ref[...] = (x1 * _rms_scale(x1) * gpre_ref[...]).astype(BF16)


def _out_proj(mix_a, col_a, mix_b, col_b, w_out, x, g_post, g_pre, *, tm):
    m, d = x.shape

    def mix_spec(mix, col):
        if mix.ndim == 2:
            return pl.BlockSpec((tm, WIDTH), lambda i: (i, col))
        return pl.BlockSpec((N_HEADS, tm, 128), lambda i: (0, i, 0))

    return pl.pallas_call(
        _out_proj_kernel,
        grid=(m // tm,),
        in_specs=[mix_spec(mix_a, col_a),
                  mix_spec(mix_b, col_b),
                  pl.BlockSpec((WIDTH, d), lambda i: (0, 0)),
                  pl.BlockSpec((WIDTH, d), lambda i: (1, 0)),
                  pl.BlockSpec((tm, d), lambda i: (i, 0)),
                  pl.BlockSpec((1, d), lambda i: (0, 0)),
                  pl.BlockSpec((1, d), lambda i: (0, 0))],
        out_specs=[pl.BlockSpec((tm, d), lambda i: (i, 0)),
                   pl.BlockSpec((tm, d), lambda i: (i, 0))],
        out_shape=[jax.ShapeDtypeStruct((m, d), F32), jax.ShapeDtypeStruct((m, d), BF16)],
        compiler_params=pltpu.CompilerParams(dimension_semantics=("parallel",),
                                             vmem_limit_bytes=VMEM_LIMIT),
        name="out_proj",
    )(mix_a, mix_b, w_out, w_out, x, g_post, g_pre)


def _ffn_kernel(u_ref, wg_ref, wu_ref, wd_ref, x1_ref, gain_ref, o_ref, acc_ref):
    f = pl.program_id(1)

    @pl.when(f == 0)
    def _():
        acc_ref[...] = jnp.zeros_like(acc_ref)

    u = u_ref[...]
    g = _dot(u, wg_ref[...])
    up = _dot(u, wu_ref[...])
    hidden = (g * _sigmoid(g) * up).astype(BF16)
    acc_ref[...] += _dot(hidden, wd_ref[...])

    @pl.when(f == pl.num_programs(1) - 1)
    def _():
        y = acc_ref[...]
        o_ref[...] = x1_ref[...] + y * _rms_scale(y) * gain_ref[...]


def _ffn(u2, w_gate_up, w_down, x1, gain, *, tm, tf):
    m, d = x1.shape
    nf = D_FF // tf
    return pl.pallas_call(
        _ffn_kernel,
        grid=(m // tm, nf),
        in_specs=[pl.BlockSpec((tm, d), lambda i, f: (i, 0)),
                  pl.BlockSpec((d, tf), lambda i, f: (0, f)),
                  pl.BlockSpec((d, tf), lambda i, f: (0, nf + f)),
                  pl.BlockSpec((tf, d), lambda i, f: (f, 0)),
                  pl.BlockSpec((tm, d), lambda i, f: (i, 0)),
                  pl.BlockSpec((1, d), lambda i, f: (0, 0))],
        out_specs=pl.BlockSpec((tm, d), lambda i, f: (i, 0)),
        out_shape=jax.ShapeDtypeStruct((m, d), F32),
        scratch_shapes=[pltpu.VMEM((tm, d), F32)],
        compiler_params=pltpu.CompilerParams(dimension_semantics=("parallel", "arbitrary"),
                                             vmem_limit_bytes=VMEM_LIMIT),
        name="ffn",
    )(u2, w_gate_up, w_gate_up, w_down, x1, gain)


def _gate_act_kernel(g_ref, gbias_ref, arow_ref, o_ref):
    g = g_ref[0] + gbias_ref[...]
    lane = lax.broadcasted_iota(jnp.int32, g.shape, 1)
    decay = -jnp.exp(arow_ref[...]) * _softplus(g)
    o_ref[...] = jnp.where(lane < LANE_MF, g,
                           jnp.where(lane < LANE_GA, _log_sigmoid(g), jnp.where(lane < LANE_GB, decay, _sigmoid(g))))


def _gate_act(proj, gbias, arow, *, tm):
    m = proj.shape[1]
    return pl.pallas_call(
        _gate_act_kernel,
        grid=(m // tm,),
        in_specs=[pl.BlockSpec((1, tm, 128), lambda i: (COL_GATES, i, 0)),
                  pl.BlockSpec((1, 128), lambda i: (0, 0)),
                  pl.BlockSpec((1, 128), lambda i: (0, 0))],
        out_specs=pl.BlockSpec((tm, 128), lambda i: (i, 0)),
        out_shape=jax.ShapeDtypeStruct((m, 128), F32),
        compiler_params=pltpu.CompilerParams(dimension_semantics=("parallel",), vmem_limit_bytes=VMEM_LIMIT),
        name="gate_act",
    )(proj, gbias, arow)


def _mlstm_chunk_kernel(q_ref, k_ref, v_ref, og_ref, gates_ref, mnorm_ref, c0_ref, n0_ref, m0_ref,
                        h_ref, c_out_ref, n_out_ref, m_out_ref, c_s, n_s, m_s, *, chunk):
    L = chunk
    nb = q_ref.shape[1]
    shared = c0_ref.shape[0] == 1
    head = pl.program_id(0)
    c = pl.program_id(1)

    @pl.when(c == 0)
    def _():
        for b in range(nb):
            b0 = 0 if shared else b
            c_s[b] = c0_ref[b0, 0]
            n_s[b] = n0_ref[b0, 0]
            m_s[b] = m0_ref[b0, 0]

    lane = lax.broadcasted_iota(jnp.int32, (L, 128), 1)
    rows = lax.broadcasted_iota(jnp.int32, (L, L), 0)
    cols = lax.broadcasted_iota(jnp.int32, (L, L), 1)
    tril = rows >= cols
    triu = rows <= cols
    eye = rows == cols
    last = c == pl.num_programs(1) - 1

    seqs = range(nb)
    gates = [gates_ref[b] for b in seqs]
    li_col = [_select_lane(g, lane, head + LANE_MI) for g in gates]
    lf_col = [_select_lane(g, lane, head + LANE_MF) for g in gates]
    lf_row = [jnp.sum(jnp.where(eye, x, 0.0), axis=0, keepdims=True) for x in lf_col]
    li_row = [jnp.sum(jnp.where(eye, x, 0.0), axis=0, keepdims=True) for x in li_col]
    b_col = [jnp.sum(jnp.where(tril, x, 0.0), axis=1, keepdims=True) for x in lf_row]
    b_row = [jnp.sum(jnp.where(triu, x, 0.0), axis=0, keepdims=True) for x in lf_col]

    qb = [(q_ref[0, b] * Q_SCALE).astype(BF16) for b in seqs]
    kb = [k_ref[0, b].astype(BF16) for b in seqs]
    vb = [v_ref[0, b].astype(BF16) for b in seqs]
    qk = [_dot_nt(x, y) for x, y in zip(qb, kb)]
    qc = [_dot(x, c_s[b].astype(BF16)) for b, x in zip(seqs, qb)]

    m_prev = [m_s[b, :, 0:1] for b in seqs]
    a_col = [x + y for x, y in zip(b_col, m_prev)]
    dm = [jnp.where(tril, bc - br + lr, -jnp.inf) for bc, br, lr in zip(b_col, b_row, li_row)]
    m_t = [jnp.maximum(a, jnp.max(d, axis=1, keepdims=True)) for a, d in zip(a_col, dm)]
    inter = [jnp.exp(a - m) for a, m in zip(a_col, m_t)]
    s = [x * jnp.exp(d - m) for x, d, m in zip(qk, dm, m_t)]
    sv = [_dot(x.astype(BF16), v) for x, v in zip(s, vb)]

    m_last = [m[L - 1:L, :] for m in m_t]
    b_last = [x[L - 1:L, :] for x in b_col]
    d_c = [jnp.exp(bl + mp - ml) for bl, mp, ml in zip(b_last, m_prev, m_last)]
    kw = [jnp.exp(bl - bc + li - ml) * k_ref[0, b]
          for b, bl, bc, li, ml in zip(seqs, b_last, b_col, li_col, m_last)]
    c_upd = [_dot_tn(x.astype(BF16), v) for x, v in zip(kw, vb)]

    for b in seqs:
        q = q_ref[0, b] * Q_SCALE
        num = sv[b] + inter[b] * qc[b]
        den = jnp.sum(s[b], axis=1, keepdims=True) + inter[b] * jnp.sum(q * n_s[b], axis=1, keepdims=True)
        hh = num / jnp.maximum(jnp.abs(den), jnp.exp(-m_t[b]))
        out = hh * _rms_scale(hh) * mnorm_ref[0] * _sigmoid(og_ref[0, b])
        h_ref[0, b] = out.astype(h_ref.dtype)
        c_s[b] = d_c[b] * c_s[b] + c_upd[b]
        n_s[b] = d_c[b] * n_s[b] + jnp.sum(kw[b], axis=0, keepdims=True)
        m_s[b] = jnp.broadcast_to(m_last[b], (1, 128))

    @pl.when(last)
    def _():
        for b in range(nb):
            c_out_ref[b, 0] = c_s[b]
            n_out_ref[b, 0] = n_s[b]
            m_out_ref[b, 0] = m_s[b]


def _mlstm_chunked(proj, gates, mnorm, c0, n0, m0, *, chunk):
    _, batch, seq, _ = proj.shape
    nb0 = c0.shape[0]

    def col(base):
        return pl.BlockSpec((1, batch, chunk, 128), lambda h, c: (base + h, 0, c, 0))

    def state(nb, shape):
        return pl.BlockSpec((nb, 1) + shape, lambda h, c: (0, h, 0, 0))

    return pl.pallas_call(
        functools.partial(_mlstm_chunk_kernel, chunk=chunk),
        grid=(N_HEADS, seq // chunk),
        in_specs=[col(COL_MQ), col(COL_MK), col(COL_MV), col(COL_MO),
                  pl.BlockSpec((batch, chunk, 128), lambda h, c: (0, c, 0)),
                  pl.BlockSpec((1, 1, 128), lambda h, c: (h, 0, 0)),
                  state(nb0, (128, 128)), state(nb0, (1, 128)), state(nb0, (1, 128))],
        out_specs=[pl.BlockSpec((1, batch, chunk, 128), lambda h, c: (h, 0, c, 0)),
                   state(batch, (128, 128)), state(batch, (1, 128)), state(batch, (1, 128))],
        out_shape=[jax.ShapeDtypeStruct((N_HEADS, batch, seq, 128), BF16),
                   jax.ShapeDtypeStruct((batch, N_HEADS, 128, 128), F32),
                   jax.ShapeDtypeStruct((batch, N_HEADS, 1, 128), F32),
                   jax.ShapeDtypeStruct((batch, N_HEADS, 1, 128), F32)],
        scratch_shapes=[pltpu.VMEM((batch, 128, 128), F32), pltpu.VMEM((batch, 1, 128), F32),
                        pltpu.VMEM((batch, 1, 128), F32)],
        compiler_params=pltpu.CompilerParams(dimension_semantics=("parallel", "arbitrary"),
                                             vmem_limit_bytes=VMEM_LIMIT),
        name="mlstm_chunk",
    )(proj, proj, proj, proj, gates, mnorm, c0, n0, m0)


def _split(x):
    hi = x.astype(BF16)
    return hi, (x - hi.astype(F32)).astype(BF16)


def _dot_split(a, b):
    a_hi, a_lo = a
    b_hi, b_lo = b
    if a_hi.shape[1] % 128 == 0:
        main = _dot(jnp.concatenate([a_hi, a_lo], axis=1), jnp.concatenate([b_hi, b_hi], axis=0))
    else:
        main = _dot(a_hi, b_hi) + _dot(a_lo, b_hi)
    return main + _dot(a_hi, b_lo)


def _unit_lower_inverse(mats, rows, cols, size):
    def same_block(shift):
        return (rows >> shift) == (cols >> shift)

    ident = jnp.where(rows == cols, 1.0, 0.0)
    d = [jnp.where(same_block(3), a, 0.0) for a in mats]
    d_s = [_split(x) for x in d]
    d2_s = [_split(_dot_split(x, x)) for x in d_s]
    d4_s = [_split(_dot_split(x, x)) for x in d2_s]
    t = [ident - x for x in d]
    t = [x + _dot_split(_split(x), y) for x, y in zip(t, d2_s)]
    t = [x + _dot_split(_split(x), y) for x, y in zip(t, d4_s)]
    shift = 3
    while (1 << shift) < size:
        off = jnp.logical_and(same_block(shift + 1), jnp.logical_not(same_block(shift)))
        e_s = [_split(jnp.where(off, a, 0.0)) for a in mats]
        t_s = [_split(x) for x in t]
        te_s = [_split(_dot_split(x, y)) for x, y in zip(t_s, e_s)]
        t = [x - _dot_split(y, z) for x, y, z in zip(t, te_s, t_s)]
        shift += 1
    return t


def _gdn_prep_kernel(q_ref, k_ref, v_ref, pq_ref, pk_ref, pv_ref, iq_ref, ik_ref, iv_ref, wq_ref, wk_ref, wv_ref,
                     gates_ref,
                     u_ref, w_ref, qe_ref, kd_ref, attn_ref, glast_ref, ext, *, chunk, nsub):
    L = chunk
    R = nsub * L
    head = pl.program_id(1)
    first = pl.program_id(2) == 0

    def conv(x_ref, prev_ref, init_ref, cw_ref):
        prev = jnp.where(first, init_ref[0], prev_ref[0, 0])
        ext[0:8, :] = prev
        ext[8:8 + R, :] = x_ref[0, 0]
        base = 8 - (CONV_W - 1)
        acc = ext[base:base + R, :] * cw_ref[0:1, :]
        for i in range(1, CONV_W):
            acc = acc + ext[base + i:base + i + R, :] * cw_ref[i:i + 1, :]
        return acc * _sigmoid(acc)

    q = conv(q_ref, pq_ref, iq_ref, wq_ref)
    k = conv(k_ref, pk_ref, ik_ref, wk_ref)
    v_all = conv(v_ref, pv_ref, iv_ref, wv_ref)
    qn_all = q * _l2_scale(q) * Q_SCALE
    kn_all = k * _l2_scale(k)

    gates = gates_ref[0]
    lane = lax.broadcasted_iota(jnp.int32, (R, 128), 1)
    lg_all = _select_lane(gates, lane, head + LANE_GA)
    beta_all = _select_lane(gates, lane, head + LANE_GB)

    U = min(R, 128)
    per_unit = U // L
    rows = lax.broadcasted_iota(jnp.int32, (U, U), 0)
    cols = lax.broadcasted_iota(jnp.int32, (U, U), 1)
    shift = L.bit_length() - 1
    same_chunk = (rows >> shift) == (cols >> shift)
    causal = jnp.logical_and(rows >= cols, same_chunk)
    strict = jnp.logical_and(rows > cols, same_chunk)
    upper = jnp.logical_and(rows <= cols, same_chunk)
    eye = rows == cols
    row_id = lax.broadcasted_iota(jnp.int32, (U, 1), 0)

    units = range(R // U)
    sls = [slice(j * U, (j + 1) * U) for j in units]
    g_cols, decays, a_mats, qks = [], [], [], []
    for sl in sls:
        lg_col = lg_all[sl]
        lg_row = jnp.sum(jnp.where(eye, lg_col, 0.0), axis=0, keepdims=True)
        g_col = jnp.sum(jnp.where(causal, lg_row, 0.0), axis=1, keepdims=True)
        g_row = jnp.sum(jnp.where(upper, lg_col, 0.0), axis=0, keepdims=True)
        decay = jnp.where(causal, jnp.exp(jnp.where(causal, g_col - g_row, 0.0)), 0.0)
        kb = kn_all[sl].astype(BF16)
        kk = _dot_nt(kb, kb)
        qks.append(_dot_nt(qn_all[sl].astype(BF16), kb))
        a_mats.append(jnp.where(strict, beta_all[sl] * kk * decay, 0.0))
        g_cols.append(g_col)
        decays.append(decay)

    t_mats = _unit_lower_inverse(a_mats, rows, cols, L)

    e_gs = [jnp.exp(g) for g in g_cols]
    rhs_s = [_split(jnp.concatenate([beta_all[sl] * v_all[sl], (beta_all[sl] * e_g) * kn_all[sl]], axis=1))
             for sl, e_g in zip(sls, e_gs)]
    uws = [_dot_split(_split(t), r) for t, r in zip(t_mats, rhs_s)]

    for j in units:
        sl, g_col = sls[j], g_cols[j]
        u_ref[0, 0, sl, :] = uws[j][:, 0:128]
        w_ref[0, 0, sl, :] = uws[j][:, 128:256].astype(BF16)
        qe_ref[0, 0, sl, :] = (qn_all[sl] * e_gs[j]).astype(BF16)
        attn = qks[j] * decays[j]
        g_last_col = jnp.zeros((U, 1), F32)
        for i in range(per_unit):
            g_last = g_col[(i + 1) * L - 1:(i + 1) * L, :]
            g_last_col = jnp.where((row_id >> shift) == i, g_last, g_last_col)
            glast_ref[0, 0, j * per_unit + i] = jnp.broadcast_to(jnp.exp(g_last), (1, 128))
            diag = attn[i * L:(i + 1) * L, :]
            if i > 0:
                diag = pltpu.roll(diag, U - i * L, axis=1)
            attn_ref[0, 0, j * per_unit + i] = diag[:, 0:L].astype(BF16)
        kd_ref[0, 0, sl, :] = (kn_all[sl] * jnp.exp(g_last_col - g_col)).astype(BF16)


def _gdn_prep(proj, gates, conv_init, conv_w, *, chunk, nsub):
    _, batch, seq, _ = proj.shape
    nc = seq // chunk
    R = nsub * chunk

    def col(base):
        return pl.BlockSpec((1, 1, R, 128), lambda b, h, c: (base + h, b, c, 0))

    def prev(base):
        return pl.BlockSpec((1, 1, 8, 128), lambda b, h, c: (base + h, b, jnp.maximum(c * (R // 8) - 1, 0), 0))

    def init(base):
        return pl.BlockSpec((1, 8, 128), lambda b, h, c: (base + h, 0, 0))

    def cw(base):
        return pl.BlockSpec((CONV_W, 128), lambda b, h, c: (0, base + h))

    def per_token(dtype):
        return (pl.BlockSpec((1, 1, R, 128), lambda b, h, c: (b, h, c, 0)),
                jax.ShapeDtypeStruct((batch, N_HEADS, seq, 128), dtype))

    outs = [per_token(F32), per_token(BF16), per_token(BF16), per_token(BF16),
            (pl.BlockSpec((1, 1, nsub, chunk, chunk), lambda b, h, c: (b, h, c, 0, 0)),
             jax.ShapeDtypeStruct((batch, N_HEADS, nc, chunk, chunk), BF16)),
            (pl.BlockSpec((1, 1, nsub, 1, 128), lambda b, h, c: (b, h, c, 0, 0)),
             jax.ShapeDtypeStruct((batch, N_HEADS, nc, 1, 128), F32))]
    return pl.pallas_call(
        functools.partial(_gdn_prep_kernel, chunk=chunk, nsub=nsub),
        grid=(batch, N_HEADS, nc // nsub),
        in_specs=[col(COL_GQ), col(COL_GK), col(COL_GV), prev(COL_GQ), prev(COL_GK), prev(COL_GV),
                  init(0), init(8), init(16), cw(0), cw(8), cw(16),
                  pl.BlockSpec((1, R, 128), lambda b, h, c: (b, c, 0))],
        out_specs=[o[0] for o in outs],
        out_shape=[o[1] for o in outs],
        scratch_shapes=[pltpu.VMEM((8 + R, 128), F32)],
        compiler_params=pltpu.CompilerParams(dimension_semantics=("parallel", "parallel", "parallel"),
                                             vmem_limit_bytes=VMEM_LIMIT),
        name="gdn_prep",
    )(proj, proj, proj, proj, proj, proj, conv_init, conv_init, conv_init, conv_w, conv_w, conv_w,
      gates)


def _gdn_scan_kernel(u_ref, w_ref, qe_ref, kd_ref, attn_ref, glast_ref, gz_ref, gnorm_ref, s0_ref,
                     o_ref, s_out_ref, s_s):
    nb, nh = u_ref.shape[0], u_ref.shape[1]
    shared = s0_ref.shape[0] == 1
    c = pl.program_id(1)
    last = c == pl.num_programs(1) - 1
    chains = [(b, h) for b in range(nb) for h in range(nh)]

    @pl.when(c == 0)
    def _():
        for b, h in chains:
            s_s[b, h] = s0_ref[0 if shared else b, h]

    L = u_ref.shape[2]
    s_prev = [s_s[b, h] for b, h in chains]
    ws = [_dot(jnp.concatenate([w_ref[b, h], qe_ref[b, h]], axis=0), sp.astype(BF16))
          for (b, h), sp in zip(chains, s_prev)]
    vb = [(u_ref[b, h] - r[0:L]).astype(BF16) for (b, h), r in zip(chains, ws)]
    intra = [_dot(attn_ref[b, h, 0], v) for (b, h), v in zip(chains, vb)]
    upd = [_dot_tn(kd_ref[b, h], v) for (b, h), v in zip(chains, vb)]
    for i, (b, h) in enumerate(chains):
        s_s[b, h] = glast_ref[b, h, 0][:, 0:1] * s_prev[i] + upd[i]
        o = ws[i][L:2 * L] + intra[i]
        gz = gz_ref[h, b]
        o_ref[h, b] = (o * _rms_scale(o) * gnorm_ref[h] * (gz * _sigmoid(gz))).astype(o_ref.dtype)

    @pl.when(last)
    def _():
        for b, h in chains:
            s_out_ref[b, h] = s_s[b, h]


def _gdn_scan(prep, proj, gnorm, s0, *, chunk):
    _, batch, seq, _ = proj.shape
    u, w, qe, kd, attn, glast = prep
    hb = GDN_SCAN_HEADS
    assert COL_GZ % hb == 0 and N_HEADS % hb == 0
    tok = pl.BlockSpec((batch, hb, chunk, 128), lambda h, c: (0, h, c, 0))
    return pl.pallas_call(
        _gdn_scan_kernel,
        grid=(N_HEADS // hb, seq // chunk),
        in_specs=[tok, tok, tok, tok,
                  pl.BlockSpec((batch, hb, 1, chunk, chunk), lambda h, c: (0, h, c, 0, 0)),
                  pl.BlockSpec((batch, hb, 1, 1, 128), lambda h, c: (0, h, c, 0, 0)),
                  pl.BlockSpec((hb, batch, chunk, 128), lambda h, c: (COL_GZ // hb + h, 0, c, 0)),
                  pl.BlockSpec((hb, 1, 128), lambda h, c: (h, 0, 0)),
                  pl.BlockSpec((s0.shape[0], hb, 128, 128), lambda h, c: (0, h, 0, 0))],
        out_specs=[pl.BlockSpec((hb, batch, chunk, 128), lambda h, c: (h, 0, c, 0)),
                   pl.BlockSpec((batch, hb, 128, 128), lambda h, c: (0, h, 0, 0))],
        out_shape=[jax.ShapeDtypeStruct((N_HEADS, batch, seq, 128), BF16),
                   jax.ShapeDtypeStruct((batch, N_HEADS, 128, 128), F32)],
        scratch_shapes=[pltpu.VMEM((batch, hb, 128, 128), F32)],
        compiler_params=pltpu.CompilerParams(dimension_semantics=("parallel", "arbitrary"),
                                             vmem_limit_bytes=VMEM_LIMIT),
        name="gdn_scan",
    )(u, w, qe, kd, attn, glast, proj, gnorm, s0)


def _sample_kernel(p_ref, gates_ref, c_ref, n_ref, m_ref, s_ref, cb_ref, cw_ref, mnorm_ref, gnorm_ref,
                   mix_ref, c_out_ref, n_out_ref, m_out_ref, s_out_ref, ext, xc_s, *, steps):
    T = steps
    NS = SAMPLE_SEQS
    R = NS * T
    H = N_HEADS
    nbuf = CONV_W - 1

    for s in range(NS):
        ext[8 * s:8 * s + nbuf, :] = cb_ref[0, s]
        for blk in range(COL_GZ - COL_GQ):
            ext[8 * s + nbuf:8 * s + nbuf + T, blk * 128:(blk + 1) * 128] = p_ref[COL_GQ + blk, 0, s * T:(s + 1) * T, :]
        acc = ext[8 * s:8 * s + T, :] * cw_ref[0:1, :]
        for i in range(1, CONV_W):
            acc = acc + ext[8 * s + i:8 * s + i + T, :] * cw_ref[i:i + 1, :]
        xc_s[s * T:(s + 1) * T, :] = acc * _sigmoid(acc)

    rowi = lax.broadcasted_iota(jnp.int32, (R, 1), 0)
    tpos = rowi & (T - 1)
    second = rowi >= T
    seq_rows = [jnp.logical_not(second), second]

    def pick(x, j):
        return jnp.where(second, x[T + j:T + j + 1, :], x[j:j + 1, :])

    def cumsum_t(x):
        acc = jnp.where(tpos >= 0, pick(x, 0), 0.0)
        for j in range(1, T):
            acc = acc + jnp.where(tpos >= j, pick(x, j), 0.0)
        return acc

    def pcol(block, h):
        return p_ref[block + h, 0]

    gates = gates_ref[0]
    lane = lax.broadcasted_iota(jnp.int32, (R, 128), 1)
    lf_all = lg_all = beta_all = gates
    m_in = m_ref[0]
    lane_s = lax.broadcasted_iota(jnp.int32, (NS, 128), 1)
    row_s = lax.broadcasted_iota(jnp.int32, (NS, 128), 0)
    m_acc = jnp.zeros((NS, 128), F32)

    qc_all, gdn_in = [], []
    for h in range(H):
        qb = (pcol(COL_MQ, h) * Q_SCALE).astype(BF16)
        qc_all.append(jnp.where(second, _dot(qb, c_ref[0, 1, h].astype(BF16)),
                                _dot(qb, c_ref[0, 0, h].astype(BF16))))
        cq = xc_s[:, h * 128:(h + 1) * 128]
        ck = xc_s[:, WIDTH + h * 128:WIDTH + (h + 1) * 128]
        qn = cq * _l2_scale(cq) * Q_SCALE
        kn = ck * _l2_scale(ck)
        kq = jnp.concatenate([kn, qn], axis=0).astype(BF16)
        r0 = _dot(kq, s_ref[0, 0, h].astype(BF16))
        r1 = _dot(kq, s_ref[0, 1, h].astype(BF16))
        gdn_in.append((qn, kn, jnp.where(second, r1[0:R], r0[0:R]), jnp.where(second, r1[R:2 * R], r0[R:2 * R])))

    c_upd, s_upd = [], []
    for h in range(H):
        li = _select_lane(gates, lane, LANE_MI + h)
        lf = _select_lane(lf_all, lane, LANE_MF + h)
        mh = _select_lane(m_in, lane_s, h)
        m0 = jnp.where(second, mh[1:2, :], mh[0:1, :])
        b = cumsum_t(lf)
        a = b + m0
        dcol = [b - pick(b, j) + pick(li, j) for j in range(T)]
        m_t = a
        for j in range(T):
            m_t = jnp.maximum(m_t, jnp.where(tpos >= j, dcol[j], -jnp.inf))
        inter = jnp.exp(a - m_t)

        q = pcol(COL_MQ, h) * Q_SCALE
        k = pcol(COL_MK, h)
        v = pcol(COL_MV, h)
        num = jnp.zeros((R, 128), F32)
        den = jnp.zeros((R, 1), F32)
        for j in range(T):
            w_j = jnp.where(tpos >= j, jnp.exp(jnp.where(tpos >= j, dcol[j] - m_t, 0.0)), 0.0)
            s_j = jnp.sum(q * pick(k, j), axis=1, keepdims=True) * w_j
            num = num + s_j * pick(v, j)
            den = den + s_j
        qc = qc_all[h]
        n_sel =jnp.where(second, n_ref[0, 1, h:h + 1, :], n_ref[0, 0, h:h + 1, :])
        num = num + inter * qc
        den = den + inter * jnp.sum(q * n_sel, axis=1, keepdims=True)
        hh = num / jnp.maximum(jnp.abs(den), jnp.exp(-m_t))
        mix_ref[0, :, h * 128:(h + 1) * 128] = hh * _rms_scale(hh) * mnorm_ref[h:h + 1, :] * _sigmoid(pcol(COL_MO, h))

        m_l = pick(m_t, T - 1)
        b_l = pick(b, T - 1)
        kw = jnp.exp(b_l - b + li - m_l) * k
        d_c = jnp.exp(b_l + m0 - m_l)
        c_upd.append((kw, d_c, v.astype(BF16)))
        for s in range(NS):
            kws = jnp.where(seq_rows[s], kw, 0.0)
            n_out_ref[0, s, h:h + 1, :] = (d_c[s * T:s * T + 1, :] * n_ref[0, s, h:h + 1, :]
                                           + jnp.sum(kws, axis=0, keepdims=True))
            m_acc = jnp.where(jnp.logical_and(lane_s == h, row_s == s), m_l[s * T:s * T + 1, :], m_acc)

        lg = _select_lane(lg_all, lane, LANE_GA + h)
        beta = _select_lane(beta_all, lane, LANE_GB + h)
        g = cumsum_t(lg)
        e_g = jnp.exp(g)
        cv = xc_s[:, 2 * WIDTH + h * 128:2 * WIDTH + (h + 1) * 128]
        qn, kn, k_s, q_s = gdn_in[h]
        x = beta * (cv - e_g * k_s)
        rel = [jnp.exp(jnp.where(tpos >= j, g - pick(g, j), 0.0)) for j in range(T)]
        for j in range(T - 1):
            kk_j = jnp.sum(kn * pick(kn, j), axis=1, keepdims=True)
            a_j = jnp.where(tpos > j, beta * kk_j * rel[j], 0.0)
            x = x - a_j * pick(x, j)
        o = e_g * q_s
        for j in range(T):
            qk_j = jnp.sum(qn * pick(kn, j), axis=1, keepdims=True)
            o = o + jnp.where(tpos >= j, qk_j * rel[j], 0.0) * pick(x, j)
        gz = pcol(COL_GZ, h)
        mix_ref[0, :, WIDTH + h * 128:WIDTH + (h + 1) * 128] = (
            o * _rms_scale(o) * gnorm_ref[h:h + 1, :] * (gz * _sigmoid(gz)))

        g_l = pick(g, T - 1)
        s_upd.append((kn * jnp.exp(g_l - g), jnp.exp(g_l), x.astype(BF16)))
    m_out_ref[0] = m_acc

    for h in range(H):
        for (lhs, scale, rhs), st_ref, out_ref in ((c_upd[h], c_ref, c_out_ref), (s_upd[h], s_ref, s_out_ref)):
            for s in range(NS):
                rows_s = jnp.where(seq_rows[s], lhs, 0.0).astype(BF16)
                out_ref[0, s, h] = scale[s * T:s * T + 1, :] * st_ref[0, s, h] + _dot_tn(rows_s, rhs)


def _sample_step(proj, gates, c, n, m, s, conv_buf, conv_w, mnorm, gnorm, *, steps):
    groups = c.shape[0]
    ns = SAMPLE_SEQS
    assert ns == 2 and ns * steps == 8 and steps >= CONV_W - 1 and steps & (steps - 1) == 0

    def per_group(*shape):
        return pl.BlockSpec((1,) + shape, lambda b: (b,) + (0,) * len(shape))

    def const(*shape):
        return pl.BlockSpec(shape, lambda b: (0,) * len(shape))

    state = (ns, N_HEADS, 128, 128)
    return pl.pallas_call(
        functools.partial(_sample_kernel, steps=steps),
        grid=(groups,),
        in_specs=[pl.BlockSpec((PROJ_COLS // 128, 1, ns * steps, 128), lambda b: (0, b, 0, 0)),
                  per_group(ns * steps, 128), per_group(*state), per_group(ns, N_HEADS, 128),
                  per_group(ns, 128), per_group(*state), per_group(ns, CONV_W - 1, 3 * WIDTH),
                  const(CONV_W, 3 * WIDTH), const(N_HEADS, 128), const(N_HEADS, 128)],
        out_specs=[per_group(ns * steps, 2 * WIDTH), per_group(*state), per_group(ns, N_HEADS, 128),
                   per_group(ns, 128), per_group(*state)],
        out_shape=[jax.ShapeDtypeStruct((groups, ns * steps, 2 * WIDTH), F32),
                   jax.ShapeDtypeStruct((groups,) + state, F32),
                   jax.ShapeDtypeStruct((groups, ns, N_HEADS, 128), F32),
                   jax.ShapeDtypeStruct((groups, ns, 128), F32),
                   jax.ShapeDtypeStruct((groups,) + state, F32)],
        scratch_shapes=[pltpu.VMEM((8 * ns, 3 * WIDTH), F32), pltpu.VMEM((ns * steps, 3 * WIDTH), F32)],
        compiler_params=pltpu.CompilerParams(dimension_semantics=("parallel",),
                                             vmem_limit_bytes=VMEM_LIMIT),
        name="sample_step",
    )(proj, gates, c, n, m, s, conv_buf, conv_w, mnorm, gnorm)


def kernel(x_prompt, x_sample, state_mlstm_C, state_mlstm_n, state_mlstm_m, state_gdn_S, state_gdn_conv, meta_tokens, norm_pre_mix, norm_post_mix, norm_pre_ffn, norm_post_ffn, w_in, mlstm_b_i, mlstm_b_f, mlstm_norm, gdn_conv_w, gdn_A_log, gdn_dt_bias, gdn_norm, w_out, w_gate_up, w_down):
    assert w_in.shape[0] == 1, "single-layer trunk"
    B, S, D = x_prompt.shape
    Bs, Ts, _ = x_sample.shape
    H = N_HEADS

    w_proj = _regroup_weights(jnp.transpose(w_in[0]))
    w_out_b = w_out[0].astype(BF16)
    w_gu_b = w_gate_up[0].astype(BF16)
    w_dn_b = w_down[0].astype(BF16)
    zeros_h = jnp.zeros((H,), F32)
    gbias = jnp.concatenate([mlstm_b_i[0], mlstm_b_f[0], gdn_dt_bias[0], zeros_h,
                             jnp.zeros((128 - 4 * H,), F32)]).reshape(1, 128)
    arow = jnp.concatenate([zeros_h, zeros_h, gdn_A_log[0], zeros_h,
                            jnp.zeros((128 - 4 * H,), F32)]).reshape(1, 128)
    g_pre_mix = norm_pre_mix[0].reshape(1, D)
    g_post_mix = norm_post_mix[0].reshape(1, D)
    g_pre_ffn = norm_pre_ffn[0].reshape(1, D)
    g_post_ffn = norm_post_ffn[0].reshape(1, D)
    mnorm = mlstm_norm[0]
    gnorm = gdn_norm[0]
    mnorm3 = mnorm.reshape(H, 1, 128)
    gnorm3 = gnorm.reshape(H, 1, 128)
    conv_w = gdn_conv_w[0]

    xp = x_prompt.reshape(B * S, D)
    xs = x_sample.reshape(Bs * Ts, D)

    nblk = PROJ_COLS // 128
    proj_p = _norm_matmul(xp, g_pre_mix, w_proj, tm=1024, tn=768).reshape(nblk, B, S, 128)
    n_sm = Bs * Ts + N_META
    proj_sm = _norm_matmul(jnp.concatenate([xs, meta_tokens], axis=0), g_pre_mix, w_proj, tm=n_sm, tn=768)
    proj_m = proj_sm[:, Bs * Ts:].reshape(nblk, 1, N_META, 128)

    gates_p = _gate_act(proj_p.reshape(nblk, B * S, 128), gbias, arow, tm=1024).reshape(B, S, 128)
    gates_sm = _gate_act(proj_sm, gbias, arow, tm=n_sm)
    gates_m = gates_sm[Bs * Ts:].reshape(1, N_META, 128)

    def conv_rows(blocks):
        return jnp.moveaxis(blocks, 0, -2).reshape(blocks.shape[1:-1] + (3 * WIDTH,))

    c0, n0, m0 = (jnp.zeros((1, H, 128, 128), F32), jnp.zeros((1, H, 1, 128), F32), jnp.zeros((1, H, 1, 128), F32))
    _, c_l, n_l, m_l = _mlstm_chunked(proj_m, gates_m, mnorm3, c0, n0, m0, chunk=N_META)
    prep_l = _gdn_prep(proj_m, gates_m, jnp.zeros((3 * H, 8, 128), F32), conv_w, chunk=N_META, nsub=1)
    _, s_l = _gdn_scan(prep_l, proj_m, gnorm3, c0, chunk=N_META)

    mix_a, p_c, p_n, p_m = _mlstm_chunked(proj_p, gates_p, mnorm3, c_l, n_l, m_l, chunk=MLSTM_CHUNK)
    conv_init = proj_m[COL_GQ:COL_GZ, 0, N_META - 8:, :]
    prep_p = _gdn_prep(proj_p, gates_p, conv_init, conv_w, chunk=GDN_CHUNK, nsub=GDN_PREP_CHUNKS)
    mix_b, p_s = _gdn_scan(prep_p, proj_p, gnorm3, s_l, chunk=GDN_CHUNK)
    p_conv = conv_rows(proj_p[COL_GQ:COL_GZ, :, S - (CONV_W - 1):, :])

    x1_p, u2_p = _out_proj(mix_a.reshape(H, B * S, 128), 0, mix_b.reshape(H, B * S, 128), 0, w_out_b, xp,
                           g_post_mix, g_pre_ffn, tm=512)
    y_p = _ffn(u2_p, w_gu_b, w_dn_b, x1_p, g_post_ffn, tm=512, tf=512)

    ns = SAMPLE_SEQS
    grp = Bs // ns
    m_in = jnp.pad(state_mlstm_m[0], ((0, 0), (0, 128 - H))).reshape(grp, ns, 128)
    mix_s, s_c, s_n, s_m, s_s = _sample_step(
        proj_sm.reshape(nblk, n_sm // (ns * Ts), ns * Ts, 128), gates_sm.reshape(n_sm // (ns * Ts), ns * Ts, 128),
        state_mlstm_C[0].reshape(grp, ns, H, 128, 128),
        state_mlstm_n[0].reshape(grp, ns, H, 128), m_in, state_gdn_S[0].reshape(grp, ns, H, 128, 128),
        state_gdn_conv[0].reshape(grp, ns, CONV_W - 1, 3 * WIDTH), conv_w, mnorm, gnorm, steps=Ts)
    mix_s2 = mix_s.reshape(Bs * Ts, 2 * WIDTH)
    s_conv = conv_rows(proj_sm[COL_GQ:COL_GZ, :Bs * Ts].reshape(3 * H, Bs, Ts, 128)[:, :, Ts - (CONV_W - 1):, :])
    x1_s, u2_s = _out_proj(mix_s2, 0, mix_s2, 1, w_out_b, xs, g_post_mix, g_pre_ffn, tm=512)
    y_s = _ffn(u2_s, w_gu_b, w_dn_b, x1_s, g_post_ffn, tm=512, tf=512)

    return (y_p.reshape(B, S, D), y_s.reshape(Bs, Ts, D),
            p_c[None], p_n[:, :, 0, :][None], p_m[:, :, 0, 0][None], p_s[None], p_conv[None],
            s_c.reshape(Bs, H, 128, 128)[None], s_n.reshape(Bs, H, 128)[None], s_m.reshape(Bs, 128)[:, :H][None],
            s_s.reshape(Bs, H, 128, 128)[None], s_conv[None])
```

```python
import functools

import jax
import jax.numpy as jnp
from jax import lax
from jax.experimental import pallas as pl
from jax.experimental.pallas import tpu as pltpu

F32 = jnp.float32
BF16 = jnp.bfloat16

D_MODEL = 2048
HEAD_DIM = 128
N_HEADS = 8
WIDTH = N_HEADS * HEAD_DIM
CONV_W = 4
N_META = 16
D_FF = 5632
EPS = 1e-6
Q_SCALE = HEAD_DIM ** -0.5

COL_MQ, COL_MK, COL_MV, COL_MO = 0, 8, 16, 24
COL_GQ, COL_GK, COL_GV, COL_GZ = 32, 40, 48, 56
COL_GATES = 64
PROJ_COLS = 66 * 128
LANE_MI, LANE_MF, LANE_GA, LANE_GB = 0, 8, 16, 24

MLSTM_CHUNK = 256
GDN_CHUNK = 64
GDN_PREP_CHUNKS = 16
GDN_SCAN_HEADS = 8
SAMPLE_SEQS = 2
VMEM_LIMIT = 56 * 1024 * 1024


def _sigmoid(x):
    return 0.5 * jnp.tanh(0.5 * x) + 0.5


def _softplus(x):
    return jnp.maximum(x, 0.0) + jnp.log1p(jnp.exp(-jnp.abs(x)))


def _log_sigmoid(x):
    return -_softplus(-x)


def _rms_scale(x):
    return lax.rsqrt(jnp.mean(x * x, axis=-1, keepdims=True) + EPS)


def _l2_scale(x):
    return lax.rsqrt(jnp.sum(x * x, axis=-1, keepdims=True) + EPS)


def _dot(a, b):
    return jnp.dot(a, b, preferred_element_type=F32)


def _dot_nt(a, b):
    return lax.dot_general(a, b, (((1,), (1,)), ((), ())), preferred_element_type=F32)


def _dot_tn(a, b):
    return lax.dot_general(a, b, (((0,), (0,)), ((), ())), preferred_element_type=F32)


def _dot_f32(a, b):
    return jnp.dot(a, b, precision=lax.Precision.HIGHEST, preferred_element_type=F32)


def _select_lane(x, lane_idx, target):
    return jnp.sum(jnp.where(lane_idx == target, x, 0.0), axis=1, keepdims=True)


REGROUP_TILE = 256


def _regroup_kernel(w_ref, mg_ref, gg_ref, o_ref):
    j = pl.program_id(0)
    wide_tiles = 2 * 4 * WIDTH // REGROUP_TILE

    @pl.when(j < wide_tiles)
    def _():
        o_ref[...] = w_ref[...].T.astype(BF16)

    @pl.when(j == wide_tiles)
    def _():
        pad = jnp.zeros((REGROUP_TILE - 4 * N_HEADS, w_ref.shape[1]), F32)
        o_ref[...] = jnp.concatenate([mg_ref[...], gg_ref[...], pad], axis=0).T.astype(BF16)


def _regroup_weights(w_t):
    n, d = w_t.shape
    wide = 4 * WIDTH
    narrow = 2 * N_HEADS
    assert n == 2 * (wide + narrow) and PROJ_COLS == 2 * wide + REGROUP_TILE
    wide_tiles = 2 * wide // REGROUP_TILE

    def src_row(j):
        j = jnp.minimum(j, wide_tiles - 1)
        return (j * (REGROUP_TILE // narrow) + jnp.where(j >= wide_tiles // 2, 1, 0)) * narrow

    return pl.pallas_call(
        _regroup_kernel,
        grid=(wide_tiles + 1,),
        in_specs=[pl.BlockSpec((pl.Element(REGROUP_TILE), pl.Element(d)), lambda j: (src_row(j), 0)),
                  pl.BlockSpec((narrow, d), lambda j: (wide // narrow, 0)),
                  pl.BlockSpec((narrow, d), lambda j: ((2 * wide + narrow) // narrow, 0))],
        out_specs=pl.BlockSpec((d, REGROUP_TILE), lambda j: (0, j)),
        out_shape=jax.ShapeDtypeStruct((d, PROJ_COLS), BF16),
        compiler_params=pltpu.CompilerParams(dimension_semantics=("parallel",), vmem_limit_bytes=VMEM_LIMIT),
        name="regroup_weights",
    )(w_t, w_t, w_t)


def _norm_matmul_kernel(x_ref, g_ref, w_ref, o_ref, u_ref):
    @pl.when(pl.program_id(1) == 0)
    def _():
        x = x_ref[...]
        u_ref[...] = (x * _rms_scale(x) * g_ref[...]).astype(BF16)

    res = _dot(u_ref[...], w_ref[...])
    for blk in range(o_ref.shape[0]):
        o_ref[blk] = res[:, blk * 128:(blk + 1) * 128]


def _norm_matmul(x, gain, w, *, tm, tn):
    m, d = x.shape
    n = w.shape[1]
    return pl.pallas_call(
        _norm_matmul_kernel,
        grid=(m // tm, n // tn),
        in_specs=[pl.BlockSpec((tm, d), lambda i, j: (i, 0)),
                  pl.BlockSpec((1, d), lambda i, j: (0, 0)),
                  pl.BlockSpec((d, tn), lambda i, j: (0, j))],
        out_specs=pl.BlockSpec((tn // 128, tm, 128), lambda i, j: (j, i, 0)),
        out_shape=jax.ShapeDtypeStruct((n // 128, m, 128), F32),
        scratch_shapes=[pltpu.VMEM((tm, d), BF16)],
        compiler_params=pltpu.CompilerParams(dimension_semantics=("parallel", "arbitrary"),
                                             vmem_limit_bytes=VMEM_LIMIT),
        name="norm_matmul",
    )(x, gain, w)


def _out_proj_kernel(ma_ref, mb_ref, wa_ref, wb_ref, x_ref, gpost_ref, gpre_ref, x1_ref, u2_ref):
    def rows(ref):
        if len(ref.shape) == 2:
            return ref[...].astype(BF16)
        return jnp.concatenate([ref[h] for h in range(ref.shape[0])], axis=1).astype(BF16)

    mo = _dot(rows(ma_ref), wa_ref[...]) + _dot(rows(mb_ref), wb_ref[...])
    x1 = x_ref[...] + mo * _rms_scale(mo) * gpost_ref[...]
    x1_ref[...] = x1
    u2_ref[...] = (x1 * _rms_scale(x1) * gpre_ref[...]).astype(BF16)


def _out_proj(mix_a, col_a, mix_b, col_b, w_out, x, g_post, g_pre, *, tm):
    m, d = x.shape

    def mix_spec(mix, col):
        if mix.ndim == 2:
            return pl.BlockSpec((tm, WIDTH), lambda i: (i, col))
        return pl.BlockSpec((N_HEADS, tm, 128), lambda i: (0, i, 0))

    return pl.pallas_call(
        _out_proj_kernel,
        grid=(m // tm,),
        in_specs=[mix_spec(mix_a, col_a),
                  mix_spec(mix_b, col_b),
                  pl.BlockSpec((WIDTH, d), lambda i: (0, 0)),
                  pl.BlockSpec((WIDTH, d), lambda i: (1, 0)),
                  pl.BlockSpec((tm, d), lambda i: (i, 0)),
                  pl.BlockSpec((1, d), lambda i: (0, 0)),
                  pl.BlockSpec((1, d), lambda i: (0, 0))],
        out_specs=[pl.BlockSpec((tm, d), lambda i: (i, 0)),
                   pl.BlockSpec((tm, d), lambda i: (i, 0))],
        out_shape=[jax.ShapeDtypeStruct((m, d), F32), jax.ShapeDtypeStruct((m, d), BF16)],
        compiler_params=pltpu.CompilerParams(dimension_semantics=("parallel",),
                                             vmem_limit_bytes=VMEM_LIMIT),
        name="out_proj",
    )(mix_a, mix_b, w_out, w_out, x, g_post, g_pre)


def _ffn_kernel(u_ref, wg_ref, wu_ref, wd_ref, x1_ref, gain_ref, o_ref, acc_ref):
    f = pl.program_id(1)

    @pl.when(f == 0)
    def _():
        acc_ref[...] = jnp.zeros_like(acc_ref)

    u = u_ref[...]
    g = _dot(u, wg_ref[...])
    up = _dot(u, wu_ref[...])
    hidden = (g * _sigmoid(g) * up).astype(BF16)
    acc_ref[...] += _dot(hidden, wd_ref[...])

    @pl.when(f == pl.num_programs(1) - 1)
    def _():
        y = acc_ref[...]
        o_ref[...] = x1_ref[...] + y * _rms_scale(y) * gain_ref[...]


def _ffn(u2, w_gate_up, w_down, x1, gain, *, tm, tf):
    m, d = x1.shape
    nf = D_FF // tf
    return pl.pallas_call(
        _ffn_kernel,
        grid=(m // tm, nf),
        in_specs=[pl.BlockSpec((tm, d), lambda i, f: (i, 0)),
                  pl.BlockSpec((d, tf), lambda i, f: (0, f)),
                  pl.BlockSpec((d, tf), lambda i, f: (0, nf + f)),
                  pl.BlockSpec((tf, d), lambda i, f: (f, 0)),
                  pl.BlockSpec((tm, d), lambda i, f: (i, 0)),
                  pl.BlockSpec((1, d), lambda i, f: (0, 0))],
        out_specs=pl.BlockSpec((tm, d), lambda i, f: (i, 0)),
        out_shape=jax.ShapeDtypeStruct((m, d), F32),
        scratch_shapes=[pltpu.VMEM((tm, d), F32)],
        compiler_params=pltpu.CompilerParams(dimension_semantics=("parallel", "arbitrary"),
                                             vmem_limit_bytes=VMEM_LIMIT),
        name="ffn",
    )(u2, w_gate_up, w_gate_up, w_down, x1, gain)


def _gate_act_kernel(g_ref, gbias_ref, arow_ref, o_ref):
    g = g_ref[0] + gbias_ref[...]
    lane = lax.broadcasted_iota(jnp.int32, g.shape, 1)
    decay = -jnp.exp(arow_ref[...]) * _softplus(g)
    o_ref[...] = jnp.where(lane < LANE_MF, g,
                           jnp.where(lane < LANE_GA, _log_sigmoid(g), jnp.where(lane < LANE_GB, decay, _sigmoid(g))))


def _gate_act(proj, gbias, arow, *, tm):
    m = proj.shape[1]
    return pl.pallas_call(
        _gate_act_kernel,
        grid=(m // tm,),
        in_specs=[pl.BlockSpec((1, tm, 128), lambda i: (COL_GATES, i, 0)),
                  pl.BlockSpec((1, 128), lambda i: (0, 0)),
                  pl.BlockSpec((1, 128), lambda i: (0, 0))],
        out_specs=pl.BlockSpec((tm, 128), lambda i: (i, 0)),
        out_shape=jax.ShapeDtypeStruct((m, 128), F32),
        compiler_params=pltpu.CompilerParams(dimension_semantics=("parallel",), vmem_limit_bytes=VMEM_LIMIT),
        name="gate_act",
    )(proj, gbias, arow)


def _mlstm_chunk_kernel(q_ref, k_ref, v_ref, og_ref, gates_ref, mnorm_ref, c0_ref, n0_ref, m0_ref,
                        h_ref, c_out_ref, n_out_ref, m_out_ref, c_s, n_s, m_s, *, chunk):
    L = chunk
    nb = q_ref.shape[1]
    shared = c0_ref.shape[0] == 1
    head = pl.program_id(0)
    c = pl.program_id(1)

    @pl.when(c == 0)
    def _():
        for b in range(nb):
            b0 = 0 if shared else b
            c_s[b] = c0_ref[b0, 0]
            n_s[b] = n0_ref[b0, 0]
            m_s[b] = m0_ref[b0, 0]

    lane = lax.broadcasted_iota(jnp.int32, (L, 128), 1)
    rows = lax.broadcasted_iota(jnp.int32, (L, L), 0)
    cols = lax.broadcasted_iota(jnp.int32, (L, L), 1)
    tril = rows >= cols
    triu = rows <= cols
    eye = rows == cols
    last = c == pl.num_programs(1) - 1

    seqs = range(nb)
    gates = [gates_ref[b] for b in seqs]
    li_col = [_select_lane(g, lane, head + LANE_MI) for g in gates]
    lf_col = [_select_lane(g, lane, head + LANE_MF) for g in gates]
    lf_row = [jnp.sum(jnp.where(eye, x, 0.0), axis=0, keepdims=True) for x in lf_col]
    li_row = [jnp.sum(jnp.where(eye, x, 0.0), axis=0, keepdims=True) for x in li_col]
    b_col = [jnp.sum(jnp.where(tril, x, 0.0), axis=1, keepdims=True) for x in lf_row]
    b_row = [jnp.sum(jnp.where(triu, x, 0.0), axis=0, keepdims=True) for x in lf_col]

    qb = [(q_ref[0, b] * Q_SCALE).astype(BF16) for b in seqs]
    kb = [k_ref[0, b].astype(BF16) for b in seqs]
    vb = [v_ref[0, b].astype(BF16) for b in seqs]
    qk = [_dot_nt(x, y) for x, y in zip(qb, kb)]
    qc = [_dot(x, c_s[b].astype(BF16)) for b, x in zip(seqs, qb)]

    m_prev = [m_s[b, :, 0:1] for b in seqs]
    a_col = [x + y for x, y in zip(b_col, m_prev)]
    dm = [jnp.where(tril, bc - br + lr, -jnp.inf) for bc, br, lr in zip(b_col, b_row, li_row)]
    m_t = [jnp.maximum(a, jnp.max(d, axis=1, keepdims=True)) for a, d in zip(a_col, dm)]
    inter = [jnp.exp(a - m) for a, m in zip(a_col, m_t)]
    s = [x * jnp.exp(d - m) for x, d, m in zip(qk, dm, m_t)]
    sv = [_dot(x.astype(BF16), v) for x, v in zip(s, vb)]

    m_last = [m[L - 1:L, :] for m in m_t]
    b_last = [x[L - 1:L, :] for x in b_col]
    d_c = [jnp.exp(bl + mp - ml) for bl, mp, ml in zip(b_last, m_prev, m_last)]
    kw = [jnp.exp(bl - bc + li - ml) * k_ref[0, b]
          for b, bl, bc, li, ml in zip(seqs, b_last, b_col, li_col, m_last)]
    c_upd = [_dot_tn(x.astype(BF16), v) for x, v in zip(kw, vb)]

    for b in seqs:
        q = q_ref[0, b] * Q_SCALE
        num = sv[b] + inter[b] * qc[b]
        den = jnp.sum(s[b], axis=1, keepdims=True) + inter[b] * jnp.sum(q * n_s[b], axis=1, keepdims=True)
        hh = num / jnp.maximum(jnp.abs(den), jnp.exp(-m_t[b]))
        out = hh * _rms_scale(hh) * mnorm_ref[0] * _sigmoid(og_ref[0, b])
        h_ref[0, b] = out.astype(h_ref.dtype)
        c_new = d_c[b] * c_s[b] + c_upd[b]
        n_new = d_c[b] * n_s[b] + jnp.sum(kw[b], axis=0, keepdims=True)
        m_new = jnp.broadcast_to(m_last[b], (1, 128))
        c_s[b], n_s[b], m_s[b] = c_new, n_new, m_new
        c_out_ref[b, 0], n_out_ref[b, 0], m_out_ref[b, 0] = c_new, n_new, m_new


def _mlstm_chunked(proj, gates, mnorm, c0, n0, m0, *, chunk):
    _, batch, seq, _ = proj.shape
    nb0 = c0.shape[0]

    def col(base):
        return pl.BlockSpec((1, batch, chunk, 128), lambda h, c: (base + h, 0, c, 0))

    def state(nb, shape):
        return pl.BlockSpec((nb, 1) + shape, lambda h, c: (0, h, 0, 0))

    return dict(
        body=functools.partial(_mlstm_chunk_kernel, chunk=chunk),
        grid=(N_HEADS, seq // chunk),
        in_specs=[col(COL_MQ), col(COL_MK), col(COL_MV), col(COL_MO),
                  pl.BlockSpec((batch, chunk, 128), lambda h, c: (0, c, 0)),
                  pl.BlockSpec((1, 1, 128), lambda h, c: (h, 0, 0)),
                  state(nb0, (128, 128)), state(nb0, (1, 128)), state(nb0, (1, 128))],
        out_specs=[pl.BlockSpec((1, batch, chunk, 128), lambda h, c: (h, 0, c, 0)),
                   state(batch, (128, 128)), state(batch, (1, 128)), state(batch, (1, 128))],
        out_shape=[jax.ShapeDtypeStruct((N_HEADS, batch, seq, 128), BF16),
                   jax.ShapeDtypeStruct((batch, N_HEADS, 128, 128), F32),
                   jax.ShapeDtypeStruct((batch, N_HEADS, 1, 128), F32),
                   jax.ShapeDtypeStruct((batch, N_HEADS, 1, 128), F32)],
        scratch_shapes=[pltpu.VMEM((batch, 128, 128), F32), pltpu.VMEM((batch, 1, 128), F32),
                        pltpu.VMEM((batch, 1, 128), F32)],
        args=(proj, proj, proj, proj, gates, mnorm, c0, n0, m0))


def _run_parts(parts, *, semantics, name):
    grid = parts[0]["grid"]
    assert all(p["grid"] == grid for p in parts)
    n_in = [len(p["in_specs"]) for p in parts]
    n_out = [len(p["out_specs"]) for p in parts]
    n_scr = [len(p["scratch_shapes"]) for p in parts]

    def body(*refs):
        ins, outs, scr = refs[:sum(n_in)], refs[sum(n_in):sum(n_in) + sum(n_out)], refs[sum(n_in) + sum(n_out):]
        for k, p in enumerate(parts):
            p["body"](*ins[sum(n_in[:k]):sum(n_in[:k + 1])], *outs[sum(n_out[:k]):sum(n_out[:k + 1])],
                      *scr[sum(n_scr[:k]):sum(n_scr[:k + 1])])

    outs = pl.pallas_call(
        body,
        grid=grid,
        in_specs=[s for p in parts for s in p["in_specs"]],
        out_specs=[s for p in parts for s in p["out_specs"]],
        out_shape=[s for p in parts for s in p["out_shape"]],
        scratch_shapes=[s for p in parts for s in p["scratch_shapes"]],
        compiler_params=pltpu.CompilerParams(dimension_semantics=semantics, vmem_limit_bytes=VMEM_LIMIT),
        name=name,
    )(*[a for p in parts for a in p["args"]])
    return [outs[sum(n_out[:k]):sum(n_out[:k + 1])] for k in range(len(parts))]


def _split(x):
    hi = x.astype(BF16)
    return hi, (x - hi.astype(F32)).astype(BF16)


def _dot_split(a, b):
    a_hi, a_lo = a
    b_hi, b_lo = b if isinstance(b, tuple) else (b, None)
    if a_hi.shape[1] % 128 == 0:
        main = _dot(jnp.concatenate([a_hi, a_lo], axis=1), jnp.concatenate([b_hi, b_hi], axis=0))
    else:
        main = _dot(a_hi, b_hi) + _dot(a_lo, b_hi)
    return main if b_lo is None else main + _dot(a_hi, b_lo)


def _unit_lower_inverse(mats, rows, cols, size):
    def same_block(shift):
        return (rows >> shift) == (cols >> shift)

    ident = jnp.where(rows == cols, 1.0, 0.0)
    d = [jnp.where(same_block(3), a, 0.0) for a in mats]
    d_s = [_split(x) for x in d]
    d2_s = [_split(_dot_split(x, x)) for x in d_s]
    d4_s = [_split(_dot_split(x, x)) for x in d2_s]
    t = [ident - x for x in d]
    t = [x + _dot_split(_split(x), y) for x, y in zip(t, d2_s)]
    t = [x + _dot_split(_split(x), y) for x, y in zip(t, d4_s)]
    a_hi = [x.astype(BF16) for x in mats]
    shift = 3
    while (1 << shift) < size:
        off = jnp.logical_and(same_block(shift + 1), jnp.logical_not(same_block(shift)))
        off_bf = jnp.where(off, 1.0, 0.0).astype(BF16)
        t_s = [_split(x) for x in t]
        te_s = [_split(_dot_split(x, y * off_bf)) for x, y in zip(t_s, a_hi)]
        t = [x - _dot_split(y, z[0]) for x, y, z in zip(t, te_s, t_s)]
        shift += 1
    return t


def _gdn_prep_kernel(q_ref, k_ref, v_ref, pq_ref, pk_ref, pv_ref, iq_ref, ik_ref, iv_ref, wq_ref, wk_ref, wv_ref,
                     gates_ref,
                     u_ref, w_ref, qe_ref, kd_ref, attn_ref, glast_ref, ext, *, chunk, nsub):
    L = chunk
    R = nsub * L
    head = pl.program_id(1)
    first = pl.program_id(2) == 0

    def conv(x_ref, prev_ref, init_ref, cw_ref):
        prev = jnp.where(first, init_ref[0], prev_ref[0, 0])
        ext[0:8, :] = prev
        ext[8:8 + R, :] = x_ref[0, 0]
        base = 8 - (CONV_W - 1)
        acc = ext[base:base + R, :] * cw_ref[0:1, :]
        for i in range(1, CONV_W):
            acc = acc + ext[base + i:base + i + R, :] * cw_ref[i:i + 1, :]
        return acc * _sigmoid(acc)

    q = conv(q_ref, pq_ref, iq_ref, wq_ref)
    k = conv(k_ref, pk_ref, ik_ref, wk_ref)
    v_all = conv(v_ref, pv_ref, iv_ref, wv_ref)
    qn_all = q * _l2_scale(q) * Q_SCALE
    kn_all = k * _l2_scale(k)

    gates = gates_ref[0]
    lane = lax.broadcasted_iota(jnp.int32, (R, 128), 1)
    lg_all = _select_lane(gates, lane, head + LANE_GA)
    beta_all = _select_lane(gates, lane, head + LANE_GB)

    U = min(R, 128)
    per_unit = U // L
    rows = lax.broadcasted_iota(jnp.int32, (U, U), 0)
    cols = lax.broadcasted_iota(jnp.int32, (U, U), 1)
    shift = L.bit_length() - 1
    same_chunk = (rows >> shift) == (cols >> shift)
    causal = jnp.logical_and(rows >= cols, same_chunk)
    strict = jnp.logical_and(rows > cols, same_chunk)
    upper = jnp.logical_and(rows <= cols, same_chunk)
    eye = rows == cols
    row_id = lax.broadcasted_iota(jnp.int32, (U, 1), 0)

    units = range(R // U)
    sls = [slice(j * U, (j + 1) * U) for j in units]
    g_cols, decays, a_mats, qks = [], [], [], []
    for sl in sls:
        lg_col = lg_all[sl]
        lg_row = jnp.sum(jnp.where(eye, lg_col, 0.0), axis=0, keepdims=True)
        g_col = jnp.sum(jnp.where(causal, lg_row, 0.0), axis=1, keepdims=True)
        g_row = jnp.sum(jnp.where(upper, lg_col, 0.0), axis=0, keepdims=True)
        decay = jnp.where(causal, jnp.exp(jnp.where(causal, g_col - g_row, 0.0)), 0.0)
        kb = kn_all[sl].astype(BF16)
        kk = _dot_nt(kb, kb)
        qks.append(_dot_nt(qn_all[sl].astype(BF16), kb))
        a_mats.append(jnp.where(strict, beta_all[sl] * kk * decay, 0.0))
        g_cols.append(g_col)
        decays.append(decay)

    t_mats = _unit_lower_inverse(a_mats, rows, cols, L)

    e_gs = [jnp.exp(g) for g in g_cols]
    rhs_s = [_split(jnp.concatenate([beta_all[sl] * v_all[sl], (beta_all[sl] * e_g) * kn_all[sl]], axis=1))
             for sl, e_g in zip(sls, e_gs)]
    uws = [_dot_split(_split(t), r) for t, r in zip(t_mats, rhs_s)]

    for j in units:
        sl, g_col = sls[j], g_cols[j]
        u_ref[0, 0, sl, :] = uws[j][:, 0:128]
        w_ref[0, 0, sl, :] = uws[j][:, 128:256].astype(BF16)
        qe_ref[0, 0, sl, :] = (qn_all[sl] * e_gs[j]).astype(BF16)
        attn = qks[j] * decays[j]
        g_last_col = jnp.zeros((U, 1), F32)
        for i in range(per_unit):
            g_last = g_col[(i + 1) * L - 1:(i + 1) * L, :]
            g_last_col = jnp.where((row_id >> shift) == i, g_last, g_last_col)
            glast_ref[0, 0, j * per_unit + i] = jnp.broadcast_to(jnp.exp(g_last), (1, 128))
            diag = attn[i * L:(i + 1) * L, :]
            if i > 0:
                diag = pltpu.roll(diag, U - i * L, axis=1)
            attn_ref[0, 0, j * per_unit + i] = diag[:, 0:L].astype(BF16)
        kd_ref[0, 0, sl, :] = (kn_all[sl] * jnp.exp(g_last_col - g_col)).astype(BF16)


def _gdn_prep(proj, gates, conv_init, conv_w, *, chunk, nsub):
    _, batch, seq, _ = proj.shape
    nc = seq // chunk
    R = nsub * chunk

    def col(base):
        return pl.BlockSpec((1, 1, R, 128), lambda b, h, c: (base + h, b, c, 0))

    def prev(base):
        return pl.BlockSpec((1, 1, 8, 128), lambda b, h, c: (base + h, b, jnp.maximum(c * (R // 8) - 1, 0), 0))

    def init(base):
        return pl.BlockSpec((1, 8, 128), lambda b, h, c: (base + h, 0, 0))

    def cw(base):
        return pl.BlockSpec((CONV_W, 128), lambda b, h, c: (0, base + h))

    def per_token(dtype):
        return (pl.BlockSpec((1, 1, R, 128), lambda b, h, c: (b, h, c, 0)),
                jax.ShapeDtypeStruct((batch, N_HEADS, seq, 128), dtype))

    outs = [per_token(F32), per_token(BF16), per_token(BF16), per_token(BF16),
            (pl.BlockSpec((1, 1, nsub, chunk, chunk), lambda b, h, c: (b, h, c, 0, 0)),
             jax.ShapeDtypeStruct((batch, N_HEADS, nc, chunk, chunk), BF16)),
            (pl.BlockSpec((1, 1, nsub, 1, 128), lambda b, h, c: (b, h, c, 0, 0)),
             jax.ShapeDtypeStruct((batch, N_HEADS, nc, 1, 128), F32))]
    return pl.pallas_call(
        functools.partial(_gdn_prep_kernel, chunk=chunk, nsub=nsub),
        grid=(batch, N_HEADS, nc // nsub),
        in_specs=[col(COL_GQ), col(COL_GK), col(COL_GV), prev(COL_GQ), prev(COL_GK), prev(COL_GV),
                  init(0), init(8), init(16), cw(0), cw(8), cw(16),
                  pl.BlockSpec((1, R, 128), lambda b, h, c: (b, c, 0))],
        out_specs=[o[0] for o in outs],
        out_shape=[o[1] for o in outs],
        scratch_shapes=[pltpu.VMEM((8 + R, 128), F32)],
        compiler_params=pltpu.CompilerParams(dimension_semantics=("parallel", "parallel", "parallel"),
                                             vmem_limit_bytes=VMEM_LIMIT),
        name="gdn_prep",
    )(proj, proj, proj, proj, proj, proj, conv_init, conv_init, conv_init, conv_w, conv_w, conv_w,
      gates)


def _gdn_scan_kernel(u_ref, w_ref, qe_ref, kd_ref, attn_ref, glast_ref, gz_ref, gnorm_ref, s0_ref,
                     o_ref, s_out_ref, s_s):
    nb, nh = u_ref.shape[0], u_ref.shape[1]
    shared = s0_ref.shape[0] == 1
    c = pl.program_id(1)
    last = c == pl.num_programs(1) - 1
    chains = [(b, h) for b in range(nb) for h in range(nh)]

    @pl.when(c == 0)
    def _():
        for b, h in chains:
            s_s[b, h] = s0_ref[0 if shared else b, h]

    L = u_ref.shape[2]
    s_prev = [s_s[b, h] for b, h in chains]
    ws = [_dot(jnp.concatenate([w_ref[b, h], qe_ref[b, h]], axis=0), sp.astype(BF16))
          for (b, h), sp in zip(chains, s_prev)]
    vb = [(u_ref[b, h] - r[0:L]).astype(BF16) for (b, h), r in zip(chains, ws)]
    intra = [_dot(attn_ref[b, h, 0], v) for (b, h), v in zip(chains, vb)]
    upd = [_dot_tn(kd_ref[b, h], v) for (b, h), v in zip(chains, vb)]
    for i, (b, h) in enumerate(chains):
        s_s[b, h] = glast_ref[b, h, 0][:, 0:1] * s_prev[i] + upd[i]
        o = ws[i][L:2 * L] + intra[i]
        gz = gz_ref[h, b]
        o_ref[h, b] = (o * _rms_scale(o) * gnorm_ref[h] * (gz * _sigmoid(gz))).astype(o_ref.dtype)

    @pl.when(last)
    def _():
        for b, h in chains:
            s_out_ref[b, h] = s_s[b, h]


def _gdn_scan(prep, proj, gnorm, s0, *, chunk):
    _, batch, seq, _ = proj.shape
    u, w, qe, kd, attn, glast = prep
    hb = GDN_SCAN_HEADS
    assert COL_GZ % hb == 0 and N_HEADS % hb == 0
    tok = pl.BlockSpec((batch, hb, chunk, 128), lambda h, c: (0, h, c, 0))
    return pl.pallas_call(
        _gdn_scan_kernel,
        grid=(N_HEADS // hb, seq // chunk),
        in_specs=[tok, tok, tok, tok,
                  pl.BlockSpec((batch, hb, 1, chunk, chunk), lambda h, c: (0, h, c, 0, 0)),
                  pl.BlockSpec((batch, hb, 1, 1, 128), lambda h, c: (0, h, c, 0, 0)),
                  pl.BlockSpec((hb, batch, chunk, 128), lambda h, c: (COL_GZ // hb + h, 0, c, 0)),
                  pl.BlockSpec((hb, 1, 128), lambda h, c: (h, 0, 0)),
                  pl.BlockSpec((s0.shape[0], hb, 128, 128), lambda h, c: (0, h, 0, 0))],
        out_specs=[pl.BlockSpec((hb, batch, chunk, 128), lambda h, c: (h, 0, c, 0)),
                   pl.BlockSpec((batch, hb, 128, 128), lambda h, c: (0, h, 0, 0))],
        out_shape=[jax.ShapeDtypeStruct((N_HEADS, batch, seq, 128), BF16),
                   jax.ShapeDtypeStruct((batch, N_HEADS, 128, 128), F32)],
        scratch_shapes=[pltpu.VMEM((batch, hb, 128, 128), F32)],
        compiler_params=pltpu.CompilerParams(dimension_semantics=("parallel", "arbitrary"),
                                             vmem_limit_bytes=VMEM_LIMIT),
        name="gdn_scan",
    )(u, w, qe, kd, attn, glast, proj, gnorm, s0)


def _sample_kernel(p_ref, gates_ref, c_ref, n_ref, m_ref, s_ref, cb_ref, cw_ref, mnorm_ref, gnorm_ref,
                   mix_ref, c_out_ref, n_out_ref, m_out_ref, s_out_ref, ext, xc_s, *, steps):
    T = steps
    NS = SAMPLE_SEQS
    R = NS * T
    H = N_HEADS
    nbuf = CONV_W - 1

    for s in range(NS):
        ext[8 * s:8 * s + nbuf, :] = cb_ref[0, s]
        for blk in range(COL_GZ - COL_GQ):
            ext[8 * s + nbuf:8 * s + nbuf + T, blk * 128:(blk + 1) * 128] = p_ref[COL_GQ + blk, 0, s * T:(s + 1) * T, :]
        acc = ext[8 * s:8 * s + T, :] * cw_ref[0:1, :]
        for i in range(1, CONV_W):
            acc = acc + ext[8 * s + i:8 * s + i + T, :] * cw_ref[i:i + 1, :]
        xc_s[s * T:(s + 1) * T, :] = acc * _sigmoid(acc)

    rowi = lax.broadcasted_iota(jnp.int32, (R, 1), 0)
    tpos = rowi & (T - 1)
    second = rowi >= T
    seq_rows = [jnp.logical_not(second), second]

    def pick(x, j):
        return jnp.where(second, x[T + j:T + j + 1, :], x[j:j + 1, :])

    def cumsum_t(x):
        acc = jnp.where(tpos >= 0, pick(x, 0), 0.0)
        for j in range(1, T):
            acc = acc + jnp.where(tpos >= j, pick(x, j), 0.0)
        return acc

    def pcol(block, h):
        return p_ref[block + h, 0]

    gates = gates_ref[0]
    lane = lax.broadcasted_iota(jnp.int32, (R, 128), 1)
    lf_all = lg_all = beta_all = gates
    m_in = m_ref[0]
    lane_s = lax.broadcasted_iota(jnp.int32, (NS, 128), 1)
    row_s = lax.broadcasted_iota(jnp.int32, (NS, 128), 0)
    m_acc = jnp.zeros((NS, 128), F32)

    qc_all, gdn_in = [], []
    for h in range(H):
        qb = (pcol(COL_MQ, h) * Q_SCALE).astype(BF16)
        qc_all.append(jnp.where(second, _dot(qb, c_ref[0, 1, h].astype(BF16)),
                                _dot(qb, c_ref[0, 0, h].astype(BF16))))
        cq = xc_s[:, h * 128:(h + 1) * 128]
        ck = xc_s[:, WIDTH + h * 128:WIDTH + (h + 1) * 128]
        qn = cq * _l2_scale(cq) * Q_SCALE
        kn = ck * _l2_scale(ck)
        kq = jnp.concatenate([kn, qn], axis=0).astype(BF16)
        r0 = _dot(kq, s_ref[0, 0, h].astype(BF16))
        r1 = _dot(kq, s_ref[0, 1, h].astype(BF16))
        gdn_in.append((qn, kn, jnp.where(second, r1[0:R], r0[0:R]), jnp.where(second, r1[R:2 * R], r0[R:2 * R])))

    c_upd, s_upd = [], []
    for h in range(H):
        li = _select_lane(gates, lane, LANE_MI + h)
        lf = _select_lane(lf_all, lane, LANE_MF + h)
        mh = _select_lane(m_in, lane_s, h)
        m0 = jnp.where(second, mh[1:2, :], mh[0:1, :])
        b = cumsum_t(lf)
        a = b + m0
        dcol = [b - pick(b, j) + pick(li, j) for j in range(T)]
        m_t = a
        for j in range(T):
            m_t = jnp.maximum(m_t, jnp.where(tpos >= j, dcol[j], -jnp.inf))
        inter = jnp.exp(a - m_t)

        q = pcol(COL_MQ, h) * Q_SCALE
        k = pcol(COL_MK, h)
        v = pcol(COL_MV, h)
        num = jnp.zeros((R, 128), F32)
        den = jnp.zeros((R, 1), F32)
        for j in range(T):
            w_j = jnp.where(tpos >= j, jnp.exp(jnp.where(tpos >= j, dcol[j] - m_t, 0.0)), 0.0)
            s_j = jnp.sum(q * pick(k, j), axis=1, keepdims=True) * w_j
            num = num + s_j * pick(v, j)
            den = den + s_j
        qc = qc_all[h]
        n_sel =jnp.where(second, n_ref[0, 1, h:h + 1, :], n_ref[0, 0, h:h + 1, :])
        num = num + inter * qc
        den = den + inter * jnp.sum(q * n_sel, axis=1, keepdims=True)
        hh = num / jnp.maximum(jnp.abs(den), jnp.exp(-m_t))
        mix_ref[0, :, h * 128:(h + 1) * 128] = hh * _rms_scale(hh) * mnorm_ref[h:h + 1, :] * _sigmoid(pcol(COL_MO, h))

        m_l = pick(m_t, T - 1)
        b_l = pick(b, T - 1)
        kw = jnp.exp(b_l - b + li - m_l) * k
        d_c = jnp.exp(b_l + m0 - m_l)
        c_upd.append((kw, d_c, v.astype(BF16)))
        for s in range(NS):
            kws = jnp.where(seq_rows[s], kw, 0.0)
            n_out_ref[0, s, h:h + 1, :] = (d_c[s * T:s * T + 1, :] * n_ref[0, s, h:h + 1, :]
                                           + jnp.sum(kws, axis=0, keepdims=True))
            m_acc = jnp.where(jnp.logical_and(lane_s == h, row_s == s), m_l[s * T:s * T + 1, :], m_acc)

        lg = _select_lane(lg_all, lane, LANE_GA + h)
        beta = _select_lane(beta_all, lane, LANE_GB + h)
        g = cumsum_t(lg)
        e_g = jnp.exp(g)
        cv = xc_s[:, 2 * WIDTH + h * 128:2 * WIDTH + (h + 1) * 128]
        qn, kn, k_s, q_s = gdn_in[h]
        x = beta * (cv - e_g * k_s)
        rel = [jnp.exp(jnp.where(tpos >= j, g - pick(g, j), 0.0)) for j in range(T)]
        for j in range(T - 1):
            kk_j = jnp.sum(kn * pick(kn, j), axis=1, keepdims=True)
            a_j = jnp.where(tpos > j, beta * kk_j * rel[j], 0.0)
            x = x - a_j * pick(x, j)
        o = e_g * q_s
        for j in range(T):
            qk_j = jnp.sum(qn * pick(kn, j), axis=1, keepdims=True)
            o = o + jnp.where(tpos >= j, qk_j * rel[j], 0.0) * pick(x, j)
        gz = pcol(COL_GZ, h)
        mix_ref[0, :, WIDTH + h * 128:WIDTH + (h + 1) * 128] = (
            o * _rms_scale(o) * gnorm_ref[h:h + 1, :] * (gz * _sigmoid(gz)))

        g_l = pick(g, T - 1)
        s_upd.append((kn * jnp.exp(g_l - g), jnp.exp(g_l), x.astype(BF16)))
    m_out_ref[0] = m_acc

    for h in range(H):
        for (lhs, scale, rhs), st_ref, out_ref in ((c_upd[h], c_ref, c_out_ref), (s_upd[h], s_ref, s_out_ref)):
            for s in range(NS):
                rows_s = jnp.where(seq_rows[s], lhs, 0.0).astype(BF16)
                out_ref[0, s, h] = scale[s * T:s * T + 1, :] * st_ref[0, s, h] + _dot_tn(rows_s, rhs)


def _sample_step(proj, gates, c, n, m, s, conv_buf, conv_w, mnorm, gnorm, *, steps, grid):
    groups = c.shape[0]
    ns = SAMPLE_SEQS
    assert ns == 2 and ns * steps == 8 and steps >= CONV_W - 1 and steps & (steps - 1) == 0
    assert len(grid) == 2 and grid[0] * grid[1] == groups

    def per_group(*shape):
        return pl.BlockSpec((1,) + shape, lambda i, j: (i * grid[1] + j,) + (0,) * len(shape))

    def const(*shape):
        return pl.BlockSpec(shape, lambda i, j: (0,) * len(shape))

    state = (ns, N_HEADS, 128, 128)
    return dict(
        body=functools.partial(_sample_kernel, steps=steps),
        grid=grid,
        in_specs=[pl.BlockSpec((PROJ_COLS // 128, 1, ns * steps, 128), lambda i, j: (0, i * grid[1] + j, 0, 0)),
                  per_group(ns * steps, 128), per_group(*state), per_group(ns, N_HEADS, 128),
                  per_group(ns, 128), per_group(*state), per_group(ns, CONV_W - 1, 3 * WIDTH),
                  const(CONV_W, 3 * WIDTH), const(N_HEADS, 128), const(N_HEADS, 128)],
        out_specs=[per_group(ns * steps, 2 * WIDTH), per_group(*state), per_group(ns, N_HEADS, 128),
                   per_group(ns, 128), per_group(*state)],
        out_shape=[jax.ShapeDtypeStruct((groups, ns * steps, 2 * WIDTH), F32),
                   jax.ShapeDtypeStruct((groups,) + state, F32),
                   jax.ShapeDtypeStruct((groups, ns, N_HEADS, 128), F32),
                   jax.ShapeDtypeStruct((groups, ns, 128), F32),
                   jax.ShapeDtypeStruct((groups,) + state, F32)],
        scratch_shapes=[pltpu.VMEM((8 * ns, 3 * WIDTH), F32), pltpu.VMEM((ns * steps, 3 * WIDTH), F32)],
        args=(proj, gates, c, n, m, s, conv_buf, conv_w, mnorm, gnorm))


def kernel(x_prompt, x_sample, state_mlstm_C, state_mlstm_n, state_mlstm_m, state_gdn_S, state_gdn_conv, meta_tokens, norm_pre_mix, norm_post_mix, norm_pre_ffn, norm_post_ffn, w_in, mlstm_b_i, mlstm_b_f, mlstm_norm, gdn_conv_w, gdn_A_log, gdn_dt_bias, gdn_norm, w_out, w_gate_up, w_down):
    assert w_in.shape[0] == 1, "single-layer trunk"
    B, S, D = x_prompt.shape
    Bs, Ts, _ = x_sample.shape
    H = N_HEADS

    w_proj = _regroup_weights(jnp.transpose(w_in[0]))
    w_out_b = w_out[0].astype(BF16)
    w_gu_b = w_gate_up[0].astype(BF16)
    w_dn_b = w_down[0].astype(BF16)
    zeros_h = jnp.zeros((H,), F32)
    gbias = jnp.concatenate([mlstm_b_i[0], mlstm_b_f[0], gdn_dt_bias[0], zeros_h,
                             jnp.zeros((128 - 4 * H,), F32)]).reshape(1, 128)
    arow = jnp.concatenate([zeros_h, zeros_h, gdn_A_log[0], zeros_h,
                            jnp.zeros((128 - 4 * H,), F32)]).reshape(1, 128)
    g_pre_mix = norm_pre_mix[0].reshape(1, D)
    g_post_mix = norm_post_mix[0].reshape(1, D)
    g_pre_ffn = norm_pre_ffn[0].reshape(1, D)
    g_post_ffn = norm_post_ffn[0].reshape(1, D)
    mnorm = mlstm_norm[0]
    gnorm = gdn_norm[0]
    mnorm3 = mnorm.reshape(H, 1, 128)
    gnorm3 = gnorm.reshape(H, 1, 128)
    conv_w = gdn_conv_w[0]

    xp = x_prompt.reshape(B * S, D)
    xs = x_sample.reshape(Bs * Ts, D)

    nblk = PROJ_COLS // 128
    proj_p = _norm_matmul(xp, g_pre_mix, w_proj, tm=1024, tn=768).reshape(nblk, B, S, 128)
    n_sm = Bs * Ts + N_META
    proj_sm = _norm_matmul(jnp.concatenate([xs, meta_tokens], axis=0), g_pre_mix, w_proj, tm=n_sm, tn=768)
    proj_m = proj_sm[:, Bs * Ts:].reshape(nblk, 1, N_META, 128)

    gates_p = _gate_act(proj_p.reshape(nblk, B * S, 128), gbias, arow, tm=1024).reshape(B, S, 128)
    gates_sm = _gate_act(proj_sm, gbias, arow, tm=n_sm)
    gates_m = gates_sm[Bs * Ts:].reshape(1, N_META, 128)

    def conv_rows(blocks):
        return jnp.moveaxis(blocks, 0, -2).reshape(blocks.shape[1:-1] + (3 * WIDTH,))

    c0, n0, m0 = (jnp.zeros((1, H, 128, 128), F32), jnp.zeros((1, H, 1, 128), F32), jnp.zeros((1, H, 1, 128), F32))
    (_, c_l, n_l, m_l), = _run_parts([_mlstm_chunked(proj_m, gates_m, mnorm3, c0, n0, m0, chunk=N_META)],
                                     semantics=("parallel", "arbitrary"), name="mlstm_lead")
    prep_l = _gdn_prep(proj_m, gates_m, jnp.zeros((3 * H, 8, 128), F32), conv_w, chunk=N_META, nsub=1)
    _, s_l = _gdn_scan(prep_l, proj_m, gnorm3, c0, chunk=N_META)

    ns = SAMPLE_SEQS
    grp = Bs // ns
    m_in = jnp.pad(state_mlstm_m[0], ((0, 0), (0, 128 - H))).reshape(grp, ns, 128)
    mlstm_parts = _mlstm_chunked(proj_p, gates_p, mnorm3, c_l, n_l, m_l, chunk=MLSTM_CHUNK)
    sample_parts = _sample_step(
        proj_sm.reshape(nblk, n_sm // (ns * Ts), ns * Ts, 128), gates_sm.reshape(n_sm // (ns * Ts), ns * Ts, 128),
        state_mlstm_C[0].reshape(grp, ns, H, 128, 128),
        state_mlstm_n[0].reshape(grp, ns, H, 128), m_in, state_gdn_S[0].reshape(grp, ns, H, 128, 128),
        state_gdn_conv[0].reshape(grp, ns, CONV_W - 1, 3 * WIDTH), conv_w, mnorm, gnorm, steps=Ts,
        grid=mlstm_parts["grid"])
    (mix_a, p_c, p_n, p_m), (mix_s, s_c, s_n, s_m, s_s) = _run_parts(
        [mlstm_parts, sample_parts], semantics=("parallel", "arbitrary"), name="mlstm_and_sample")
    conv_init = proj_m[COL_GQ:COL_GZ, 0, N_META - 8:, :]
    prep_p = _gdn_prep(proj_p, gates_p, conv_init, conv_w, chunk=GDN_CHUNK, nsub=GDN_PREP_CHUNKS)
    mix_b, p_s = _gdn_scan(prep_p, proj_p, gnorm3, s_l, chunk=GDN_CHUNK)
    p_conv = conv_rows(proj_p[COL_GQ:COL_GZ, :, S - (CONV_W - 1):, :])

    x1_p, u2_p = _out_proj(mix_a.reshape(H, B * S, 128), 0, mix_b.reshape(H, B * S, 128), 0, w_out_b, xp,
                           g_post_mix, g_pre_ffn, tm=512)
    y_p = _ffn(u2_p, w_gu_b, w_dn_b, x1_p, g_post_ffn, tm=512, tf=512)

    mix_s2 = mix_s.reshape(Bs * Ts, 2 * WIDTH)
    s_conv = conv_rows(proj_sm[COL_GQ:COL_GZ, :Bs * Ts].reshape(3 * H, Bs, Ts, 128)[:, :, Ts - (CONV_W - 1):, :])
    x1_s, u2_s = _out_proj(mix_s2, 0, mix_s2, 1, w_out_b, xs, g_post_mix, g_pre_ffn, tm=512)
    y_s = _ffn(u2_s, w_gu_b, w_dn_b, x1_s, g_post_ffn, tm=512, tf=512)

    return (y_p.reshape(B, S, D), y_s.reshape(Bs, Ts, D),
            p_c[None], p_n[:, :, 0, :][None], p_m[:, :, 0, 0][None], p_s[None], p_conv[None],
            s_c.reshape(Bs, H, 128, 128)[None], s_n.reshape(Bs, H, 128)[None], s_m.reshape(Bs, 128)[:, :H][None],
            s_s.reshape(Bs, H, 128, 128)[None], s_conv[None])
```

```python
import functools

import jax
import jax.numpy as jnp
from jax import lax
from jax.experimental import pallas as pl
from jax.experimental.pallas import tpu as pltpu

F32 = jnp.float32
BF16 = jnp.bfloat16

D_MODEL = 2048
HEAD_DIM = 128
N_HEADS = 8
WIDTH = N_HEADS * HEAD_DIM
CONV_W = 4
N_META = 16
D_FF = 5632
EPS = 1e-6
Q_SCALE = HEAD_DIM ** -0.5

COL_MQ, COL_MK, COL_MV, COL_MO = 0, 8, 16, 24
COL_GQ, COL_GK, COL_GV, COL_GZ = 32, 40, 48, 56
COL_GATES = 64
PROJ_COLS = 66 * 128
LANE_MI, LANE_MF, LANE_GA, LANE_GB = 0, 8, 16, 24

MLSTM_CHUNK = 256
GDN_CHUNK = 64
GDN_PREP_CHUNKS = 32
GDN_SCAN_HEADS = 8
SAMPLE_SEQS = 2
VMEM_LIMIT = 56 * 1024 * 1024


def _sigmoid(x):
    return 0.5 * jnp.tanh(0.5 * x) + 0.5


def _softplus(x):
    return jnp.maximum(x, 0.0) + jnp.log1p(jnp.exp(-jnp.abs(x)))


def _log_sigmoid(x):
    return -_softplus(-x)


def _rms_scale(x):
    return lax.rsqrt(jnp.mean(x * x, axis=-1, keepdims=True) + EPS)


def _l2_scale(x):
    return lax.rsqrt(jnp.sum(x * x, axis=-1, keepdims=True) + EPS)


def _dot(a, b):
    return jnp.dot(a, b, preferred_element_type=F32)


def _dot_nt(a, b):
    return lax.dot_general(a, b, (((1,), (1,)), ((), ())), preferred_element_type=F32)


def _dot_tn(a, b):
    return lax.dot_general(a, b, (((0,), (0,)), ((), ())), preferred_element_type=F32)


def _dot_f32(a, b):
    return jnp.dot(a, b, precision=lax.Precision.HIGHEST, preferred_element_type=F32)


def _select_lane(x, lane_idx, target):
    return jnp.sum(jnp.where(lane_idx == target, x, 0.0), axis=1, keepdims=True)


REGROUP_TILE = 256


def _regroup_kernel(w_ref, mg_ref, gg_ref, o_ref):
    j = pl.program_id(0)
    wide_tiles = 2 * 4 * WIDTH // REGROUP_TILE

    @pl.when(j < wide_tiles)
    def _():
        o_ref[...] = w_ref[...].T.astype(BF16)

    @pl.when(j == wide_tiles)
    def _():
        pad = jnp.zeros((REGROUP_TILE - 4 * N_HEADS, w_ref.shape[1]), F32)
        o_ref[...] = jnp.concatenate([mg_ref[...], gg_ref[...], pad], axis=0).T.astype(BF16)


def _regroup_weights(w_t):
    n, d = w_t.shape
    wide = 4 * WIDTH
    narrow = 2 * N_HEADS
    assert n == 2 * (wide + narrow) and PROJ_COLS == 2 * wide + REGROUP_TILE
    wide_tiles = 2 * wide // REGROUP_TILE

    def src_row(j):
        j = jnp.minimum(j, wide_tiles - 1)
        return (j * (REGROUP_TILE // narrow) + jnp.where(j >= wide_tiles // 2, 1, 0)) * narrow

    return pl.pallas_call(
        _regroup_kernel,
        grid=(wide_tiles + 1,),
        in_specs=[pl.BlockSpec((pl.Element(REGROUP_TILE), pl.Element(d)), lambda j: (src_row(j), 0)),
                  pl.BlockSpec((narrow, d), lambda j: (wide // narrow, 0)),
                  pl.BlockSpec((narrow, d), lambda j: ((2 * wide + narrow) // narrow, 0))],
        out_specs=pl.BlockSpec((d, REGROUP_TILE), lambda j: (0, j)),
        out_shape=jax.ShapeDtypeStruct((d, PROJ_COLS), BF16),
        compiler_params=pltpu.CompilerParams(dimension_semantics=("parallel",), vmem_limit_bytes=VMEM_LIMIT),
        name="regroup_weights",
    )(w_t, w_t, w_t)


def _norm_matmul_kernel(x_ref, g_ref, w_ref, o_ref, u_ref):
    @pl.when(pl.program_id(1) == 0)
    def _():
        x = x_ref[...]
        u_ref[...] = (x * _rms_scale(x) * g_ref[...]).astype(BF16)

    res = _dot(u_ref[...], w_ref[...])
    for blk in range(o_ref.shape[0]):
        o_ref[blk] = res[:, blk * 128:(blk + 1) * 128]


def _norm_matmul(x, gain, w, *, tm, tn):
    m, d = x.shape
    n = w.shape[1]
    return pl.pallas_call(
        _norm_matmul_kernel,
        grid=(m // tm, n // tn),
        in_specs=[pl.BlockSpec((tm, d), lambda i, j: (i, 0)),
                  pl.BlockSpec((1, d), lambda i, j: (0, 0)),
                  pl.BlockSpec((d, tn), lambda i, j: (0, j))],
        out_specs=pl.BlockSpec((tn // 128, tm, 128), lambda i, j: (j, i, 0)),
        out_shape=jax.ShapeDtypeStruct((n // 128, m, 128), F32),
        scratch_shapes=[pltpu.VMEM((tm, d), BF16)],
        compiler_params=pltpu.CompilerParams(dimension_semantics=("parallel", "arbitrary"),
                                             vmem_limit_bytes=VMEM_LIMIT),
        name="norm_matmul",
    )(x, gain, w)


def _out_proj_kernel(ma_ref, mb_ref, wa_ref, wb_ref, x_ref, gpost_ref, gpre_ref, x1_ref, u2_ref):
    def rows(ref, sl):
        if len(ref.shape) == 2:
            return ref[sl, :].astype(BF16)
        return jnp.concatenate([ref[h, sl, :] for h in range(ref.shape[0])], axis=1).astype(BF16)

    tm = x_ref.shape[0]
    halves = [slice(0, tm // 2), slice(tm // 2, tm)]
    mos = [_dot(rows(ma_ref, sl), wa_ref[...]) + _dot(rows(mb_ref, sl), wb_ref[...]) for sl in halves]
    for sl, mo in zip(halves, mos):
        x1 = x_ref[sl, :] + mo * _rms_scale(mo) * gpost_ref[...]
        x1_ref[sl, :] = x1
        u2_ref[sl, :] = (x1 * _rms_scale(x1) * gpre_ref[...]).astype(BF16)


def _out_proj(mix_a, col_a, mix_b, col_b, w_out, x, g_post, g_pre, *, tm):
    m, d = x.shape

    def mix_spec(mix, col):
        if mix.ndim == 2:
            return pl.BlockSpec((tm, WIDTH), lambda i: (i, col))
        return pl.BlockSpec((N_HEADS, tm, 128), lambda i: (0, i, 0))

    return pl.pallas_call(
        _out_proj_kernel,
        grid=(m // tm,),
        in_specs=[mix_spec(mix_a, col_a),
                  mix_spec(mix_b, col_b),
                  pl.BlockSpec((WIDTH, d), lambda i: (0, 0)),
                  pl.BlockSpec((WIDTH, d), lambda i: (1, 0)),
                  pl.BlockSpec((tm, d), lambda i: (i, 0)),
                  pl.BlockSpec((1, d), lambda i: (0, 0)),
                  pl.BlockSpec((1, d), lambda i: (0, 0))],
        out_specs=[pl.BlockSpec((tm, d), lambda i: (i, 0)),
                   pl.BlockSpec((tm, d), lambda i: (i, 0))],
        out_shape=[jax.ShapeDtypeStruct((m, d), F32), jax.ShapeDtypeStruct((m, d), BF16)],
        compiler_params=pltpu.CompilerParams(dimension_semantics=("parallel",),
                                             vmem_limit_bytes=VMEM_LIMIT),
        name="out_proj",
    )(mix_a, mix_b, w_out, w_out, x, g_post, g_pre)


def _ffn_kernel(u_ref, wg_ref, wu_ref, wd_ref, x1_ref, gain_ref, o_ref, acc_ref):
    f = pl.program_id(1)

    @pl.when(f == 0)
    def _():
        acc_ref[...] = jnp.zeros_like(acc_ref)

    u = u_ref[...]
    g = _dot(u, wg_ref[...])
    up = _dot(u, wu_ref[...])
    hidden = (g * _sigmoid(g) * up).astype(BF16)
    acc_ref[...] += _dot(hidden, wd_ref[...])

    @pl.when(f == pl.num_programs(1) - 1)
    def _():
        y = acc_ref[...]
        o_ref[...] = x1_ref[...] + y * _rms_scale(y) * gain_ref[...]


def _ffn(u2, w_gate_up, w_down, x1, gain, *, tm, tf):
    m, d = x1.shape
    nf = D_FF // tf
    return pl.pallas_call(
        _ffn_kernel,
        grid=(m // tm, nf),
        in_specs=[pl.BlockSpec((tm, d), lambda i, f: (i, 0)),
                  pl.BlockSpec((d, tf), lambda i, f: (0, f)),
                  pl.BlockSpec((d, tf), lambda i, f: (0, nf + f)),
                  pl.BlockSpec((tf, d), lambda i, f: (f, 0)),
                  pl.BlockSpec((tm, d), lambda i, f: (i, 0)),
                  pl.BlockSpec((1, d), lambda i, f: (0, 0))],
        out_specs=pl.BlockSpec((tm, d), lambda i, f: (i, 0)),
        out_shape=jax.ShapeDtypeStruct((m, d), F32),
        scratch_shapes=[pltpu.VMEM((tm, d), F32)],
        compiler_params=pltpu.CompilerParams(dimension_semantics=("parallel", "arbitrary"),
                                             vmem_limit_bytes=VMEM_LIMIT),
        name="ffn",
    )(u2, w_gate_up, w_gate_up, w_down, x1, gain)


def _gate_act_kernel(g_ref, gbias_ref, arow_ref, o_ref):
    g = g_ref[0] + gbias_ref[...]
    lane = lax.broadcasted_iota(jnp.int32, g.shape, 1)
    decay = -jnp.exp(arow_ref[...]) * _softplus(g)
    o_ref[...] = jnp.where(lane < LANE_MF, g,
                           jnp.where(lane < LANE_GA, _log_sigmoid(g), jnp.where(lane < LANE_GB, decay, _sigmoid(g))))


def _gate_act(proj, gbias, arow, *, tm):
    m = proj.shape[1]
    return pl.pallas_call(
        _gate_act_kernel,
        grid=(m // tm,),
        in_specs=[pl.BlockSpec((1, tm, 128), lambda i: (COL_GATES, i, 0)),
                  pl.BlockSpec((1, 128), lambda i: (0, 0)),
                  pl.BlockSpec((1, 128), lambda i: (0, 0))],
        out_specs=pl.BlockSpec((tm, 128), lambda i: (i, 0)),
        out_shape=jax.ShapeDtypeStruct((m, 128), F32),
        compiler_params=pltpu.CompilerParams(dimension_semantics=("parallel",), vmem_limit_bytes=VMEM_LIMIT),
        name="gate_act",
    )(proj, gbias, arow)


def _mlstm_chunk_kernel(q_ref, k_ref, v_ref, og_ref, gates_ref, mnorm_ref, c0_ref, n0_ref, m0_ref,
                        h_ref, c_out_ref, n_out_ref, m_out_ref, c_s, n_s, m_s, *, chunk):
    L = chunk
    nb = q_ref.shape[1]
    shared = c0_ref.shape[0] == 1
    head = pl.program_id(0)
    c = pl.program_id(1)

    @pl.when(c == 0)
    def _():
        for b in range(nb):
            b0 = 0 if shared else b
            c_s[b] = c0_ref[b0, 0]
            n_s[b] = n0_ref[b0, 0]
            m_s[b] = m0_ref[b0, 0]

    lane = lax.broadcasted_iota(jnp.int32, (L, 128), 1)
    rows = lax.broadcasted_iota(jnp.int32, (L, L), 0)
    cols = lax.broadcasted_iota(jnp.int32, (L, L), 1)
    tril = rows >= cols
    triu = rows <= cols
    eye = rows == cols
    last = c == pl.num_programs(1) - 1

    seqs = range(nb)
    gates = [gates_ref[b] for b in seqs]
    li_col = [_select_lane(g, lane, head + LANE_MI) for g in gates]
    lf_col = [_select_lane(g, lane, head + LANE_MF) for g in gates]
    lf_row = [jnp.sum(jnp.where(eye, x, 0.0), axis=0, keepdims=True) for x in lf_col]
    li_row = [jnp.sum(jnp.where(eye, x, 0.0), axis=0, keepdims=True) for x in li_col]
    b_col = [jnp.sum(jnp.where(tril, x, 0.0), axis=1, keepdims=True) for x in lf_row]
    b_row = [jnp.sum(jnp.where(triu, x, 0.0), axis=0, keepdims=True) for x in lf_col]

    qb = [(q_ref[0, b] * Q_SCALE).astype(BF16) for b in seqs]
    kb = [k_ref[0, b].astype(BF16) for b in seqs]
    vb = [v_ref[0, b].astype(BF16) for b in seqs]
    qk = [_dot_nt(x, y) for x, y in zip(qb, kb)]
    qc = [_dot(x, c_s[b].astype(BF16)) for b, x in zip(seqs, qb)]

    m_prev = [m_s[b, :, 0:1] for b in seqs]
    a_col = [x + y for x, y in zip(b_col, m_prev)]
    dm = [jnp.where(tril, bc - br + lr, -jnp.inf) for bc, br, lr in zip(b_col, b_row, li_row)]
    m_t = [jnp.maximum(a, jnp.max(d, axis=1, keepdims=True)) for a, d in zip(a_col, dm)]
    inter = [jnp.exp(a - m) for a, m in zip(a_col, m_t)]
    s = [x * jnp.exp(d - m) for x, d, m in zip(qk, dm, m_t)]
    sv = [_dot(x.astype(BF16), v) for x, v in zip(s, vb)]

    m_last = [m[L - 1:L, :] for m in m_t]
    b_last = [x[L - 1:L, :] for x in b_col]
    d_c = [jnp.exp(bl + mp - ml) for bl, mp, ml in zip(b_last, m_prev, m_last)]
    kw = [jnp.exp(bl - bc + li - ml) * k_ref[0, b]
          for b, bl, bc, li, ml in zip(seqs, b_last, b_col, li_col, m_last)]
    c_upd = [_dot_tn(x.astype(BF16), v) for x, v in zip(kw, vb)]

    for b in seqs:
        q = q_ref[0, b] * Q_SCALE
        num = sv[b] + inter[b] * qc[b]
        den = jnp.sum(s[b], axis=1, keepdims=True) + inter[b] * jnp.sum(q * n_s[b], axis=1, keepdims=True)
        hh = num / jnp.maximum(jnp.abs(den), jnp.exp(-m_t[b]))
        out = hh * _rms_scale(hh) * mnorm_ref[0] * _sigmoid(og_ref[0, b])
        h_ref[0, b] = out.astype(h_ref.dtype)
        c_new = d_c[b] * c_s[b] + c_upd[b]
        n_new = d_c[b] * n_s[b] + jnp.sum(kw[b], axis=0, keepdims=True)
        m_new = jnp.broadcast_to(m_last[b], (1, 128))
        c_s[b], n_s[b], m_s[b] = c_new, n_new, m_new
        c_out_ref[b, 0], n_out_ref[b, 0], m_out_ref[b, 0] = c_new, n_new, m_new


def _mlstm_chunked(proj, gates, mnorm, c0, n0, m0, *, chunk):
    _, batch, seq, _ = proj.shape
    nb0 = c0.shape[0]

    def col(base):
        return pl.BlockSpec((1, batch, chunk, 128), lambda h, c: (base + h, 0, c, 0))

    def state(nb, shape):
        return pl.BlockSpec((nb, 1) + shape, lambda h, c: (0, h, 0, 0))

    return dict(
        body=functools.partial(_mlstm_chunk_kernel, chunk=chunk),
        grid=(N_HEADS, seq // chunk),
        in_specs=[col(COL_MQ), col(COL_MK), col(COL_MV), col(COL_MO),
                  pl.BlockSpec((batch, chunk, 128), lambda h, c: (0, c, 0)),
                  pl.BlockSpec((1, 1, 128), lambda h, c: (h, 0, 0)),
                  state(nb0, (128, 128)), state(nb0, (1, 128)), state(nb0, (1, 128))],
        out_specs=[pl.BlockSpec((1, batch, chunk, 128), lambda h, c: (h, 0, c, 0)),
                   state(batch, (128, 128)), state(batch, (1, 128)), state(batch, (1, 128))],
        out_shape=[jax.ShapeDtypeStruct((N_HEADS, batch, seq, 128), BF16),
                   jax.ShapeDtypeStruct((batch, N_HEADS, 128, 128), F32),
                   jax.ShapeDtypeStruct((batch, N_HEADS, 1, 128), F32),
                   jax.ShapeDtypeStruct((batch, N_HEADS, 1, 128), F32)],
        scratch_shapes=[pltpu.VMEM((batch, 128, 128), F32), pltpu.VMEM((batch, 1, 128), F32),
                        pltpu.VMEM((batch, 1, 128), F32)],
        args=(proj, proj, proj, proj, gates, mnorm, c0, n0, m0))


def _run_parts(parts, *, semantics, name):
    grid = parts[0]["grid"]
    assert all(p["grid"] == grid for p in parts)
    n_in = [len(p["in_specs"]) for p in parts]
    n_out = [len(p["out_specs"]) for p in parts]
    n_scr = [len(p["scratch_shapes"]) for p in parts]

    def body(*refs):
        ins, outs, scr = refs[:sum(n_in)], refs[sum(n_in):sum(n_in) + sum(n_out)], refs[sum(n_in) + sum(n_out):]
        for k, p in enumerate(parts):
            p["body"](*ins[sum(n_in[:k]):sum(n_in[:k + 1])], *outs[sum(n_out[:k]):sum(n_out[:k + 1])],
                      *scr[sum(n_scr[:k]):sum(n_scr[:k + 1])])

    outs = pl.pallas_call(
        body,
        grid=grid,
        in_specs=[s for p in parts for s in p["in_specs"]],
        out_specs=[s for p in parts for s in p["out_specs"]],
        out_shape=[s for p in parts for s in p["out_shape"]],
        scratch_shapes=[s for p in parts for s in p["scratch_shapes"]],
        compiler_params=pltpu.CompilerParams(dimension_semantics=semantics, vmem_limit_bytes=VMEM_LIMIT),
        name=name,
    )(*[a for p in parts for a in p["args"]])
    return [outs[sum(n_out[:k]):sum(n_out[:k + 1])] for k in range(len(parts))]


def _split(x):
    hi = x.astype(BF16)
    return hi, (x - hi.astype(F32)).astype(BF16)


def _dot_split(a, b):
    a_hi, a_lo = a
    b_hi, b_lo = b if isinstance(b, tuple) else (b, None)
    if a_hi.shape[1] % 128 == 0:
        main = _dot(jnp.concatenate([a_hi, a_lo], axis=1), jnp.concatenate([b_hi, b_hi], axis=0))
    else:
        main = _dot(a_hi, b_hi) + _dot(a_lo, b_hi)
    return main if b_lo is None else main + _dot(a_hi, b_lo)


def _unit_lower_inverse(mats, rows, cols, size):
    def same_block(shift):
        return (rows >> shift) == (cols >> shift)

    ident = jnp.where(rows == cols, 1.0, 0.0)
    d = [jnp.where(same_block(3), a, 0.0) for a in mats]
    d_s = [_split(x) for x in d]
    d2_s = [_split(_dot_split(x, x)) for x in d_s]
    d4_s = [_split(_dot_split(x, x)) for x in d2_s]
    t = [ident - x for x in d]
    t = [x + _dot_split(_split(x), y) for x, y in zip(t, d2_s)]
    t = [x + _dot_split(_split(x), y) for x, y in zip(t, d4_s)]
    a_hi = [x.astype(BF16) for x in mats]
    shift = 3
    while (1 << shift) < size:
        off = jnp.logical_and(same_block(shift + 1), jnp.logical_not(same_block(shift)))
        off_bf = jnp.where(off, 1.0, 0.0).astype(BF16)
        t_s = [_split(x) for x in t]
        te_s = [_split(_dot_split(x, y * off_bf)) for x, y in zip(t_s, a_hi)]
        t = [x - _dot_split(y, z[0]) for x, y, z in zip(t, te_s, t_s)]
        shift += 1
    return t


def _gdn_prep_kernel(q_ref, k_ref, v_ref, pq_ref, pk_ref, pv_ref, iq_ref, ik_ref, iv_ref, wq_ref, wk_ref, wv_ref,
                     gates_ref,
                     u_ref, w_ref, qe_ref, kd_ref, attn_ref, glast_ref, ext, *, chunk, nsub):
    L = chunk
    R = nsub * L
    head = pl.program_id(1)
    first = pl.program_id(2) == 0

    def conv(x_ref, prev_ref, init_ref, cw_ref):
        prev = jnp.where(first, init_ref[0], prev_ref[0, 0])
        ext[0:8, :] = prev
        ext[8:8 + R, :] = x_ref[0, 0]
        base = 8 - (CONV_W - 1)
        acc = ext[base:base + R, :] * cw_ref[0:1, :]
        for i in range(1, CONV_W):
            acc = acc + ext[base + i:base + i + R, :] * cw_ref[i:i + 1, :]
        return acc * _sigmoid(acc)

    q = conv(q_ref, pq_ref, iq_ref, wq_ref)
    k = conv(k_ref, pk_ref, ik_ref, wk_ref)
    v_all = conv(v_ref, pv_ref, iv_ref, wv_ref)
    qn_all = q * _l2_scale(q) * Q_SCALE
    kn_all = k * _l2_scale(k)

    gates = gates_ref[0]
    lane = lax.broadcasted_iota(jnp.int32, (R, 128), 1)
    lg_all = _select_lane(gates, lane, head + LANE_GA)
    beta_all = _select_lane(gates, lane, head + LANE_GB)

    U = min(R, 128)
    per_unit = U // L
    rows = lax.broadcasted_iota(jnp.int32, (U, U), 0)
    cols = lax.broadcasted_iota(jnp.int32, (U, U), 1)
    shift = L.bit_length() - 1
    same_chunk = (rows >> shift) == (cols >> shift)
    causal = jnp.logical_and(rows >= cols, same_chunk)
    strict = jnp.logical_and(rows > cols, same_chunk)
    upper = jnp.logical_and(rows <= cols, same_chunk)
    eye = rows == cols
    row_id = lax.broadcasted_iota(jnp.int32, (U, 1), 0)

    units = range(R // U)
    sls = [slice(j * U, (j + 1) * U) for j in units]
    g_cols, decays, a_mats, qks = [], [], [], []
    for sl in sls:
        lg_col = lg_all[sl]
        lg_row = jnp.sum(jnp.where(eye, lg_col, 0.0), axis=0, keepdims=True)
        g_col = jnp.sum(jnp.where(causal, lg_row, 0.0), axis=1, keepdims=True)
        g_row = jnp.sum(jnp.where(upper, lg_col, 0.0), axis=0, keepdims=True)
        decay = jnp.where(causal, jnp.exp(jnp.where(causal, g_col - g_row, 0.0)), 0.0)
        kb = kn_all[sl].astype(BF16)
        kk = _dot_nt(kb, kb)
        qks.append(_dot_nt(qn_all[sl].astype(BF16), kb))
        a_mats.append(jnp.where(strict, beta_all[sl] * kk * decay, 0.0))
        g_cols.append(g_col)
        decays.append(decay)

    t_mats = _unit_lower_inverse(a_mats, rows, cols, L)

    e_gs = [jnp.exp(g) for g in g_cols]
    rhs_s = [_split(jnp.concatenate([beta_all[sl] * v_all[sl], (beta_all[sl] * e_g) * kn_all[sl]], axis=1))
             for sl, e_g in zip(sls, e_gs)]
    uws = [_dot_split(_split(t), r) for t, r in zip(t_mats, rhs_s)]

    for j in units:
        sl, g_col = sls[j], g_cols[j]
        u_ref[0, 0, sl, :] = uws[j][:, 0:128]
        w_ref[0, 0, sl, :] = uws[j][:, 128:256].astype(BF16)
        qe_ref[0, 0, sl, :] = (qn_all[sl] * e_gs[j]).astype(BF16)
        attn = qks[j] * decays[j]
        g_last_col = jnp.zeros((U, 1), F32)
        for i in range(per_unit):
            g_last = g_col[(i + 1) * L - 1:(i + 1) * L, :]
            g_last_col = jnp.where((row_id >> shift) == i, g_last, g_last_col)
            glast_ref[0, 0, j * per_unit + i] = jnp.broadcast_to(jnp.exp(g_last), (1, 128))
            diag = attn[i * L:(i + 1) * L, :]
            if i > 0:
                diag = pltpu.roll(diag, U - i * L, axis=1)
            attn_ref[0, 0, j * per_unit + i] = diag[:, 0:L].astype(BF16)
        kd_ref[0, 0, sl, :] = (kn_all[sl] * jnp.exp(g_last_col - g_col)).astype(BF16)


def _gdn_prep(proj, gates, conv_init, conv_w, *, chunk, nsub):
    _, batch, seq, _ = proj.shape
    nc = seq // chunk
    R = nsub * chunk

    def col(base):
        return pl.BlockSpec((1, 1, R, 128), lambda b, h, c: (base + h, b, c, 0))

    def prev(base):
        return pl.BlockSpec((1, 1, 8, 128), lambda b, h, c: (base + h, b, jnp.maximum(c * (R // 8) - 1, 0), 0))

    def init(base):
        return pl.BlockSpec((1, 8, 128), lambda b, h, c: (base + h, 0, 0))

    def cw(base):
        return pl.BlockSpec((CONV_W, 128), lambda b, h, c: (0, base + h))

    def per_token(dtype):
        return (pl.BlockSpec((1, 1, R, 128), lambda b, h, c: (b, h, c, 0)),
                jax.ShapeDtypeStruct((batch, N_HEADS, seq, 128), dtype))

    outs = [per_token(F32), per_token(BF16), per_token(BF16), per_token(BF16),
            (pl.BlockSpec((1, 1, nsub, chunk, chunk), lambda b, h, c: (b, h, c, 0, 0)),
             jax.ShapeDtypeStruct((batch, N_HEADS, nc, chunk, chunk), BF16)),
            (pl.BlockSpec((1, 1, nsub, 1, 128), lambda b, h, c: (b, h, c, 0, 0)),
             jax.ShapeDtypeStruct((batch, N_HEADS, nc, 1, 128), F32))]
    return pl.pallas_call(
        functools.partial(_gdn_prep_kernel, chunk=chunk, nsub=nsub),
        grid=(batch, N_HEADS, nc // nsub),
        in_specs=[col(COL_GQ), col(COL_GK), col(COL_GV), prev(COL_GQ), prev(COL_GK), prev(COL_GV),
                  init(0), init(8), init(16), cw(0), cw(8), cw(16),
                  pl.BlockSpec((1, R, 128), lambda b, h, c: (b, c, 0))],
        out_specs=[o[0] for o in outs],
        out_shape=[o[1] for o in outs],
        scratch_shapes=[pltpu.VMEM((8 + R, 128), F32)],
        compiler_params=pltpu.CompilerParams(dimension_semantics=("parallel", "parallel", "parallel"),
                                             vmem_limit_bytes=VMEM_LIMIT),
        name="gdn_prep",
    )(proj, proj, proj, proj, proj, proj, conv_init, conv_init, conv_init, conv_w, conv_w, conv_w,
      gates)


def _gdn_scan_kernel(u_ref, w_ref, qe_ref, kd_ref, attn_ref, glast_ref, gz_ref, gnorm_ref, s0_ref,
                     o_ref, s_out_ref, s_s):
    nb, nh = u_ref.shape[0], u_ref.shape[1]
    shared = s0_ref.shape[0] == 1
    c = pl.program_id(1)
    last = c == pl.num_programs(1) - 1
    chains = [(b, h) for b in range(nb) for h in range(nh)]

    @pl.when(c == 0)
    def _():
        for b, h in chains:
            s_s[b, h] = s0_ref[0 if shared else b, h]

    L = u_ref.shape[2]
    s_prev = [s_s[b, h] for b, h in chains]
    ws = [_dot(jnp.concatenate([w_ref[b, h], qe_ref[b, h]], axis=0), sp.astype(BF16))
          for (b, h), sp in zip(chains, s_prev)]
    vb = [(u_ref[b, h] - r[0:L]).astype(BF16) for (b, h), r in zip(chains, ws)]
    intra = [_dot(attn_ref[b, h, 0], v) for (b, h), v in zip(chains, vb)]
    upd = [_dot_tn(kd_ref[b, h], v) for (b, h), v in zip(chains, vb)]
    for i, (b, h) in enumerate(chains):
        s_s[b, h] = glast_ref[b, h, 0][:, 0:1] * s_prev[i] + upd[i]
        o = ws[i][L:2 * L] + intra[i]
        gz = gz_ref[h, b]
        o_ref[h, b] = (o * _rms_scale(o) * gnorm_ref[h] * (gz * _sigmoid(gz))).astype(o_ref.dtype)

    @pl.when(last)
    def _():
        for b, h in chains:
            s_out_ref[b, h] = s_s[b, h]


def _gdn_scan(prep, proj, gnorm, s0, *, chunk):
    _, batch, seq, _ = proj.shape
    u, w, qe, kd, attn, glast = prep
    hb = GDN_SCAN_HEADS
    assert COL_GZ % hb == 0 and N_HEADS % hb == 0
    tok = pl.BlockSpec((batch, hb, chunk, 128), lambda h, c: (0, h, c, 0))
    return pl.pallas_call(
        _gdn_scan_kernel,
        grid=(N_HEADS // hb, seq // chunk),
        in_specs=[tok, tok, tok, tok,
                  pl.BlockSpec((batch, hb, 1, chunk, chunk), lambda h, c: (0, h, c, 0, 0)),
                  pl.BlockSpec((batch, hb, 1, 1, 128), lambda h, c: (0, h, c, 0, 0)),
                  pl.BlockSpec((hb, batch, chunk, 128), lambda h, c: (COL_GZ // hb + h, 0, c, 0)),
                  pl.BlockSpec((hb, 1, 128), lambda h, c: (h, 0, 0)),
                  pl.BlockSpec((s0.shape[0], hb, 128, 128), lambda h, c: (0, h, 0, 0))],
        out_specs=[pl.BlockSpec((hb, batch, chunk, 128), lambda h, c: (h, 0, c, 0)),
                   pl.BlockSpec((batch, hb, 128, 128), lambda h, c: (0, h, 0, 0))],
        out_shape=[jax.ShapeDtypeStruct((N_HEADS, batch, seq, 128), BF16),
                   jax.ShapeDtypeStruct((batch, N_HEADS, 128, 128), F32)],
        scratch_shapes=[pltpu.VMEM((batch, hb, 128, 128), F32)],
        compiler_params=pltpu.CompilerParams(dimension_semantics=("parallel", "arbitrary"),
                                             vmem_limit_bytes=VMEM_LIMIT),
        name="gdn_scan",
    )(u, w, qe, kd, attn, glast, proj, gnorm, s0)


def _sample_kernel(p_ref, gates_ref, c_ref, n_ref, m_ref, s_ref, cb_ref, cw_ref, mnorm_ref, gnorm_ref,
                   mix_ref, c_out_ref, n_out_ref, m_out_ref, s_out_ref, ext, xc_s, *, steps):
    T = steps
    NS = SAMPLE_SEQS
    R = NS * T
    H = N_HEADS
    nbuf = CONV_W - 1

    for s in range(NS):
        ext[8 * s:8 * s + nbuf, :] = cb_ref[0, s]
        for blk in range(COL_GZ - COL_GQ):
            ext[8 * s + nbuf:8 * s + nbuf + T, blk * 128:(blk + 1) * 128] = p_ref[COL_GQ + blk, 0, s * T:(s + 1) * T, :]
        acc = ext[8 * s:8 * s + T, :] * cw_ref[0:1, :]
        for i in range(1, CONV_W):
            acc = acc + ext[8 * s + i:8 * s + i + T, :] * cw_ref[i:i + 1, :]
        xc_s[s * T:(s + 1) * T, :] = acc * _sigmoid(acc)

    rowi = lax.broadcasted_iota(jnp.int32, (R, 1), 0)
    tpos = rowi & (T - 1)
    second = rowi >= T
    seq_rows = [jnp.logical_not(second), second]

    def pick(x, j):
        return jnp.where(second, x[T + j:T + j + 1, :], x[j:j + 1, :])

    def cumsum_t(x):
        acc = jnp.where(tpos >= 0, pick(x, 0), 0.0)
        for j in range(1, T):
            acc = acc + jnp.where(tpos >= j, pick(x, j), 0.0)
        return acc

    def pcol(block, h):
        return p_ref[block + h, 0]

    gates = gates_ref[0]
    lane = lax.broadcasted_iota(jnp.int32, (R, 128), 1)
    lf_all = lg_all = beta_all = gates
    m_in = m_ref[0]
    lane_s = lax.broadcasted_iota(jnp.int32, (NS, 128), 1)
    row_s = lax.broadcasted_iota(jnp.int32, (NS, 128), 0)
    m_acc = jnp.zeros((NS, 128), F32)

    qc_all, gdn_in = [], []
    for h in range(H):
        qb = (pcol(COL_MQ, h) * Q_SCALE).astype(BF16)
        qc_all.append(jnp.where(second, _dot(qb, c_ref[0, 1, h].astype(BF16)),
                                _dot(qb, c_ref[0, 0, h].astype(BF16))))
        cq = xc_s[:, h * 128:(h + 1) * 128]
        ck = xc_s[:, WIDTH + h * 128:WIDTH + (h + 1) * 128]
        qn = cq * _l2_scale(cq) * Q_SCALE
        kn = ck * _l2_scale(ck)
        kq = jnp.concatenate([kn, qn], axis=0).astype(BF16)
        r0 = _dot(kq, s_ref[0, 0, h].astype(BF16))
        r1 = _dot(kq, s_ref[0, 1, h].astype(BF16))
        gdn_in.append((qn, kn, jnp.where(second, r1[0:R], r0[0:R]), jnp.where(second, r1[R:2 * R], r0[R:2 * R])))

    c_upd, s_upd = [], []
    for h in range(H):
        li = _select_lane(gates, lane, LANE_MI + h)
        lf = _select_lane(lf_all, lane, LANE_MF + h)
        mh = _select_lane(m_in, lane_s, h)
        m0 = jnp.where(second, mh[1:2, :], mh[0:1, :])
        b = cumsum_t(lf)
        a = b + m0
        dcol = [b - pick(b, j) + pick(li, j) for j in range(T)]
        m_t = a
        for j in range(T):
            m_t = jnp.maximum(m_t, jnp.where(tpos >= j, dcol[j], -jnp.inf))
        inter = jnp.exp(a - m_t)

        q = pcol(COL_MQ, h) * Q_SCALE
        k = pcol(COL_MK, h)
        v = pcol(COL_MV, h)
        num = jnp.zeros((R, 128), F32)
        den = jnp.zeros((R, 1), F32)
        for j in range(T):
            w_j = jnp.where(tpos >= j, jnp.exp(jnp.where(tpos >= j, dcol[j] - m_t, 0.0)), 0.0)
            s_j = jnp.sum(q * pick(k, j), axis=1, keepdims=True) * w_j
            num = num + s_j * pick(v, j)
            den = den + s_j
        qc = qc_all[h]
        n_sel =jnp.where(second, n_ref[0, 1, h:h + 1, :], n_ref[0, 0, h:h + 1, :])
        num = num + inter * qc
        den = den + inter * jnp.sum(q * n_sel, axis=1, keepdims=True)
        hh = num / jnp.maximum(jnp.abs(den), jnp.exp(-m_t))
        mix_ref[0, :, h * 128:(h + 1) * 128] = hh * _rms_scale(hh) * mnorm_ref[h:h + 1, :] * _sigmoid(pcol(COL_MO, h))

        m_l = pick(m_t, T - 1)
        b_l = pick(b, T - 1)
        kw = jnp.exp(b_l - b + li - m_l) * k
        d_c = jnp.exp(b_l + m0 - m_l)
        c_upd.append((kw, d_c, v.astype(BF16)))
        for s in range(NS):
            kws = jnp.where(seq_rows[s], kw, 0.0)
            n_out_ref[0, s, h:h + 1, :] = (d_c[s * T:s * T + 1, :] * n_ref[0, s, h:h + 1, :]
                                           + jnp.sum(kws, axis=0, keepdims=True))
            m_acc = jnp.where(jnp.logical_and(lane_s == h, row_s == s), m_l[s * T:s * T + 1, :], m_acc)

        lg = _select_lane(lg_all, lane, LANE_GA + h)
        beta = _select_lane(beta_all, lane, LANE_GB + h)
        g = cumsum_t(lg)
        e_g = jnp.exp(g)
        cv = xc_s[:, 2 * WIDTH + h * 128:2 * WIDTH + (h + 1) * 128]
        qn, kn, k_s, q_s = gdn_in[h]
        x = beta * (cv - e_g * k_s)
        rel = [jnp.exp(jnp.where(tpos >= j, g - pick(g, j), 0.0)) for j in range(T)]
        for j in range(T - 1):
            kk_j = jnp.sum(kn * pick(kn, j), axis=1, keepdims=True)
            a_j = jnp.where(tpos > j, beta * kk_j * rel[j], 0.0)
            x = x - a_j * pick(x, j)
        o = e_g * q_s
        for j in range(T):
            qk_j = jnp.sum(qn * pick(kn, j), axis=1, keepdims=True)
            o = o + jnp.where(tpos >= j, qk_j * rel[j], 0.0) * pick(x, j)
        gz = pcol(COL_GZ, h)
        mix_ref[0, :, WIDTH + h * 128:WIDTH + (h + 1) * 128] = (
            o * _rms_scale(o) * gnorm_ref[h:h + 1, :] * (gz * _sigmoid(gz)))

        g_l = pick(g, T - 1)
        s_upd.append((kn * jnp.exp(g_l - g), jnp.exp(g_l), x.astype(BF16)))
    m_out_ref[0] = m_acc

    for h in range(H):
        for (lhs, scale, rhs), st_ref, out_ref in ((c_upd[h], c_ref, c_out_ref), (s_upd[h], s_ref, s_out_ref)):
            for s in range(NS):
                rows_s = jnp.where(seq_rows[s], lhs, 0.0).astype(BF16)
                out_ref[0, s, h] = scale[s * T:s * T + 1, :] * st_ref[0, s, h] + _dot_tn(rows_s, rhs)


def _sample_step(proj, gates, c, n, m, s, conv_buf, conv_w, mnorm, gnorm, *, steps, grid):
    groups = c.shape[0]
    ns = SAMPLE_SEQS
    assert ns == 2 and ns * steps == 8 and steps >= CONV_W - 1 and steps & (steps - 1) == 0
    assert len(grid) == 2 and grid[0] * grid[1] == groups

    def per_group(*shape):
        return pl.BlockSpec((1,) + shape, lambda i, j: (i * grid[1] + j,) + (0,) * len(shape))

    def const(*shape):
        return pl.BlockSpec(shape, lambda i, j: (0,) * len(shape))

    state = (ns, N_HEADS, 128, 128)
    return dict(
        body=functools.partial(_sample_kernel, steps=steps),
        grid=grid,
        in_specs=[pl.BlockSpec((PROJ_COLS // 128, 1, ns * steps, 128), lambda i, j: (0, i * grid[1] + j, 0, 0)),
                  per_group(ns * steps, 128), per_group(*state), per_group(ns, N_HEADS, 128),
                  per_group(ns, 128), per_group(*state), per_group(ns, CONV_W - 1, 3 * WIDTH),
                  const(CONV_W, 3 * WIDTH), const(N_HEADS, 128), const(N_HEADS, 128)],
        out_specs=[per_group(ns * steps, 2 * WIDTH), per_group(*state), per_group(ns, N_HEADS, 128),
                   per_group(ns, 128), per_group(*state)],
        out_shape=[jax.ShapeDtypeStruct((groups, ns * steps, 2 * WIDTH), F32),
                   jax.ShapeDtypeStruct((groups,) + state, F32),
                   jax.ShapeDtypeStruct((groups, ns, N_HEADS, 128), F32),
                   jax.ShapeDtypeStruct((groups, ns, 128), F32),
                   jax.ShapeDtypeStruct((groups,) + state, F32)],
        scratch_shapes=[pltpu.VMEM((8 * ns, 3 * WIDTH), F32), pltpu.VMEM((ns * steps, 3 * WIDTH), F32)],
        args=(proj, gates, c, n, m, s, conv_buf, conv_w, mnorm, gnorm))


def kernel(x_prompt, x_sample, state_mlstm_C, state_mlstm_n, state_mlstm_m, state_gdn_S, state_gdn_conv, meta_tokens, norm_pre_mix, norm_post_mix, norm_pre_ffn, norm_post_ffn, w_in, mlstm_b_i, mlstm_b_f, mlstm_norm, gdn_conv_w, gdn_A_log, gdn_dt_bias, gdn_norm, w_out, w_gate_up, w_down):
    assert w_in.shape[0] == 1, "single-layer trunk"
    B, S, D = x_prompt.shape
    Bs, Ts, _ = x_sample.shape
    H = N_HEADS

    w_proj = _regroup_weights(jnp.transpose(w_in[0]))
    w_out_b = w_out[0].astype(BF16)
    w_gu_b = w_gate_up[0].astype(BF16)
    w_dn_b = w_down[0].astype(BF16)
    zeros_h = jnp.zeros((H,), F32)
    gbias = jnp.concatenate([mlstm_b_i[0], mlstm_b_f[0], gdn_dt_bias[0], zeros_h,
                             jnp.zeros((128 - 4 * H,), F32)]).reshape(1, 128)
    arow = jnp.concatenate([zeros_h, zeros_h, gdn_A_log[0], zeros_h,
                            jnp.zeros((128 - 4 * H,), F32)]).reshape(1, 128)
    g_pre_mix = norm_pre_mix[0].reshape(1, D)
    g_post_mix = norm_post_mix[0].reshape(1, D)
    g_pre_ffn = norm_pre_ffn[0].reshape(1, D)
    g_post_ffn = norm_post_ffn[0].reshape(1, D)
    mnorm = mlstm_norm[0]
    gnorm = gdn_norm[0]
    mnorm3 = mnorm.reshape(H, 1, 128)
    gnorm3 = gnorm.reshape(H, 1, 128)
    conv_w = gdn_conv_w[0]

    xp = x_prompt.reshape(B * S, D)
    xs = x_sample.reshape(Bs * Ts, D)

    nblk = PROJ_COLS // 128
    proj_p = _norm_matmul(xp, g_pre_mix, w_proj, tm=1024, tn=768).reshape(nblk, B, S, 128)
    n_sm = Bs * Ts + N_META
    proj_sm = _norm_matmul(jnp.concatenate([xs, meta_tokens], axis=0), g_pre_mix, w_proj, tm=n_sm, tn=768)
    proj_m = proj_sm[:, Bs * Ts:].reshape(nblk, 1, N_META, 128)

    gates_p = _gate_act(proj_p.reshape(nblk, B * S, 128), gbias, arow, tm=1024).reshape(B, S, 128)
    gates_sm = _gate_act(proj_sm, gbias, arow, tm=n_sm)
    gates_m = gates_sm[Bs * Ts:].reshape(1, N_META, 128)

    def conv_rows(blocks):
        return jnp.moveaxis(blocks, 0, -2).reshape(blocks.shape[1:-1] + (3 * WIDTH,))

    c0, n0, m0 = (jnp.zeros((1, H, 128, 128), F32), jnp.zeros((1, H, 1, 128), F32), jnp.zeros((1, H, 1, 128), F32))
    (_, c_l, n_l, m_l), = _run_parts([_mlstm_chunked(proj_m, gates_m, mnorm3, c0, n0, m0, chunk=N_META)],
                                     semantics=("parallel", "arbitrary"), name="mlstm_lead")
    prep_l = _gdn_prep(proj_m, gates_m, jnp.zeros((3 * H, 8, 128), F32), conv_w, chunk=N_META, nsub=1)
    _, s_l = _gdn_scan(prep_l, proj_m, gnorm3, c0, chunk=N_META)

    ns = SAMPLE_SEQS
    grp = Bs // ns
    m_in = jnp.pad(state_mlstm_m[0], ((0, 0), (0, 128 - H))).reshape(grp, ns, 128)
    mlstm_parts = _mlstm_chunked(proj_p, gates_p, mnorm3, c_l, n_l, m_l, chunk=MLSTM_CHUNK)
    sample_parts = _sample_step(
        proj_sm.reshape(nblk, n_sm // (ns * Ts), ns * Ts, 128), gates_sm.reshape(n_sm // (ns * Ts), ns * Ts, 128),
        state_mlstm_C[0].reshape(grp, ns, H, 128, 128),
        state_mlstm_n[0].reshape(grp, ns, H, 128), m_in, state_gdn_S[0].reshape(grp, ns, H, 128, 128),
        state_gdn_conv[0].reshape(grp, ns, CONV_W - 1, 3 * WIDTH), conv_w, mnorm, gnorm, steps=Ts,
        grid=mlstm_parts["grid"])
    (mix_a, p_c, p_n, p_m), (mix_s, s_c, s_n, s_m, s_s) = _run_parts(
        [mlstm_parts, sample_parts], semantics=("parallel", "arbitrary"), name="mlstm_and_sample")
    conv_init = proj_m[COL_GQ:COL_GZ, 0, N_META - 8:, :]
    prep_p = _gdn_prep(proj_p, gates_p, conv_init, conv_w, chunk=GDN_CHUNK, nsub=GDN_PREP_CHUNKS)
    mix_b, p_s = _gdn_scan(prep_p, proj_p, gnorm3, s_l, chunk=GDN_CHUNK)
    p_conv = conv_rows(proj_p[COL_GQ:COL_GZ, :, S - (CONV_W - 1):, :])

    x1_p, u2_p = _out_proj(mix_a.reshape(H, B * S, 128), 0, mix_b.reshape(H, B * S, 128), 0, w_out_b, xp,
                           g_post_mix, g_pre_ffn, tm=512)
    y_p = _ffn(u2_p, w_gu_b, w_dn_b, x1_p, g_post_ffn, tm=512, tf=512)

    mix_s2 = mix_s.reshape(Bs * Ts, 2 * WIDTH)
    s_conv = conv_rows(proj_sm[COL_GQ:COL_GZ, :Bs * Ts].reshape(3 * H, Bs, Ts, 128)[:, :, Ts - (CONV_W - 1):, :])
    x1_s, u2_s = _out_proj(mix_s2, 0, mix_s2, 1, w_out_b, xs, g_post_mix, g_pre_ffn, tm=512)
    y_s = _ffn(u2_s, w_gu_b, w_dn_b, x1_s, g_post_ffn, tm=512, tf=512)

    return (y_p.reshape(B, S, D), y_s.reshape(Bs, Ts, D),
            p_c[None], p_n[:, :, 0, :][None], p_m[:, :, 0, 0][None], p_s[None], p_conv[None],
            s_c.reshape(Bs, H, 128, 128)[None], s_n.reshape(Bs, H, 128)[None], s_m.reshape(Bs, 128)[:, :H][None],
            s_s.reshape(Bs, H, 128, 128)[None], s_conv[None])
```

```python
import functools

import jax
import jax.numpy as jnp
from jax import lax
from jax.experimental import pallas as pl
from jax.experimental.pallas import tpu as pltpu

F32 = jnp.float32
BF16 = jnp.bfloat16

D_MODEL = 2048
HEAD_DIM = 128
N_HEADS = 8
WIDTH = N_HEADS * HEAD_DIM
CONV_W = 4
N_META = 16
D_FF = 5632
EPS = 1e-6
Q_SCALE = HEAD_DIM ** -0.5

COL_MQ, COL_MK, COL_MV, COL_MO = 0, 8, 16, 24
COL_GQ, COL_GK, COL_GV, COL_GZ = 32, 40, 48, 56
COL_GATES = 64
PROJ_COLS = 66 * 128
LANE_MI, LANE_MF, LANE_GA, LANE_GB = 0, 8, 16, 24

MLSTM_CHUNK = 256
GDN_CHUNK = 64
GDN_PREP_CHUNKS = 32
GDN_SCAN_HEADS = 8
SAMPLE_SEQS = 2
VMEM_LIMIT = 56 * 1024 * 1024


def _sigmoid(x):
    return 0.5 * jnp.tanh(0.5 * x) + 0.5


def _softplus(x):
    return jnp.maximum(x, 0.0) + jnp.log1p(jnp.exp(-jnp.abs(x)))


def _log_sigmoid(x):
    return -_softplus(-x)


def _rms_scale(x):
    return lax.rsqrt(jnp.mean(x * x, axis=-1, keepdims=True) + EPS)


def _l2_scale(x):
    return lax.rsqrt(jnp.sum(x * x, axis=-1, keepdims=True) + EPS)


def _dot(a, b):
    return jnp.dot(a, b, preferred_element_type=F32)


def _dot_nt(a, b):
    return lax.dot_general(a, b, (((1,), (1,)), ((), ())), preferred_element_type=F32)


def _dot_tn(a, b):
    return lax.dot_general(a, b, (((0,), (0,)), ((), ())), preferred_element_type=F32)


def _dot_f32(a, b):
    return jnp.dot(a, b, precision=lax.Precision.HIGHEST, preferred_element_type=F32)


def _select_lane(x, lane_idx, target):
    return jnp.sum(jnp.where(lane_idx == target, x, 0.0), axis=1, keepdims=True)


REGROUP_TILE = 256


def _regroup_kernel(w_ref, mg_ref, gg_ref, o_ref):
    j = pl.program_id(0)
    wide_tiles = 2 * 4 * WIDTH // REGROUP_TILE

    @pl.when(j < wide_tiles)
    def _():
        o_ref[...] = w_ref[...].T.astype(BF16)

    @pl.when(j == wide_tiles)
    def _():
        pad = jnp.zeros((REGROUP_TILE - 4 * N_HEADS, w_ref.shape[1]), F32)
        o_ref[...] = jnp.concatenate([mg_ref[...], gg_ref[...], pad], axis=0).T.astype(BF16)


def _regroup_weights(w_t):
    n, d = w_t.shape
    wide = 4 * WIDTH
    narrow = 2 * N_HEADS
    assert n == 2 * (wide + narrow) and PROJ_COLS == 2 * wide + REGROUP_TILE
    wide_tiles = 2 * wide // REGROUP_TILE

    def src_row(j):
        j = jnp.minimum(j, wide_tiles - 1)
        return (j * (REGROUP_TILE // narrow) + jnp.where(j >= wide_tiles // 2, 1, 0)) * narrow

    return pl.pallas_call(
        _regroup_kernel,
        grid=(wide_tiles + 1,),
        in_specs=[pl.BlockSpec((pl.Element(REGROUP_TILE), pl.Element(d)), lambda j: (src_row(j), 0)),
                  pl.BlockSpec((narrow, d), lambda j: (wide // narrow, 0)),
                  pl.BlockSpec((narrow, d), lambda j: ((2 * wide + narrow) // narrow, 0))],
        out_specs=pl.BlockSpec((d, REGROUP_TILE), lambda j: (0, j)),
        out_shape=jax.ShapeDtypeStruct((d, PROJ_COLS), BF16),
        compiler_params=pltpu.CompilerParams(dimension_semantics=("parallel",), vmem_limit_bytes=VMEM_LIMIT),
        name="regroup_weights",
    )(w_t, w_t, w_t)


def _norm_matmul_kernel(x_ref, g_ref, w_ref, o_ref, u_ref):
    @pl.when(pl.program_id(1) == 0)
    def _():
        x = x_ref[...]
        u_ref[...] = (x * _rms_scale(x) * g_ref[...]).astype(BF16)

    res = _dot(u_ref[...], w_ref[...])
    for blk in range(o_ref.shape[0]):
        o_ref[blk] = res[:, blk * 128:(blk + 1) * 128]


def _norm_matmul(x, gain, w, *, tm, tn):
    m, d = x.shape
    n = w.shape[1]
    return pl.pallas_call(
        _norm_matmul_kernel,
        grid=(m // tm, n // tn),
        in_specs=[pl.BlockSpec((tm, d), lambda i, j: (i, 0)),
                  pl.BlockSpec((1, d), lambda i, j: (0, 0)),
                  pl.BlockSpec((d, tn), lambda i, j: (0, j))],
        out_specs=pl.BlockSpec((tn // 128, tm, 128), lambda i, j: (j, i, 0)),
        out_shape=jax.ShapeDtypeStruct((n // 128, m, 128), F32),
        scratch_shapes=[pltpu.VMEM((tm, d), BF16)],
        compiler_params=pltpu.CompilerParams(dimension_semantics=("parallel", "arbitrary"),
                                             vmem_limit_bytes=VMEM_LIMIT),
        name="norm_matmul",
    )(x, gain, w)


def _out_proj_kernel(ma_ref, mb_ref, wa_ref, wb_ref, x_ref, gpost_ref, gpre_ref, x1_ref, u2_ref):
    def rows(ref, sl):
        if len(ref.shape) == 2:
            return ref[sl, :].astype(BF16)
        return jnp.concatenate([ref[h, sl, :] for h in range(ref.shape[0])], axis=1).astype(BF16)

    tm = x_ref.shape[0]
    halves = [slice(0, tm // 2), slice(tm // 2, tm)]
    mos = [_dot(rows(ma_ref, sl), wa_ref[...]) + _dot(rows(mb_ref, sl), wb_ref[...]) for sl in halves]
    for sl, mo in zip(halves, mos):
        x1 = x_ref[sl, :] + mo * _rms_scale(mo) * gpost_ref[...]
        x1_ref[sl, :] = x1
        u2_ref[sl, :] = (x1 * _rms_scale(x1) * gpre_ref[...]).astype(BF16)


def _out_proj(mix_a, col_a, mix_b, col_b, w_out, x, g_post, g_pre, *, tm):
    m, d = x.shape

    def mix_spec(mix, col):
        if mix.ndim == 2:
            return pl.BlockSpec((tm, WIDTH), lambda i: (i, col))
        return pl.BlockSpec((N_HEADS, tm, 128), lambda i: (0, i, 0))

    return pl.pallas_call(
        _out_proj_kernel,
        grid=(m // tm,),
        in_specs=[mix_spec(mix_a, col_a),
                  mix_spec(mix_b, col_b),
                  pl.BlockSpec((WIDTH, d), lambda i: (0, 0)),
                  pl.BlockSpec((WIDTH, d), lambda i: (1, 0)),
                  pl.BlockSpec((tm, d), lambda i: (i, 0)),
                  pl.BlockSpec((1, d), lambda i: (0, 0)),
                  pl.BlockSpec((1, d), lambda i: (0, 0))],
        out_specs=[pl.BlockSpec((tm, d), lambda i: (i, 0)),
                   pl.BlockSpec((tm, d), lambda i: (i, 0))],
        out_shape=[jax.ShapeDtypeStruct((m, d), F32), jax.ShapeDtypeStruct((m, d), BF16)],
        compiler_params=pltpu.CompilerParams(dimension_semantics=("parallel",),
                                             vmem_limit_bytes=VMEM_LIMIT),
        name="out_proj",
    )(mix_a, mix_b, w_out, w_out, x, g_post, g_pre)


def _ffn_kernel(u_ref, wg_ref, wu_ref, wd_ref, x1_ref, gain_ref, o_ref, acc_ref):
    f = pl.program_id(1)

    @pl.when(f == 0)
    def _():
        acc_ref[...] = jnp.zeros_like(acc_ref)

    u = u_ref[...]
    g = _dot(u, wg_ref[...])
    up = _dot(u, wu_ref[...])
    hidden = (g * _sigmoid(g) * up).astype(BF16)
    acc_ref[...] += _dot(hidden, wd_ref[...])

    @pl.when(f == pl.num_programs(1) - 1)
    def _():
        y = acc_ref[...]
        o_ref[...] = x1_ref[...] + y * _rms_scale(y) * gain_ref[...]


def _ffn(u2, w_gate_up, w_down, x1, gain, *, tm, tf):
    m, d = x1.shape
    nf = D_FF // tf
    return pl.pallas_call(
        _ffn_kernel,
        grid=(m // tm, nf),
        in_specs=[pl.BlockSpec((tm, d), lambda i, f: (i, 0)),
                  pl.BlockSpec((d, tf), lambda i, f: (0, f)),
                  pl.BlockSpec((d, tf), lambda i, f: (0, nf + f)),
                  pl.BlockSpec((tf, d), lambda i, f: (f, 0)),
                  pl.BlockSpec((tm, d), lambda i, f: (i, 0)),
                  pl.BlockSpec((1, d), lambda i, f: (0, 0))],
        out_specs=pl.BlockSpec((tm, d), lambda i, f: (i, 0)),
        out_shape=jax.ShapeDtypeStruct((m, d), F32),
        scratch_shapes=[pltpu.VMEM((tm, d), F32)],
        compiler_params=pltpu.CompilerParams(dimension_semantics=("parallel", "arbitrary"),
                                             vmem_limit_bytes=VMEM_LIMIT),
        name="ffn",
    )(u2, w_gate_up, w_gate_up, w_down, x1, gain)


def _gate_act_kernel(g_ref, gbias_ref, arow_ref, o_ref):
    g = g_ref[0] + gbias_ref[...]
    lane = lax.broadcasted_iota(jnp.int32, g.shape, 1)
    decay = -jnp.exp(arow_ref[...]) * _softplus(g)
    o_ref[...] = jnp.where(lane < LANE_MF, g,
                           jnp.where(lane < LANE_GA, _log_sigmoid(g), jnp.where(lane < LANE_GB, decay, _sigmoid(g))))


def _gate_act(proj, gbias, arow, *, tm):
    m = proj.shape[1]
    return pl.pallas_call(
        _gate_act_kernel,
        grid=(m // tm,),
        in_specs=[pl.BlockSpec((1, tm, 128), lambda i: (COL_GATES, i, 0)),
                  pl.BlockSpec((1, 128), lambda i: (0, 0)),
                  pl.BlockSpec((1, 128), lambda i: (0, 0))],
        out_specs=pl.BlockSpec((tm, 128), lambda i: (i, 0)),
        out_shape=jax.ShapeDtypeStruct((m, 128), F32),
        compiler_params=pltpu.CompilerParams(dimension_semantics=("parallel",), vmem_limit_bytes=VMEM_LIMIT),
        name="gate_act",
    )(proj, gbias, arow)


def _mlstm_chunk_kernel(q_ref, k_ref, v_ref, og_ref, gates_ref, mnorm_ref, c0_ref, n0_ref, m0_ref,
                        h_ref, c_out_ref, n_out_ref, m_out_ref, c_s, n_s, m_s, *, chunk):
    L = chunk
    nb = q_ref.shape[1]
    shared = c0_ref.shape[0] == 1
    head = pl.program_id(0)
    c = pl.program_id(1)

    @pl.when(c == 0)
    def _():
        for b in range(nb):
            b0 = 0 if shared else b
            c_s[b] = c0_ref[b0, 0]
            n_s[b] = n0_ref[b0, 0]
            m_s[b] = m0_ref[b0, 0]

    lane = lax.broadcasted_iota(jnp.int32, (L, 128), 1)
    rows = lax.broadcasted_iota(jnp.int32, (L, L), 0)
    cols = lax.broadcasted_iota(jnp.int32, (L, L), 1)
    tril = rows >= cols
    triu = rows <= cols
    eye = rows == cols
    last = c == pl.num_programs(1) - 1

    seqs = range(nb)
    gates = [gates_ref[b] for b in seqs]
    li_col = [_select_lane(g, lane, head + LANE_MI) for g in gates]
    lf_col = [_select_lane(g, lane, head + LANE_MF) for g in gates]
    lf_row = [jnp.sum(jnp.where(eye, x, 0.0), axis=0, keepdims=True) for x in lf_col]
    li_row = [jnp.sum(jnp.where(eye, x, 0.0), axis=0, keepdims=True) for x in li_col]
    b_col = [jnp.sum(jnp.where(tril, x, 0.0), axis=1, keepdims=True) for x in lf_row]
    b_row = [jnp.sum(jnp.where(triu, x, 0.0), axis=0, keepdims=True) for x in lf_col]

    qb = [(q_ref[0, b] * Q_SCALE).astype(BF16) for b in seqs]
    kb = [k_ref[0, b].astype(BF16) for b in seqs]
    vb = [v_ref[0, b].astype(BF16) for b in seqs]
    qk = [_dot_nt(x, y) for x, y in zip(qb, kb)]
    qc = [_dot(x, c_s[b].astype(BF16)) for b, x in zip(seqs, qb)]

    m_prev = [m_s[b, :, 0:1] for b in seqs]
    a_col = [x + y for x, y in zip(b_col, m_prev)]
    dm = [jnp.where(tril, bc - br + lr, -jnp.inf) for bc, br, lr in zip(b_col, b_row, li_row)]
    m_t = [jnp.maximum(a, jnp.max(d, axis=1, keepdims=True)) for a, d in zip(a_col, dm)]
    inter = [jnp.exp(a - m) for a, m in zip(a_col, m_t)]
    s = [x * jnp.exp(d - m) for x, d, m in zip(qk, dm, m_t)]
    sv = [_dot(x.astype(BF16), v) for x, v in zip(s, vb)]

    m_last = [m[L - 1:L, :] for m in m_t]
    b_last = [x[L - 1:L, :] for x in b_col]
    d_c = [jnp.exp(bl + mp - ml) for bl, mp, ml in zip(b_last, m_prev, m_last)]
    kw = [jnp.exp(bl - bc + li - ml) * k_ref[0, b]
          for b, bl, bc, li, ml in zip(seqs, b_last, b_col, li_col, m_last)]
    c_upd = [_dot_tn(x.astype(BF16), v) for x, v in zip(kw, vb)]

    for b in seqs:
        q = q_ref[0, b] * Q_SCALE
        num = sv[b] + inter[b] * qc[b]
        den = jnp.sum(s[b], axis=1, keepdims=True) + inter[b] * jnp.sum(q * n_s[b], axis=1, keepdims=True)
        hh = num / jnp.maximum(jnp.abs(den), jnp.exp(-m_t[b]))
        out = hh * _rms_scale(hh) * mnorm_ref[0] * _sigmoid(og_ref[0, b])
        h_ref[0, b] = out.astype(h_ref.dtype)
        c_new = d_c[b] * c_s[b] + c_upd[b]
        n_new = d_c[b] * n_s[b] + jnp.sum(kw[b], axis=0, keepdims=True)
        m_new = jnp.broadcast_to(m_last[b], (1, 128))
        c_s[b], n_s[b], m_s[b] = c_new, n_new, m_new
        c_out_ref[b, 0], n_out_ref[b, 0], m_out_ref[b, 0] = c_new, n_new, m_new


def _mlstm_chunked(proj, gates, mnorm, c0, n0, m0, *, chunk):
    _, batch, seq, _ = proj.shape
    nb0 = c0.shape[0]

    def col(base):
        return pl.BlockSpec((1, batch, chunk, 128), lambda h, c: (base + h, 0, c, 0))

    def state(nb, shape):
        return pl.BlockSpec((nb, 1) + shape, lambda h, c: (0, h, 0, 0))

    return dict(
        body=functools.partial(_mlstm_chunk_kernel, chunk=chunk),
        grid=(N_HEADS, seq // chunk),
        in_specs=[col(COL_MQ), col(COL_MK), col(COL_MV), col(COL_MO),
                  pl.BlockSpec((batch, chunk, 128), lambda h, c: (0, c, 0)),
                  pl.BlockSpec((1, 1, 128), lambda h, c: (h, 0, 0)),
                  state(nb0, (128, 128)), state(nb0, (1, 128)), state(nb0, (1, 128))],
        out_specs=[pl.BlockSpec((1, batch, chunk, 128), lambda h, c: (h, 0, c, 0)),
                   state(batch, (128, 128)), state(batch, (1, 128)), state(batch, (1, 128))],
        out_shape=[jax.ShapeDtypeStruct((N_HEADS, batch, seq, 128), BF16),
                   jax.ShapeDtypeStruct((batch, N_HEADS, 128, 128), F32),
                   jax.ShapeDtypeStruct((batch, N_HEADS, 1, 128), F32),
                   jax.ShapeDtypeStruct((batch, N_HEADS, 1, 128), F32)],
        scratch_shapes=[pltpu.VMEM((batch, 128, 128), F32), pltpu.VMEM((batch, 1, 128), F32),
                        pltpu.VMEM((batch, 1, 128), F32)],
        args=(proj, proj, proj, proj, gates, mnorm, c0, n0, m0))


def _run_parts(parts, *, semantics, name):
    grid = parts[0]["grid"]
    assert all(p["grid"] == grid for p in parts)
    n_in = [len(p["in_specs"]) for p in parts]
    n_out = [len(p["out_specs"]) for p in parts]
    n_scr = [len(p["scratch_shapes"]) for p in parts]

    def body(*refs):
        ins, outs, scr = refs[:sum(n_in)], refs[sum(n_in):sum(n_in) + sum(n_out)], refs[sum(n_in) + sum(n_out):]
        for k, p in enumerate(parts):
            p["body"](*ins[sum(n_in[:k]):sum(n_in[:k + 1])], *outs[sum(n_out[:k]):sum(n_out[:k + 1])],
                      *scr[sum(n_scr[:k]):sum(n_scr[:k + 1])])

    outs = pl.pallas_call(
        body,
        grid=grid,
        in_specs=[s for p in parts for s in p["in_specs"]],
        out_specs=[s for p in parts for s in p["out_specs"]],
        out_shape=[s for p in parts for s in p["out_shape"]],
        scratch_shapes=[s for p in parts for s in p["scratch_shapes"]],
        compiler_params=pltpu.CompilerParams(dimension_semantics=semantics, vmem_limit_bytes=VMEM_LIMIT),
        name=name,
    )(*[a for p in parts for a in p["args"]])
    return [outs[sum(n_out[:k]):sum(n_out[:k + 1])] for k in range(len(parts))]


def _split(x):
    hi = x.astype(BF16)
    return hi, (x - hi.astype(F32)).astype(BF16)


def _dot_split(a, b):
    a_hi, a_lo = a
    b_hi, b_lo = b if isinstance(b, tuple) else (b, None)
    if a_hi.shape[1] % 128 == 0:
        main = _dot(jnp.concatenate([a_hi, a_lo], axis=1), jnp.concatenate([b_hi, b_hi], axis=0))
    else:
        main = _dot(a_hi, b_hi) + _dot(a_lo, b_hi)
    return main if b_lo is None else main + _dot(a_hi, b_lo)


def _unit_lower_inverse(mats, rows, cols, size):
    def same_block(shift):
        return (rows >> shift) == (cols >> shift)

    ident = jnp.where(rows == cols, 1.0, 0.0)
    d = [jnp.where(same_block(3), a, 0.0) for a in mats]
    d_s = [_split(x) for x in d]
    d2_s = [_split(_dot_split(x, x)) for x in d_s]
    d4_s = [_split(_dot_split(x, x)) for x in d2_s]
    t = [ident - x for x in d]
    t = [x + _dot_split(_split(x), y) for x, y in zip(t, d2_s)]
    t = [x + _dot_split(_split(x), y) for x, y in zip(t, d4_s)]
    a_hi = [x.astype(BF16) for x in mats]
    shift = 3
    while (1 << shift) < size:
        off = jnp.logical_and(same_block(shift + 1), jnp.logical_not(same_block(shift)))
        off_bf = jnp.where(off, 1.0, 0.0).astype(BF16)
        t_s = [_split(x) for x in t]
        te_s = [_split(_dot_split(x, y * off_bf)) for x, y in zip(t_s, a_hi)]
        t = [x - _dot_split(y, z[0]) for x, y, z in zip(t, te_s, t_s)]
        shift += 1
    return t


def _gdn_prep_kernel(q_ref, k_ref, v_ref, pq_ref, pk_ref, pv_ref, iq_ref, ik_ref, iv_ref, wq_ref, wk_ref, wv_ref,
                     gates_ref,
                     u_ref, w_ref, qe_ref, kd_ref, attn_ref, glast_ref, ext, *, chunk, nsub):
    L = chunk
    R = nsub * L
    head = pl.program_id(1)
    first = pl.program_id(2) == 0

    def conv(x_ref, prev_ref, init_ref, cw_ref):
        prev = jnp.where(first, init_ref[0], prev_ref[0, 0])
        ext[0:8, :] = prev
        ext[8:8 + R, :] = x_ref[0, 0]
        base = 8 - (CONV_W - 1)
        acc = ext[base:base + R, :] * cw_ref[0:1, :]
        for i in range(1, CONV_W):
            acc = acc + ext[base + i:base + i + R, :] * cw_ref[i:i + 1, :]
        return acc * _sigmoid(acc)

    q = conv(q_ref, pq_ref, iq_ref, wq_ref)
    k = conv(k_ref, pk_ref, ik_ref, wk_ref)
    v_all = conv(v_ref, pv_ref, iv_ref, wv_ref)
    qn_all = q * _l2_scale(q) * Q_SCALE
    kn_all = k * _l2_scale(k)

    gates = gates_ref[0]
    lane = lax.broadcasted_iota(jnp.int32, (R, 128), 1)
    lg_all = _select_lane(gates, lane, head + LANE_GA)
    beta_all = _select_lane(gates, lane, head + LANE_GB)

    U = min(R, 128)
    per_unit = U // L
    rows = lax.broadcasted_iota(jnp.int32, (U, U), 0)
    cols = lax.broadcasted_iota(jnp.int32, (U, U), 1)
    shift = L.bit_length() - 1
    same_chunk = (rows >> shift) == (cols >> shift)
    causal = jnp.logical_and(rows >= cols, same_chunk)
    strict = jnp.logical_and(rows > cols, same_chunk)
    upper = jnp.logical_and(rows <= cols, same_chunk)
    eye = rows == cols
    row_id = lax.broadcasted_iota(jnp.int32, (U, 1), 0)

    units = range(R // U)
    sls = [slice(j * U, (j + 1) * U) for j in units]
    g_cols, decays, a_mats, qks = [], [], [], []
    for sl in sls:
        lg_col = lg_all[sl]
        lg_row = jnp.sum(jnp.where(eye, lg_col, 0.0), axis=0, keepdims=True)
        g_col = jnp.sum(jnp.where(causal, lg_row, 0.0), axis=1, keepdims=True)
        g_row = jnp.sum(jnp.where(upper, lg_col, 0.0), axis=0, keepdims=True)
        decay = jnp.where(causal, jnp.exp(jnp.where(causal, g_col - g_row, 0.0)), 0.0)
        kb = kn_all[sl].astype(BF16)
        kk = _dot_nt(kb, kb)
        qks.append(_dot_nt(qn_all[sl].astype(BF16), kb))
        a_mats.append(jnp.where(strict, beta_all[sl] * kk * decay, 0.0))
        g_cols.append(g_col)
        decays.append(decay)

    t_mats = _unit_lower_inverse(a_mats, rows, cols, L)

    e_gs = [jnp.exp(g) for g in g_cols]
    rhs_s = [_split(jnp.concatenate([beta_all[sl] * v_all[sl], (beta_all[sl] * e_g) * kn_all[sl]], axis=1))
             for sl, e_g in zip(sls, e_gs)]
    uws = [_dot_split(_split(t), r) for t, r in zip(t_mats, rhs_s)]

    for j in units:
        sl, g_col = sls[j], g_cols[j]
        u_ref[0, 0, sl, :] = uws[j][:, 0:128]
        w_ref[0, 0, sl, :] = uws[j][:, 128:256].astype(BF16)
        qe_ref[0, 0, sl, :] = (qn_all[sl] * e_gs[j]).astype(BF16)
        attn = qks[j] * decays[j]
        g_last_col = jnp.zeros((U, 1), F32)
        for i in range(per_unit):
            g_last = g_col[(i + 1) * L - 1:(i + 1) * L, :]
            g_last_col = jnp.where((row_id >> shift) == i, g_last, g_last_col)
            glast_ref[0, 0, j * per_unit + i] = jnp.broadcast_to(jnp.exp(g_last), (1, 128))
            diag = attn[i * L:(i + 1) * L, :]
            if i > 0:
                diag = pltpu.roll(diag, U - i * L, axis=1)
            attn_ref[0, 0, j * per_unit + i] = diag[:, 0:L].astype(BF16)
        kd_ref[0, 0, sl, :] = (kn_all[sl] * jnp.exp(g_last_col - g_col)).astype(BF16)


def _gdn_prep(proj, gates, conv_init, conv_w, *, chunk, nsub):
    _, batch, seq, _ = proj.shape
    nc = seq // chunk
    R = nsub * chunk

    def col(base):
        return pl.BlockSpec((1, 1, R, 128), lambda b, h, c: (base + h, b, c, 0))

    def prev(base):
        return pl.BlockSpec((1, 1, 8, 128), lambda b, h, c: (base + h, b, jnp.maximum(c * (R // 8) - 1, 0), 0))

    def init(base):
        return pl.BlockSpec((1, 8, 128), lambda b, h, c: (base + h, 0, 0))

    def cw(base):
        return pl.BlockSpec((CONV_W, 128), lambda b, h, c: (0, base + h))

    def per_token(dtype):
        return (pl.BlockSpec((1, 1, R, 128), lambda b, h, c: (b, h, c, 0)),
                jax.ShapeDtypeStruct((batch, N_HEADS, seq, 128), dtype))

    outs = [per_token(F32), per_token(BF16), per_token(BF16), per_token(BF16),
            (pl.BlockSpec((1, 1, nsub, chunk, chunk), lambda b, h, c: (b, h, c, 0, 0)),
             jax.ShapeDtypeStruct((batch, N_HEADS, nc, chunk, chunk), BF16)),
            (pl.BlockSpec((1, 1, nsub, 1, 128), lambda b, h, c: (b, h, c, 0, 0)),
             jax.ShapeDtypeStruct((batch, N_HEADS, nc, 1, 128), F32))]
    return dict(
        body=functools.partial(_gdn_prep_kernel, chunk=chunk, nsub=nsub),
        grid=(batch, N_HEADS, nc // nsub),
        in_specs=[col(COL_GQ), col(COL_GK), col(COL_GV), prev(COL_GQ), prev(COL_GK), prev(COL_GV),
                  init(0), init(8), init(16), cw(0), cw(8), cw(16),
                  pl.BlockSpec((1, R, 128), lambda b, h, c: (b, c, 0))],
        out_specs=[o[0] for o in outs],
        out_shape=[o[1] for o in outs],
        scratch_shapes=[pltpu.VMEM((8 + R, 128), F32)],
        args=(proj, proj, proj, proj, proj, proj, conv_init, conv_init, conv_init, conv_w, conv_w, conv_w, gates))


def _cast_kernel(*refs):
    for src, dst in zip(refs[:len(refs) // 2], refs[len(refs) // 2:]):
        dst[...] = src[...].astype(dst.dtype)


def _cast_rows(weights, grid):
    steps = 1
    for g in grid:
        steps *= g

    def slab(w):
        rows = w.shape[0] // steps
        assert rows * steps == w.shape[0] and rows % 16 == 0

        def index(*idx):
            flat = idx[0]
            for i, g in zip(idx[1:], grid[1:]):
                flat = flat * g + i
            return (flat, 0)

        return pl.BlockSpec((rows, w.shape[1]), index)

    return dict(body=_cast_kernel, grid=grid, in_specs=[slab(w) for w in weights],
                out_specs=[slab(w) for w in weights],
                out_shape=[jax.ShapeDtypeStruct(w.shape, BF16) for w in weights], scratch_shapes=[],
                args=tuple(weights))


def _gdn_scan_kernel(u_ref, w_ref, qe_ref, kd_ref, attn_ref, glast_ref, gz_ref, gnorm_ref, s0_ref,
                     o_ref, s_out_ref, s_s):
    nb, nh = u_ref.shape[0], u_ref.shape[1]
    shared = s0_ref.shape[0] == 1
    c = pl.program_id(1)
    last = c == pl.num_programs(1) - 1
    chains = [(b, h) for b in range(nb) for h in range(nh)]

    @pl.when(c == 0)
    def _():
        for b, h in chains:
            s_s[b, h] = s0_ref[0 if shared else b, h]

    L = u_ref.shape[2]
    s_prev = [s_s[b, h] for b, h in chains]
    ws = [_dot(jnp.concatenate([w_ref[b, h], qe_ref[b, h]], axis=0), sp.astype(BF16))
          for (b, h), sp in zip(chains, s_prev)]
    vb = [(u_ref[b, h] - r[0:L]).astype(BF16) for (b, h), r in zip(chains, ws)]
    intra = [_dot(attn_ref[b, h, 0], v) for (b, h), v in zip(chains, vb)]
    upd = [_dot_tn(kd_ref[b, h], v) for (b, h), v in zip(chains, vb)]
    for i, (b, h) in enumerate(chains):
        s_s[b, h] = glast_ref[b, h, 0][:, 0:1] * s_prev[i] + upd[i]
        o = ws[i][L:2 * L] + intra[i]
        gz = gz_ref[h, b]
        o_ref[h, b] = (o * _rms_scale(o) * gnorm_ref[h] * (gz * _sigmoid(gz))).astype(o_ref.dtype)

    @pl.when(last)
    def _():
        for b, h in chains:
            s_out_ref[b, h] = s_s[b, h]


def _gdn_scan(prep, proj, gnorm, s0, *, chunk):
    _, batch, seq, _ = proj.shape
    u, w, qe, kd, attn, glast = prep
    hb = GDN_SCAN_HEADS
    assert COL_GZ % hb == 0 and N_HEADS % hb == 0
    tok = pl.BlockSpec((batch, hb, chunk, 128), lambda h, c: (0, h, c, 0))
    return pl.pallas_call(
        _gdn_scan_kernel,
        grid=(N_HEADS // hb, seq // chunk),
        in_specs=[tok, tok, tok, tok,
                  pl.BlockSpec((batch, hb, 1, chunk, chunk), lambda h, c: (0, h, c, 0, 0)),
                  pl.BlockSpec((batch, hb, 1, 1, 128), lambda h, c: (0, h, c, 0, 0)),
                  pl.BlockSpec((hb, batch, chunk, 128), lambda h, c: (COL_GZ // hb + h, 0, c, 0)),
                  pl.BlockSpec((hb, 1, 128), lambda h, c: (h, 0, 0)),
                  pl.BlockSpec((s0.shape[0], hb, 128, 128), lambda h, c: (0, h, 0, 0))],
        out_specs=[pl.BlockSpec((hb, batch, chunk, 128), lambda h, c: (h, 0, c, 0)),
                   pl.BlockSpec((batch, hb, 128, 128), lambda h, c: (0, h, 0, 0))],
        out_shape=[jax.ShapeDtypeStruct((N_HEADS, batch, seq, 128), BF16),
                   jax.ShapeDtypeStruct((batch, N_HEADS, 128, 128), F32)],
        scratch_shapes=[pltpu.VMEM((batch, hb, 128, 128), F32)],
        compiler_params=pltpu.CompilerParams(dimension_semantics=("parallel", "arbitrary"),
                                             vmem_limit_bytes=VMEM_LIMIT),
        name="gdn_scan",
    )(u, w, qe, kd, attn, glast, proj, gnorm, s0)


def _sample_kernel(p_ref, gates_ref, c_ref, n_ref, m_ref, s_ref, cb_ref, cw_ref, mnorm_ref, gnorm_ref,
                   mix_ref, c_out_ref, n_out_ref, m_out_ref, s_out_ref, ext, xc_s, *, steps):
    T = steps
    NS = SAMPLE_SEQS
    R = NS * T
    H = N_HEADS
    nbuf = CONV_W - 1

    for s in range(NS):
        ext[8 * s:8 * s + nbuf, :] = cb_ref[0, s]
        for blk in range(COL_GZ - COL_GQ):
            ext[8 * s + nbuf:8 * s + nbuf + T, blk * 128:(blk + 1) * 128] = p_ref[COL_GQ + blk, 0, s * T:(s + 1) * T, :]
        acc = ext[8 * s:8 * s + T, :] * cw_ref[0:1, :]
        for i in range(1, CONV_W):
            acc = acc + ext[8 * s + i:8 * s + i + T, :] * cw_ref[i:i + 1, :]
        xc_s[s * T:(s + 1) * T, :] = acc * _sigmoid(acc)

    rowi = lax.broadcasted_iota(jnp.int32, (R, 1), 0)
    tpos = rowi & (T - 1)
    second = rowi >= T
    seq_rows = [jnp.logical_not(second), second]

    def pick(x, j):
        return jnp.where(second, x[T + j:T + j + 1, :], x[j:j + 1, :])

    def cumsum_t(x):
        acc = jnp.where(tpos >= 0, pick(x, 0), 0.0)
        for j in range(1, T):
            acc = acc + jnp.where(tpos >= j, pick(x, j), 0.0)
        return acc

    def pcol(block, h):
        return p_ref[block + h, 0]

    gates = gates_ref[0]
    lane = lax.broadcasted_iota(jnp.int32, (R, 128), 1)
    lf_all = lg_all = beta_all = gates
    m_in = m_ref[0]
    lane_s = lax.broadcasted_iota(jnp.int32, (NS, 128), 1)
    row_s = lax.broadcasted_iota(jnp.int32, (NS, 128), 0)
    m_acc = jnp.zeros((NS, 128), F32)

    qc_all, gdn_in = [], []
    for h in range(H):
        qb = (pcol(COL_MQ, h) * Q_SCALE).astype(BF16)
        qc_all.append(jnp.where(second, _dot(qb, c_ref[0, 1, h].astype(BF16)),
                                _dot(qb, c_ref[0, 0, h].astype(BF16))))
        cq = xc_s[:, h * 128:(h + 1) * 128]
        ck = xc_s[:, WIDTH + h * 128:WIDTH + (h + 1) * 128]
        qn = cq * _l2_scale(cq) * Q_SCALE
        kn = ck * _l2_scale(ck)
        kq = jnp.concatenate([kn, qn], axis=0).astype(BF16)
        r0 = _dot(kq, s_ref[0, 0, h].astype(BF16))
        r1 = _dot(kq, s_ref[0, 1, h].astype(BF16))
        gdn_in.append((qn, kn, jnp.where(second, r1[0:R], r0[0:R]), jnp.where(second, r1[R:2 * R], r0[R:2 * R])))

    c_upd, s_upd = [], []
    for h in range(H):
        li = _select_lane(gates, lane, LANE_MI + h)
        lf = _select_lane(lf_all, lane, LANE_MF + h)
        mh = _select_lane(m_in, lane_s, h)
        m0 = jnp.where(second, mh[1:2, :], mh[0:1, :])
        b = cumsum_t(lf)
        a = b + m0
        dcol = [b - pick(b, j) + pick(li, j) for j in range(T)]
        m_t = a
        for j in range(T):
            m_t = jnp.maximum(m_t, jnp.where(tpos >= j, dcol[j], -jnp.inf))
        inter = jnp.exp(a - m_t)

        q = pcol(COL_MQ, h) * Q_SCALE
        k = pcol(COL_MK, h)
        v = pcol(COL_MV, h)
        num = jnp.zeros((R, 128), F32)
        den = jnp.zeros((R, 1), F32)
        for j in range(T):
            w_j = jnp.where(tpos >= j, jnp.exp(jnp.where(tpos >= j, dcol[j] - m_t, 0.0)), 0.0)
            s_j = jnp.sum(q * pick(k, j), axis=1, keepdims=True) * w_j
            num = num + s_j * pick(v, j)
            den = den + s_j
        qc = qc_all[h]
        n_sel =jnp.where(second, n_ref[0, 1, h:h + 1, :], n_ref[0, 0, h:h + 1, :])
        num = num + inter * qc
        den = den + inter * jnp.sum(q * n_sel, axis=1, keepdims=True)
        hh = num / jnp.maximum(jnp.abs(den), jnp.exp(-m_t))
        mix_ref[0, :, h * 128:(h + 1) * 128] = hh * _rms_scale(hh) * mnorm_ref[h:h + 1, :] * _sigmoid(pcol(COL_MO, h))

        m_l = pick(m_t, T - 1)
        b_l = pick(b, T - 1)
        kw = jnp.exp(b_l - b + li - m_l) * k
        d_c = jnp.exp(b_l + m0 - m_l)
        c_upd.append((kw, d_c, v.astype(BF16)))
        for s in range(NS):
            kws = jnp.where(seq_rows[s], kw, 0.0)
            n_out_ref[0, s, h:h + 1, :] = (d_c[s * T:s * T + 1, :] * n_ref[0, s, h:h + 1, :]
                                           + jnp.sum(kws, axis=0, keepdims=True))
            m_acc = jnp.where(jnp.logical_and(lane_s == h, row_s == s), m_l[s * T:s * T + 1, :], m_acc)

        lg = _select_lane(lg_all, lane, LANE_GA + h)
        beta = _select_lane(beta_all, lane, LANE_GB + h)
        g = cumsum_t(lg)
        e_g = jnp.exp(g)
        cv = xc_s[:, 2 * WIDTH + h * 128:2 * WIDTH + (h + 1) * 128]
        qn, kn, k_s, q_s = gdn_in[h]
        x = beta * (cv - e_g * k_s)
        rel = [jnp.exp(jnp.where(tpos >= j, g - pick(g, j), 0.0)) for j in range(T)]
        for j in range(T - 1):
            kk_j = jnp.sum(kn * pick(kn, j), axis=1, keepdims=True)
            a_j = jnp.where(tpos > j, beta * kk_j * rel[j], 0.0)
            x = x - a_j * pick(x, j)
        o = e_g * q_s
        for j in range(T):
            qk_j = jnp.sum(qn * pick(kn, j), axis=1, keepdims=True)
            o = o + jnp.where(tpos >= j, qk_j * rel[j], 0.0) * pick(x, j)
        gz = pcol(COL_GZ, h)
        mix_ref[0, :, WIDTH + h * 128:WIDTH + (h + 1) * 128] = (
            o * _rms_scale(o) * gnorm_ref[h:h + 1, :] * (gz * _sigmoid(gz)))

        g_l = pick(g, T - 1)
        s_upd.append((kn * jnp.exp(g_l - g), jnp.exp(g_l), x.astype(BF16)))
    m_out_ref[0] = m_acc

    for h in range(H):
        for (lhs, scale, rhs), st_ref, out_ref in ((c_upd[h], c_ref, c_out_ref), (s_upd[h], s_ref, s_out_ref)):
            for s in range(NS):
                rows_s = jnp.where(seq_rows[s], lhs, 0.0).astype(BF16)
                out_ref[0, s, h] = scale[s * T:s * T + 1, :] * st_ref[0, s, h] + _dot_tn(rows_s, rhs)


def _sample_step(proj, gates, c, n, m, s, conv_buf, conv_w, mnorm, gnorm, *, steps, grid):
    groups = c.shape[0]
    ns = SAMPLE_SEQS
    assert ns == 2 and ns * steps == 8 and steps >= CONV_W - 1 and steps & (steps - 1) == 0
    assert len(grid) == 2 and grid[0] * grid[1] == groups

    def per_group(*shape):
        return pl.BlockSpec((1,) + shape, lambda i, j: (i * grid[1] + j,) + (0,) * len(shape))

    def const(*shape):
        return pl.BlockSpec(shape, lambda i, j: (0,) * len(shape))

    state = (ns, N_HEADS, 128, 128)
    return dict(
        body=functools.partial(_sample_kernel, steps=steps),
        grid=grid,
        in_specs=[pl.BlockSpec((PROJ_COLS // 128, 1, ns * steps, 128), lambda i, j: (0, i * grid[1] + j, 0, 0)),
                  per_group(ns * steps, 128), per_group(*state), per_group(ns, N_HEADS, 128),
                  per_group(ns, 128), per_group(*state), per_group(ns, CONV_W - 1, 3 * WIDTH),
                  const(CONV_W, 3 * WIDTH), const(N_HEADS, 128), const(N_HEADS, 128)],
        out_specs=[per_group(ns * steps, 2 * WIDTH), per_group(*state), per_group(ns, N_HEADS, 128),
                   per_group(ns, 128), per_group(*state)],
        out_shape=[jax.ShapeDtypeStruct((groups, ns * steps, 2 * WIDTH), F32),
                   jax.ShapeDtypeStruct((groups,) + state, F32),
                   jax.ShapeDtypeStruct((groups, ns, N_HEADS, 128), F32),
                   jax.ShapeDtypeStruct((groups, ns, 128), F32),
                   jax.ShapeDtypeStruct((groups,) + state, F32)],
        scratch_shapes=[pltpu.VMEM((8 * ns, 3 * WIDTH), F32), pltpu.VMEM((ns * steps, 3 * WIDTH), F32)],
        args=(proj, gates, c, n, m, s, conv_buf, conv_w, mnorm, gnorm))


def kernel(x_prompt, x_sample, state_mlstm_C, state_mlstm_n, state_mlstm_m, state_gdn_S, state_gdn_conv, meta_tokens, norm_pre_mix, norm_post_mix, norm_pre_ffn, norm_post_ffn, w_in, mlstm_b_i, mlstm_b_f, mlstm_norm, gdn_conv_w, gdn_A_log, gdn_dt_bias, gdn_norm, w_out, w_gate_up, w_down):
    assert w_in.shape[0] == 1, "single-layer trunk"
    B, S, D = x_prompt.shape
    Bs, Ts, _ = x_sample.shape
    H = N_HEADS

    w_proj = _regroup_weights(jnp.transpose(w_in[0]))
    zeros_h = jnp.zeros((H,), F32)
    gbias = jnp.concatenate([mlstm_b_i[0], mlstm_b_f[0], gdn_dt_bias[0], zeros_h,
                             jnp.zeros((128 - 4 * H,), F32)]).reshape(1, 128)
    arow = jnp.concatenate([zeros_h, zeros_h, gdn_A_log[0], zeros_h,
                            jnp.zeros((128 - 4 * H,), F32)]).reshape(1, 128)
    g_pre_mix = norm_pre_mix[0].reshape(1, D)
    g_post_mix = norm_post_mix[0].reshape(1, D)
    g_pre_ffn = norm_pre_ffn[0].reshape(1, D)
    g_post_ffn = norm_post_ffn[0].reshape(1, D)
    mnorm = mlstm_norm[0]
    gnorm = gdn_norm[0]
    mnorm3 = mnorm.reshape(H, 1, 128)
    gnorm3 = gnorm.reshape(H, 1, 128)
    conv_w = gdn_conv_w[0]

    xp = x_prompt.reshape(B * S, D)
    xs = x_sample.reshape(Bs * Ts, D)

    nblk = PROJ_COLS // 128
    proj_p = _norm_matmul(xp, g_pre_mix, w_proj, tm=1024, tn=768).reshape(nblk, B, S, 128)
    n_sm = Bs * Ts + N_META
    proj_sm = _norm_matmul(jnp.concatenate([xs, meta_tokens], axis=0), g_pre_mix, w_proj, tm=n_sm, tn=768)
    proj_m = proj_sm[:, Bs * Ts:].reshape(nblk, 1, N_META, 128)

    gates_p = _gate_act(proj_p.reshape(nblk, B * S, 128), gbias, arow, tm=1024).reshape(B, S, 128)
    gates_sm = _gate_act(proj_sm, gbias, arow, tm=n_sm)
    gates_m = gates_sm[Bs * Ts:].reshape(1, N_META, 128)

    def conv_rows(blocks):
        return jnp.moveaxis(blocks, 0, -2).reshape(blocks.shape[1:-1] + (3 * WIDTH,))

    c0, n0, m0 = (jnp.zeros((1, H, 128, 128), F32), jnp.zeros((1, H, 1, 128), F32), jnp.zeros((1, H, 1, 128), F32))
    (_, c_l, n_l, m_l), = _run_parts([_mlstm_chunked(proj_m, gates_m, mnorm3, c0, n0, m0, chunk=N_META)],
                                     semantics=("parallel", "arbitrary"), name="mlstm_lead")
    prep_l, = _run_parts([_gdn_prep(proj_m, gates_m, jnp.zeros((3 * H, 8, 128), F32), conv_w, chunk=N_META, nsub=1)],
                         semantics=("parallel", "parallel", "parallel"), name="gdn_prep_lead")
    _, s_l = _gdn_scan(prep_l, proj_m, gnorm3, c0, chunk=N_META)

    ns = SAMPLE_SEQS
    grp = Bs // ns
    m_in = jnp.pad(state_mlstm_m[0], ((0, 0), (0, 128 - H))).reshape(grp, ns, 128)
    mlstm_parts = _mlstm_chunked(proj_p, gates_p, mnorm3, c_l, n_l, m_l, chunk=MLSTM_CHUNK)
    sample_parts = _sample_step(
        proj_sm.reshape(nblk, n_sm // (ns * Ts), ns * Ts, 128), gates_sm.reshape(n_sm // (ns * Ts), ns * Ts, 128),
        state_mlstm_C[0].reshape(grp, ns, H, 128, 128),
        state_mlstm_n[0].reshape(grp, ns, H, 128), m_in, state_gdn_S[0].reshape(grp, ns, H, 128, 128),
        state_gdn_conv[0].reshape(grp, ns, CONV_W - 1, 3 * WIDTH), conv_w, mnorm, gnorm, steps=Ts,
        grid=mlstm_parts["grid"])
    (mix_a, p_c, p_n, p_m), (mix_s, s_c, s_n, s_m, s_s) = _run_parts(
        [mlstm_parts, sample_parts], semantics=("parallel", "arbitrary"), name="mlstm_and_sample")
    conv_init = proj_m[COL_GQ:COL_GZ, 0, N_META - 8:, :]
    prep_parts = _gdn_prep(proj_p, gates_p, conv_init, conv_w, chunk=GDN_CHUNK, nsub=GDN_PREP_CHUNKS)
    prep_p, (w_out_b, w_gu_b, w_dn_b) = _run_parts(
        [prep_parts, _cast_rows([w_out[0], w_gate_up[0], w_down[0]], prep_parts["grid"])],
        semantics=("parallel", "parallel", "parallel"), name="gdn_prep_and_casts")
    mix_b, p_s = _gdn_scan(prep_p, proj_p, gnorm3, s_l, chunk=GDN_CHUNK)
    p_conv = conv_rows(proj_p[COL_GQ:COL_GZ, :, S - (CONV_W - 1):, :])

    x1_p, u2_p = _out_proj(mix_a.reshape(H, B * S, 128), 0, mix_b.reshape(H, B * S, 128), 0, w_out_b, xp,
                           g_post_mix, g_pre_ffn, tm=512)
    y_p = _ffn(u2_p, w_gu_b, w_dn_b, x1_p, g_post_ffn, tm=512, tf=512)

    mix_s2 = mix_s.reshape(Bs * Ts, 2 * WIDTH)
    s_conv = conv_rows(proj_sm[COL_GQ:COL_GZ, :Bs * Ts].reshape(3 * H, Bs, Ts, 128)[:, :, Ts - (CONV_W - 1):, :])
    x1_s, u2_s = _out_proj(mix_s2, 0, mix_s2, 1, w_out_b, xs, g_post_mix, g_pre_ffn, tm=512)
    y_s = _ffn(u2_s, w_gu_b, w_dn_b, x1_s, g_post_ffn, tm=512, tf=512)

    return (y_p.reshape(B, S, D), y_s.reshape(Bs, Ts, D),
            p_c[None], p_n[:, :, 0, :][None], p_m[:, :, 0, 0][None], p_s[None], p_conv[None],
            s_c.reshape(Bs, H, 128, 128)[None], s_n.reshape(Bs, H, 128)[None], s_m.reshape(Bs, 128)[:, :H][None],
            s_s.reshape(Bs, H, 128, 128)[None], s_conv[None])
```

```python
import functools

import jax
import jax.numpy as jnp
from jax import lax
from jax.experimental import pallas as pl
from jax.experimental.pallas import tpu as pltpu

F32 = jnp.float32
BF16 = jnp.bfloat16

HEAD_DIM = 128
N_HEADS = 8
WIDTH = N_HEADS * HEAD_DIM
CONV_W = 4
N_META = 16
D_FF = 5632
EPS = 1e-6
Q_SCALE = HEAD_DIM ** -0.5

COL_MQ, COL_MK, COL_MV, COL_MO = 0, 8, 16, 24
COL_GQ, COL_GK, COL_GV, COL_GZ = 32, 40, 48, 56
COL_GATES = 64
PROJ_COLS = 66 * 128
LANE_MI, LANE_MF, LANE_GA, LANE_GB = 0, 8, 16, 24

MLSTM_CHUNK = 256
GDN_CHUNK = 64
GDN_PREP_CHUNKS = 32
GDN_SCAN_HEADS = 8
SAMPLE_SEQS = 2

MXU_TILE = 256
VMEM_LIMIT = 56 * 1024 * 1024
PROJ_TM, PROJ_TN = 1024, 3 * MXU_TILE
ROW_TM = 512
FFN_TF = 2 * MXU_TILE


def _sigmoid(x):
    return 0.5 * jnp.tanh(0.5 * x) + 0.5


def _softplus(x):
    return jnp.maximum(x, 0.0) + jnp.log1p(jnp.exp(-jnp.abs(x)))


def _log_sigmoid(x):
    return -_softplus(-x)


def _rms_scale(x):
    return lax.rsqrt(jnp.mean(x * x, axis=-1, keepdims=True) + EPS)


def _l2_scale(x):
    return lax.rsqrt(jnp.sum(x * x, axis=-1, keepdims=True) + EPS)


def _dot(a, b):
    return jnp.dot(a, b, preferred_element_type=F32)


def _dot_nt(a, b):
    return lax.dot_general(a, b, (((1,), (1,)), ((), ())), preferred_element_type=F32)


def _dot_tn(a, b):
    return lax.dot_general(a, b, (((0,), (0,)), ((), ())), preferred_element_type=F32)


def _select_lane(x, lane_idx, target):
    return jnp.sum(jnp.where(lane_idx == target, x, 0.0), axis=1, keepdims=True)


REGROUP_TILE = 256


def _regroup_kernel(w_ref, mg_ref, gg_ref, o_ref):
    j = pl.program_id(0)
    wide_tiles = 2 * 4 * WIDTH // REGROUP_TILE

    @pl.when(j < wide_tiles)
    def _():
        o_ref[...] = w_ref[...].T.astype(BF16)

    @pl.when(j == wide_tiles)
    def _():
        pad = jnp.zeros((REGROUP_TILE - 4 * N_HEADS, w_ref.shape[1]), F32)
        o_ref[...] = jnp.concatenate([mg_ref[...], gg_ref[...], pad], axis=0).T.astype(BF16)


def _regroup_weights(w_t):
    n, d = w_t.shape
    wide = 4 * WIDTH
    narrow = 2 * N_HEADS
    assert n == 2 * (wide + narrow) and PROJ_COLS == 2 * wide + REGROUP_TILE
    wide_tiles = 2 * wide // REGROUP_TILE

    def src_row(j):
        j = jnp.minimum(j, wide_tiles - 1)
        return (j * (REGROUP_TILE // narrow) + jnp.where(j >= wide_tiles // 2, 1, 0)) * narrow

    return pl.pallas_call(
        _regroup_kernel,
        grid=(wide_tiles + 1,),
        in_specs=[pl.BlockSpec((pl.Element(REGROUP_TILE), pl.Element(d)), lambda j: (src_row(j), 0)),
                  pl.BlockSpec((narrow, d), lambda j: (wide // narrow, 0)),
                  pl.BlockSpec((narrow, d), lambda j: ((2 * wide + narrow) // narrow, 0))],
        out_specs=pl.BlockSpec((d, REGROUP_TILE), lambda j: (0, j)),
        out_shape=jax.ShapeDtypeStruct((d, PROJ_COLS), BF16),
        compiler_params=pltpu.CompilerParams(dimension_semantics=("parallel",), vmem_limit_bytes=VMEM_LIMIT),
        name="regroup_weights",
    )(w_t, w_t, w_t)


def _norm_matmul_kernel(x_ref, g_ref, w_ref, o_ref, u_ref):
    @pl.when(pl.program_id(1) == 0)
    def _():
        x = x_ref[...]
        u_ref[...] = (x * _rms_scale(x) * g_ref[...]).astype(BF16)

    res = _dot(u_ref[...], w_ref[...])
    for blk in range(o_ref.shape[0]):
        o_ref[blk] = res[:, blk * 128:(blk + 1) * 128]


def _norm_matmul(x, gain, w, *, tm, tn):
    m, d = x.shape
    n = w.shape[1]
    return pl.pallas_call(
        _norm_matmul_kernel,
        grid=(m // tm, n // tn),
        in_specs=[pl.BlockSpec((tm, d), lambda i, j: (i, 0)),
                  pl.BlockSpec((1, d), lambda i, j: (0, 0)),
                  pl.BlockSpec((d, tn), lambda i, j: (0, j))],
        out_specs=pl.BlockSpec((tn // 128, tm, 128), lambda i, j: (j, i, 0)),
        out_shape=jax.ShapeDtypeStruct((n // 128, m, 128), F32),
        scratch_shapes=[pltpu.VMEM((tm, d), BF16)],
        compiler_params=pltpu.CompilerParams(dimension_semantics=("parallel", "arbitrary"),
                                             vmem_limit_bytes=VMEM_LIMIT),
        name="norm_matmul",
    )(x, gain, w)


def _out_proj_kernel(ma_ref, mb_ref, wa_ref, wb_ref, x_ref, gpost_ref, gpre_ref, x1_ref, u2_ref):
    def rows(ref, sl):
        if len(ref.shape) == 2:
            return ref[sl, :].astype(BF16)
        return jnp.concatenate([ref[h, sl, :] for h in range(ref.shape[0])], axis=1).astype(BF16)

    tm = x_ref.shape[0]
    halves = [slice(0, tm // 2), slice(tm // 2, tm)]
    mos = [_dot(rows(ma_ref, sl), wa_ref[...]) + _dot(rows(mb_ref, sl), wb_ref[...]) for sl in halves]
    for sl, mo in zip(halves, mos):
        x1 = x_ref[sl, :] + mo * _rms_scale(mo) * gpost_ref[...]
        x1_ref[sl, :] = x1
        u2_ref[sl, :] = (x1 * _rms_scale(x1) * gpre_ref[...]).astype(BF16)


def _out_proj(mix_a, col_a, mix_b, col_b, w_out, x, g_post, g_pre, *, tm):
    m, d = x.shape

    def mix_spec(mix, col):
        if mix.ndim == 2:
            return pl.BlockSpec((tm, WIDTH), lambda i: (i, col))
        return pl.BlockSpec((N_HEADS, tm, 128), lambda i: (0, i, 0))

    return pl.pallas_call(
        _out_proj_kernel,
        grid=(m // tm,),
        in_specs=[mix_spec(mix_a, col_a),
                  mix_spec(mix_b, col_b),
                  pl.BlockSpec((WIDTH, d), lambda i: (0, 0)),
                  pl.BlockSpec((WIDTH, d), lambda i: (1, 0)),
                  pl.BlockSpec((tm, d), lambda i: (i, 0)),
                  pl.BlockSpec((1, d), lambda i: (0, 0)),
                  pl.BlockSpec((1, d), lambda i: (0, 0))],
        out_specs=[pl.BlockSpec((tm, d), lambda i: (i, 0)),
                   pl.BlockSpec((tm, d), lambda i: (i, 0))],
        out_shape=[jax.ShapeDtypeStruct((m, d), F32), jax.ShapeDtypeStruct((m, d), BF16)],
        compiler_params=pltpu.CompilerParams(dimension_semantics=("parallel",),
                                             vmem_limit_bytes=VMEM_LIMIT),
        name="out_proj",
    )(mix_a, mix_b, w_out, w_out, x, g_post, g_pre)


def _ffn_kernel(u_ref, wgu_ref, wd_ref, x1_ref, gain_ref, o_ref, acc_ref):
    f = pl.program_id(1)
    tf = wd_ref.shape[0]

    @pl.when(f == 0)
    def _():
        acc_ref[...] = jnp.zeros_like(acc_ref)

    gu = _dot(u_ref[...], wgu_ref[...])
    g, up = gu[:, :tf], gu[:, tf:]
    hidden = (g * _sigmoid(g) * up).astype(BF16)
    acc_ref[...] += _dot(hidden, wd_ref[...])

    @pl.when(f == pl.num_programs(1) - 1)
    def _():
        y = acc_ref[...]
        o_ref[...] = x1_ref[...] + y * _rms_scale(y) * gain_ref[...]


def _ffn(u2, w_gate_up, w_down, x1, gain, *, tm, tf):
    m, d = x1.shape
    nf = D_FF // tf
    return pl.pallas_call(
        _ffn_kernel,
        grid=(m // tm, nf),
        in_specs=[pl.BlockSpec((tm, d), lambda i, f: (i, 0)),
                  pl.BlockSpec((d, 2 * tf), lambda i, f: (0, f)),
                  pl.BlockSpec((tf, d), lambda i, f: (f, 0)),
                  pl.BlockSpec((tm, d), lambda i, f: (i, 0)),
                  pl.BlockSpec((1, d), lambda i, f: (0, 0))],
        out_specs=pl.BlockSpec((tm, d), lambda i, f: (i, 0)),
        out_shape=jax.ShapeDtypeStruct((m, d), F32),
        scratch_shapes=[pltpu.VMEM((tm, d), F32)],
        compiler_params=pltpu.CompilerParams(dimension_semantics=("parallel", "arbitrary"),
                                             vmem_limit_bytes=VMEM_LIMIT),
        name="ffn",
    )(u2, w_gate_up, w_down, x1, gain)


def _gate_act_kernel(g_ref, gbias_ref, arow_ref, o_ref):
    g = g_ref[0] + gbias_ref[...]
    lane = lax.broadcasted_iota(jnp.int32, g.shape, 1)
    decay = -jnp.exp(arow_ref[...]) * _softplus(g)
    o_ref[...] = jnp.where(lane < LANE_MF, g,
                           jnp.where(lane < LANE_GA, _log_sigmoid(g), jnp.where(lane < LANE_GB, decay, _sigmoid(g))))


def _gate_act(proj, gbias, arow, *, tm):
    m = proj.shape[1]
    return pl.pallas_call(
        _gate_act_kernel,
        grid=(m // tm,),
        in_specs=[pl.BlockSpec((1, tm, 128), lambda i: (COL_GATES, i, 0)),
                  pl.BlockSpec((1, 128), lambda i: (0, 0)),
                  pl.BlockSpec((1, 128), lambda i: (0, 0))],
        out_specs=pl.BlockSpec((tm, 128), lambda i: (i, 0)),
        out_shape=jax.ShapeDtypeStruct((m, 128), F32),
        compiler_params=pltpu.CompilerParams(dimension_semantics=("parallel",), vmem_limit_bytes=VMEM_LIMIT),
        name="gate_act",
    )(proj, gbias, arow)


def _mlstm_chunk_kernel(q_ref, k_ref, v_ref, og_ref, gates_ref, mnorm_ref, c0_ref, n0_ref, m0_ref,
                        h_ref, c_out_ref, n_out_ref, m_out_ref, c_s, n_s, m_s, *, chunk):
    L = chunk
    nb = q_ref.shape[1]
    shared = c0_ref.shape[0] == 1
    head = pl.program_id(0)
    c = pl.program_id(1)

    @pl.when(c == 0)
    def _():
        for b in range(nb):
            b0 = 0 if shared else b
            c_s[b] = c0_ref[b0, 0]
            n_s[b] = n0_ref[b0, 0]
            m_s[b] = m0_ref[b0, 0]

    lane = lax.broadcasted_iota(jnp.int32, (L, 128), 1)
    rows = lax.broadcasted_iota(jnp.int32, (L, L), 0)
    cols = lax.broadcasted_iota(jnp.int32, (L, L), 1)
    tril = rows >= cols
    triu = rows <= cols
    eye = rows == cols
    last = c == pl.num_programs(1) - 1

    seqs = range(nb)
    gates = [gates_ref[b] for b in seqs]
    li_col = [_select_lane(g, lane, head + LANE_MI) for g in gates]
    lf_col = [_select_lane(g, lane, head + LANE_MF) for g in gates]
    lf_row = [jnp.sum(jnp.where(eye, x, 0.0), axis=0, keepdims=True) for x in lf_col]
    li_row = [jnp.sum(jnp.where(eye, x, 0.0), axis=0, keepdims=True) for x in li_col]
    b_col = [jnp.sum(jnp.where(tril, x, 0.0), axis=1, keepdims=True) for x in lf_row]
    b_row = [jnp.sum(jnp.where(triu, x, 0.0), axis=0, keepdims=True) for x in lf_col]

    qb = [(q_ref[0, b] * Q_SCALE).astype(BF16) for b in seqs]
    kb = [k_ref[0, b].astype(BF16) for b in seqs]
    vb = [v_ref[0, b].astype(BF16) for b in seqs]
    qk = [_dot_nt(x, y) for x, y in zip(qb, kb)]
    qc = [_dot(x, c_s[b].astype(BF16)) for b, x in zip(seqs, qb)]

    m_prev = [m_s[b, :, 0:1] for b in seqs]
    a_col = [x + y for x, y in zip(b_col, m_prev)]
    dm = [jnp.where(tril, bc - br + lr, -jnp.inf) for bc, br, lr in zip(b_col, b_row, li_row)]
    m_t = [jnp.maximum(a, jnp.max(d, axis=1, keepdims=True)) for a, d in zip(a_col, dm)]
    inter = [jnp.exp(a - m) for a, m in zip(a_col, m_t)]
    s = [x * jnp.exp(d - m) for x, d, m in zip(qk, dm, m_t)]
    sv = [_dot(x.astype(BF16), v) for x, v in zip(s, vb)]

    m_last = [m[L - 1:L, :] for m in m_t]
    b_last = [x[L - 1:L, :] for x in b_col]
    d_c = [jnp.exp(bl + mp - ml) for bl, mp, ml in zip(b_last, m_prev, m_last)]
    kw = [jnp.exp(bl - bc + li - ml) * k_ref[0, b]
          for b, bl, bc, li, ml in zip(seqs, b_last, b_col, li_col, m_last)]
    c_upd = [_dot_tn(x.astype(BF16), v) for x, v in zip(kw, vb)]

    for b in seqs:
        q = q_ref[0, b] * Q_SCALE
        num = sv[b] + inter[b] * qc[b]
        den = jnp.sum(s[b], axis=1, keepdims=True) + inter[b] * jnp.sum(q * n_s[b], axis=1, keepdims=True)
        hh = num / jnp.maximum(jnp.abs(den), jnp.exp(-m_t[b]))
        out = hh * _rms_scale(hh) * mnorm_ref[0] * _sigmoid(og_ref[0, b])
        h_ref[0, b] = out.astype(h_ref.dtype)
        c_new = d_c[b] * c_s[b] + c_upd[b]
        n_new = d_c[b] * n_s[b] + jnp.sum(kw[b], axis=0, keepdims=True)
        m_new = jnp.broadcast_to(m_last[b], (1, 128))
        c_s[b], n_s[b], m_s[b] = c_new, n_new, m_new
        c_out_ref[b, 0], n_out_ref[b, 0], m_out_ref[b, 0] = c_new, n_new, m_new


def _mlstm_chunked(proj, gates, mnorm, c0, n0, m0, *, chunk):
    _, batch, seq, _ = proj.shape
    nb0 = c0.shape[0]

    def col(base):
        return pl.BlockSpec((1, batch, chunk, 128), lambda h, c: (base + h, 0, c, 0))

    def state(nb, shape):
        return pl.BlockSpec((nb, 1) + shape, lambda h, c: (0, h, 0, 0))

    return dict(
        body=functools.partial(_mlstm_chunk_kernel, chunk=chunk),
        grid=(N_HEADS, seq // chunk),
        in_specs=[col(COL_MQ), col(COL_MK), col(COL_MV), col(COL_MO),
                  pl.BlockSpec((batch, chunk, 128), lambda h, c: (0, c, 0)),
                  pl.BlockSpec((1, 1, 128), lambda h, c: (h, 0, 0)),
                  state(nb0, (128, 128)), state(nb0, (1, 128)), state(nb0, (1, 128))],
        out_specs=[pl.BlockSpec((1, batch, chunk, 128), lambda h, c: (h, 0, c, 0)),
                   state(batch, (128, 128)), state(batch, (1, 128)), state(batch, (1, 128))],
        out_shape=[jax.ShapeDtypeStruct((N_HEADS, batch, seq, 128), BF16),
                   jax.ShapeDtypeStruct((batch, N_HEADS, 128, 128), F32),
                   jax.ShapeDtypeStruct((batch, N_HEADS, 1, 128), F32),
                   jax.ShapeDtypeStruct((batch, N_HEADS, 1, 128), F32)],
        scratch_shapes=[pltpu.VMEM((batch, 128, 128), F32), pltpu.VMEM((batch, 1, 128), F32),
                        pltpu.VMEM((batch, 1, 128), F32)],
        args=(proj, proj, proj, proj, gates, mnorm, c0, n0, m0))


def _run_parts(parts, *, semantics, name):
    grid = parts[0]["grid"]
    assert all(p["grid"] == grid for p in parts)
    n_in = [len(p["in_specs"]) for p in parts]
    n_out = [len(p["out_specs"]) for p in parts]
    n_scr = [len(p["scratch_shapes"]) for p in parts]

    def body(*refs):
        ins, outs, scr = refs[:sum(n_in)], refs[sum(n_in):sum(n_in) + sum(n_out)], refs[sum(n_in) + sum(n_out):]
        for k, p in enumerate(parts):
            p["body"](*ins[sum(n_in[:k]):sum(n_in[:k + 1])], *outs[sum(n_out[:k]):sum(n_out[:k + 1])],
                      *scr[sum(n_scr[:k]):sum(n_scr[:k + 1])])

    outs = pl.pallas_call(
        body,
        grid=grid,
        in_specs=[s for p in parts for s in p["in_specs"]],
        out_specs=[s for p in parts for s in p["out_specs"]],
        out_shape=[s for p in parts for s in p["out_shape"]],
        scratch_shapes=[s for p in parts for s in p["scratch_shapes"]],
        compiler_params=pltpu.CompilerParams(dimension_semantics=semantics, vmem_limit_bytes=VMEM_LIMIT),
        name=name,
    )(*[a for p in parts for a in p["args"]])
    return [outs[sum(n_out[:k]):sum(n_out[:k + 1])] for k in range(len(parts))]


def _split(x):
    hi = x.astype(BF16)
    return hi, (x - hi.astype(F32)).astype(BF16)


def _dot_split(a, b):
    a_hi, a_lo = a
    b_hi, b_lo = b if isinstance(b, tuple) else (b, None)
    if a_hi.shape[1] % 128 == 0:
        main = _dot(jnp.concatenate([a_hi, a_lo], axis=1), jnp.concatenate([b_hi, b_hi], axis=0))
    else:
        main = _dot(a_hi, b_hi) + _dot(a_lo, b_hi)
    return main if b_lo is None else main + _dot(a_hi, b_lo)


def _unit_lower_inverse(mats, rows, cols, size):
    def same_block(shift):
        return (rows >> shift) == (cols >> shift)

    ident = jnp.where(rows == cols, 1.0, 0.0)
    d = [jnp.where(same_block(3), a, 0.0) for a in mats]
    d_s = [_split(x) for x in d]
    d2_s = [_split(_dot_split(x, x)) for x in d_s]
    d4_s = [_split(_dot_split(x, x)) for x in d2_s]
    t = [ident - x for x in d]
    t = [x + _dot_split(_split(x), y) for x, y in zip(t, d2_s)]
    t = [x + _dot_split(_split(x), y) for x, y in zip(t, d4_s)]
    a_hi = [x.astype(BF16) for x in mats]
    shift = 3
    while (1 << shift) < size:
        off = jnp.logical_and(same_block(shift + 1), jnp.logical_not(same_block(shift)))
        off_bf = jnp.where(off, 1.0, 0.0).astype(BF16)
        t_s = [_split(x) for x in t]
        te_s = [_split(_dot_split(x, y * off_bf)) for x, y in zip(t_s, a_hi)]
        t = [x - _dot_split(y, z[0]) for x, y, z in zip(t, te_s, t_s)]
        shift += 1
    return t


def _gdn_prep_kernel(q_ref, k_ref, v_ref, pq_ref, pk_ref, pv_ref, iq_ref, ik_ref, iv_ref, wq_ref, wk_ref, wv_ref,
                     gates_ref,
                     u_ref, w_ref, qe_ref, kd_ref, attn_ref, glast_ref, ext, *, chunk, nsub):
    L = chunk
    R = nsub * L
    head = pl.program_id(1)
    first = pl.program_id(2) == 0

    def conv(x_ref, prev_ref, init_ref, cw_ref):
        prev = jnp.where(first, init_ref[0], prev_ref[0, 0])
        ext[0:8, :] = prev
        ext[8:8 + R, :] = x_ref[0, 0]
        base = 8 - (CONV_W - 1)
        acc = ext[base:base + R, :] * cw_ref[0:1, :]
        for i in range(1, CONV_W):
            acc = acc + ext[base + i:base + i + R, :] * cw_ref[i:i + 1, :]
        return acc * _sigmoid(acc)

    q = conv(q_ref, pq_ref, iq_ref, wq_ref)
    k = conv(k_ref, pk_ref, ik_ref, wk_ref)
    v_all = conv(v_ref, pv_ref, iv_ref, wv_ref)
    qn_all = q * _l2_scale(q) * Q_SCALE
    kn_all = k * _l2_scale(k)

    gates = gates_ref[0]
    lane = lax.broadcasted_iota(jnp.int32, (R, 128), 1)
    lg_all = _select_lane(gates, lane, head + LANE_GA)
    beta_all = _select_lane(gates, lane, head + LANE_GB)

    U = min(R, 128)
    per_unit = U // L
    rows = lax.broadcasted_iota(jnp.int32, (U, U), 0)
    cols = lax.broadcasted_iota(jnp.int32, (U, U), 1)
    shift = L.bit_length() - 1
    same_chunk = (rows >> shift) == (cols >> shift)
    causal = jnp.logical_and(rows >= cols, same_chunk)
    strict = jnp.logical_and(rows > cols, same_chunk)
    upper = jnp.logical_and(rows <= cols, same_chunk)
    eye = rows == cols
    row_id = lax.broadcasted_iota(jnp.int32, (U, 1), 0)

    units = range(R // U)
    sls = [slice(j * U, (j + 1) * U) for j in units]
    g_cols, decays, a_mats, qks = [], [], [], []
    for sl in sls:
        lg_col = lg_all[sl]
        lg_row = jnp.sum(jnp.where(eye, lg_col, 0.0), axis=0, keepdims=True)
        g_col = jnp.sum(jnp.where(causal, lg_row, 0.0), axis=1, keepdims=True)
        g_row = jnp.sum(jnp.where(upper, lg_col, 0.0), axis=0, keepdims=True)
        decay = jnp.where(causal, jnp.exp(jnp.where(causal, g_col - g_row, 0.0)), 0.0)
        kb = kn_all[sl].astype(BF16)
        kk = _dot_nt(kb, kb)
        qks.append(_dot_nt(qn_all[sl].astype(BF16), kb))
        a_mats.append(jnp.where(strict, beta_all[sl] * kk * decay, 0.0))
        g_cols.append(g_col)
        decays.append(decay)

    t_mats = _unit_lower_inverse(a_mats, rows, cols, L)

    e_gs = [jnp.exp(g) for g in g_cols]
    rhs_s = [_split(jnp.concatenate([beta_all[sl] * v_all[sl], (beta_all[sl] * e_g) * kn_all[sl]], axis=1))
             for sl, e_g in zip(sls, e_gs)]
    uws = [_dot_split(_split(t), r) for t, r in zip(t_mats, rhs_s)]

    for j in units:
        sl, g_col = sls[j], g_cols[j]
        u_ref[0, 0, sl, :] = uws[j][:, 0:128]
        w_ref[0, 0, sl, :] = uws[j][:, 128:256].astype(BF16)
        qe_ref[0, 0, sl, :] = (qn_all[sl] * e_gs[j]).astype(BF16)
        attn = qks[j] * decays[j]
        g_last_col = jnp.zeros((U, 1), F32)
        for i in range(per_unit):
            g_last = g_col[(i + 1) * L - 1:(i + 1) * L, :]
            g_last_col = jnp.where((row_id >> shift) == i, g_last, g_last_col)
            glast_ref[0, 0, j * per_unit + i] = jnp.broadcast_to(jnp.exp(g_last), (1, 128))
            diag = attn[i * L:(i + 1) * L, :]
            if i > 0:
                diag = pltpu.roll(diag, U - i * L, axis=1)
            attn_ref[0, 0, j * per_unit + i] = diag[:, 0:L].astype(BF16)
        kd_ref[0, 0, sl, :] = (kn_all[sl] * jnp.exp(g_last_col - g_col)).astype(BF16)


def _gdn_prep(proj, gates, conv_init, conv_w, *, chunk, nsub):
    _, batch, seq, _ = proj.shape
    nc = seq // chunk
    R = nsub * chunk

    def col(base):
        return pl.BlockSpec((1, 1, R, 128), lambda b, h, c: (base + h, b, c, 0))

    def prev(base):
        return pl.BlockSpec((1, 1, 8, 128), lambda b, h, c: (base + h, b, jnp.maximum(c * (R // 8) - 1, 0), 0))

    def init(base):
        return pl.BlockSpec((1, 8, 128), lambda b, h, c: (base + h, 0, 0))

    def cw(base):
        return pl.BlockSpec((CONV_W, 128), lambda b, h, c: (0, base + h))

    def per_token(dtype):
        return (pl.BlockSpec((1, 1, R, 128), lambda b, h, c: (b, h, c, 0)),
                jax.ShapeDtypeStruct((batch, N_HEADS, seq, 128), dtype))

    outs = [per_token(F32), per_token(BF16), per_token(BF16), per_token(BF16),
            (pl.BlockSpec((1, 1, nsub, chunk, chunk), lambda b, h, c: (b, h, c, 0, 0)),
             jax.ShapeDtypeStruct((batch, N_HEADS, nc, chunk, chunk), BF16)),
            (pl.BlockSpec((1, 1, nsub, 1, 128), lambda b, h, c: (b, h, c, 0, 0)),
             jax.ShapeDtypeStruct((batch, N_HEADS, nc, 1, 128), F32))]
    return dict(
        body=functools.partial(_gdn_prep_kernel, chunk=chunk, nsub=nsub),
        grid=(batch, N_HEADS, nc // nsub),
        in_specs=[col(COL_GQ), col(COL_GK), col(COL_GV), prev(COL_GQ), prev(COL_GK), prev(COL_GV),
                  init(0), init(8), init(16), cw(0), cw(8), cw(16),
                  pl.BlockSpec((1, R, 128), lambda b, h, c: (b, c, 0))],
        out_specs=[o[0] for o in outs],
        out_shape=[o[1] for o in outs],
        scratch_shapes=[pltpu.VMEM((8 + R, 128), F32)],
        args=(proj, proj, proj, proj, proj, proj, conv_init, conv_init, conv_init, conv_w, conv_w, conv_w, gates))


def _cast_kernel(*refs, pair_tiles):
    for src, dst, tile in zip(refs[:len(refs) // 2], refs[len(refs) // 2:], pair_tiles):
        if tile is None:
            dst[...] = src[...].astype(dst.dtype)
        else:
            half = src.shape[1] // 2
            for t in range(half // tile):
                dst[:, 2 * t * tile:(2 * t + 1) * tile] = src[:, t * tile:(t + 1) * tile].astype(dst.dtype)
                dst[:, (2 * t + 1) * tile:(2 * t + 2) * tile] = (
                    src[:, half + t * tile:half + (t + 1) * tile].astype(dst.dtype))


def _cast_rows(weights, grid, pair_tiles):
    steps = 1
    for g in grid:
        steps *= g

    def slab(w):
        rows = w.shape[0] // steps
        assert rows * steps == w.shape[0] and rows % 16 == 0

        def index(*idx):
            flat = idx[0]
            for i, g in zip(idx[1:], grid[1:]):
                flat = flat * g + i
            return (flat, 0)

        return pl.BlockSpec((rows, w.shape[1]), index)

    return dict(body=functools.partial(_cast_kernel, pair_tiles=tuple(pair_tiles)), grid=grid,
                in_specs=[slab(w) for w in weights],
                out_specs=[slab(w) for w in weights],
                out_shape=[jax.ShapeDtypeStruct(w.shape, BF16) for w in weights], scratch_shapes=[],
                args=tuple(weights))


def _gdn_scan_kernel(u_ref, w_ref, qe_ref, kd_ref, attn_ref, glast_ref, gz_ref, gnorm_ref, s0_ref,
                     o_ref, s_out_ref, s_s):
    nb, nh = u_ref.shape[0], u_ref.shape[1]
    shared = s0_ref.shape[0] == 1
    c = pl.program_id(1)
    last = c == pl.num_programs(1) - 1
    chains = [(b, h) for b in range(nb) for h in range(nh)]

    @pl.when(c == 0)
    def _():
        for b, h in chains:
            s_s[b, h] = s0_ref[0 if shared else b, h]

    L = u_ref.shape[2]
    s_prev = [s_s[b, h] for b, h in chains]
    ws = [_dot(jnp.concatenate([w_ref[b, h], qe_ref[b, h]], axis=0), sp.astype(BF16))
          for (b, h), sp in zip(chains, s_prev)]
    vb = [(u_ref[b, h] - r[0:L]).astype(BF16) for (b, h), r in zip(chains, ws)]
    intra = [_dot(attn_ref[b, h, 0], v) for (b, h), v in zip(chains, vb)]
    upd = [_dot_tn(kd_ref[b, h], v) for (b, h), v in zip(chains, vb)]
    for i, (b, h) in enumerate(chains):
        s_s[b, h] = glast_ref[b, h, 0][:, 0:1] * s_prev[i] + upd[i]
        o = ws[i][L:2 * L] + intra[i]
        gz = gz_ref[h, b]
        o_ref[h, b] = (o * _rms_scale(o) * gnorm_ref[h] * (gz * _sigmoid(gz))).astype(o_ref.dtype)

    @pl.when(last)
    def _():
        for b, h in chains:
            s_out_ref[b, h] = s_s[b, h]


def _gdn_scan(prep, proj, gnorm, s0, *, chunk):
    _, batch, seq, _ = proj.shape
    u, w, qe, kd, attn, glast = prep
    hb = GDN_SCAN_HEADS
    assert COL_GZ % hb == 0 and N_HEADS % hb == 0
    tok = pl.BlockSpec((batch, hb, chunk, 128), lambda h, c: (0, h, c, 0))
    return pl.pallas_call(
        _gdn_scan_kernel,
        grid=(N_HEADS // hb, seq // chunk),
        in_specs=[tok, tok, tok, tok,
                  pl.BlockSpec((batch, hb, 1, chunk, chunk), lambda h, c: (0, h, c, 0, 0)),
                  pl.BlockSpec((batch, hb, 1, 1, 128), lambda h, c: (0, h, c, 0, 0)),
                  pl.BlockSpec((hb, batch, chunk, 128), lambda h, c: (COL_GZ // hb + h, 0, c, 0)),
                  pl.BlockSpec((hb, 1, 128), lambda h, c: (h, 0, 0)),
                  pl.BlockSpec((s0.shape[0], hb, 128, 128), lambda h, c: (0, h, 0, 0))],
        out_specs=[pl.BlockSpec((hb, batch, chunk, 128), lambda h, c: (h, 0, c, 0)),
                   pl.BlockSpec((batch, hb, 128, 128), lambda h, c: (0, h, 0, 0))],
        out_shape=[jax.ShapeDtypeStruct((N_HEADS, batch, seq, 128), BF16),
                   jax.ShapeDtypeStruct((batch, N_HEADS, 128, 128), F32)],
        scratch_shapes=[pltpu.VMEM((batch, hb, 128, 128), F32)],
        compiler_params=pltpu.CompilerParams(dimension_semantics=("parallel", "arbitrary"),
                                             vmem_limit_bytes=VMEM_LIMIT),
        name="gdn_scan",
    )(u, w, qe, kd, attn, glast, proj, gnorm, s0)


def _sample_kernel(p_ref, gates_ref, c_ref, n_ref, m_ref, s_ref, cb_ref, cw_ref, mnorm_ref, gnorm_ref,
                   mix_ref, c_out_ref, n_out_ref, m_out_ref, s_out_ref, ext, xc_s, *, steps):
    T = steps
    NS = SAMPLE_SEQS
    R = NS * T
    H = N_HEADS
    nbuf = CONV_W - 1

    for s in range(NS):
        ext[8 * s:8 * s + nbuf, :] = cb_ref[0, s]
        for blk in range(COL_GZ - COL_GQ):
            ext[8 * s + nbuf:8 * s + nbuf + T, blk * 128:(blk + 1) * 128] = p_ref[COL_GQ + blk, 0, s * T:(s + 1) * T, :]
        acc = ext[8 * s:8 * s + T, :] * cw_ref[0:1, :]
        for i in range(1, CONV_W):
            acc = acc + ext[8 * s + i:8 * s + i + T, :] * cw_ref[i:i + 1, :]
        xc_s[s * T:(s + 1) * T, :] = acc * _sigmoid(acc)

    rowi = lax.broadcasted_iota(jnp.int32, (R, 1), 0)
    tpos = rowi & (T - 1)
    second = rowi >= T
    seq_rows = [jnp.logical_not(second), second]

    def pick(x, j):
        return jnp.where(second, x[T + j:T + j + 1, :], x[j:j + 1, :])

    def cumsum_t(x):
        acc = jnp.where(tpos >= 0, pick(x, 0), 0.0)
        for j in range(1, T):
            acc = acc + jnp.where(tpos >= j, pick(x, j), 0.0)
        return acc

    def pcol(block, h):
        return p_ref[block + h, 0]

    gates = gates_ref[0]
    lane = lax.broadcasted_iota(jnp.int32, (R, 128), 1)
    lf_all = lg_all = beta_all = gates
    m_in = m_ref[0]
    lane_s = lax.broadcasted_iota(jnp.int32, (NS, H), 1)
    row_s = lax.broadcasted_iota(jnp.int32, (NS, H), 0)
    m_acc = jnp.zeros((NS, H), F32)

    qc_all, gdn_in = [], []
    for h in range(H):
        qb = (pcol(COL_MQ, h) * Q_SCALE).astype(BF16)
        qc_all.append(jnp.where(second, _dot(qb, c_ref[0, 1, h].astype(BF16)),
                                _dot(qb, c_ref[0, 0, h].astype(BF16))))
        cq = xc_s[:, h * 128:(h + 1) * 128]
        ck = xc_s[:, WIDTH + h * 128:WIDTH + (h + 1) * 128]
        qn = cq * _l2_scale(cq) * Q_SCALE
        kn = ck * _l2_scale(ck)
        kq = jnp.concatenate([kn, qn], axis=0).astype(BF16)
        r0 = _dot(kq, s_ref[0, 0, h].astype(BF16))
        r1 = _dot(kq, s_ref[0, 1, h].astype(BF16))
        gdn_in.append((qn, kn, jnp.where(second, r1[0:R], r0[0:R]), jnp.where(second, r1[R:2 * R], r0[R:2 * R])))

    c_upd, s_upd = [], []
    for h in range(H):
        li = _select_lane(gates, lane, LANE_MI + h)
        lf = _select_lane(lf_all, lane, LANE_MF + h)
        mh = _select_lane(m_in, lane_s, h)
        m0 = jnp.where(second, mh[1:2, :], mh[0:1, :])
        b = cumsum_t(lf)
        a = b + m0
        dcol = [b - pick(b, j) + pick(li, j) for j in range(T)]
        m_t = a
        for j in range(T):
            m_t = jnp.maximum(m_t, jnp.where(tpos >= j, dcol[j], -jnp.inf))
        inter = jnp.exp(a - m_t)

        q = pcol(COL_MQ, h) * Q_SCALE
        k = pcol(COL_MK, h)
        v = pcol(COL_MV, h)
        num = jnp.zeros((R, 128), F32)
        den = jnp.zeros((R, 1), F32)
        for j in range(T):
            w_j = jnp.where(tpos >= j, jnp.exp(jnp.where(tpos >= j, dcol[j] - m_t, 0.0)), 0.0)
            s_j = jnp.sum(q * pick(k, j), axis=1, keepdims=True) * w_j
            num = num + s_j * pick(v, j)
            den = den + s_j
        qc = qc_all[h]
        n_sel =jnp.where(second, n_ref[0, 1, h:h + 1, :], n_ref[0, 0, h:h + 1, :])
        num = num + inter * qc
        den = den + inter * jnp.sum(q * n_sel, axis=1, keepdims=True)
        hh = num / jnp.maximum(jnp.abs(den), jnp.exp(-m_t))
        mix_ref[0, :, h * 128:(h + 1) * 128] = hh * _rms_scale(hh) * mnorm_ref[h:h + 1, :] * _sigmoid(pcol(COL_MO, h))

        m_l = pick(m_t, T - 1)
        b_l = pick(b, T - 1)
        kw = jnp.exp(b_l - b + li - m_l) * k
        d_c = jnp.exp(b_l + m0 - m_l)
        c_upd.append((kw, d_c, v.astype(BF16)))
        for s in range(NS):
            kws = jnp.where(seq_rows[s], kw, 0.0)
            n_out_ref[0, s, h:h + 1, :] = (d_c[s * T:s * T + 1, :] * n_ref[0, s, h:h + 1, :]
                                           + jnp.sum(kws, axis=0, keepdims=True))
            m_acc = jnp.where(jnp.logical_and(lane_s == h, row_s == s), m_l[s * T:s * T + 1, :], m_acc)

        lg = _select_lane(lg_all, lane, LANE_GA + h)
        beta = _select_lane(beta_all, lane, LANE_GB + h)
        g = cumsum_t(lg)
        e_g = jnp.exp(g)
        cv = xc_s[:, 2 * WIDTH + h * 128:2 * WIDTH + (h + 1) * 128]
        qn, kn, k_s, q_s = gdn_in[h]
        x = beta * (cv - e_g * k_s)
        rel = [jnp.exp(jnp.where(tpos >= j, g - pick(g, j), 0.0)) for j in range(T)]
        for j in range(T - 1):
            kk_j = jnp.sum(kn * pick(kn, j), axis=1, keepdims=True)
            a_j = jnp.where(tpos > j, beta * kk_j * rel[j], 0.0)
            x = x - a_j * pick(x, j)
        o = e_g * q_s
        for j in range(T):
            qk_j = jnp.sum(qn * pick(kn, j), axis=1, keepdims=True)
            o = o + jnp.where(tpos >= j, qk_j * rel[j], 0.0) * pick(x, j)
        gz = pcol(COL_GZ, h)
        mix_ref[0, :, WIDTH + h * 128:WIDTH + (h + 1) * 128] = (
            o * _rms_scale(o) * gnorm_ref[h:h + 1, :] * (gz * _sigmoid(gz)))

        g_l = pick(g, T - 1)
        s_upd.append((kn * jnp.exp(g_l - g), jnp.exp(g_l), x.astype(BF16)))
    m_out_ref[0] = m_acc

    for h in range(H):
        for (lhs, scale, rhs), st_ref, out_ref in ((c_upd[h], c_ref, c_out_ref), (s_upd[h], s_ref, s_out_ref)):
            for s in range(NS):
                rows_s = jnp.where(seq_rows[s], lhs, 0.0).astype(BF16)
                out_ref[0, s, h] = scale[s * T:s * T + 1, :] * st_ref[0, s, h] + _dot_tn(rows_s, rhs)


def _sample_step(proj, gates, c, n, m, s, conv_buf, conv_w, mnorm, gnorm, *, steps, grid):
    groups = c.shape[0]
    ns = SAMPLE_SEQS
    assert ns == 2 and ns * steps == 8 and steps >= CONV_W - 1 and steps & (steps - 1) == 0
    assert len(grid) == 2 and grid[0] * grid[1] == groups

    def per_group(*shape):
        return pl.BlockSpec((1,) + shape, lambda i, j: (i * grid[1] + j,) + (0,) * len(shape))

    def const(*shape):
        return pl.BlockSpec(shape, lambda i, j: (0,) * len(shape))

    state = (ns, N_HEADS, 128, 128)
    return dict(
        body=functools.partial(_sample_kernel, steps=steps),
        grid=grid,
        in_specs=[pl.BlockSpec((PROJ_COLS // 128, 1, ns * steps, 128), lambda i, j: (0, i * grid[1] + j, 0, 0)),
                  per_group(ns * steps, 128), per_group(*state), per_group(ns, N_HEADS, 128),
                  per_group(ns, N_HEADS), per_group(*state), per_group(ns, CONV_W - 1, 3 * WIDTH),
                  const(CONV_W, 3 * WIDTH), const(N_HEADS, 128), const(N_HEADS, 128)],
        out_specs=[per_group(ns * steps, 2 * WIDTH), per_group(*state), per_group(ns, N_HEADS, 128),
                   per_group(ns, N_HEADS), per_group(*state)],
        out_shape=[jax.ShapeDtypeStruct((groups, ns * steps, 2 * WIDTH), F32),
                   jax.ShapeDtypeStruct((groups,) + state, F32),
                   jax.ShapeDtypeStruct((groups, ns, N_HEADS, 128), F32),
                   jax.ShapeDtypeStruct((groups, ns, N_HEADS), F32),
                   jax.ShapeDtypeStruct((groups,) + state, F32)],
        scratch_shapes=[pltpu.VMEM((8 * ns, 3 * WIDTH), F32), pltpu.VMEM((ns * steps, 3 * WIDTH), F32)],
        args=(proj, gates, c, n, m, s, conv_buf, conv_w, mnorm, gnorm))


def kernel(x_prompt, x_sample, state_mlstm_C, state_mlstm_n, state_mlstm_m, state_gdn_S, state_gdn_conv, meta_tokens, norm_pre_mix, norm_post_mix, norm_pre_ffn, norm_post_ffn, w_in, mlstm_b_i, mlstm_b_f, mlstm_norm, gdn_conv_w, gdn_A_log, gdn_dt_bias, gdn_norm, w_out, w_gate_up, w_down):
    assert w_in.shape[0] == 1, "single-layer trunk"
    B, S, D = x_prompt.shape
    Bs, Ts, _ = x_sample.shape
    H = N_HEADS

    w_proj = _regroup_weights(jnp.transpose(w_in[0]))
    zeros_h = jnp.zeros((H,), F32)
    gbias = jnp.concatenate([mlstm_b_i[0], mlstm_b_f[0], gdn_dt_bias[0], zeros_h,
                             jnp.zeros((128 - 4 * H,), F32)]).reshape(1, 128)
    arow = jnp.concatenate([zeros_h, zeros_h, gdn_A_log[0], zeros_h,
                            jnp.zeros((128 - 4 * H,), F32)]).reshape(1, 128)
    g_pre_mix = norm_pre_mix[0].reshape(1, D)
    g_post_mix = norm_post_mix[0].reshape(1, D)
    g_pre_ffn = norm_pre_ffn[0].reshape(1, D)
    g_post_ffn = norm_post_ffn[0].reshape(1, D)
    mnorm = mlstm_norm[0]
    gnorm = gdn_norm[0]
    mnorm3 = mnorm.reshape(H, 1, 128)
    gnorm3 = gnorm.reshape(H, 1, 128)
    conv_w = gdn_conv_w[0]

    xp = x_prompt.reshape(B * S, D)
    xs = x_sample.reshape(Bs * Ts, D)

    nblk = PROJ_COLS // 128
    proj_p = _norm_matmul(xp, g_pre_mix, w_proj, tm=PROJ_TM, tn=PROJ_TN).reshape(nblk, B, S, 128)
    n_sm = Bs * Ts + N_META
    proj_sm = _norm_matmul(jnp.concatenate([xs, meta_tokens], axis=0), g_pre_mix, w_proj, tm=n_sm, tn=PROJ_TN)
    proj_m = proj_sm[:, Bs * Ts:].reshape(nblk, 1, N_META, 128)

    gates_p = _gate_act(proj_p.reshape(nblk, B * S, 128), gbias, arow, tm=PROJ_TM).reshape(B, S, 128)
    gates_sm = _gate_act(proj_sm, gbias, arow, tm=n_sm)
    gates_m = gates_sm[Bs * Ts:].reshape(1, N_META, 128)

    def conv_rows(blocks):
        return jnp.moveaxis(blocks, 0, -2).reshape(blocks.shape[1:-1] + (3 * WIDTH,))

    c0, n0, m0 = (jnp.zeros((1, H, 128, 128), F32), jnp.zeros((1, H, 1, 128), F32), jnp.zeros((1, H, 1, 128), F32))
    (_, c_l, n_l, m_l), = _run_parts([_mlstm_chunked(proj_m, gates_m, mnorm3, c0, n0, m0, chunk=N_META)],
                                     semantics=("parallel", "arbitrary"), name="mlstm_lead")
    prep_l, = _run_parts([_gdn_prep(proj_m, gates_m, jnp.zeros((3 * H, 8, 128), F32), conv_w, chunk=N_META, nsub=1)],
                         semantics=("parallel", "parallel", "parallel"), name="gdn_prep_lead")
    _, s_l = _gdn_scan(prep_l, proj_m, gnorm3, c0, chunk=N_META)

    ns = SAMPLE_SEQS
    grp = Bs // ns
    m_in = state_mlstm_m[0].reshape(grp, ns, H)
    mlstm_parts = _mlstm_chunked(proj_p, gates_p, mnorm3, c_l, n_l, m_l, chunk=MLSTM_CHUNK)
    sample_parts = _sample_step(
        proj_sm.reshape(nblk, n_sm // (ns * Ts), ns * Ts, 128), gates_sm.reshape(n_sm // (ns * Ts), ns * Ts, 128),
        state_mlstm_C[0].reshape(grp, ns, H, 128, 128),
        state_mlstm_n[0].reshape(grp, ns, H, 128), m_in, state_gdn_S[0].reshape(grp, ns, H, 128, 128),
        state_gdn_conv[0].reshape(grp, ns, CONV_W - 1, 3 * WIDTH), conv_w, mnorm, gnorm, steps=Ts,
        grid=mlstm_parts["grid"])
    (mix_a, p_c, p_n, p_m), (mix_s, s_c, s_n, s_m, s_s) = _run_parts(
        [mlstm_parts, sample_parts], semantics=("parallel", "arbitrary"), name="mlstm_and_sample")
    conv_init = proj_m[COL_GQ:COL_GZ, 0, N_META - 8:, :]
    prep_parts = _gdn_prep(proj_p, gates_p, conv_init, conv_w, chunk=GDN_CHUNK, nsub=GDN_PREP_CHUNKS)
    prep_p, (w_out_b, w_gu_b, w_dn_b) = _run_parts(
        [prep_parts, _cast_rows([w_out[0], w_gate_up[0], w_down[0]], prep_parts["grid"], [None, FFN_TF, None])],
        semantics=("parallel", "parallel", "parallel"), name="gdn_prep_and_casts")
    mix_b, p_s = _gdn_scan(prep_p, proj_p, gnorm3, s_l, chunk=GDN_CHUNK)
    p_conv = conv_rows(proj_p[COL_GQ:COL_GZ, :, S - (CONV_W - 1):, :])

    x1_p, u2_p = _out_proj(mix_a.reshape(H, B * S, 128), 0, mix_b.reshape(H, B * S, 128), 0, w_out_b, xp,
                           g_post_mix, g_pre_ffn, tm=ROW_TM)
    y_p = _ffn(u2_p, w_gu_b, w_dn_b, x1_p, g_post_ffn, tm=ROW_TM, tf=FFN_TF)

    mix_s2 = mix_s.reshape(Bs * Ts, 2 * WIDTH)
    s_conv = conv_rows(proj_sm[COL_GQ:COL_GZ, :Bs * Ts].reshape(3 * H, Bs, Ts, 128)[:, :, Ts - (CONV_W - 1):, :])
    x1_s, u2_s = _out_proj(mix_s2, 0, mix_s2, 1, w_out_b, xs, g_post_mix, g_pre_ffn, tm=ROW_TM)
    y_s = _ffn(u2_s, w_gu_b, w_dn_b, x1_s, g_post_ffn, tm=ROW_TM, tf=FFN_TF)

    return (y_p.reshape(B, S, D), y_s.reshape(Bs, Ts, D),
            p_c[None], p_n[:, :, 0, :][None], p_m[:, :, 0, 0][None], p_s[None], p_conv[None],
            s_c.reshape(Bs, H, 128, 128)[None], s_n.reshape(Bs, H, 128)[None], s_m.reshape(Bs, H)[None],
            s_s.reshape(Bs, H, 128, 128)[None], s_conv[None])
```

```python
import functools

import jax
import jax.numpy as jnp
from jax import lax
from jax.experimental import pallas as pl
from jax.experimental.pallas import tpu as pltpu

F32 = jnp.float32
BF16 = jnp.bfloat16

HEAD_DIM = 128
N_HEADS = 8
WIDTH = N_HEADS * HEAD_DIM
CONV_W = 4
N_META = 16
D_FF = 5632
EPS = 1e-6
Q_SCALE = HEAD_DIM ** -0.5

COL_MQ, COL_MK, COL_MV, COL_MO = 0, 8, 16, 24
COL_GQ, COL_GK, COL_GV, COL_GZ = 32, 40, 48, 56
COL_GATES = 64
PROJ_COLS = 66 * 128
LANE_MI, LANE_MF, LANE_GA, LANE_GB = 0, 8, 16, 24

MLSTM_CHUNK = 256
GDN_CHUNK = 64
GDN_PREP_CHUNKS = 32
SAMPLE_SEQS = 2

MXU_TILE = 256
VMEM_LIMIT = 56 * 1024 * 1024
PROJ_TM, PROJ_TN = 1024, 3 * MXU_TILE
ROW_TM = 512
FFN_TF = 2 * MXU_TILE


def _sigmoid(x):
    return 0.5 * jnp.tanh(0.5 * x) + 0.5


def _softplus(x):
    return jnp.maximum(x, 0.0) + jnp.log1p(jnp.exp(-jnp.abs(x)))


def _log_sigmoid(x):
    return -_softplus(-x)


def _rms_scale(x):
    return lax.rsqrt(jnp.mean(x * x, axis=-1, keepdims=True) + EPS)


def _l2_scale(x):
    return lax.rsqrt(jnp.sum(x * x, axis=-1, keepdims=True) + EPS)


def _dot(a, b):
    return jnp.dot(a, b, preferred_element_type=F32)


def _dot_nt(a, b):
    return lax.dot_general(a, b, (((1,), (1,)), ((), ())), preferred_element_type=F32)


def _dot_tn(a, b):
    return lax.dot_general(a, b, (((0,), (0,)), ((), ())), preferred_element_type=F32)


def _select_lane(x, lane_idx, target):
    return jnp.sum(jnp.where(lane_idx == target, x, 0.0), axis=1, keepdims=True)


REGROUP_TILE = 256


def _regroup_kernel(w_ref, mg_ref, gg_ref, o_ref):
    j = pl.program_id(0)
    wide_tiles = 2 * 4 * WIDTH // REGROUP_TILE

    @pl.when(j < wide_tiles)
    def _():
        o_ref[...] = w_ref[...].T.astype(BF16)

    @pl.when(j == wide_tiles)
    def _():
        pad = jnp.zeros((REGROUP_TILE - 4 * N_HEADS, w_ref.shape[1]), F32)
        o_ref[...] = jnp.concatenate([mg_ref[...], gg_ref[...], pad], axis=0).T.astype(BF16)


def _regroup_weights(w_t):
    n, d = w_t.shape
    wide = 4 * WIDTH
    narrow = 2 * N_HEADS
    assert n == 2 * (wide + narrow) and PROJ_COLS == 2 * wide + REGROUP_TILE
    wide_tiles = 2 * wide // REGROUP_TILE

    def src_row(j):
        j = jnp.minimum(j, wide_tiles - 1)
        return (j * (REGROUP_TILE // narrow) + jnp.where(j >= wide_tiles // 2, 1, 0)) * narrow

    return pl.pallas_call(
        _regroup_kernel,
        grid=(wide_tiles + 1,),
        in_specs=[pl.BlockSpec((pl.Element(REGROUP_TILE), pl.Element(d)), lambda j: (src_row(j), 0)),
                  pl.BlockSpec((narrow, d), lambda j: (wide // narrow, 0)),
                  pl.BlockSpec((narrow, d), lambda j: ((2 * wide + narrow) // narrow, 0))],
        out_specs=pl.BlockSpec((d, REGROUP_TILE), lambda j: (0, j)),
        out_shape=jax.ShapeDtypeStruct((d, PROJ_COLS), BF16),
        compiler_params=pltpu.CompilerParams(dimension_semantics=("parallel",), vmem_limit_bytes=VMEM_LIMIT),
        name="regroup_weights",
    )(w_t, w_t, w_t)


def _norm_matmul_kernel(x_ref, g_ref, w_ref, o_ref, u_ref):
    @pl.when(pl.program_id(1) == 0)
    def _():
        x = x_ref[...]
        u_ref[...] = (x * _rms_scale(x) * g_ref[...]).astype(BF16)

    res = _dot(u_ref[...], w_ref[...])
    for blk in range(o_ref.shape[0]):
        o_ref[blk] = res[:, blk * 128:(blk + 1) * 128]


def _norm_matmul(x, gain, w, *, tm, tn):
    m, d = x.shape
    n = w.shape[1]
    return pl.pallas_call(
        _norm_matmul_kernel,
        grid=(m // tm, n // tn),
        in_specs=[pl.BlockSpec((tm, d), lambda i, j: (i, 0)),
                  pl.BlockSpec((1, d), lambda i, j: (0, 0)),
                  pl.BlockSpec((d, tn), lambda i, j: (0, j))],
        out_specs=pl.BlockSpec((tn // 128, tm, 128), lambda i, j: (j, i, 0)),
        out_shape=jax.ShapeDtypeStruct((n // 128, m, 128), F32),
        scratch_shapes=[pltpu.VMEM((tm, d), BF16)],
        compiler_params=pltpu.CompilerParams(dimension_semantics=("parallel", "arbitrary"),
                                             vmem_limit_bytes=VMEM_LIMIT),
        name="norm_matmul",
    )(x, gain, w)


def _out_proj_kernel(ma_ref, mb_ref, wa_ref, wb_ref, x_ref, gpost_ref, gpre_ref, x1_ref, u2_ref):
    def rows(ref, sl):
        if len(ref.shape) == 2:
            return ref[sl, :].astype(BF16)
        return jnp.concatenate([ref[h, sl, :] for h in range(ref.shape[0])], axis=1).astype(BF16)

    tm = x_ref.shape[0]
    halves = [slice(0, tm // 2), slice(tm // 2, tm)]
    mos = [_dot(rows(ma_ref, sl), wa_ref[...]) + _dot(rows(mb_ref, sl), wb_ref[...]) for sl in halves]
    for sl, mo in zip(halves, mos):
        x1 = x_ref[sl, :] + mo * _rms_scale(mo) * gpost_ref[...]
        x1_ref[sl, :] = x1
        u2_ref[sl, :] = (x1 * _rms_scale(x1) * gpre_ref[...]).astype(BF16)


def _out_proj(mix_a, col_a, mix_b, col_b, w_out, x, g_post, g_pre, *, tm):
    m, d = x.shape

    def mix_spec(mix, col):
        if mix.ndim == 2:
            return pl.BlockSpec((tm, WIDTH), lambda i: (i, col))
        return pl.BlockSpec((N_HEADS, tm, 128), lambda i: (0, i, 0))

    return pl.pallas_call(
        _out_proj_kernel,
        grid=(m // tm,),
        in_specs=[mix_spec(mix_a, col_a),
                  mix_spec(mix_b, col_b),
                  pl.BlockSpec((WIDTH, d), lambda i: (0, 0)),
                  pl.BlockSpec((WIDTH, d), lambda i: (1, 0)),
                  pl.BlockSpec((tm, d), lambda i: (i, 0)),
                  pl.BlockSpec((1, d), lambda i: (0, 0)),
                  pl.BlockSpec((1, d), lambda i: (0, 0))],
        out_specs=[pl.BlockSpec((tm, d), lambda i: (i, 0)),
                   pl.BlockSpec((tm, d), lambda i: (i, 0))],
        out_shape=[jax.ShapeDtypeStruct((m, d), F32), jax.ShapeDtypeStruct((m, d), BF16)],
        compiler_params=pltpu.CompilerParams(dimension_semantics=("parallel",),
                                             vmem_limit_bytes=VMEM_LIMIT),
        name="out_proj",
    )(mix_a, mix_b, w_out, w_out, x, g_post, g_pre)


def _ffn_kernel(u_ref, wgu_ref, wd_ref, x1_ref, gain_ref, o_ref, acc_ref):
    f = pl.program_id(1)
    tf = wd_ref.shape[0]

    @pl.when(f == 0)
    def _():
        acc_ref[...] = jnp.zeros_like(acc_ref)

    gu = _dot(u_ref[...], wgu_ref[...])
    g, up = gu[:, :tf], gu[:, tf:]
    hidden = (g * _sigmoid(g) * up).astype(BF16)
    acc_ref[...] += _dot(hidden, wd_ref[...])

    @pl.when(f == pl.num_programs(1) - 1)
    def _():
        y = acc_ref[...]
        o_ref[...] = x1_ref[...] + y * _rms_scale(y) * gain_ref[...]


def _ffn(u2, w_gate_up, w_down, x1, gain, *, tm, tf):
    m, d = x1.shape
    nf = D_FF // tf
    return pl.pallas_call(
        _ffn_kernel,
        grid=(m // tm, nf),
        in_specs=[pl.BlockSpec((tm, d), lambda i, f: (i, 0)),
                  pl.BlockSpec((d, 2 * tf), lambda i, f: (0, f)),
                  pl.BlockSpec((tf, d), lambda i, f: (f, 0)),
                  pl.BlockSpec((tm, d), lambda i, f: (i, 0)),
                  pl.BlockSpec((1, d), lambda i, f: (0, 0))],
        out_specs=pl.BlockSpec((tm, d), lambda i, f: (i, 0)),
        out_shape=jax.ShapeDtypeStruct((m, d), F32),
        scratch_shapes=[pltpu.VMEM((tm, d), F32)],
        compiler_params=pltpu.CompilerParams(dimension_semantics=("parallel", "arbitrary"),
                                             vmem_limit_bytes=VMEM_LIMIT),
        name="ffn",
    )(u2, w_gate_up, w_down, x1, gain)


def _gate_act_kernel(g_ref, gbias_ref, arow_ref, o_ref):
    g = g_ref[0] + gbias_ref[...]
    lane = lax.broadcasted_iota(jnp.int32, g.shape, 1)
    decay = -jnp.exp(arow_ref[...]) * _softplus(g)
    o_ref[...] = jnp.where(lane < LANE_MF, g,
                           jnp.where(lane < LANE_GA, _log_sigmoid(g), jnp.where(lane < LANE_GB, decay, _sigmoid(g))))


def _gate_act(proj, gbias, arow, *, tm):
    m = proj.shape[1]
    return pl.pallas_call(
        _gate_act_kernel,
        grid=(m // tm,),
        in_specs=[pl.BlockSpec((1, tm, 128), lambda i: (COL_GATES, i, 0)),
                  pl.BlockSpec((1, 128), lambda i: (0, 0)),
                  pl.BlockSpec((1, 128), lambda i: (0, 0))],
        out_specs=pl.BlockSpec((tm, 128), lambda i: (i, 0)),
        out_shape=jax.ShapeDtypeStruct((m, 128), F32),
        compiler_params=pltpu.CompilerParams(dimension_semantics=("parallel",), vmem_limit_bytes=VMEM_LIMIT),
        name="gate_act",
    )(proj, gbias, arow)


def _mlstm_init(q_ref, k_ref, v_ref, og_ref, gates_ref, mnorm_ref, c0_ref, n0_ref, m0_ref,
                h_ref, c_out_ref, n_out_ref, m_out_ref, c_s, n_s, m_s):
    shared = c0_ref.shape[0] == 1

    @pl.when(pl.program_id(1) == 0)
    def _():
        for b in range(q_ref.shape[1]):
            b0 = 0 if shared else b
            c_s[b] = c0_ref[b0, 0]
            n_s[b] = n0_ref[b0, 0]
            m_s[b] = m0_ref[b0, 0]


def _mlstm_chunk_kernel(q_ref, k_ref, v_ref, og_ref, gates_ref, mnorm_ref, c0_ref, n0_ref, m0_ref,
                        h_ref, c_out_ref, n_out_ref, m_out_ref, c_s, n_s, m_s, *, chunk):
    L = chunk
    nb = q_ref.shape[1]
    head = pl.program_id(0)

    lane = lax.broadcasted_iota(jnp.int32, (L, 128), 1)
    rows = lax.broadcasted_iota(jnp.int32, (L, L), 0)
    cols = lax.broadcasted_iota(jnp.int32, (L, L), 1)
    tril = rows >= cols
    triu = rows <= cols
    eye = rows == cols

    seqs = range(nb)
    gates = [gates_ref[b] for b in seqs]
    li_col = [_select_lane(g, lane, head + LANE_MI) for g in gates]
    lf_col = [_select_lane(g, lane, head + LANE_MF) for g in gates]
    lf_row = [jnp.sum(jnp.where(eye, x, 0.0), axis=0, keepdims=True) for x in lf_col]
    li_row = [jnp.sum(jnp.where(eye, x, 0.0), axis=0, keepdims=True) for x in li_col]
    b_col = [jnp.sum(jnp.where(tril, x, 0.0), axis=1, keepdims=True) for x in lf_row]
    b_row = [jnp.sum(jnp.where(triu, x, 0.0), axis=0, keepdims=True) for x in lf_col]

    qb = [(q_ref[0, b] * Q_SCALE).astype(BF16) for b in seqs]
    kb = [k_ref[0, b].astype(BF16) for b in seqs]
    vb = [v_ref[0, b].astype(BF16) for b in seqs]
    qk = [_dot_nt(x, y) for x, y in zip(qb, kb)]
    qc = [_dot(x, c_s[b].astype(BF16)) for b, x in zip(seqs, qb)]

    m_prev = [m_s[b, :, 0:1] for b in seqs]
    a_col = [x + y for x, y in zip(b_col, m_prev)]
    dm = [jnp.where(tril, bc - br + lr, -jnp.inf) for bc, br, lr in zip(b_col, b_row, li_row)]
    m_t = [jnp.maximum(a, jnp.max(d, axis=1, keepdims=True)) for a, d in zip(a_col, dm)]
    inter = [jnp.exp(a - m) for a, m in zip(a_col, m_t)]
    s = [x * jnp.exp(d - m) for x, d, m in zip(qk, dm, m_t)]
    sv = [_dot(x.astype(BF16), v) for x, v in zip(s, vb)]

    m_last = [m[L - 1:L, :] for m in m_t]
    b_last = [x[L - 1:L, :] for x in b_col]
    d_c = [jnp.exp(bl + mp - ml) for bl, mp, ml in zip(b_last, m_prev, m_last)]
    kw = [jnp.exp(bl - bc + li - ml) * k_ref[0, b]
          for b, bl, bc, li, ml in zip(seqs, b_last, b_col, li_col, m_last)]
    c_upd = [_dot_tn(x.astype(BF16), v) for x, v in zip(kw, vb)]

    for b in seqs:
        q = q_ref[0, b] * Q_SCALE
        num = sv[b] + inter[b] * qc[b]
        den = jnp.sum(s[b], axis=1, keepdims=True) + inter[b] * jnp.sum(q * n_s[b], axis=1, keepdims=True)
        hh = num / jnp.maximum(jnp.abs(den), jnp.exp(-m_t[b]))
        out = hh * _rms_scale(hh) * mnorm_ref[0] * _sigmoid(og_ref[0, b])
        h_ref[0, b] = out.astype(h_ref.dtype)
        c_new = d_c[b] * c_s[b] + c_upd[b]
        n_new = d_c[b] * n_s[b] + jnp.sum(kw[b], axis=0, keepdims=True)
        m_new = jnp.broadcast_to(m_last[b], (1, 128))
        c_s[b], n_s[b], m_s[b] = c_new, n_new, m_new
        c_out_ref[b, 0], n_out_ref[b, 0], m_out_ref[b, 0] = c_new, n_new, m_new


def _mlstm_chunked(proj, gates, mnorm, c0, n0, m0, *, chunk):
    _, batch, seq, _ = proj.shape
    nb0 = c0.shape[0]

    def col(base):
        return pl.BlockSpec((1, batch, chunk, 128), lambda h, c: (base + h, 0, c, 0))

    def state(nb, shape):
        return pl.BlockSpec((nb, 1) + shape, lambda h, c: (0, h, 0, 0))

    return dict(
        init=_mlstm_init,
        body=functools.partial(_mlstm_chunk_kernel, chunk=chunk),
        grid=(N_HEADS, seq // chunk),
        in_specs=[col(COL_MQ), col(COL_MK), col(COL_MV), col(COL_MO),
                  pl.BlockSpec((batch, chunk, 128), lambda h, c: (0, c, 0)),
                  pl.BlockSpec((1, 1, 128), lambda h, c: (h, 0, 0)),
                  state(nb0, (128, 128)), state(nb0, (1, 128)), state(nb0, (1, 128))],
        out_specs=[pl.BlockSpec((1, batch, chunk, 128), lambda h, c: (h, 0, c, 0)),
                   state(batch, (128, 128)), state(batch, (1, 128)), state(batch, (1, 128))],
        out_shape=[jax.ShapeDtypeStruct((N_HEADS, batch, seq, 128), BF16),
                   jax.ShapeDtypeStruct((batch, N_HEADS, 128, 128), F32),
                   jax.ShapeDtypeStruct((batch, N_HEADS, 1, 128), F32),
                   jax.ShapeDtypeStruct((batch, N_HEADS, 1, 128), F32)],
        scratch_shapes=[pltpu.VMEM((batch, 128, 128), F32), pltpu.VMEM((batch, 1, 128), F32),
                        pltpu.VMEM((batch, 1, 128), F32)],
        args=(proj, proj, proj, proj, gates, mnorm, c0, n0, m0))


def _run_parts(parts, *, semantics, name):
    grid = parts[0]["grid"]
    assert all(p["grid"] == grid for p in parts)
    n_in = [len(p["in_specs"]) for p in parts]
    n_out = [len(p["out_specs"]) for p in parts]
    n_scr = [len(p["scratch_shapes"]) for p in parts]

    def body(*refs):
        ins, outs, scr = refs[:sum(n_in)], refs[sum(n_in):sum(n_in) + sum(n_out)], refs[sum(n_in) + sum(n_out):]
        part_refs = [ins[sum(n_in[:k]):sum(n_in[:k + 1])] + outs[sum(n_out[:k]):sum(n_out[:k + 1])]
                     + scr[sum(n_scr[:k]):sum(n_scr[:k + 1])] for k in range(len(parts))]
        for p, r in zip(parts, part_refs):
            if "init" in p:
                p["init"](*r)
        for p, r in zip(parts, part_refs):
            p["body"](*r)

    outs = pl.pallas_call(
        body,
        grid=grid,
        in_specs=[s for p in parts for s in p["in_specs"]],
        out_specs=[s for p in parts for s in p["out_specs"]],
        out_shape=[s for p in parts for s in p["out_shape"]],
        scratch_shapes=[s for p in parts for s in p["scratch_shapes"]],
        compiler_params=pltpu.CompilerParams(dimension_semantics=semantics, vmem_limit_bytes=VMEM_LIMIT),
        name=name,
    )(*[a for p in parts for a in p["args"]])
    return [outs[sum(n_out[:k]):sum(n_out[:k + 1])] for k in range(len(parts))]


def _split(x):
    hi = x.astype(BF16)
    return hi, (x - hi.astype(F32)).astype(BF16)


def _dot_split(a, b):
    a_hi, a_lo = a
    b_hi, b_lo = b if isinstance(b, tuple) else (b, None)
    if a_hi.shape[1] % 128 == 0:
        main = _dot(jnp.concatenate([a_hi, a_lo], axis=1), jnp.concatenate([b_hi, b_hi], axis=0))
    else:
        main = _dot(a_hi, b_hi) + _dot(a_lo, b_hi)
    return main if b_lo is None else main + _dot(a_hi, b_lo)


def _unit_lower_inverse(mats, rows, cols, size):
    def same_block(shift):
        return (rows >> shift) == (cols >> shift)

    ident = jnp.where(rows == cols, 1.0, 0.0)
    d = [jnp.where(same_block(3), a, 0.0) for a in mats]
    d_s = [_split(x) for x in d]
    d2_s = [_split(_dot_split(x, x)) for x in d_s]
    d4_s = [_split(_dot_split(x, x)) for x in d2_s]
    t = [ident - x for x in d]
    t = [x + _dot_split(_split(x), y) for x, y in zip(t, d2_s)]
    t = [x + _dot_split(_split(x), y) for x, y in zip(t, d4_s)]
    a_hi = [x.astype(BF16) for x in mats]
    shift = 3
    while (1 << shift) < size:
        off = jnp.logical_and(same_block(shift + 1), jnp.logical_not(same_block(shift)))
        off_bf = jnp.where(off, 1.0, 0.0).astype(BF16)
        t_s = [_split(x) for x in t]
        te_s = [_split(_dot_split(x, y * off_bf)) for x, y in zip(t_s, a_hi)]
        t = [x - _dot_split(y, z[0]) for x, y, z in zip(t, te_s, t_s)]
        shift += 1
    return t


def _gdn_prep_kernel(q_ref, k_ref, v_ref, pq_ref, pk_ref, pv_ref, iq_ref, ik_ref, iv_ref, wq_ref, wk_ref, wv_ref,
                     gates_ref,
                     u_ref, w_ref, qe_ref, kd_ref, attn_ref, glast_ref, ext, *, chunk, nsub):
    L = chunk
    R = nsub * L
    head = pl.program_id(1)
    first = pl.program_id(2) == 0

    def conv(x_ref, prev_ref, init_ref, cw_ref):
        prev = jnp.where(first, init_ref[0], prev_ref[0, 0])
        ext[0:8, :] = prev
        ext[8:8 + R, :] = x_ref[0, 0]
        base = 8 - (CONV_W - 1)
        acc = ext[base:base + R, :] * cw_ref[0:1, :]
        for i in range(1, CONV_W):
            acc = acc + ext[base + i:base + i + R, :] * cw_ref[i:i + 1, :]
        return acc * _sigmoid(acc)

    q = conv(q_ref, pq_ref, iq_ref, wq_ref)
    k = conv(k_ref, pk_ref, ik_ref, wk_ref)
    v_all = conv(v_ref, pv_ref, iv_ref, wv_ref)
    qn_all = q * _l2_scale(q) * Q_SCALE
    kn_all = k * _l2_scale(k)

    gates = gates_ref[0]
    lane = lax.broadcasted_iota(jnp.int32, (R, 128), 1)
    lg_all = _select_lane(gates, lane, head + LANE_GA)
    beta_all = _select_lane(gates, lane, head + LANE_GB)

    U = min(R, 128)
    per_unit = U // L
    rows = lax.broadcasted_iota(jnp.int32, (U, U), 0)
    cols = lax.broadcasted_iota(jnp.int32, (U, U), 1)
    shift = L.bit_length() - 1
    same_chunk = (rows >> shift) == (cols >> shift)
    causal = jnp.logical_and(rows >= cols, same_chunk)
    strict = jnp.logical_and(rows > cols, same_chunk)
    upper = jnp.logical_and(rows <= cols, same_chunk)
    eye = rows == cols
    row_id = lax.broadcasted_iota(jnp.int32, (U, 1), 0)

    units = range(R // U)
    sls = [slice(j * U, (j + 1) * U) for j in units]
    g_cols, decays, a_mats, qks = [], [], [], []
    for sl in sls:
        lg_col = lg_all[sl]
        lg_row = jnp.sum(jnp.where(eye, lg_col, 0.0), axis=0, keepdims=True)
        g_col = jnp.sum(jnp.where(causal, lg_row, 0.0), axis=1, keepdims=True)
        g_row = jnp.sum(jnp.where(upper, lg_col, 0.0), axis=0, keepdims=True)
        decay = jnp.where(causal, jnp.exp(jnp.where(causal, g_col - g_row, 0.0)), 0.0)
        kb = kn_all[sl].astype(BF16)
        kk = _dot_nt(kb, kb)
        qks.append(_dot_nt(qn_all[sl].astype(BF16), kb))
        a_mats.append(jnp.where(strict, beta_all[sl] * kk * decay, 0.0))
        g_cols.append(g_col)
        decays.append(decay)

    t_mats = _unit_lower_inverse(a_mats, rows, cols, L)

    e_gs = [jnp.exp(g) for g in g_cols]
    rhs_s = [_split(jnp.concatenate([beta_all[sl] * v_all[sl], (beta_all[sl] * e_g) * kn_all[sl]], axis=1))
             for sl, e_g in zip(sls, e_gs)]
    uws = [_dot_split(_split(t), r) for t, r in zip(t_mats, rhs_s)]

    for j in units:
        sl, g_col = sls[j], g_cols[j]
        u_ref[0, 0, sl, :] = uws[j][:, 0:128]
        w_ref[0, 0, sl, :] = uws[j][:, 128:256].astype(BF16)
        qe_ref[0, 0, sl, :] = (qn_all[sl] * e_gs[j]).astype(BF16)
        attn = qks[j] * decays[j]
        g_last_col = jnp.zeros((U, 1), F32)
        for i in range(per_unit):
            g_last = g_col[(i + 1) * L - 1:(i + 1) * L, :]
            g_last_col = jnp.where((row_id >> shift) == i, g_last, g_last_col)
            glast_ref[0, 0, j * per_unit + i] = jnp.broadcast_to(jnp.exp(g_last), (1, 128))
            diag = attn[i * L:(i + 1) * L, :]
            if i > 0:
                diag = pltpu.roll(diag, U - i * L, axis=1)
            attn_ref[0, 0, j * per_unit + i] = diag[:, 0:L].astype(BF16)
        kd_ref[0, 0, sl, :] = (kn_all[sl] * jnp.exp(g_last_col - g_col)).astype(BF16)


def _gdn_prep(proj, gates, conv_init, conv_w, *, chunk, nsub):
    _, batch, seq, _ = proj.shape
    nc = seq // chunk
    R = nsub * chunk

    def col(base):
        return pl.BlockSpec((1, 1, R, 128), lambda b, h, c: (base + h, b, c, 0))

    def prev(base):
        return pl.BlockSpec((1, 1, 8, 128), lambda b, h, c: (base + h, b, jnp.maximum(c * (R // 8) - 1, 0), 0))

    def init(base):
        return pl.BlockSpec((1, 8, 128), lambda b, h, c: (base + h, 0, 0))

    def cw(base):
        return pl.BlockSpec((CONV_W, 128), lambda b, h, c: (0, base + h))

    def per_token(dtype):
        return (pl.BlockSpec((1, 1, R, 128), lambda b, h, c: (b, h, c, 0)),
                jax.ShapeDtypeStruct((batch, N_HEADS, seq, 128), dtype))

    outs = [per_token(F32), per_token(BF16), per_token(BF16), per_token(BF16),
            (pl.BlockSpec((1, 1, nsub, chunk, chunk), lambda b, h, c: (b, h, c, 0, 0)),
             jax.ShapeDtypeStruct((batch, N_HEADS, nc, chunk, chunk), BF16)),
            (pl.BlockSpec((1, 1, nsub, 1, 128), lambda b, h, c: (b, h, c, 0, 0)),
             jax.ShapeDtypeStruct((batch, N_HEADS, nc, 1, 128), F32))]
    return dict(
        body=functools.partial(_gdn_prep_kernel, chunk=chunk, nsub=nsub),
        grid=(batch, N_HEADS, nc // nsub),
        in_specs=[col(COL_GQ), col(COL_GK), col(COL_GV), prev(COL_GQ), prev(COL_GK), prev(COL_GV),
                  init(0), init(8), init(16), cw(0), cw(8), cw(16),
                  pl.BlockSpec((1, R, 128), lambda b, h, c: (b, c, 0))],
        out_specs=[o[0] for o in outs],
        out_shape=[o[1] for o in outs],
        scratch_shapes=[pltpu.VMEM((8 + R, 128), F32)],
        args=(proj, proj, proj, proj, proj, proj, conv_init, conv_init, conv_init, conv_w, conv_w, conv_w, gates))


def _cast_kernel(*refs, pair_tiles):
    for src, dst, tile in zip(refs[:len(refs) // 2], refs[len(refs) // 2:], pair_tiles):
        if tile is None:
            dst[...] = src[...].astype(dst.dtype)
        else:
            half = src.shape[1] // 2
            for t in range(half // tile):
                dst[:, 2 * t * tile:(2 * t + 1) * tile] = src[:, t * tile:(t + 1) * tile].astype(dst.dtype)
                dst[:, (2 * t + 1) * tile:(2 * t + 2) * tile] = (
                    src[:, half + t * tile:half + (t + 1) * tile].astype(dst.dtype))


def _cast_rows(weights, grid, pair_tiles):
    steps = 1
    for g in grid:
        steps *= g

    def slab(w):
        rows = w.shape[0] // steps
        assert rows * steps == w.shape[0] and rows % 16 == 0

        def index(*idx):
            flat = idx[0]
            for i, g in zip(idx[1:], grid[1:]):
                flat = flat * g + i
            return (flat, 0)

        return pl.BlockSpec((rows, w.shape[1]), index)

    return dict(body=functools.partial(_cast_kernel, pair_tiles=tuple(pair_tiles)), grid=grid,
                in_specs=[slab(w) for w in weights],
                out_specs=[slab(w) for w in weights],
                out_shape=[jax.ShapeDtypeStruct(w.shape, BF16) for w in weights], scratch_shapes=[],
                args=tuple(weights))


def _gdn_scan_init(u_ref, w_ref, qe_ref, kd_ref, attn_ref, glast_ref, gz_ref, gnorm_ref, s0_ref,
                   o_ref, s_out_ref, s_s):
    shared = s0_ref.shape[0] == 1

    @pl.when(pl.program_id(1) == 0)
    def _():
        for b in range(u_ref.shape[0]):
            for h in range(u_ref.shape[1]):
                s_s[b, h] = s0_ref[0 if shared else b, h]


def _gdn_scan_kernel(u_ref, w_ref, qe_ref, kd_ref, attn_ref, glast_ref, gz_ref, gnorm_ref, s0_ref,
                     o_ref, s_out_ref, s_s):
    nb, nh = u_ref.shape[0], u_ref.shape[1]
    L = attn_ref.shape[3]
    chains = [(b, h) for b in range(nb) for h in range(nh)]
    s_cur = [s_s[b, h] for b, h in chains]
    for j in range(attn_ref.shape[2]):
        sl = slice(j * L, (j + 1) * L)
        ws = [_dot(jnp.concatenate([w_ref[b, h, sl, :], qe_ref[b, h, sl, :]], axis=0), sp.astype(BF16))
              for (b, h), sp in zip(chains, s_cur)]
        vb = [(u_ref[b, h, sl, :] - r[0:L]).astype(BF16) for (b, h), r in zip(chains, ws)]
        intra = [_dot(attn_ref[b, h, j], v) for (b, h), v in zip(chains, vb)]
        upd = [_dot_tn(kd_ref[b, h, sl, :], v) for (b, h), v in zip(chains, vb)]
        s_cur = [glast_ref[b, h, j][:, 0:1] * sp + up for (b, h), sp, up in zip(chains, s_cur, upd)]
        for i, (b, h) in enumerate(chains):
            o = ws[i][L:2 * L] + intra[i]
            gz = gz_ref[h, b, sl, :]
            o_ref[h, b, sl, :] = (o * _rms_scale(o) * gnorm_ref[h] * (gz * _sigmoid(gz))).astype(o_ref.dtype)
    for (b, h), sp in zip(chains, s_cur):
        s_s[b, h] = sp
        s_out_ref[b, h] = sp


def _gdn_scan(prep, proj, gnorm, s0, *, chunk, heads, nsub):
    _, batch, seq, _ = proj.shape
    u, w, qe, kd, attn, glast = prep
    hb = heads
    assert COL_GZ % hb == 0 and N_HEADS % hb == 0
    tok = pl.BlockSpec((batch, hb, nsub * chunk, 128), lambda h, c: (0, h, c, 0))
    return dict(
        init=_gdn_scan_init,
        body=_gdn_scan_kernel,
        grid=(N_HEADS // hb, seq // (nsub * chunk)),
        in_specs=[tok, tok, tok, tok,
                  pl.BlockSpec((batch, hb, nsub, chunk, chunk), lambda h, c: (0, h, c, 0, 0)),
                  pl.BlockSpec((batch, hb, nsub, 1, 128), lambda h, c: (0, h, c, 0, 0)),
                  pl.BlockSpec((hb, batch, nsub * chunk, 128), lambda h, c: (COL_GZ // hb + h, 0, c, 0)),
                  pl.BlockSpec((hb, 1, 128), lambda h, c: (h, 0, 0)),
                  pl.BlockSpec((s0.shape[0], hb, 128, 128), lambda h, c: (0, h, 0, 0))],
        out_specs=[pl.BlockSpec((hb, batch, nsub * chunk, 128), lambda h, c: (h, 0, c, 0)),
                   pl.BlockSpec((batch, hb, 128, 128), lambda h, c: (0, h, 0, 0))],
        out_shape=[jax.ShapeDtypeStruct((N_HEADS, batch, seq, 128), BF16),
                   jax.ShapeDtypeStruct((batch, N_HEADS, 128, 128), F32)],
        scratch_shapes=[pltpu.VMEM((batch, hb, 128, 128), F32)],
        args=(u, w, qe, kd, attn, glast, proj, gnorm, s0))


def _sample_kernel(p_ref, gates_ref, c_ref, n_ref, m_ref, s_ref, cb_ref, cw_ref, mnorm_ref, gnorm_ref,
                   mix_ref, c_out_ref, n_out_ref, m_out_ref, s_out_ref, ext, xc_s, *, steps):
    T = steps
    NS = SAMPLE_SEQS
    R = NS * T
    H = N_HEADS
    nbuf = CONV_W - 1

    for s in range(NS):
        ext[8 * s:8 * s + nbuf, :] = cb_ref[0, s]
        for blk in range(COL_GZ - COL_GQ):
            ext[8 * s + nbuf:8 * s + nbuf + T, blk * 128:(blk + 1) * 128] = p_ref[COL_GQ + blk, 0, s * T:(s + 1) * T, :]
        acc = ext[8 * s:8 * s + T, :] * cw_ref[0:1, :]
        for i in range(1, CONV_W):
            acc = acc + ext[8 * s + i:8 * s + i + T, :] * cw_ref[i:i + 1, :]
        xc_s[s * T:(s + 1) * T, :] = acc * _sigmoid(acc)

    rowi = lax.broadcasted_iota(jnp.int32, (R, 1), 0)
    tpos = rowi & (T - 1)
    second = rowi >= T
    seq_rows = [jnp.logical_not(second), second]

    def pick(x, j):
        return jnp.where(second, x[T + j:T + j + 1, :], x[j:j + 1, :])

    def cumsum_t(x):
        acc = jnp.where(tpos >= 0, pick(x, 0), 0.0)
        for j in range(1, T):
            acc = acc + jnp.where(tpos >= j, pick(x, j), 0.0)
        return acc

    def pcol(block, h):
        return p_ref[block + h, 0]

    gates = gates_ref[0]
    lane = lax.broadcasted_iota(jnp.int32, (R, 128), 1)
    lf_all = lg_all = beta_all = gates
    m_in = m_ref[0]
    lane_s = lax.broadcasted_iota(jnp.int32, (NS, H), 1)
    row_s = lax.broadcasted_iota(jnp.int32, (NS, H), 0)
    m_acc = jnp.zeros((NS, H), F32)

    qc_all, gdn_in = [], []
    for h in range(H):
        qb = (pcol(COL_MQ, h) * Q_SCALE).astype(BF16)
        qc_all.append(jnp.where(second, _dot(qb, c_ref[0, 1, h].astype(BF16)),
                                _dot(qb, c_ref[0, 0, h].astype(BF16))))
        cq = xc_s[:, h * 128:(h + 1) * 128]
        ck = xc_s[:, WIDTH + h * 128:WIDTH + (h + 1) * 128]
        qn = cq * _l2_scale(cq) * Q_SCALE
        kn = ck * _l2_scale(ck)
        kq = jnp.concatenate([kn, qn], axis=0).astype(BF16)
        r0 = _dot(kq, s_ref[0, 0, h].astype(BF16))
        r1 = _dot(kq, s_ref[0, 1, h].astype(BF16))
        gdn_in.append((qn, kn, jnp.where(second, r1[0:R], r0[0:R]), jnp.where(second, r1[R:2 * R], r0[R:2 * R])))

    c_upd, s_upd = [], []
    for h in range(H):
        li = _select_lane(gates, lane, LANE_MI + h)
        lf = _select_lane(lf_all, lane, LANE_MF + h)
        mh = _select_lane(m_in, lane_s, h)
        m0 = jnp.where(second, mh[1:2, :], mh[0:1, :])
        b = cumsum_t(lf)
        a = b + m0
        dcol = [b - pick(b, j) + pick(li, j) for j in range(T)]
        m_t = a
        for j in range(T):
            m_t = jnp.maximum(m_t, jnp.where(tpos >= j, dcol[j], -jnp.inf))
        inter = jnp.exp(a - m_t)

        q = pcol(COL_MQ, h) * Q_SCALE
        k = pcol(COL_MK, h)
        v = pcol(COL_MV, h)
        num = jnp.zeros((R, 128), F32)
        den = jnp.zeros((R, 1), F32)
        for j in range(T):
            w_j = jnp.where(tpos >= j, jnp.exp(jnp.where(tpos >= j, dcol[j] - m_t, 0.0)), 0.0)
            s_j = jnp.sum(q * pick(k, j), axis=1, keepdims=True) * w_j
            num = num + s_j * pick(v, j)
            den = den + s_j
        qc = qc_all[h]
        n_sel =jnp.where(second, n_ref[0, 1, h:h + 1, :], n_ref[0, 0, h:h + 1, :])
        num = num + inter * qc
        den = den + inter * jnp.sum(q * n_sel, axis=1, keepdims=True)
        hh = num / jnp.maximum(jnp.abs(den), jnp.exp(-m_t))
        mix_ref[0, :, h * 128:(h + 1) * 128] = hh * _rms_scale(hh) * mnorm_ref[h:h + 1, :] * _sigmoid(pcol(COL_MO, h))

        m_l = pick(m_t, T - 1)
        b_l = pick(b, T - 1)
        kw = jnp.exp(b_l - b + li - m_l) * k
        d_c = jnp.exp(b_l + m0 - m_l)
        c_upd.append((kw, d_c, v.astype(BF16)))
        for s in range(NS):
            kws = jnp.where(seq_rows[s], kw, 0.0)
            n_out_ref[0, s, h:h + 1, :] = (d_c[s * T:s * T + 1, :] * n_ref[0, s, h:h + 1, :]
                                           + jnp.sum(kws, axis=0, keepdims=True))
            m_acc = jnp.where(jnp.logical_and(lane_s == h, row_s == s), m_l[s * T:s * T + 1, :], m_acc)

        lg = _select_lane(lg_all, lane, LANE_GA + h)
        beta = _select_lane(beta_all, lane, LANE_GB + h)
        g = cumsum_t(lg)
        e_g = jnp.exp(g)
        cv = xc_s[:, 2 * WIDTH + h * 128:2 * WIDTH + (h + 1) * 128]
        qn, kn, k_s, q_s = gdn_in[h]
        x = beta * (cv - e_g * k_s)
        rel = [jnp.exp(jnp.where(tpos >= j, g - pick(g, j), 0.0)) for j in range(T)]
        for j in range(T - 1):
            kk_j = jnp.sum(kn * pick(kn, j), axis=1, keepdims=True)
            a_j = jnp.where(tpos > j, beta * kk_j * rel[j], 0.0)
            x = x - a_j * pick(x, j)
        o = e_g * q_s
        for j in range(T):
            qk_j = jnp.sum(qn * pick(kn, j), axis=1, keepdims=True)
            o = o + jnp.where(tpos >= j, qk_j * rel[j], 0.0) * pick(x, j)
        gz = pcol(COL_GZ, h)
        mix_ref[0, :, WIDTH + h * 128:WIDTH + (h + 1) * 128] = (
            o * _rms_scale(o) * gnorm_ref[h:h + 1, :] * (gz * _sigmoid(gz)))

        g_l = pick(g, T - 1)
        s_upd.append((kn * jnp.exp(g_l - g), jnp.exp(g_l), x.astype(BF16)))
    m_out_ref[0] = m_acc

    for h in range(H):
        for (lhs, scale, rhs), st_ref, out_ref in ((c_upd[h], c_ref, c_out_ref), (s_upd[h], s_ref, s_out_ref)):
            for s in range(NS):
                rows_s = jnp.where(seq_rows[s], lhs, 0.0).astype(BF16)
                out_ref[0, s, h] = scale[s * T:s * T + 1, :] * st_ref[0, s, h] + _dot_tn(rows_s, rhs)


def _sample_step(proj, gates, c, n, m, s, conv_buf, conv_w, mnorm, gnorm, *, steps, grid):
    groups = c.shape[0]
    ns = SAMPLE_SEQS
    assert ns == 2 and ns * steps == 8 and steps >= CONV_W - 1 and steps & (steps - 1) == 0
    assert len(grid) == 2 and grid[0] * grid[1] == groups

    def per_group(*shape):
        return pl.BlockSpec((1,) + shape, lambda i, j: (i * grid[1] + j,) + (0,) * len(shape))

    def const(*shape):
        return pl.BlockSpec(shape, lambda i, j: (0,) * len(shape))

    state = (ns, N_HEADS, 128, 128)
    return dict(
        body=functools.partial(_sample_kernel, steps=steps),
        grid=grid,
        in_specs=[pl.BlockSpec((PROJ_COLS // 128, 1, ns * steps, 128), lambda i, j: (0, i * grid[1] + j, 0, 0)),
                  per_group(ns * steps, 128), per_group(*state), per_group(ns, N_HEADS, 128),
                  per_group(ns, N_HEADS), per_group(*state), per_group(ns, CONV_W - 1, 3 * WIDTH),
                  const(CONV_W, 3 * WIDTH), const(N_HEADS, 128), const(N_HEADS, 128)],
        out_specs=[per_group(ns * steps, 2 * WIDTH), per_group(*state), per_group(ns, N_HEADS, 128),
                   per_group(ns, N_HEADS), per_group(*state)],
        out_shape=[jax.ShapeDtypeStruct((groups, ns * steps, 2 * WIDTH), F32),
                   jax.ShapeDtypeStruct((groups,) + state, F32),
                   jax.ShapeDtypeStruct((groups, ns, N_HEADS, 128), F32),
                   jax.ShapeDtypeStruct((groups, ns, N_HEADS), F32),
                   jax.ShapeDtypeStruct((groups,) + state, F32)],
        scratch_shapes=[pltpu.VMEM((8 * ns, 3 * WIDTH), F32), pltpu.VMEM((ns * steps, 3 * WIDTH), F32)],
        args=(proj, gates, c, n, m, s, conv_buf, conv_w, mnorm, gnorm))


def kernel(x_prompt, x_sample, state_mlstm_C, state_mlstm_n, state_mlstm_m, state_gdn_S, state_gdn_conv, meta_tokens, norm_pre_mix, norm_post_mix, norm_pre_ffn, norm_post_ffn, w_in, mlstm_b_i, mlstm_b_f, mlstm_norm, gdn_conv_w, gdn_A_log, gdn_dt_bias, gdn_norm, w_out, w_gate_up, w_down):
    assert w_in.shape[0] == 1, "single-layer trunk"
    B, S, D = x_prompt.shape
    Bs, Ts, _ = x_sample.shape
    H = N_HEADS

    w_proj = _regroup_weights(jnp.transpose(w_in[0]))
    zeros_h = jnp.zeros((H,), F32)
    gbias = jnp.concatenate([mlstm_b_i[0], mlstm_b_f[0], gdn_dt_bias[0], zeros_h,
                             jnp.zeros((128 - 4 * H,), F32)]).reshape(1, 128)
    arow = jnp.concatenate([zeros_h, zeros_h, gdn_A_log[0], zeros_h,
                            jnp.zeros((128 - 4 * H,), F32)]).reshape(1, 128)
    g_pre_mix = norm_pre_mix[0].reshape(1, D)
    g_post_mix = norm_post_mix[0].reshape(1, D)
    g_pre_ffn = norm_pre_ffn[0].reshape(1, D)
    g_post_ffn = norm_post_ffn[0].reshape(1, D)
    mnorm = mlstm_norm[0]
    gnorm = gdn_norm[0]
    mnorm3 = mnorm.reshape(H, 1, 128)
    gnorm3 = gnorm.reshape(H, 1, 128)
    conv_w = gdn_conv_w[0]

    xp = x_prompt.reshape(B * S, D)
    xs = x_sample.reshape(Bs * Ts, D)

    nblk = PROJ_COLS // 128
    proj_p = _norm_matmul(xp, g_pre_mix, w_proj, tm=PROJ_TM, tn=PROJ_TN).reshape(nblk, B, S, 128)
    n_sm = Bs * Ts + N_META
    proj_sm = _norm_matmul(jnp.concatenate([xs, meta_tokens], axis=0), g_pre_mix, w_proj, tm=n_sm, tn=PROJ_TN)
    proj_m = proj_sm[:, Bs * Ts:].reshape(nblk, 1, N_META, 128)

    gates_p = _gate_act(proj_p.reshape(nblk, B * S, 128), gbias, arow, tm=PROJ_TM).reshape(B, S, 128)
    gates_sm = _gate_act(proj_sm, gbias, arow, tm=n_sm)
    gates_m = gates_sm[Bs * Ts:].reshape(1, N_META, 128)

    def conv_rows(blocks):
        return jnp.moveaxis(blocks, 0, -2).reshape(blocks.shape[1:-1] + (3 * WIDTH,))

    c0, n0, m0 = (jnp.zeros((1, H, 128, 128), F32), jnp.zeros((1, H, 1, 128), F32), jnp.zeros((1, H, 1, 128), F32))
    (_, c_l, n_l, m_l), = _run_parts([_mlstm_chunked(proj_m, gates_m, mnorm3, c0, n0, m0, chunk=N_META)],
                                     semantics=("parallel", "arbitrary"), name="mlstm_lead")
    prep_l, = _run_parts([_gdn_prep(proj_m, gates_m, jnp.zeros((3 * H, 8, 128), F32), conv_w, chunk=N_META, nsub=1)],
                         semantics=("parallel", "parallel", "parallel"), name="gdn_prep_lead")
    (_, s_l), = _run_parts([_gdn_scan(prep_l, proj_m, gnorm3, c0, chunk=N_META, heads=N_HEADS, nsub=1)],
                           semantics=("parallel", "arbitrary"), name="gdn_scan_lead")

    conv_init = proj_m[COL_GQ:COL_GZ, 0, N_META - 8:, :]
    prep_parts = _gdn_prep(proj_p, gates_p, conv_init, conv_w, chunk=GDN_CHUNK, nsub=GDN_PREP_CHUNKS)
    prep_p, (w_out_b, w_gu_b, w_dn_b) = _run_parts(
        [prep_parts, _cast_rows([w_out[0], w_gate_up[0], w_down[0]], prep_parts["grid"], [None, FFN_TF, None])],
        semantics=("parallel", "parallel", "parallel"), name="gdn_prep_and_casts")

    ns = SAMPLE_SEQS
    grp = Bs // ns
    m_in = state_mlstm_m[0].reshape(grp, ns, H)
    mlstm_parts = _mlstm_chunked(proj_p, gates_p, mnorm3, c_l, n_l, m_l, chunk=MLSTM_CHUNK)
    scan_parts = _gdn_scan(prep_p, proj_p, gnorm3, s_l, chunk=GDN_CHUNK, heads=1, nsub=MLSTM_CHUNK // GDN_CHUNK)
    sample_parts = _sample_step(
        proj_sm.reshape(nblk, n_sm // (ns * Ts), ns * Ts, 128), gates_sm.reshape(n_sm // (ns * Ts), ns * Ts, 128),
        state_mlstm_C[0].reshape(grp, ns, H, 128, 128),
        state_mlstm_n[0].reshape(grp, ns, H, 128), m_in, state_gdn_S[0].reshape(grp, ns, H, 128, 128),
        state_gdn_conv[0].reshape(grp, ns, CONV_W - 1, 3 * WIDTH), conv_w, mnorm, gnorm, steps=Ts,
        grid=mlstm_parts["grid"])
    (mix_b, p_s), (mix_a, p_c, p_n, p_m), (mix_s, s_c, s_n, s_m, s_s) = _run_parts(
        [scan_parts, mlstm_parts, sample_parts], semantics=("parallel", "arbitrary"), name="scan_mlstm_sample")
    p_conv = conv_rows(proj_p[COL_GQ:COL_GZ, :, S - (CONV_W - 1):, :])

    x1_p, u2_p = _out_proj(mix_a.reshape(H, B * S, 128), 0, mix_b.reshape(H, B * S, 128), 0, w_out_b, xp,
                           g_post_mix, g_pre_ffn, tm=ROW_TM)
    y_p = _ffn(u2_p, w_gu_b, w_dn_b, x1_p, g_post_ffn, tm=ROW_TM, tf=FFN_TF)

    mix_s2 = mix_s.reshape(Bs * Ts, 2 * WIDTH)
    s_conv = conv_rows(proj_sm[COL_GQ:COL_GZ, :Bs * Ts].reshape(3 * H, Bs, Ts, 128)[:, :, Ts - (CONV_W - 1):, :])
    x1_s, u2_s = _out_proj(mix_s2, 0, mix_s2, 1, w_out_b, xs, g_post_mix, g_pre_ffn, tm=ROW_TM)
    y_s = _ffn(u2_s, w_gu_b, w_dn_b, x1_s, g_post_ffn, tm=ROW_TM, tf=FFN_TF)

    return (y_p.reshape(B, S, D), y_s.reshape(Bs, Ts, D),
            p_c[None], p_n[:, :, 0, :][None], p_m[:, :, 0, 0][None], p_s[None], p_conv[None],
            s_c.reshape(Bs, H, 128, 128)[None], s_n.reshape(Bs, H, 128)[None], s_m.reshape(Bs, H)[None],
            s_s.reshape(Bs, H, 128, 128)[None], s_conv[None])
```

```python
import functools

import jax
import jax.numpy as jnp
from jax import lax
from jax.experimental import pallas as pl
from jax.experimental.pallas import tpu as pltpu

F32 = jnp.float32
BF16 = jnp.bfloat16

HEAD_DIM = 128
N_HEADS = 8
WIDTH = N_HEADS * HEAD_DIM
CONV_W = 4
N_META = 16
D_FF = 5632
EPS = 1e-6
Q_SCALE = HEAD_DIM ** -0.5

COL_MQ, COL_MK, COL_MV, COL_MO = 0, 8, 16, 24
COL_GQ, COL_GK, COL_GV, COL_GZ = 32, 40, 48, 56
COL_GATES = 64
PROJ_COLS = 66 * 128
LANE_MI, LANE_MF, LANE_GA, LANE_GB = 0, 8, 16, 24

MLSTM_CHUNK = 256
GDN_CHUNK = 64
GDN_PREP_CHUNKS = 32
SAMPLE_SEQS = 2

MXU_TILE = 256
VMEM_LIMIT = 56 * 1024 * 1024
PROJ_TM, PROJ_TN = 1024, 3 * MXU_TILE
ROW_TM = 512
FFN_TF = 2 * MXU_TILE


def _sigmoid(x):
    return 0.5 * jnp.tanh(0.5 * x) + 0.5


def _softplus(x):
    return jnp.maximum(x, 0.0) + jnp.log1p(jnp.exp(-jnp.abs(x)))


def _log_sigmoid(x):
    return -_softplus(-x)


def _rms_scale(x):
    return lax.rsqrt(jnp.mean(x * x, axis=-1, keepdims=True) + EPS)


def _l2_scale(x):
    return lax.rsqrt(jnp.sum(x * x, axis=-1, keepdims=True) + EPS)


def _dot(a, b):
    return jnp.dot(a, b, preferred_element_type=F32)


def _dot_nt(a, b):
    return lax.dot_general(a, b, (((1,), (1,)), ((), ())), preferred_element_type=F32)


def _dot_tn(a, b):
    return lax.dot_general(a, b, (((0,), (0,)), ((), ())), preferred_element_type=F32)


def _select_lane(x, lane_idx, target):
    return jnp.sum(jnp.where(lane_idx == target, x, 0.0), axis=1, keepdims=True)


REGROUP_TILE = 256


def _regroup_kernel(w_ref, mg_ref, gg_ref, o_ref):
    j = pl.program_id(0)
    wide_tiles = 2 * 4 * WIDTH // REGROUP_TILE

    @pl.when(j < wide_tiles)
    def _():
        o_ref[...] = w_ref[...].T.astype(BF16)

    @pl.when(j == wide_tiles)
    def _():
        pad = jnp.zeros((REGROUP_TILE - 4 * N_HEADS, w_ref.shape[1]), F32)
        o_ref[...] = jnp.concatenate([mg_ref[...], gg_ref[...], pad], axis=0).T.astype(BF16)


def _regroup_weights(w_t):
    n, d = w_t.shape
    wide = 4 * WIDTH
    narrow = 2 * N_HEADS
    assert n == 2 * (wide + narrow) and PROJ_COLS == 2 * wide + REGROUP_TILE
    wide_tiles = 2 * wide // REGROUP_TILE

    def src_row(j):
        j = jnp.minimum(j, wide_tiles - 1)
        return (j * (REGROUP_TILE // narrow) + jnp.where(j >= wide_tiles // 2, 1, 0)) * narrow

    return pl.pallas_call(
        _regroup_kernel,
        grid=(wide_tiles + 1,),
        in_specs=[pl.BlockSpec((pl.Element(REGROUP_TILE), pl.Element(d)), lambda j: (src_row(j), 0)),
                  pl.BlockSpec((narrow, d), lambda j: (wide // narrow, 0)),
                  pl.BlockSpec((narrow, d), lambda j: ((2 * wide + narrow) // narrow, 0))],
        out_specs=pl.BlockSpec((d, REGROUP_TILE), lambda j: (0, j)),
        out_shape=jax.ShapeDtypeStruct((d, PROJ_COLS), BF16),
        compiler_params=pltpu.CompilerParams(dimension_semantics=("parallel",), vmem_limit_bytes=VMEM_LIMIT),
        name="regroup_weights",
    )(w_t, w_t, w_t)


def _norm_matmul_kernel(x_ref, g_ref, w_ref, o_ref, u_ref):
    @pl.when(pl.program_id(1) == 0)
    def _():
        x = x_ref[...]
        u_ref[...] = (x * _rms_scale(x) * g_ref[...]).astype(BF16)

    res = _dot(u_ref[...], w_ref[...])
    for blk in range(o_ref.shape[0]):
        o_ref[blk] = res[:, blk * 128:(blk + 1) * 128]


def _norm_matmul(x, gain, w, *, tm, tn):
    m, d = x.shape
    n = w.shape[1]
    return pl.pallas_call(
        _norm_matmul_kernel,
        grid=(m // tm, n // tn),
        in_specs=[pl.BlockSpec((tm, d), lambda i, j: (i, 0)),
                  pl.BlockSpec((1, d), lambda i, j: (0, 0)),
                  pl.BlockSpec((d, tn), lambda i, j: (0, j))],
        out_specs=pl.BlockSpec((tn // 128, tm, 128), lambda i, j: (j, i, 0)),
        out_shape=jax.ShapeDtypeStruct((n // 128, m, 128), F32),
        scratch_shapes=[pltpu.VMEM((tm, d), BF16)],
        compiler_params=pltpu.CompilerParams(dimension_semantics=("parallel", "arbitrary"),
                                             vmem_limit_bytes=VMEM_LIMIT),
        name="norm_matmul",
    )(x, gain, w)


def _out_proj_kernel(ma_ref, mb_ref, wa_ref, wb_ref, x_ref, gpost_ref, gpre_ref, x1_ref, u2_ref):
    def rows(ref, sl):
        if len(ref.shape) == 2:
            return ref[sl, :].astype(BF16)
        return jnp.concatenate([ref[h, sl, :] for h in range(ref.shape[0])], axis=1).astype(BF16)

    tm = x_ref.shape[0]
    halves = [slice(0, tm // 2), slice(tm // 2, tm)]
    mos = [_dot(rows(ma_ref, sl), wa_ref[...]) + _dot(rows(mb_ref, sl), wb_ref[...]) for sl in halves]
    for sl, mo in zip(halves, mos):
        x1 = x_ref[sl, :] + mo * _rms_scale(mo) * gpost_ref[...]
        x1_ref[sl, :] = x1
        u2_ref[sl, :] = (x1 * _rms_scale(x1) * gpre_ref[...]).astype(BF16)


def _out_proj(mix_a, col_a, mix_b, col_b, w_out, x, g_post, g_pre, *, tm):
    m, d = x.shape

    def mix_spec(mix, col):
        if mix.ndim == 2:
            return pl.BlockSpec((tm, WIDTH), lambda i: (i, col))
        return pl.BlockSpec((N_HEADS, tm, 128), lambda i: (0, i, 0))

    return pl.pallas_call(
        _out_proj_kernel,
        grid=(m // tm,),
        in_specs=[mix_spec(mix_a, col_a),
                  mix_spec(mix_b, col_b),
                  pl.BlockSpec((WIDTH, d), lambda i: (0, 0)),
                  pl.BlockSpec((WIDTH, d), lambda i: (1, 0)),
                  pl.BlockSpec((tm, d), lambda i: (i, 0)),
                  pl.BlockSpec((1, d), lambda i: (0, 0)),
                  pl.BlockSpec((1, d), lambda i: (0, 0))],
        out_specs=[pl.BlockSpec((tm, d), lambda i: (i, 0)),
                   pl.BlockSpec((tm, d), lambda i: (i, 0))],
        out_shape=[jax.ShapeDtypeStruct((m, d), F32), jax.ShapeDtypeStruct((m, d), BF16)],
        compiler_params=pltpu.CompilerParams(dimension_semantics=("parallel",),
                                             vmem_limit_bytes=VMEM_LIMIT),
        name="out_proj",
    )(mix_a, mix_b, w_out, w_out, x, g_post, g_pre)


def _ffn_kernel(u_ref, wgu_ref, wd_ref, x1_ref, gain_ref, o_ref, acc_ref):
    f = pl.program_id(1)
    tf = wd_ref.shape[0]

    @pl.when(f == 0)
    def _():
        acc_ref[...] = jnp.zeros_like(acc_ref)

    gu = _dot(u_ref[...], wgu_ref[...])
    g, up = gu[:, :tf], gu[:, tf:]
    hidden = (g * _sigmoid(g) * up).astype(BF16)
    acc_ref[...] += _dot(hidden, wd_ref[...])

    @pl.when(f == pl.num_programs(1) - 1)
    def _():
        y = acc_ref[...]
        o_ref[...] = x1_ref[...] + y * _rms_scale(y) * gain_ref[...]


def _ffn(u2, w_gate_up, w_down, x1, gain, *, tm, tf):
    m, d = x1.shape
    nf = D_FF // tf
    return pl.pallas_call(
        _ffn_kernel,
        grid=(m // tm, nf),
        in_specs=[pl.BlockSpec((tm, d), lambda i, f: (i, 0)),
                  pl.BlockSpec((d, 2 * tf), lambda i, f: (0, f)),
                  pl.BlockSpec((tf, d), lambda i, f: (f, 0)),
                  pl.BlockSpec((tm, d), lambda i, f: (i, 0)),
                  pl.BlockSpec((1, d), lambda i, f: (0, 0))],
        out_specs=pl.BlockSpec((tm, d), lambda i, f: (i, 0)),
        out_shape=jax.ShapeDtypeStruct((m, d), F32),
        scratch_shapes=[pltpu.VMEM((tm, d), F32)],
        compiler_params=pltpu.CompilerParams(dimension_semantics=("parallel", "arbitrary"),
                                             vmem_limit_bytes=VMEM_LIMIT),
        name="ffn",
    )(u2, w_gate_up, w_down, x1, gain)


def _gate_act_kernel(g_ref, gbias_ref, arow_ref, o_ref):
    g = g_ref[0] + gbias_ref[...]
    lane = lax.broadcasted_iota(jnp.int32, g.shape, 1)
    decay = -jnp.exp(arow_ref[...]) * _softplus(g)
    o_ref[...] = jnp.where(lane < LANE_MF, g,
                           jnp.where(lane < LANE_GA, _log_sigmoid(g), jnp.where(lane < LANE_GB, decay, _sigmoid(g))))


def _gate_act(proj, gbias, arow, *, tm):
    m = proj.shape[1]
    return pl.pallas_call(
        _gate_act_kernel,
        grid=(m // tm,),
        in_specs=[pl.BlockSpec((1, tm, 128), lambda i: (COL_GATES, i, 0)),
                  pl.BlockSpec((1, 128), lambda i: (0, 0)),
                  pl.BlockSpec((1, 128), lambda i: (0, 0))],
        out_specs=pl.BlockSpec((tm, 128), lambda i: (i, 0)),
        out_shape=jax.ShapeDtypeStruct((m, 128), F32),
        compiler_params=pltpu.CompilerParams(dimension_semantics=("parallel",), vmem_limit_bytes=VMEM_LIMIT),
        name="gate_act",
    )(proj, gbias, arow)


def _mlstm_init(q_ref, k_ref, v_ref, og_ref, gates_ref, mnorm_ref, c0_ref, n0_ref, m0_ref,
                h_ref, c_out_ref, n_out_ref, m_out_ref, c_s, n_s, m_s):
    shared = c0_ref.shape[0] == 1

    @pl.when(pl.program_id(1) == 0)
    def _():
        for b in range(q_ref.shape[1]):
            b0 = 0 if shared else b
            c_s[b] = c0_ref[b0, 0]
            n_s[b] = n0_ref[b0, 0]
            m_s[b] = m0_ref[b0, 0]


def _mlstm_chunk_kernel(q_ref, k_ref, v_ref, og_ref, gates_ref, mnorm_ref, c0_ref, n0_ref, m0_ref,
                        h_ref, c_out_ref, n_out_ref, m_out_ref, c_s, n_s, m_s, *, chunk):
    L = chunk
    nb = q_ref.shape[1]
    head = pl.program_id(0)

    lane = lax.broadcasted_iota(jnp.int32, (L, 128), 1)
    rows = lax.broadcasted_iota(jnp.int32, (L, L), 0)
    cols = lax.broadcasted_iota(jnp.int32, (L, L), 1)
    tril = rows >= cols
    triu = rows <= cols
    eye = rows == cols

    seqs = range(nb)
    gates = [gates_ref[b] for b in seqs]
    li_col = [_select_lane(g, lane, head + LANE_MI) for g in gates]
    lf_col = [_select_lane(g, lane, head + LANE_MF) for g in gates]
    lf_row = [jnp.sum(jnp.where(eye, x, 0.0), axis=0, keepdims=True) for x in lf_col]
    li_row = [jnp.sum(jnp.where(eye, x, 0.0), axis=0, keepdims=True) for x in li_col]
    b_col = [jnp.sum(jnp.where(tril, x, 0.0), axis=1, keepdims=True) for x in lf_row]
    b_row = [jnp.sum(jnp.where(triu, x, 0.0), axis=0, keepdims=True) for x in lf_col]

    qb = [(q_ref[0, b] * Q_SCALE).astype(BF16) for b in seqs]
    kb = [k_ref[0, b].astype(BF16) for b in seqs]
    vb = [v_ref[0, b].astype(BF16) for b in seqs]
    qk = [_dot_nt(x, y) for x, y in zip(qb, kb)]
    qc = [_dot(x, c_s[b].astype(BF16)) for b, x in zip(seqs, qb)]

    m_prev = [m_s[b, :, 0:1] for b in seqs]
    a_col = [x + y for x, y in zip(b_col, m_prev)]
    dm = [jnp.where(tril, bc - br + lr, -jnp.inf) for bc, br, lr in zip(b_col, b_row, li_row)]
    m_t = [jnp.maximum(a, jnp.max(d, axis=1, keepdims=True)) for a, d in zip(a_col, dm)]
    inter = [jnp.exp(a - m) for a, m in zip(a_col, m_t)]
    s = [x * jnp.exp(d - m) for x, d, m in zip(qk, dm, m_t)]
    sv = [_dot(x.astype(BF16), v) for x, v in zip(s, vb)]

    m_last = [m[L - 1:L, :] for m in m_t]
    b_last = [x[L - 1:L, :] for x in b_col]
    d_c = [jnp.exp(bl + mp - ml) for bl, mp, ml in zip(b_last, m_prev, m_last)]
    kw = [jnp.exp(bl - bc + li - ml) * k_ref[0, b]
          for b, bl, bc, li, ml in zip(seqs, b_last, b_col, li_col, m_last)]
    c_upd = [_dot_tn(x.astype(BF16), v) for x, v in zip(kw, vb)]

    for b in seqs:
        q = q_ref[0, b] * Q_SCALE
        num = sv[b] + inter[b] * qc[b]
        den = jnp.sum(s[b], axis=1, keepdims=True) + inter[b] * jnp.sum(q * n_s[b], axis=1, keepdims=True)
        hh = num / jnp.maximum(jnp.abs(den), jnp.exp(-m_t[b]))
        out = hh * _rms_scale(hh) * mnorm_ref[0] * _sigmoid(og_ref[0, b])
        h_ref[0, b] = out.astype(h_ref.dtype)
        c_new = d_c[b] * c_s[b] + c_upd[b]
        n_new = d_c[b] * n_s[b] + jnp.sum(kw[b], axis=0, keepdims=True)
        m_new = jnp.broadcast_to(m_last[b], (1, 128))
        c_s[b], n_s[b], m_s[b] = c_new, n_new, m_new
        c_out_ref[b, 0], n_out_ref[b, 0], m_out_ref[b, 0] = c_new, n_new, m_new


def _mlstm_chunked(proj, gates, mnorm, c0, n0, m0, *, chunk):
    _, batch, seq, _ = proj.shape
    nb0 = c0.shape[0]

    def col(base):
        return pl.BlockSpec((1, batch, chunk, 128), lambda h, c: (base + h, 0, c, 0))

    def state(nb, shape):
        return pl.BlockSpec((nb, 1) + shape, lambda h, c: (0, h, 0, 0))

    return dict(
        init=_mlstm_init,
        body=functools.partial(_mlstm_chunk_kernel, chunk=chunk),
        grid=(N_HEADS, seq // chunk),
        in_specs=[col(COL_MQ), col(COL_MK), col(COL_MV), col(COL_MO),
                  pl.BlockSpec((batch, chunk, 128), lambda h, c: (0, c, 0)),
                  pl.BlockSpec((1, 1, 128), lambda h, c: (h, 0, 0)),
                  state(nb0, (128, 128)), state(nb0, (1, 128)), state(nb0, (1, 128))],
        out_specs=[pl.BlockSpec((1, batch, chunk, 128), lambda h, c: (h, 0, c, 0)),
                   state(batch, (128, 128)), state(batch, (1, 128)), state(batch, (1, 128))],
        out_shape=[jax.ShapeDtypeStruct((N_HEADS, batch, seq, 128), BF16),
                   jax.ShapeDtypeStruct((batch, N_HEADS, 128, 128), F32),
                   jax.ShapeDtypeStruct((batch, N_HEADS, 1, 128), F32),
                   jax.ShapeDtypeStruct((batch, N_HEADS, 1, 128), F32)],
        scratch_shapes=[pltpu.VMEM((batch, 128, 128), F32), pltpu.VMEM((batch, 1, 128), F32),
                        pltpu.VMEM((batch, 1, 128), F32)],
        args=(proj, proj, proj, proj, gates, mnorm, c0, n0, m0))


def _run_parts(parts, *, semantics, name):
    grid = parts[0]["grid"]
    assert all(p["grid"] == grid for p in parts)
    n_in = [len(p["in_specs"]) for p in parts]
    n_out = [len(p["out_specs"]) for p in parts]
    n_scr = [len(p["scratch_shapes"]) for p in parts]

    def body(*refs):
        ins, outs, scr = refs[:sum(n_in)], refs[sum(n_in):sum(n_in) + sum(n_out)], refs[sum(n_in) + sum(n_out):]
        part_refs = [ins[sum(n_in[:k]):sum(n_in[:k + 1])] + outs[sum(n_out[:k]):sum(n_out[:k + 1])]
                     + scr[sum(n_scr[:k]):sum(n_scr[:k + 1])] for k in range(len(parts))]
        for p, r in zip(parts, part_refs):
            if "init" in p:
                p["init"](*r)
        for p, r in zip(parts, part_refs):
            p["body"](*r)

    outs = pl.pallas_call(
        body,
        grid=grid,
        in_specs=[s for p in parts for s in p["in_specs"]],
        out_specs=[s for p in parts for s in p["out_specs"]],
        out_shape=[s for p in parts for s in p["out_shape"]],
        scratch_shapes=[s for p in parts for s in p["scratch_shapes"]],
        compiler_params=pltpu.CompilerParams(dimension_semantics=semantics, vmem_limit_bytes=VMEM_LIMIT),
        name=name,
    )(*[a for p in parts for a in p["args"]])
    return [outs[sum(n_out[:k]):sum(n_out[:k + 1])] for k in range(len(parts))]


def _split(x):
    hi = x.astype(BF16)
    return hi, (x - hi.astype(F32)).astype(BF16)


def _dot_split(a, b):
    a_hi, a_lo = a
    b_hi, b_lo = b if isinstance(b, tuple) else (b, None)
    if a_hi.shape[1] % 128 == 0:
        main = _dot(jnp.concatenate([a_hi, a_lo], axis=1), jnp.concatenate([b_hi, b_hi], axis=0))
    else:
        main = _dot(a_hi, b_hi) + _dot(a_lo, b_hi)
    return main if b_lo is None else main + _dot(a_hi, b_lo)


def _unit_lower_inverse(mats, rows, cols, size):
    def same_block(shift):
        return (rows >> shift) == (cols >> shift)

    ident = jnp.where(rows == cols, 1.0, 0.0)
    d = [jnp.where(same_block(3), a, 0.0) for a in mats]
    d_s = [_split(x) for x in d]
    d2_s = [_split(_dot_split(x, x[0])) for x in d_s]
    d4_s = [_split(_dot_split(x, x[0])) for x in d2_s]
    t = [ident - x for x in d]
    t = [x + _dot_split(_split(x), y[0]) for x, y in zip(t, d2_s)]
    t = [x + _dot_split(_split(x), y[0]) for x, y in zip(t, d4_s)]
    a_hi = [x.astype(BF16) for x in mats]
    shift = 3
    while (1 << shift) < size:
        off = jnp.logical_and(same_block(shift + 1), jnp.logical_not(same_block(shift)))
        off_bf = jnp.where(off, 1.0, 0.0).astype(BF16)
        t_b = [x.astype(BF16) for x in t]
        te_b = [_dot(x, y * off_bf).astype(BF16) for x, y in zip(t_b, a_hi)]
        t = [x - _dot(y, z) for x, y, z in zip(t, te_b, t_b)]
        shift += 1
    return t


def _gdn_prep_kernel(q_ref, k_ref, v_ref, pq_ref, pk_ref, pv_ref, iq_ref, ik_ref, iv_ref, wq_ref, wk_ref, wv_ref,
                     gates_ref,
                     u_ref, w_ref, qe_ref, kd_ref, attn_ref, glast_ref, ext, *, chunk, nsub):
    L = chunk
    R = nsub * L
    head = pl.program_id(1)
    first = pl.program_id(2) == 0

    def conv(x_ref, prev_ref, init_ref, cw_ref):
        prev = jnp.where(first, init_ref[0], prev_ref[0, 0])
        ext[0:8, :] = prev
        ext[8:8 + R, :] = x_ref[0, 0]
        base = 8 - (CONV_W - 1)
        acc = ext[base:base + R, :] * cw_ref[0:1, :]
        for i in range(1, CONV_W):
            acc = acc + ext[base + i:base + i + R, :] * cw_ref[i:i + 1, :]
        return acc * _sigmoid(acc)

    q = conv(q_ref, pq_ref, iq_ref, wq_ref)
    k = conv(k_ref, pk_ref, ik_ref, wk_ref)
    v_all = conv(v_ref, pv_ref, iv_ref, wv_ref)
    qn_all = q * _l2_scale(q) * Q_SCALE
    kn_all = k * _l2_scale(k)

    gates = gates_ref[0]
    lane = lax.broadcasted_iota(jnp.int32, (R, 128), 1)
    lg_all = _select_lane(gates, lane, head + LANE_GA)
    beta_all = _select_lane(gates, lane, head + LANE_GB)

    U = min(R, 128)
    per_unit = U // L
    rows = lax.broadcasted_iota(jnp.int32, (U, U), 0)
    cols = lax.broadcasted_iota(jnp.int32, (U, U), 1)
    shift = L.bit_length() - 1
    same_chunk = (rows >> shift) == (cols >> shift)
    causal = jnp.logical_and(rows >= cols, same_chunk)
    strict = jnp.logical_and(rows > cols, same_chunk)
    upper = jnp.logical_and(rows <= cols, same_chunk)
    eye = rows == cols
    row_id = lax.broadcasted_iota(jnp.int32, (U, 1), 0)

    units = range(R // U)
    sls = [slice(j * U, (j + 1) * U) for j in units]
    g_cols, decays, a_mats, qks = [], [], [], []
    for sl in sls:
        lg_col = lg_all[sl]
        lg_row = jnp.sum(jnp.where(eye, lg_col, 0.0), axis=0, keepdims=True)
        g_col = jnp.sum(jnp.where(causal, lg_row, 0.0), axis=1, keepdims=True)
        g_row = jnp.sum(jnp.where(upper, lg_col, 0.0), axis=0, keepdims=True)
        decay = jnp.where(causal, jnp.exp(jnp.where(causal, g_col - g_row, 0.0)), 0.0)
        kb = kn_all[sl].astype(BF16)
        kk = _dot_nt(kb, kb)
        qks.append(_dot_nt(qn_all[sl].astype(BF16), kb))
        a_mats.append(jnp.where(strict, beta_all[sl] * kk * decay, 0.0))
        g_cols.append(g_col)
        decays.append(decay)

    t_mats = _unit_lower_inverse(a_mats, rows, cols, L)

    e_gs = [jnp.exp(g) for g in g_cols]
    rhs_s = [jnp.concatenate([beta_all[sl] * v_all[sl], (beta_all[sl] * e_g) * kn_all[sl]], axis=1).astype(BF16)
             for sl, e_g in zip(sls, e_gs)]
    uws = [_dot_split(_split(t), r) for t, r in zip(t_mats, rhs_s)]

    for j in units:
        sl, g_col = sls[j], g_cols[j]
        u_ref[0, 0, sl, :] = uws[j][:, 0:128]
        w_ref[0, 0, sl, :] = uws[j][:, 128:256].astype(BF16)
        qe_ref[0, 0, sl, :] = (qn_all[sl] * e_gs[j]).astype(BF16)
        attn = qks[j] * decays[j]
        g_last_col = jnp.zeros((U, 1), F32)
        for i in range(per_unit):
            g_last = g_col[(i + 1) * L - 1:(i + 1) * L, :]
            g_last_col = jnp.where((row_id >> shift) == i, g_last, g_last_col)
            glast_ref[0, 0, j * per_unit + i] = jnp.broadcast_to(jnp.exp(g_last), (1, 128))
            diag = attn[i * L:(i + 1) * L, :]
            if i > 0:
                diag = pltpu.roll(diag, U - i * L, axis=1)
            attn_ref[0, 0, j * per_unit + i] = diag[:, 0:L].astype(BF16)
        kd_ref[0, 0, sl, :] = (kn_all[sl] * jnp.exp(g_last_col - g_col)).astype(BF16)


def _gdn_prep(proj, gates, conv_init, conv_w, *, chunk, nsub):
    _, batch, seq, _ = proj.shape
    nc = seq // chunk
    R = nsub * chunk

    def col(base):
        return pl.BlockSpec((1, 1, R, 128), lambda b, h, c: (base + h, b, c, 0))

    def prev(base):
        return pl.BlockSpec((1, 1, 8, 128), lambda b, h, c: (base + h, b, jnp.maximum(c * (R // 8) - 1, 0), 0))

    def init(base):
        return pl.BlockSpec((1, 8, 128), lambda b, h, c: (base + h, 0, 0))

    def cw(base):
        return pl.BlockSpec((CONV_W, 128), lambda b, h, c: (0, base + h))

    def per_token(dtype):
        return (pl.BlockSpec((1, 1, R, 128), lambda b, h, c: (b, h, c, 0)),
                jax.ShapeDtypeStruct((batch, N_HEADS, seq, 128), dtype))

    outs = [per_token(F32), per_token(BF16), per_token(BF16), per_token(BF16),
            (pl.BlockSpec((1, 1, nsub, chunk, chunk), lambda b, h, c: (b, h, c, 0, 0)),
             jax.ShapeDtypeStruct((batch, N_HEADS, nc, chunk, chunk), BF16)),
            (pl.BlockSpec((1, 1, nsub, 1, 128), lambda b, h, c: (b, h, c, 0, 0)),
             jax.ShapeDtypeStruct((batch, N_HEADS, nc, 1, 128), F32))]
    return dict(
        body=functools.partial(_gdn_prep_kernel, chunk=chunk, nsub=nsub),
        grid=(batch, N_HEADS, nc // nsub),
        in_specs=[col(COL_GQ), col(COL_GK), col(COL_GV), prev(COL_GQ), prev(COL_GK), prev(COL_GV),
                  init(0), init(8), init(16), cw(0), cw(8), cw(16),
                  pl.BlockSpec((1, R, 128), lambda b, h, c: (b, c, 0))],
        out_specs=[o[0] for o in outs],
        out_shape=[o[1] for o in outs],
        scratch_shapes=[pltpu.VMEM((8 + R, 128), F32)],
        args=(proj, proj, proj, proj, proj, proj, conv_init, conv_init, conv_init, conv_w, conv_w, conv_w, gates))


def _cast_kernel(*refs, pair_tiles):
    for src, dst, tile in zip(refs[:len(refs) // 2], refs[len(refs) // 2:], pair_tiles):
        if tile is None:
            dst[...] = src[...].astype(dst.dtype)
        else:
            half = src.shape[1] // 2
            for t in range(half // tile):
                dst[:, 2 * t * tile:(2 * t + 1) * tile] = src[:, t * tile:(t + 1) * tile].astype(dst.dtype)
                dst[:, (2 * t + 1) * tile:(2 * t + 2) * tile] = (
                    src[:, half + t * tile:half + (t + 1) * tile].astype(dst.dtype))


def _cast_rows(weights, grid, pair_tiles):
    steps = 1
    for g in grid:
        steps *= g

    def slab(w):
        rows = w.shape[0] // steps
        assert rows * steps == w.shape[0] and rows % 16 == 0

        def index(*idx):
            flat = idx[0]
            for i, g in zip(idx[1:], grid[1:]):
                flat = flat * g + i
            return (flat, 0)

        return pl.BlockSpec((rows, w.shape[1]), index)

    return dict(body=functools.partial(_cast_kernel, pair_tiles=tuple(pair_tiles)), grid=grid,
                in_specs=[slab(w) for w in weights],
                out_specs=[slab(w) for w in weights],
                out_shape=[jax.ShapeDtypeStruct(w.shape, BF16) for w in weights], scratch_shapes=[],
                args=tuple(weights))


def _gdn_scan_init(u_ref, w_ref, qe_ref, kd_ref, attn_ref, glast_ref, gz_ref, gnorm_ref, s0_ref,
                   o_ref, s_out_ref, s_s):
    shared = s0_ref.shape[0] == 1

    @pl.when(pl.program_id(1) == 0)
    def _():
        for b in range(u_ref.shape[0]):
            for h in range(u_ref.shape[1]):
                s_s[b, h] = s0_ref[0 if shared else b, h]


def _gdn_scan_kernel(u_ref, w_ref, qe_ref, kd_ref, attn_ref, glast_ref, gz_ref, gnorm_ref, s0_ref,
                     o_ref, s_out_ref, s_s):
    nb, nh = u_ref.shape[0], u_ref.shape[1]
    L = attn_ref.shape[3]
    chains = [(b, h) for b in range(nb) for h in range(nh)]
    s_cur = [s_s[b, h] for b, h in chains]
    for j in range(attn_ref.shape[2]):
        sl = slice(j * L, (j + 1) * L)
        ws = [_dot(jnp.concatenate([w_ref[b, h, sl, :], qe_ref[b, h, sl, :]], axis=0), sp.astype(BF16))
              for (b, h), sp in zip(chains, s_cur)]
        vb = [(u_ref[b, h, sl, :] - r[0:L]).astype(BF16) for (b, h), r in zip(chains, ws)]
        intra = [_dot(attn_ref[b, h, j], v) for (b, h), v in zip(chains, vb)]
        upd = [_dot_tn(kd_ref[b, h, sl, :], v) for (b, h), v in zip(chains, vb)]
        s_cur = [glast_ref[b, h, j][:, 0:1] * sp + up for (b, h), sp, up in zip(chains, s_cur, upd)]
        for i, (b, h) in enumerate(chains):
            o = ws[i][L:2 * L] + intra[i]
            gz = gz_ref[h, b, sl, :]
            o_ref[h, b, sl, :] = (o * _rms_scale(o) * gnorm_ref[h] * (gz * _sigmoid(gz))).astype(o_ref.dtype)
    for (b, h), sp in zip(chains, s_cur):
        s_s[b, h] = sp
        s_out_ref[b, h] = sp


def _gdn_scan(prep, proj, gnorm, s0, *, chunk, heads, nsub):
    _, batch, seq, _ = proj.shape
    u, w, qe, kd, attn, glast = prep
    hb = heads
    assert COL_GZ % hb == 0 and N_HEADS % hb == 0
    tok = pl.BlockSpec((batch, hb, nsub * chunk, 128), lambda h, c: (0, h, c, 0))
    return dict(
        init=_gdn_scan_init,
        body=_gdn_scan_kernel,
        grid=(N_HEADS // hb, seq // (nsub * chunk)),
        in_specs=[tok, tok, tok, tok,
                  pl.BlockSpec((batch, hb, nsub, chunk, chunk), lambda h, c: (0, h, c, 0, 0)),
                  pl.BlockSpec((batch, hb, nsub, 1, 128), lambda h, c: (0, h, c, 0, 0)),
                  pl.BlockSpec((hb, batch, nsub * chunk, 128), lambda h, c: (COL_GZ // hb + h, 0, c, 0)),
                  pl.BlockSpec((hb, 1, 128), lambda h, c: (h, 0, 0)),
                  pl.BlockSpec((s0.shape[0], hb, 128, 128), lambda h, c: (0, h, 0, 0))],
        out_specs=[pl.BlockSpec((hb, batch, nsub * chunk, 128), lambda h, c: (h, 0, c, 0)),
                   pl.BlockSpec((batch, hb, 128, 128), lambda h, c: (0, h, 0, 0))],
        out_shape=[jax.ShapeDtypeStruct((N_HEADS, batch, seq, 128), BF16),
                   jax.ShapeDtypeStruct((batch, N_HEADS, 128, 128), F32)],
        scratch_shapes=[pltpu.VMEM((batch, hb, 128, 128), F32)],
        args=(u, w, qe, kd, attn, glast, proj, gnorm, s0))


def _sample_kernel(p_ref, gates_ref, c_ref, n_ref, m_ref, s_ref, cb_ref, cw_ref, mnorm_ref, gnorm_ref,
                   mix_ref, c_out_ref, n_out_ref, m_out_ref, s_out_ref, ext, xc_s, *, steps):
    T = steps
    NS = SAMPLE_SEQS
    R = NS * T
    H = N_HEADS
    nbuf = CONV_W - 1

    for s in range(NS):
        ext[8 * s:8 * s + nbuf, :] = cb_ref[0, s]
        for blk in range(COL_GZ - COL_GQ):
            ext[8 * s + nbuf:8 * s + nbuf + T, blk * 128:(blk + 1) * 128] = p_ref[COL_GQ + blk, 0, s * T:(s + 1) * T, :]
        acc = ext[8 * s:8 * s + T, :] * cw_ref[0:1, :]
        for i in range(1, CONV_W):
            acc = acc + ext[8 * s + i:8 * s + i + T, :] * cw_ref[i:i + 1, :]
        xc_s[s * T:(s + 1) * T, :] = acc * _sigmoid(acc)

    rowi = lax.broadcasted_iota(jnp.int32, (R, 1), 0)
    tpos = rowi & (T - 1)
    second = rowi >= T
    seq_rows = [jnp.logical_not(second), second]

    def pick(x, j):
        return jnp.where(second, x[T + j:T + j + 1, :], x[j:j + 1, :])

    def cumsum_t(x):
        acc = jnp.where(tpos >= 0, pick(x, 0), 0.0)
        for j in range(1, T):
            acc = acc + jnp.where(tpos >= j, pick(x, j), 0.0)
        return acc

    def pcol(block, h):
        return p_ref[block + h, 0]

    gates = gates_ref[0]
    lane = lax.broadcasted_iota(jnp.int32, (R, 128), 1)
    lf_all = lg_all = beta_all = gates
    m_in = m_ref[0]
    lane_s = lax.broadcasted_iota(jnp.int32, (NS, H), 1)
    row_s = lax.broadcasted_iota(jnp.int32, (NS, H), 0)
    m_acc = jnp.zeros((NS, H), F32)

    qc_all, gdn_in = [], []
    for h in range(H):
        qb = (pcol(COL_MQ, h) * Q_SCALE).astype(BF16)
        qc_all.append(jnp.where(second, _dot(qb, c_ref[0, 1, h].astype(BF16)),
                                _dot(qb, c_ref[0, 0, h].astype(BF16))))
        cq = xc_s[:, h * 128:(h + 1) * 128]
        ck = xc_s[:, WIDTH + h * 128:WIDTH + (h + 1) * 128]
        qn = cq * _l2_scale(cq) * Q_SCALE
        kn = ck * _l2_scale(ck)
        kq = jnp.concatenate([kn, qn], axis=0).astype(BF16)
        r0 = _dot(kq, s_ref[0, 0, h].astype(BF16))
        r1 = _dot(kq, s_ref[0, 1, h].astype(BF16))
        gdn_in.append((qn, kn, jnp.where(second, r1[0:R], r0[0:R]), jnp.where(second, r1[R:2 * R], r0[R:2 * R])))

    c_upd, s_upd = [], []
    for h in range(H):
        li = _select_lane(gates, lane, LANE_MI + h)
        lf = _select_lane(lf_all, lane, LANE_MF + h)
        mh = _select_lane(m_in, lane_s, h)
        m0 = jnp.where(second, mh[1:2, :], mh[0:1, :])
        b = cumsum_t(lf)
        a = b + m0
        dcol = [b - pick(b, j) + pick(li, j) for j in range(T)]
        m_t = a
        for j in range(T):
            m_t = jnp.maximum(m_t, jnp.where(tpos >= j, dcol[j], -jnp.inf))
        inter = jnp.exp(a - m_t)

        q = pcol(COL_MQ, h) * Q_SCALE
        k = pcol(COL_MK, h)
        v = pcol(COL_MV, h)
        num = jnp.zeros((R, 128), F32)
        den = jnp.zeros((R, 1), F32)
        for j in range(T):
            w_j = jnp.where(tpos >= j, jnp.exp(jnp.where(tpos >= j, dcol[j] - m_t, 0.0)), 0.0)
            s_j = jnp.sum(q * pick(k, j), axis=1, keepdims=True) * w_j
            num = num + s_j * pick(v, j)
            den = den + s_j
        qc = qc_all[h]
        n_sel =jnp.where(second, n_ref[0, 1, h:h + 1, :], n_ref[0, 0, h:h + 1, :])
        num = num + inter * qc
        den = den + inter * jnp.sum(q * n_sel, axis=1, keepdims=True)
        hh = num / jnp.maximum(jnp.abs(den), jnp.exp(-m_t))
        mix_ref[0, :, h * 128:(h + 1) * 128] = hh * _rms_scale(hh) * mnorm_ref[h:h + 1, :] * _sigmoid(pcol(COL_MO, h))

        m_l = pick(m_t, T - 1)
        b_l = pick(b, T - 1)
        kw = jnp.exp(b_l - b + li - m_l) * k
        d_c = jnp.exp(b_l + m0 - m_l)
        c_upd.append((kw, d_c, v.astype(BF16)))
        for s in range(NS):
            kws = jnp.where(seq_rows[s], kw, 0.0)
            n_out_ref[0, s, h:h + 1, :] = (d_c[s * T:s * T + 1, :] * n_ref[0, s, h:h + 1, :]
                                           + jnp.sum(kws, axis=0, keepdims=True))
            m_acc = jnp.where(jnp.logical_and(lane_s == h, row_s == s), m_l[s * T:s * T + 1, :], m_acc)

        lg = _select_lane(lg_all, lane, LANE_GA + h)
        beta = _select_lane(beta_all, lane, LANE_GB + h)
        g = cumsum_t(lg)
        e_g = jnp.exp(g)
        cv = xc_s[:, 2 * WIDTH + h * 128:2 * WIDTH + (h + 1) * 128]
        qn, kn, k_s, q_s = gdn_in[h]
        x = beta * (cv - e_g * k_s)
        rel = [jnp.exp(jnp.where(tpos >= j, g - pick(g, j), 0.0)) for j in range(T)]
        for j in range(T - 1):
            kk_j = jnp.sum(kn * pick(kn, j), axis=1, keepdims=True)
            a_j = jnp.where(tpos > j, beta * kk_j * rel[j], 0.0)
            x = x - a_j * pick(x, j)
        o = e_g * q_s
        for j in range(T):
            qk_j = jnp.sum(qn * pick(kn, j), axis=1, keepdims=True)
            o = o + jnp.where(tpos >= j, qk_j * rel[j], 0.0) * pick(x, j)
        gz = pcol(COL_GZ, h)
        mix_ref[0, :, WIDTH + h * 128:WIDTH + (h + 1) * 128] = (
            o * _rms_scale(o) * gnorm_ref[h:h + 1, :] * (gz * _sigmoid(gz)))

        g_l = pick(g, T - 1)
        s_upd.append((kn * jnp.exp(g_l - g), jnp.exp(g_l), x.astype(BF16)))
    m_out_ref[0] = m_acc

    for h in range(H):
        for (lhs, scale, rhs), st_ref, out_ref in ((c_upd[h], c_ref, c_out_ref), (s_upd[h], s_ref, s_out_ref)):
            for s in range(NS):
                rows_s = jnp.where(seq_rows[s], lhs, 0.0).astype(BF16)
                out_ref[0, s, h] = scale[s * T:s * T + 1, :] * st_ref[0, s, h] + _dot_tn(rows_s, rhs)


def _sample_step(proj, gates, c, n, m, s, conv_buf, conv_w, mnorm, gnorm, *, steps, grid):
    groups = c.shape[0]
    ns = SAMPLE_SEQS
    assert ns == 2 and ns * steps == 8 and steps >= CONV_W - 1 and steps & (steps - 1) == 0
    assert len(grid) == 2 and grid[0] * grid[1] == groups

    def per_group(*shape):
        return pl.BlockSpec((1,) + shape, lambda i, j: (i * grid[1] + j,) + (0,) * len(shape))

    def const(*shape):
        return pl.BlockSpec(shape, lambda i, j: (0,) * len(shape))

    state = (ns, N_HEADS, 128, 128)
    return dict(
        body=functools.partial(_sample_kernel, steps=steps),
        grid=grid,
        in_specs=[pl.BlockSpec((PROJ_COLS // 128, 1, ns * steps, 128), lambda i, j: (0, i * grid[1] + j, 0, 0)),
                  per_group(ns * steps, 128), per_group(*state), per_group(ns, N_HEADS, 128),
                  per_group(ns, N_HEADS), per_group(*state), per_group(ns, CONV_W - 1, 3 * WIDTH),
                  const(CONV_W, 3 * WIDTH), const(N_HEADS, 128), const(N_HEADS, 128)],
        out_specs=[per_group(ns * steps, 2 * WIDTH), per_group(*state), per_group(ns, N_HEADS, 128),
                   per_group(ns, N_HEADS), per_group(*state)],
        out_shape=[jax.ShapeDtypeStruct((groups, ns * steps, 2 * WIDTH), F32),
                   jax.ShapeDtypeStruct((groups,) + state, F32),
                   jax.ShapeDtypeStruct((groups, ns, N_HEADS, 128), F32),
                   jax.ShapeDtypeStruct((groups, ns, N_HEADS), F32),
                   jax.ShapeDtypeStruct((groups,) + state, F32)],
        scratch_shapes=[pltpu.VMEM((8 * ns, 3 * WIDTH), F32), pltpu.VMEM((ns * steps, 3 * WIDTH), F32)],
        args=(proj, gates, c, n, m, s, conv_buf, conv_w, mnorm, gnorm))


def kernel(x_prompt, x_sample, state_mlstm_C, state_mlstm_n, state_mlstm_m, state_gdn_S, state_gdn_conv, meta_tokens, norm_pre_mix, norm_post_mix, norm_pre_ffn, norm_post_ffn, w_in, mlstm_b_i, mlstm_b_f, mlstm_norm, gdn_conv_w, gdn_A_log, gdn_dt_bias, gdn_norm, w_out, w_gate_up, w_down):
    assert w_in.shape[0] == 1, "single-layer trunk"
    B, S, D = x_prompt.shape
    Bs, Ts, _ = x_sample.shape
    H = N_HEADS

    w_proj = _regroup_weights(jnp.transpose(w_in[0]))
    zeros_h = jnp.zeros((H,), F32)
    gbias = jnp.concatenate([mlstm_b_i[0], mlstm_b_f[0], gdn_dt_bias[0], zeros_h,
                             jnp.zeros((128 - 4 * H,), F32)]).reshape(1, 128)
    arow = jnp.concatenate([zeros_h, zeros_h, gdn_A_log[0], zeros_h,
                            jnp.zeros((128 - 4 * H,), F32)]).reshape(1, 128)
    g_pre_mix = norm_pre_mix[0].reshape(1, D)
    g_post_mix = norm_post_mix[0].reshape(1, D)
    g_pre_ffn = norm_pre_ffn[0].reshape(1, D)
    g_post_ffn = norm_post_ffn[0].reshape(1, D)
    mnorm = mlstm_norm[0]
    gnorm = gdn_norm[0]
    mnorm3 = mnorm.reshape(H, 1, 128)
    gnorm3 = gnorm.reshape(H, 1, 128)
    conv_w = gdn_conv_w[0]

    xp = x_prompt.reshape(B * S, D)
    xs = x_sample.reshape(Bs * Ts, D)

    nblk = PROJ_COLS // 128
    proj_p = _norm_matmul(xp, g_pre_mix, w_proj, tm=PROJ_TM, tn=PROJ_TN).reshape(nblk, B, S, 128)
    n_sm = Bs * Ts + N_META
    proj_sm = _norm_matmul(jnp.concatenate([xs, meta_tokens], axis=0), g_pre_mix, w_proj, tm=n_sm, tn=PROJ_TN)
    proj_m = proj_sm[:, Bs * Ts:].reshape(nblk, 1, N_META, 128)

    gates_p = _gate_act(proj_p.reshape(nblk, B * S, 128), gbias, arow, tm=PROJ_TM).reshape(B, S, 128)
    gates_sm = _gate_act(proj_sm, gbias, arow, tm=n_sm)
    gates_m = gates_sm[Bs * Ts:].reshape(1, N_META, 128)

    def conv_rows(blocks):
        return jnp.moveaxis(blocks, 0, -2).reshape(blocks.shape[1:-1] + (3 * WIDTH,))

    c0, n0, m0 = (jnp.zeros((1, H, 128, 128), F32), jnp.zeros((1, H, 1, 128), F32), jnp.zeros((1, H, 1, 128), F32))
    (_, c_l, n_l, m_l), = _run_parts([_mlstm_chunked(proj_m, gates_m, mnorm3, c0, n0, m0, chunk=N_META)],
                                     semantics=("parallel", "arbitrary"), name="mlstm_lead")
    prep_l, = _run_parts([_gdn_prep(proj_m, gates_m, jnp.zeros((3 * H, 8, 128), F32), conv_w, chunk=N_META, nsub=1)],
                         semantics=("parallel", "parallel", "parallel"), name="gdn_prep_lead")
    (_, s_l), = _run_parts([_gdn_scan(prep_l, proj_m, gnorm3, c0, chunk=N_META, heads=N_HEADS, nsub=1)],
                           semantics=("parallel", "arbitrary"), name="gdn_scan_lead")

    conv_init = proj_m[COL_GQ:COL_GZ, 0, N_META - 8:, :]
    prep_parts = _gdn_prep(proj_p, gates_p, conv_init, conv_w, chunk=GDN_CHUNK, nsub=GDN_PREP_CHUNKS)
    prep_p, (w_out_b, w_gu_b, w_dn_b) = _run_parts(
        [prep_parts, _cast_rows([w_out[0], w_gate_up[0], w_down[0]], prep_parts["grid"], [None, FFN_TF, None])],
        semantics=("parallel", "parallel", "parallel"), name="gdn_prep_and_casts")

    ns = SAMPLE_SEQS
    grp = Bs // ns
    m_in = state_mlstm_m[0].reshape(grp, ns, H)
    mlstm_parts = _mlstm_chunked(proj_p, gates_p, mnorm3, c_l, n_l, m_l, chunk=MLSTM_CHUNK)
    scan_parts = _gdn_scan(prep_p, proj_p, gnorm3, s_l, chunk=GDN_CHUNK, heads=1, nsub=MLSTM_CHUNK // GDN_CHUNK)
    sample_parts = _sample_step(
        proj_sm.reshape(nblk, n_sm // (ns * Ts), ns * Ts, 128), gates_sm.reshape(n_sm // (ns * Ts), ns * Ts, 128),
        state_mlstm_C[0].reshape(grp, ns, H, 128, 128),
        state_mlstm_n[0].reshape(grp, ns, H, 128), m_in, state_gdn_S[0].reshape(grp, ns, H, 128, 128),
        state_gdn_conv[0].reshape(grp, ns, CONV_W - 1, 3 * WIDTH), conv_w, mnorm, gnorm, steps=Ts,
        grid=mlstm_parts["grid"])
    (mix_b, p_s), (mix_a, p_c, p_n, p_m), (mix_s, s_c, s_n, s_m, s_s) = _run_parts(
        [scan_parts, mlstm_parts, sample_parts], semantics=("parallel", "arbitrary"), name="scan_mlstm_sample")
    p_conv = conv_rows(proj_p[COL_GQ:COL_GZ, :, S - (CONV_W - 1):, :])

    x1_p, u2_p = _out_proj(mix_a.reshape(H, B * S, 128), 0, mix_b.reshape(H, B * S, 128), 0, w_out_b, xp,
                           g_post_mix, g_pre_ffn, tm=ROW_TM)
    y_p = _ffn(u2_p, w_gu_b, w_dn_b, x1_p, g_post_ffn, tm=ROW_TM, tf=FFN_TF)

    mix_s2 = mix_s.reshape(Bs * Ts, 2 * WIDTH)
    s_conv = conv_rows(proj_sm[COL_GQ:COL_GZ, :Bs * Ts].reshape(3 * H, Bs, Ts, 128)[:, :, Ts - (CONV_W - 1):, :])
    x1_s, u2_s = _out_proj(mix_s2, 0, mix_s2, 1, w_out_b, xs, g_post_mix, g_pre_ffn, tm=ROW_TM)
    y_s = _ffn(u2_s, w_gu_b, w_dn_b, x1_s, g_post_ffn, tm=ROW_TM, tf=FFN_TF)

    return (y_p.reshape(B, S, D), y_s.reshape(Bs, Ts, D),
            p_c[None], p_n[:, :, 0, :][None], p_m[:, :, 0, 0][None], p_s[None], p_conv[None],
            s_c.reshape(Bs, H, 128, 128)[None], s_n.reshape(Bs, H, 128)[None], s_m.reshape(Bs, H)[None],
            s_s.reshape(Bs, H, 128, 128)[None], s_conv[None])
```

```python
import functools

import jax
import jax.numpy as jnp
from jax import lax
from jax.experimental import pallas as pl
from jax.experimental.pallas import tpu as pltpu

F32 = jnp.float32
BF16 = jnp.bfloat16

HEAD_DIM = 128
N_HEADS = 8
WIDTH = N_HEADS * HEAD_DIM
CONV_W = 4
N_META = 16
D_FF = 5632
EPS = 1e-6
Q_SCALE = HEAD_DIM ** -0.5

COL_MQ, COL_MK, COL_MV, COL_MO = 0, 8, 16, 24
COL_GQ, COL_GK, COL_GV, COL_GZ = 32, 40, 48, 56
COL_GATES = 64
PROJ_COLS = 66 * 128
LANE_MI, LANE_MF, LANE_GA, LANE_GB = 0, 8, 16, 24

MLSTM_CHUNK = 256
MIXER_HEADS = 1
GDN_CHUNK = 64
GDN_PREP_CHUNKS = 32
SAMPLE_SEQS = 2

MXU_TILE = 256
VMEM_LIMIT = 56 * 1024 * 1024
PROJ_TM, PROJ_TN = 1024, 3 * MXU_TILE
ROW_TM = 512
FFN_TF = 2 * MXU_TILE


def _sigmoid(x):
    return 0.5 * jnp.tanh(0.5 * x) + 0.5


def _softplus(x):
    return jnp.maximum(x, 0.0) + jnp.log1p(jnp.exp(-jnp.abs(x)))


def _log_sigmoid(x):
    return -_softplus(-x)


def _rms_scale(x):
    return lax.rsqrt(jnp.mean(x * x, axis=-1, keepdims=True) + EPS)


def _l2_scale(x):
    return lax.rsqrt(jnp.sum(x * x, axis=-1, keepdims=True) + EPS)


def _dot(a, b):
    return jnp.dot(a, b, preferred_element_type=F32)


def _dot_nt(a, b):
    return lax.dot_general(a, b, (((1,), (1,)), ((), ())), preferred_element_type=F32)


def _dot_tn(a, b):
    return lax.dot_general(a, b, (((0,), (0,)), ((), ())), preferred_element_type=F32)


def _select_lane(x, lane_idx, target):
    return jnp.sum(jnp.where(lane_idx == target, x, 0.0), axis=1, keepdims=True)


REGROUP_TILE = 256


def _regroup_kernel(w_ref, mg_ref, gg_ref, o_ref):
    j = pl.program_id(0)
    wide_tiles = 2 * 4 * WIDTH // REGROUP_TILE

    @pl.when(j < wide_tiles)
    def _():
        o_ref[...] = w_ref[...].T.astype(BF16)

    @pl.when(j == wide_tiles)
    def _():
        pad = jnp.zeros((REGROUP_TILE - 4 * N_HEADS, w_ref.shape[1]), F32)
        o_ref[...] = jnp.concatenate([mg_ref[...], gg_ref[...], pad], axis=0).T.astype(BF16)


def _regroup_weights(w_t):
    n, d = w_t.shape
    wide = 4 * WIDTH
    narrow = 2 * N_HEADS
    assert n == 2 * (wide + narrow) and PROJ_COLS == 2 * wide + REGROUP_TILE
    wide_tiles = 2 * wide // REGROUP_TILE

    def src_row(j):
        j = jnp.minimum(j, wide_tiles - 1)
        return (j * (REGROUP_TILE // narrow) + jnp.where(j >= wide_tiles // 2, 1, 0)) * narrow

    return pl.pallas_call(
        _regroup_kernel,
        grid=(wide_tiles + 1,),
        in_specs=[pl.BlockSpec((pl.Element(REGROUP_TILE), pl.Element(d)), lambda j: (src_row(j), 0)),
                  pl.BlockSpec((narrow, d), lambda j: (wide // narrow, 0)),
                  pl.BlockSpec((narrow, d), lambda j: ((2 * wide + narrow) // narrow, 0))],
        out_specs=pl.BlockSpec((d, REGROUP_TILE), lambda j: (0, j)),
        out_shape=jax.ShapeDtypeStruct((d, PROJ_COLS), BF16),
        compiler_params=pltpu.CompilerParams(dimension_semantics=("parallel",), vmem_limit_bytes=VMEM_LIMIT),
        name="regroup_weights",
    )(w_t, w_t, w_t)


def _norm_matmul_kernel(x_ref, g_ref, w_ref, o_ref, u_ref):
    @pl.when(pl.program_id(1) == 0)
    def _():
        x = x_ref[...]
        u_ref[...] = (x * _rms_scale(x) * g_ref[...]).astype(BF16)

    res = _dot(u_ref[...], w_ref[...])
    for blk in range(o_ref.shape[0]):
        o_ref[blk] = res[:, blk * 128:(blk + 1) * 128]


def _norm_matmul(x, gain, w, *, tm, tn):
    m, d = x.shape
    n = w.shape[1]
    return pl.pallas_call(
        _norm_matmul_kernel,
        grid=(m // tm, n // tn),
        in_specs=[pl.BlockSpec((tm, d), lambda i, j: (i, 0)),
                  pl.BlockSpec((1, d), lambda i, j: (0, 0)),
                  pl.BlockSpec((d, tn), lambda i, j: (0, j))],
        out_specs=pl.BlockSpec((tn // 128, tm, 128), lambda i, j: (j, i, 0)),
        out_shape=jax.ShapeDtypeStruct((n // 128, m, 128), F32),
        scratch_shapes=[pltpu.VMEM((tm, d), BF16)],
        compiler_params=pltpu.CompilerParams(dimension_semantics=("parallel", "arbitrary"),
                                             vmem_limit_bytes=VMEM_LIMIT),
        name="norm_matmul",
    )(x, gain, w)


def _out_proj_kernel(ma_ref, mb_ref, wa_ref, wb_ref, x_ref, gpost_ref, gpre_ref, x1_ref, u2_ref):
    def rows(ref, sl):
        if len(ref.shape) == 2:
            return ref[sl, :].astype(BF16)
        return jnp.concatenate([ref[h, sl, :] for h in range(ref.shape[0])], axis=1).astype(BF16)

    tq = x_ref.shape[0] // 4
    quarters = [slice(i * tq, (i + 1) * tq) for i in range(4)]
    mos = [_dot(rows(ma_ref, sl), wa_ref[...]) + _dot(rows(mb_ref, sl), wb_ref[...]) for sl in quarters]
    for sl, mo in zip(quarters, mos):
        x1 = x_ref[sl, :] + mo * _rms_scale(mo) * gpost_ref[...]
        x1_ref[sl, :] = x1
        u2_ref[sl, :] = (x1 * _rms_scale(x1) * gpre_ref[...]).astype(BF16)


def _out_proj(mix_a, col_a, mix_b, col_b, w_out, x, g_post, g_pre, *, tm):
    m, d = x.shape

    def mix_spec(mix, col):
        if mix.ndim == 2:
            return pl.BlockSpec((tm, WIDTH), lambda i: (i, col))
        return pl.BlockSpec((N_HEADS, tm, 128), lambda i: (0, i, 0))

    return pl.pallas_call(
        _out_proj_kernel,
        grid=(m // tm,),
        in_specs=[mix_spec(mix_a, col_a),
                  mix_spec(mix_b, col_b),
                  pl.BlockSpec((WIDTH, d), lambda i: (0, 0)),
                  pl.BlockSpec((WIDTH, d), lambda i: (1, 0)),
                  pl.BlockSpec((tm, d), lambda i: (i, 0)),
                  pl.BlockSpec((1, d), lambda i: (0, 0)),
                  pl.BlockSpec((1, d), lambda i: (0, 0))],
        out_specs=[pl.BlockSpec((tm, d), lambda i: (i, 0)),
                   pl.BlockSpec((tm, d), lambda i: (i, 0))],
        out_shape=[jax.ShapeDtypeStruct((m, d), F32), jax.ShapeDtypeStruct((m, d), BF16)],
        compiler_params=pltpu.CompilerParams(dimension_semantics=("parallel",),
                                             vmem_limit_bytes=VMEM_LIMIT),
        name="out_proj",
    )(mix_a, mix_b, w_out, w_out, x, g_post, g_pre)


def _ffn_kernel(u_ref, wgu_ref, wd_ref, x1_ref, gain_ref, o_ref, acc_ref):
    f = pl.program_id(1)
    tf = wd_ref.shape[0]

    @pl.when(f == 0)
    def _():
        acc_ref[...] = jnp.zeros_like(acc_ref)

    gu = _dot(u_ref[...], wgu_ref[...])
    g, up = gu[:, :tf], gu[:, tf:]
    hidden = (g * _sigmoid(g) * up).astype(BF16)
    acc_ref[...] += _dot(hidden, wd_ref[...])

    @pl.when(f == pl.num_programs(1) - 1)
    def _():
        y = acc_ref[...]
        o_ref[...] = x1_ref[...] + y * _rms_scale(y) * gain_ref[...]


def _ffn(u2, w_gate_up, w_down, x1, gain, *, tm, tf):
    m, d = x1.shape
    nf = D_FF // tf
    return pl.pallas_call(
        _ffn_kernel,
        grid=(m // tm, nf),
        in_specs=[pl.BlockSpec((tm, d), lambda i, f: (i, 0)),
                  pl.BlockSpec((d, 2 * tf), lambda i, f: (0, f)),
                  pl.BlockSpec((tf, d), lambda i, f: (f, 0)),
                  pl.BlockSpec((tm, d), lambda i, f: (i, 0)),
                  pl.BlockSpec((1, d), lambda i, f: (0, 0))],
        out_specs=pl.BlockSpec((tm, d), lambda i, f: (i, 0)),
        out_shape=jax.ShapeDtypeStruct((m, d), F32),
        scratch_shapes=[pltpu.VMEM((tm, d), F32)],
        compiler_params=pltpu.CompilerParams(dimension_semantics=("parallel", "arbitrary"),
                                             vmem_limit_bytes=VMEM_LIMIT),
        name="ffn",
    )(u2, w_gate_up, w_down, x1, gain)


def _gate_act_kernel(g_ref, gbias_ref, arow_ref, o_ref):
    g = g_ref[0] + gbias_ref[...]
    lane = lax.broadcasted_iota(jnp.int32, g.shape, 1)
    decay = -jnp.exp(arow_ref[...]) * _softplus(g)
    o_ref[...] = jnp.where(lane < LANE_MF, g,
                           jnp.where(lane < LANE_GA, _log_sigmoid(g), jnp.where(lane < LANE_GB, decay, _sigmoid(g))))


def _gate_act(proj, gbias, arow, *, tm):
    m = proj.shape[1]
    return pl.pallas_call(
        _gate_act_kernel,
        grid=(m // tm,),
        in_specs=[pl.BlockSpec((1, tm, 128), lambda i: (COL_GATES, i, 0)),
                  pl.BlockSpec((1, 128), lambda i: (0, 0)),
                  pl.BlockSpec((1, 128), lambda i: (0, 0))],
        out_specs=pl.BlockSpec((tm, 128), lambda i: (i, 0)),
        out_shape=jax.ShapeDtypeStruct((m, 128), F32),
        compiler_params=pltpu.CompilerParams(dimension_semantics=("parallel",), vmem_limit_bytes=VMEM_LIMIT),
        name="gate_act",
    )(proj, gbias, arow)


def _mlstm_init(q_ref, k_ref, v_ref, og_ref, gates_ref, mnorm_ref, c0_ref, n0_ref, m0_ref,
                h_ref, c_out_ref, n_out_ref, m_out_ref, c_s, n_s, m_s):
    shared = c0_ref.shape[0] == 1

    @pl.when(pl.program_id(1) == 0)
    def _():
        for hh in range(q_ref.shape[0]):
            for b in range(q_ref.shape[1]):
                b0 = 0 if shared else b
                c_s[b, hh] = c0_ref[b0, hh]
                n_s[b, hh] = n0_ref[b0, hh]
                m_s[b, hh] = m0_ref[b0, hh]


def _mlstm_chunk_kernel(q_ref, k_ref, v_ref, og_ref, gates_ref, mnorm_ref, c0_ref, n0_ref, m0_ref,
                        h_ref, c_out_ref, n_out_ref, m_out_ref, c_s, n_s, m_s, *, chunk):
    L = chunk
    nh, nb = q_ref.shape[0], q_ref.shape[1]
    chains = [(hh, b) for hh in range(nh) for b in range(nb)]
    heads = [pl.program_id(0) * nh + hh for hh, _ in chains]

    lane = lax.broadcasted_iota(jnp.int32, (L, 128), 1)
    rows = lax.broadcasted_iota(jnp.int32, (L, L), 0)
    cols = lax.broadcasted_iota(jnp.int32, (L, L), 1)
    tril = rows >= cols
    triu = rows <= cols
    eye = rows == cols

    seqs = range(len(chains))
    gates = [gates_ref[b] for _, b in chains]
    li_col = [_select_lane(g, lane, hd + LANE_MI) for g, hd in zip(gates, heads)]
    lf_col = [_select_lane(g, lane, hd + LANE_MF) for g, hd in zip(gates, heads)]
    lf_row = [jnp.sum(jnp.where(eye, x, 0.0), axis=0, keepdims=True) for x in lf_col]
    li_row = [jnp.sum(jnp.where(eye, x, 0.0), axis=0, keepdims=True) for x in li_col]
    b_col = [jnp.sum(jnp.where(tril, x, 0.0), axis=1, keepdims=True) for x in lf_row]
    b_row = [jnp.sum(jnp.where(triu, x, 0.0), axis=0, keepdims=True) for x in lf_col]

    qb = [(q_ref[hh, b] * Q_SCALE).astype(BF16) for hh, b in chains]
    kb = [k_ref[hh, b].astype(BF16) for hh, b in chains]
    vb = [v_ref[hh, b].astype(BF16) for hh, b in chains]
    qk = [_dot_nt(x, y) for x, y in zip(qb, kb)]
    qc = [_dot(x, c_s[b, hh].astype(BF16)) for (hh, b), x in zip(chains, qb)]

    m_prev = [m_s[b, hh, :, 0:1] for hh, b in chains]
    a_col = [x + y for x, y in zip(b_col, m_prev)]
    dm = [jnp.where(tril, bc - br + lr, -jnp.inf) for bc, br, lr in zip(b_col, b_row, li_row)]
    m_t = [jnp.maximum(a, jnp.max(d, axis=1, keepdims=True)) for a, d in zip(a_col, dm)]
    inter = [jnp.exp(a - m) for a, m in zip(a_col, m_t)]
    s = [x * jnp.exp(d - m) for x, d, m in zip(qk, dm, m_t)]
    sv = [_dot(x.astype(BF16), v) for x, v in zip(s, vb)]

    m_last = [m[L - 1:L, :] for m in m_t]
    b_last = [x[L - 1:L, :] for x in b_col]
    d_c = [jnp.exp(bl + mp - ml) for bl, mp, ml in zip(b_last, m_prev, m_last)]
    kw = [jnp.exp(bl - bc + li - ml) * k_ref[hh, b]
          for (hh, b), bl, bc, li, ml in zip(chains, b_last, b_col, li_col, m_last)]
    c_upd = [_dot_tn(x.astype(BF16), v) for x, v in zip(kw, vb)]

    for i, (hh, b) in enumerate(chains):
        q = q_ref[hh, b] * Q_SCALE
        num = sv[i] + inter[i] * qc[i]
        den = (jnp.sum(s[i], axis=1, keepdims=True)
               + inter[i] * jnp.sum(q * n_s[b, hh], axis=1, keepdims=True))
        hid = num / jnp.maximum(jnp.abs(den), jnp.exp(-m_t[i]))
        out = hid * _rms_scale(hid) * mnorm_ref[hh] * _sigmoid(og_ref[hh, b])
        h_ref[hh, b] = out.astype(h_ref.dtype)
        c_new = d_c[i] * c_s[b, hh] + c_upd[i]
        n_new = d_c[i] * n_s[b, hh] + jnp.sum(kw[i], axis=0, keepdims=True)
        m_new = jnp.broadcast_to(m_last[i], (1, 128))
        c_s[b, hh], n_s[b, hh], m_s[b, hh] = c_new, n_new, m_new
        c_out_ref[b, hh], n_out_ref[b, hh], m_out_ref[b, hh] = c_new, n_new, m_new


def _mlstm_chunked(proj, gates, mnorm, c0, n0, m0, *, chunk, heads):
    _, batch, seq, _ = proj.shape
    nb0 = c0.shape[0]
    hb = heads
    assert N_HEADS % hb == 0 and all(base % hb == 0 for base in (COL_MQ, COL_MK, COL_MV, COL_MO))

    def col(base):
        return pl.BlockSpec((hb, batch, chunk, 128), lambda h, c: (base // hb + h, 0, c, 0))

    def state(nb, shape):
        return pl.BlockSpec((nb, hb) + shape, lambda h, c: (0, h, 0, 0))

    return dict(
        init=_mlstm_init,
        body=functools.partial(_mlstm_chunk_kernel, chunk=chunk),
        grid=(N_HEADS // hb, seq // chunk),
        in_specs=[col(COL_MQ), col(COL_MK), col(COL_MV), col(COL_MO),
                  pl.BlockSpec((batch, chunk, 128), lambda h, c: (0, c, 0)),
                  pl.BlockSpec((hb, 1, 128), lambda h, c: (h, 0, 0)),
                  state(nb0, (128, 128)), state(nb0, (1, 128)), state(nb0, (1, 128))],
        out_specs=[pl.BlockSpec((hb, batch, chunk, 128), lambda h, c: (h, 0, c, 0)),
                   state(batch, (128, 128)), state(batch, (1, 128)), state(batch, (1, 128))],
        out_shape=[jax.ShapeDtypeStruct((N_HEADS, batch, seq, 128), BF16),
                   jax.ShapeDtypeStruct((batch, N_HEADS, 128, 128), F32),
                   jax.ShapeDtypeStruct((batch, N_HEADS, 1, 128), F32),
                   jax.ShapeDtypeStruct((batch, N_HEADS, 1, 128), F32)],
        scratch_shapes=[pltpu.VMEM((batch, hb, 128, 128), F32), pltpu.VMEM((batch, hb, 1, 128), F32),
                        pltpu.VMEM((batch, hb, 1, 128), F32)],
        args=(proj, proj, proj, proj, gates, mnorm, c0, n0, m0))


def _run_parts(parts, *, semantics, name):
    grid = parts[0]["grid"]
    assert all(p["grid"] == grid for p in parts)
    n_in = [len(p["in_specs"]) for p in parts]
    n_out = [len(p["out_specs"]) for p in parts]
    n_scr = [len(p["scratch_shapes"]) for p in parts]

    def body(*refs):
        ins, outs, scr = refs[:sum(n_in)], refs[sum(n_in):sum(n_in) + sum(n_out)], refs[sum(n_in) + sum(n_out):]
        part_refs = [ins[sum(n_in[:k]):sum(n_in[:k + 1])] + outs[sum(n_out[:k]):sum(n_out[:k + 1])]
                     + scr[sum(n_scr[:k]):sum(n_scr[:k + 1])] for k in range(len(parts))]
        for p, r in zip(parts, part_refs):
            if "init" in p:
                p["init"](*r)
        for p, r in zip(parts, part_refs):
            p["body"](*r)

    outs = pl.pallas_call(
        body,
        grid=grid,
        in_specs=[s for p in parts for s in p["in_specs"]],
        out_specs=[s for p in parts for s in p["out_specs"]],
        out_shape=[s for p in parts for s in p["out_shape"]],
        scratch_shapes=[s for p in parts for s in p["scratch_shapes"]],
        compiler_params=pltpu.CompilerParams(dimension_semantics=semantics, vmem_limit_bytes=VMEM_LIMIT),
        name=name,
    )(*[a for p in parts for a in p["args"]])
    return [outs[sum(n_out[:k]):sum(n_out[:k + 1])] for k in range(len(parts))]


def _unit_lower_inverse(mats, rows, cols, size):
    def same_block(shift):
        return (rows >> shift) == (cols >> shift)

    ident = jnp.where(rows == cols, 1.0, 0.0)
    d = [jnp.where(same_block(3), a, 0.0) for a in mats]
    d_b = [x.astype(BF16) for x in d]
    d2_b = [_dot(x, x).astype(BF16) for x in d_b]
    d4_b = [_dot(x, x).astype(BF16) for x in d2_b]
    t = [ident - x for x in d]
    t = [x + _dot(x.astype(BF16), y) for x, y in zip(t, d2_b)]
    t = [x + _dot(x.astype(BF16), y) for x, y in zip(t, d4_b)]
    a_hi = [x.astype(BF16) for x in mats]
    shift = 3
    while (1 << shift) < size:
        off = jnp.logical_and(same_block(shift + 1), jnp.logical_not(same_block(shift)))
        off_bf = jnp.where(off, 1.0, 0.0).astype(BF16)
        t_b = [x.astype(BF16) for x in t]
        te_b = [_dot(x, y * off_bf).astype(BF16) for x, y in zip(t_b, a_hi)]
        t = [x - _dot(y, z) for x, y, z in zip(t, te_b, t_b)]
        shift += 1
    return t


def _gdn_prep_kernel(q_ref, k_ref, v_ref, pq_ref, pk_ref, pv_ref, iq_ref, ik_ref, iv_ref, wq_ref, wk_ref, wv_ref,
                     gates_ref,
                     u_ref, w_ref, qe_ref, kd_ref, attn_ref, glast_ref, ext, *, chunk, nsub):
    L = chunk
    R = nsub * L
    head = pl.program_id(1)
    first = pl.program_id(2) == 0

    def conv(x_ref, prev_ref, init_ref, cw_ref):
        prev = jnp.where(first, init_ref[0], prev_ref[0, 0])
        ext[0:8, :] = prev
        ext[8:8 + R, :] = x_ref[0, 0]
        base = 8 - (CONV_W - 1)
        acc = ext[base:base + R, :] * cw_ref[0:1, :]
        for i in range(1, CONV_W):
            acc = acc + ext[base + i:base + i + R, :] * cw_ref[i:i + 1, :]
        return acc * _sigmoid(acc)

    q = conv(q_ref, pq_ref, iq_ref, wq_ref)
    k = conv(k_ref, pk_ref, ik_ref, wk_ref)
    v_all = conv(v_ref, pv_ref, iv_ref, wv_ref)
    qn_all = q * _l2_scale(q) * Q_SCALE
    kn_all = k * _l2_scale(k)

    gates = gates_ref[0]
    lane = lax.broadcasted_iota(jnp.int32, (R, 128), 1)
    lg_all = _select_lane(gates, lane, head + LANE_GA)
    beta_all = _select_lane(gates, lane, head + LANE_GB)

    U = min(R, 128)
    per_unit = U // L
    rows = lax.broadcasted_iota(jnp.int32, (U, U), 0)
    cols = lax.broadcasted_iota(jnp.int32, (U, U), 1)
    shift = L.bit_length() - 1
    same_chunk = (rows >> shift) == (cols >> shift)
    causal = jnp.logical_and(rows >= cols, same_chunk)
    strict = jnp.logical_and(rows > cols, same_chunk)
    upper = jnp.logical_and(rows <= cols, same_chunk)
    eye = rows == cols
    row_id = lax.broadcasted_iota(jnp.int32, (U, 1), 0)

    units = range(R // U)
    sls = [slice(j * U, (j + 1) * U) for j in units]
    g_cols, decays, a_mats, qks = [], [], [], []
    for sl in sls:
        lg_col = lg_all[sl]
        lg_row = jnp.sum(jnp.where(eye, lg_col, 0.0), axis=0, keepdims=True)
        g_col = jnp.sum(jnp.where(causal, lg_row, 0.0), axis=1, keepdims=True)
        g_row = jnp.sum(jnp.where(upper, lg_col, 0.0), axis=0, keepdims=True)
        decay = jnp.where(causal, jnp.exp(jnp.where(causal, g_col - g_row, 0.0)), 0.0)
        kb = kn_all[sl].astype(BF16)
        kk = _dot_nt(kb, kb)
        qks.append(_dot_nt(qn_all[sl].astype(BF16), kb))
        a_mats.append(jnp.where(strict, beta_all[sl] * kk * decay, 0.0))
        g_cols.append(g_col)
        decays.append(decay)

    t_mats = _unit_lower_inverse(a_mats, rows, cols, L)

    e_gs = [jnp.exp(g) for g in g_cols]
    rhs_s = [jnp.concatenate([beta_all[sl] * v_all[sl], (beta_all[sl] * e_g) * kn_all[sl]], axis=1).astype(BF16)
             for sl, e_g in zip(sls, e_gs)]
    uws = [_dot(t.astype(BF16), r) for t, r in zip(t_mats, rhs_s)]

    for j in units:
        sl, g_col = sls[j], g_cols[j]
        u_ref[0, 0, sl, :] = uws[j][:, 0:128]
        w_ref[0, 0, sl, :] = uws[j][:, 128:256].astype(BF16)
        qe_ref[0, 0, sl, :] = (qn_all[sl] * e_gs[j]).astype(BF16)
        attn = qks[j] * decays[j]
        g_last_col = jnp.zeros((U, 1), F32)
        for i in range(per_unit):
            g_last = g_col[(i + 1) * L - 1:(i + 1) * L, :]
            g_last_col = jnp.where((row_id >> shift) == i, g_last, g_last_col)
            glast_ref[0, 0, j * per_unit + i] = jnp.broadcast_to(jnp.exp(g_last), (1, 128))
            diag = attn[i * L:(i + 1) * L, :]
            if i > 0:
                diag = pltpu.roll(diag, U - i * L, axis=1)
            attn_ref[0, 0, j * per_unit + i] = diag[:, 0:L].astype(BF16)
        kd_ref[0, 0, sl, :] = (kn_all[sl] * jnp.exp(g_last_col - g_col)).astype(BF16)


def _gdn_prep(proj, gates, conv_init, conv_w, *, chunk, nsub):
    _, batch, seq, _ = proj.shape
    nc = seq // chunk
    R = nsub * chunk

    def col(base):
        return pl.BlockSpec((1, 1, R, 128), lambda b, h, c: (base + h, b, c, 0))

    def prev(base):
        return pl.BlockSpec((1, 1, 8, 128), lambda b, h, c: (base + h, b, jnp.maximum(c * (R // 8) - 1, 0), 0))

    def init(base):
        return pl.BlockSpec((1, 8, 128), lambda b, h, c: (base + h, 0, 0))

    def cw(base):
        return pl.BlockSpec((CONV_W, 128), lambda b, h, c: (0, base + h))

    def per_token(dtype):
        return (pl.BlockSpec((1, 1, R, 128), lambda b, h, c: (b, h, c, 0)),
                jax.ShapeDtypeStruct((batch, N_HEADS, seq, 128), dtype))

    outs = [per_token(F32), per_token(BF16), per_token(BF16), per_token(BF16),
            (pl.BlockSpec((1, 1, nsub, chunk, chunk), lambda b, h, c: (b, h, c, 0, 0)),
             jax.ShapeDtypeStruct((batch, N_HEADS, nc, chunk, chunk), BF16)),
            (pl.BlockSpec((1, 1, nsub, 1, 128), lambda b, h, c: (b, h, c, 0, 0)),
             jax.ShapeDtypeStruct((batch, N_HEADS, nc, 1, 128), F32))]
    return dict(
        body=functools.partial(_gdn_prep_kernel, chunk=chunk, nsub=nsub),
        grid=(batch, N_HEADS, nc // nsub),
        in_specs=[col(COL_GQ), col(COL_GK), col(COL_GV), prev(COL_GQ), prev(COL_GK), prev(COL_GV),
                  init(0), init(8), init(16), cw(0), cw(8), cw(16),
                  pl.BlockSpec((1, R, 128), lambda b, h, c: (b, c, 0))],
        out_specs=[o[0] for o in outs],
        out_shape=[o[1] for o in outs],
        scratch_shapes=[pltpu.VMEM((8 + R, 128), F32)],
        args=(proj, proj, proj, proj, proj, proj, conv_init, conv_init, conv_init, conv_w, conv_w, conv_w, gates))


def _cast_kernel(*refs, pair_tiles):
    for src, dst, tile in zip(refs[:len(refs) // 2], refs[len(refs) // 2:], pair_tiles):
        if tile is None:
            dst[...] = src[...].astype(dst.dtype)
        else:
            half = src.shape[1] // 2
            for t in range(half // tile):
                dst[:, 2 * t * tile:(2 * t + 1) * tile] = src[:, t * tile:(t + 1) * tile].astype(dst.dtype)
                dst[:, (2 * t + 1) * tile:(2 * t + 2) * tile] = (
                    src[:, half + t * tile:half + (t + 1) * tile].astype(dst.dtype))


def _cast_rows(weights, grid, pair_tiles):
    steps = 1
    for g in grid:
        steps *= g

    def slab(w):
        rows = w.shape[0] // steps
        assert rows * steps == w.shape[0] and rows % 16 == 0

        def index(*idx):
            flat = idx[0]
            for i, g in zip(idx[1:], grid[1:]):
                flat = flat * g + i
            return (flat, 0)

        return pl.BlockSpec((rows, w.shape[1]), index)

    return dict(body=functools.partial(_cast_kernel, pair_tiles=tuple(pair_tiles)), grid=grid,
                in_specs=[slab(w) for w in weights],
                out_specs=[slab(w) for w in weights],
                out_shape=[jax.ShapeDtypeStruct(w.shape, BF16) for w in weights], scratch_shapes=[],
                args=tuple(weights))


def _gdn_scan_init(u_ref, w_ref, qe_ref, kd_ref, attn_ref, glast_ref, gz_ref, gnorm_ref, s0_ref,
                   o_ref, s_out_ref, s_s):
    shared = s0_ref.shape[0] == 1

    @pl.when(pl.program_id(1) == 0)
    def _():
        for b in range(u_ref.shape[0]):
            for h in range(u_ref.shape[1]):
                s_s[b, h] = s0_ref[0 if shared else b, h]


def _gdn_scan_kernel(u_ref, w_ref, qe_ref, kd_ref, attn_ref, glast_ref, gz_ref, gnorm_ref, s0_ref,
                     o_ref, s_out_ref, s_s):
    nb, nh = u_ref.shape[0], u_ref.shape[1]
    L = attn_ref.shape[3]
    chains = [(b, h) for b in range(nb) for h in range(nh)]
    s_cur = [s_s[b, h] for b, h in chains]
    for j in range(attn_ref.shape[2]):
        sl = slice(j * L, (j + 1) * L)
        ws = [_dot(jnp.concatenate([w_ref[b, h, sl, :], qe_ref[b, h, sl, :]], axis=0), sp.astype(BF16))
              for (b, h), sp in zip(chains, s_cur)]
        vb = [(u_ref[b, h, sl, :] - r[0:L]).astype(BF16) for (b, h), r in zip(chains, ws)]
        intra = [_dot(attn_ref[b, h, j], v) for (b, h), v in zip(chains, vb)]
        upd = [_dot_tn(kd_ref[b, h, sl, :], v) for (b, h), v in zip(chains, vb)]
        s_cur = [glast_ref[b, h, j][:, 0:1] * sp + up for (b, h), sp, up in zip(chains, s_cur, upd)]
        for i, (b, h) in enumerate(chains):
            o = ws[i][L:2 * L] + intra[i]
            gz = gz_ref[h, b, sl, :]
            o_ref[h, b, sl, :] = (o * _rms_scale(o) * gnorm_ref[h] * (gz * _sigmoid(gz))).astype(o_ref.dtype)
    for (b, h), sp in zip(chains, s_cur):
        s_s[b, h] = sp
        s_out_ref[b, h] = sp


def _gdn_scan(prep, proj, gnorm, s0, *, chunk, heads, nsub):
    _, batch, seq, _ = proj.shape
    u, w, qe, kd, attn, glast = prep
    hb = heads
    assert COL_GZ % hb == 0 and N_HEADS % hb == 0
    tok = pl.BlockSpec((batch, hb, nsub * chunk, 128), lambda h, c: (0, h, c, 0))
    return dict(
        init=_gdn_scan_init,
        body=_gdn_scan_kernel,
        grid=(N_HEADS // hb, seq // (nsub * chunk)),
        in_specs=[tok, tok, tok, tok,
                  pl.BlockSpec((batch, hb, nsub, chunk, chunk), lambda h, c: (0, h, c, 0, 0)),
                  pl.BlockSpec((batch, hb, nsub, 1, 128), lambda h, c: (0, h, c, 0, 0)),
                  pl.BlockSpec((hb, batch, nsub * chunk, 128), lambda h, c: (COL_GZ // hb + h, 0, c, 0)),
                  pl.BlockSpec((hb, 1, 128), lambda h, c: (h, 0, 0)),
                  pl.BlockSpec((s0.shape[0], hb, 128, 128), lambda h, c: (0, h, 0, 0))],
        out_specs=[pl.BlockSpec((hb, batch, nsub * chunk, 128), lambda h, c: (h, 0, c, 0)),
                   pl.BlockSpec((batch, hb, 128, 128), lambda h, c: (0, h, 0, 0))],
        out_shape=[jax.ShapeDtypeStruct((N_HEADS, batch, seq, 128), BF16),
                   jax.ShapeDtypeStruct((batch, N_HEADS, 128, 128), F32)],
        scratch_shapes=[pltpu.VMEM((batch, hb, 128, 128), F32)],
        args=(u, w, qe, kd, attn, glast, proj, gnorm, s0))


def _sample_kernel(p_ref, gates_ref, c_ref, n_ref, m_ref, s_ref, cb_ref, cw_ref, mnorm_ref, gnorm_ref,
                   mix_ref, c_out_ref, n_out_ref, m_out_ref, s_out_ref, ext, xc_s, *, steps):
    T = steps
    NS = SAMPLE_SEQS
    R = NS * T
    H = N_HEADS
    nbuf = CONV_W - 1

    for s in range(NS):
        ext[8 * s:8 * s + nbuf, :] = cb_ref[0, s]
        for blk in range(COL_GZ - COL_GQ):
            ext[8 * s + nbuf:8 * s + nbuf + T, blk * 128:(blk + 1) * 128] = p_ref[COL_GQ + blk, 0, s * T:(s + 1) * T, :]
        acc = ext[8 * s:8 * s + T, :] * cw_ref[0:1, :]
        for i in range(1, CONV_W):
            acc = acc + ext[8 * s + i:8 * s + i + T, :] * cw_ref[i:i + 1, :]
        xc_s[s * T:(s + 1) * T, :] = acc * _sigmoid(acc)

    rowi = lax.broadcasted_iota(jnp.int32, (R, 1), 0)
    tpos = rowi & (T - 1)
    second = rowi >= T
    seq_rows = [jnp.logical_not(second), second]

    def pick(x, j):
        return jnp.where(second, x[T + j:T + j + 1, :], x[j:j + 1, :])

    def cumsum_t(x):
        acc = jnp.where(tpos >= 0, pick(x, 0), 0.0)
        for j in range(1, T):
            acc = acc + jnp.where(tpos >= j, pick(x, j), 0.0)
        return acc

    def pcol(block, h):
        return p_ref[block + h, 0]

    gates = gates_ref[0]
    lane = lax.broadcasted_iota(jnp.int32, (R, 128), 1)
    lf_all = lg_all = beta_all = gates
    m_in = m_ref[0]
    lane_s = lax.broadcasted_iota(jnp.int32, (NS, H), 1)
    row_s = lax.broadcasted_iota(jnp.int32, (NS, H), 0)
    m_acc = jnp.zeros((NS, H), F32)

    qc_all, gdn_in = [], []
    for h in range(H):
        qb = (pcol(COL_MQ, h) * Q_SCALE).astype(BF16)
        qc_all.append(jnp.where(second, _dot(qb, c_ref[0, 1, h].astype(BF16)),
                                _dot(qb, c_ref[0, 0, h].astype(BF16))))
        cq = xc_s[:, h * 128:(h + 1) * 128]
        ck = xc_s[:, WIDTH + h * 128:WIDTH + (h + 1) * 128]
        qn = cq * _l2_scale(cq) * Q_SCALE
        kn = ck * _l2_scale(ck)
        kq = jnp.concatenate([kn, qn], axis=0).astype(BF16)
        r0 = _dot(kq, s_ref[0, 0, h].astype(BF16))
        r1 = _dot(kq, s_ref[0, 1, h].astype(BF16))
        gdn_in.append((qn, kn, jnp.where(second, r1[0:R], r0[0:R]), jnp.where(second, r1[R:2 * R], r0[R:2 * R])))

    c_upd, s_upd = [], []
    for h in range(H):
        li = _select_lane(gates, lane, LANE_MI + h)
        lf = _select_lane(lf_all, lane, LANE_MF + h)
        mh = _select_lane(m_in, lane_s, h)
        m0 = jnp.where(second, mh[1:2, :], mh[0:1, :])
        b = cumsum_t(lf)
        a = b + m0
        dcol = [b - pick(b, j) + pick(li, j) for j in range(T)]
        m_t = a
        for j in range(T):
            m_t = jnp.maximum(m_t, jnp.where(tpos >= j, dcol[j], -jnp.inf))
        inter = jnp.exp(a - m_t)

        q = pcol(COL_MQ, h) * Q_SCALE
        k = pcol(COL_MK, h)
        v = pcol(COL_MV, h)
        num = jnp.zeros((R, 128), F32)
        den = jnp.zeros((R, 1), F32)
        for j in range(T):
            w_j = jnp.where(tpos >= j, jnp.exp(jnp.where(tpos >= j, dcol[j] - m_t, 0.0)), 0.0)
            s_j = jnp.sum(q * pick(k, j), axis=1, keepdims=True) * w_j
            num = num + s_j * pick(v, j)
            den = den + s_j
        qc = qc_all[h]
        n_sel =jnp.where(second, n_ref[0, 1, h:h + 1, :], n_ref[0, 0, h:h + 1, :])
        num = num + inter * qc
        den = den + inter * jnp.sum(q * n_sel, axis=1, keepdims=True)
        hh = num / jnp.maximum(jnp.abs(den), jnp.exp(-m_t))
        mix_ref[0, :, h * 128:(h + 1) * 128] = hh * _rms_scale(hh) * mnorm_ref[h:h + 1, :] * _sigmoid(pcol(COL_MO, h))

        m_l = pick(m_t, T - 1)
        b_l = pick(b, T - 1)
        kw = jnp.exp(b_l - b + li - m_l) * k
        d_c = jnp.exp(b_l + m0 - m_l)
        c_upd.append((kw, d_c, v.astype(BF16)))
        for s in range(NS):
            kws = jnp.where(seq_rows[s], kw, 0.0)
            n_out_ref[0, s, h:h + 1, :] = (d_c[s * T:s * T + 1, :] * n_ref[0, s, h:h + 1, :]
                                           + jnp.sum(kws, axis=0, keepdims=True))
            m_acc = jnp.where(jnp.logical_and(lane_s == h, row_s == s), m_l[s * T:s * T + 1, :], m_acc)

        lg = _select_lane(lg_all, lane, LANE_GA + h)
        beta = _select_lane(beta_all, lane, LANE_GB + h)
        g = cumsum_t(lg)
        e_g = jnp.exp(g)
        cv = xc_s[:, 2 * WIDTH + h * 128:2 * WIDTH + (h + 1) * 128]
        qn, kn, k_s, q_s = gdn_in[h]
        x = beta * (cv - e_g * k_s)
        rel = [jnp.exp(jnp.where(tpos >= j, g - pick(g, j), 0.0)) for j in range(T)]
        for j in range(T - 1):
            kk_j = jnp.sum(kn * pick(kn, j), axis=1, keepdims=True)
            a_j = jnp.where(tpos > j, beta * kk_j * rel[j], 0.0)
            x = x - a_j * pick(x, j)
        o = e_g * q_s
        for j in range(T):
            qk_j = jnp.sum(qn * pick(kn, j), axis=1, keepdims=True)
            o = o + jnp.where(tpos >= j, qk_j * rel[j], 0.0) * pick(x, j)
        gz = pcol(COL_GZ, h)
        mix_ref[0, :, WIDTH + h * 128:WIDTH + (h + 1) * 128] = (
            o * _rms_scale(o) * gnorm_ref[h:h + 1, :] * (gz * _sigmoid(gz)))

        g_l = pick(g, T - 1)
        s_upd.append((kn * jnp.exp(g_l - g), jnp.exp(g_l), x.astype(BF16)))
    m_out_ref[0] = m_acc

    for h in range(H):
        for (lhs, scale, rhs), st_ref, out_ref in ((c_upd[h], c_ref, c_out_ref), (s_upd[h], s_ref, s_out_ref)):
            for s in range(NS):
                rows_s = jnp.where(seq_rows[s], lhs, 0.0).astype(BF16)
                out_ref[0, s, h] = scale[s * T:s * T + 1, :] * st_ref[0, s, h] + _dot_tn(rows_s, rhs)


def _sample_step(proj, gates, c, n, m, s, conv_buf, conv_w, mnorm, gnorm, *, steps, grid):
    groups = c.shape[0]
    ns = SAMPLE_SEQS
    assert ns == 2 and ns * steps == 8 and steps >= CONV_W - 1 and steps & (steps - 1) == 0
    assert len(grid) == 2 and grid[0] * grid[1] == groups

    def per_group(*shape):
        return pl.BlockSpec((1,) + shape, lambda i, j: (i * grid[1] + j,) + (0,) * len(shape))

    def const(*shape):
        return pl.BlockSpec(shape, lambda i, j: (0,) * len(shape))

    state = (ns, N_HEADS, 128, 128)
    return dict(
        body=functools.partial(_sample_kernel, steps=steps),
        grid=grid,
        in_specs=[pl.BlockSpec((PROJ_COLS // 128, 1, ns * steps, 128), lambda i, j: (0, i * grid[1] + j, 0, 0)),
                  per_group(ns * steps, 128), per_group(*state), per_group(ns, N_HEADS, 128),
                  per_group(ns, N_HEADS), per_group(*state), per_group(ns, CONV_W - 1, 3 * WIDTH),
                  const(CONV_W, 3 * WIDTH), const(N_HEADS, 128), const(N_HEADS, 128)],
        out_specs=[per_group(ns * steps, 2 * WIDTH), per_group(*state), per_group(ns, N_HEADS, 128),
                   per_group(ns, N_HEADS), per_group(*state)],
        out_shape=[jax.ShapeDtypeStruct((groups, ns * steps, 2 * WIDTH), F32),
                   jax.ShapeDtypeStruct((groups,) + state, F32),
                   jax.ShapeDtypeStruct((groups, ns, N_HEADS, 128), F32),
                   jax.ShapeDtypeStruct((groups, ns, N_HEADS), F32),
                   jax.ShapeDtypeStruct((groups,) + state, F32)],
        scratch_shapes=[pltpu.VMEM((8 * ns, 3 * WIDTH), F32), pltpu.VMEM((ns * steps, 3 * WIDTH), F32)],
        args=(proj, gates, c, n, m, s, conv_buf, conv_w, mnorm, gnorm))


def kernel(x_prompt, x_sample, state_mlstm_C, state_mlstm_n, state_mlstm_m, state_gdn_S, state_gdn_conv, meta_tokens, norm_pre_mix, norm_post_mix, norm_pre_ffn, norm_post_ffn, w_in, mlstm_b_i, mlstm_b_f, mlstm_norm, gdn_conv_w, gdn_A_log, gdn_dt_bias, gdn_norm, w_out, w_gate_up, w_down):
    assert w_in.shape[0] == 1, "single-layer trunk"
    B, S, D = x_prompt.shape
    Bs, Ts, _ = x_sample.shape
    H = N_HEADS

    w_proj = _regroup_weights(jnp.transpose(w_in[0]))
    zeros_h = jnp.zeros((H,), F32)
    gbias = jnp.concatenate([mlstm_b_i[0], mlstm_b_f[0], gdn_dt_bias[0], zeros_h,
                             jnp.zeros((128 - 4 * H,), F32)]).reshape(1, 128)
    arow = jnp.concatenate([zeros_h, zeros_h, gdn_A_log[0], zeros_h,
                            jnp.zeros((128 - 4 * H,), F32)]).reshape(1, 128)
    g_pre_mix = norm_pre_mix[0].reshape(1, D)
    g_post_mix = norm_post_mix[0].reshape(1, D)
    g_pre_ffn = norm_pre_ffn[0].reshape(1, D)
    g_post_ffn = norm_post_ffn[0].reshape(1, D)
    mnorm = mlstm_norm[0]
    gnorm = gdn_norm[0]
    mnorm3 = mnorm.reshape(H, 1, 128)
    gnorm3 = gnorm.reshape(H, 1, 128)
    conv_w = gdn_conv_w[0]

    xp = x_prompt.reshape(B * S, D)
    xs = x_sample.reshape(Bs * Ts, D)

    nblk = PROJ_COLS // 128
    proj_p = _norm_matmul(xp, g_pre_mix, w_proj, tm=PROJ_TM, tn=PROJ_TN).reshape(nblk, B, S, 128)
    n_sm = Bs * Ts + N_META
    proj_sm = _norm_matmul(jnp.concatenate([xs, meta_tokens], axis=0), g_pre_mix, w_proj, tm=n_sm, tn=PROJ_TN)
    proj_m = proj_sm[:, Bs * Ts:].reshape(nblk, 1, N_META, 128)

    gates_p = _gate_act(proj_p.reshape(nblk, B * S, 128), gbias, arow, tm=PROJ_TM).reshape(B, S, 128)
    gates_sm = _gate_act(proj_sm, gbias, arow, tm=n_sm)
    gates_m = gates_sm[Bs * Ts:].reshape(1, N_META, 128)

    def conv_rows(blocks):
        return jnp.moveaxis(blocks, 0, -2).reshape(blocks.shape[1:-1] + (3 * WIDTH,))

    c0, n0, m0 = (jnp.zeros((1, H, 128, 128), F32), jnp.zeros((1, H, 1, 128), F32), jnp.zeros((1, H, 1, 128), F32))
    (_, c_l, n_l, m_l), = _run_parts([_mlstm_chunked(proj_m, gates_m, mnorm3, c0, n0, m0, chunk=N_META, heads=H)],
                                     semantics=("parallel", "arbitrary"), name="mlstm_lead")
    prep_l, = _run_parts([_gdn_prep(proj_m, gates_m, jnp.zeros((3 * H, 8, 128), F32), conv_w, chunk=N_META, nsub=1)],
                         semantics=("parallel", "parallel", "parallel"), name="gdn_prep_lead")
    (_, s_l), = _run_parts([_gdn_scan(prep_l, proj_m, gnorm3, c0, chunk=N_META, heads=N_HEADS, nsub=1)],
                           semantics=("parallel", "arbitrary"), name="gdn_scan_lead")

    conv_init = proj_m[COL_GQ:COL_GZ, 0, N_META - 8:, :]
    prep_parts = _gdn_prep(proj_p, gates_p, conv_init, conv_w, chunk=GDN_CHUNK, nsub=GDN_PREP_CHUNKS)
    prep_p, (w_out_b, w_gu_b, w_dn_b) = _run_parts(
        [prep_parts, _cast_rows([w_out[0], w_gate_up[0], w_down[0]], prep_parts["grid"], [None, FFN_TF, None])],
        semantics=("parallel", "parallel", "parallel"), name="gdn_prep_and_casts")

    ns = SAMPLE_SEQS
    grp = Bs // ns
    m_in = state_mlstm_m[0].reshape(grp, ns, H)
    mlstm_parts = _mlstm_chunked(proj_p, gates_p, mnorm3, c_l, n_l, m_l, chunk=MLSTM_CHUNK, heads=MIXER_HEADS)
    scan_parts = _gdn_scan(prep_p, proj_p, gnorm3, s_l, chunk=GDN_CHUNK, heads=MIXER_HEADS,
                           nsub=MLSTM_CHUNK // GDN_CHUNK)
    sample_parts = _sample_step(
        proj_sm.reshape(nblk, n_sm // (ns * Ts), ns * Ts, 128), gates_sm.reshape(n_sm // (ns * Ts), ns * Ts, 128),
        state_mlstm_C[0].reshape(grp, ns, H, 128, 128),
        state_mlstm_n[0].reshape(grp, ns, H, 128), m_in, state_gdn_S[0].reshape(grp, ns, H, 128, 128),
        state_gdn_conv[0].reshape(grp, ns, CONV_W - 1, 3 * WIDTH), conv_w, mnorm, gnorm, steps=Ts,
        grid=mlstm_parts["grid"])
    (mix_b, p_s), (mix_a, p_c, p_n, p_m), (mix_s, s_c, s_n, s_m, s_s) = _run_parts(
        [scan_parts, mlstm_parts, sample_parts], semantics=("parallel", "arbitrary"), name="scan_mlstm_sample")
    p_conv = conv_rows(proj_p[COL_GQ:COL_GZ, :, S - (CONV_W - 1):, :])

    x1_p, u2_p = _out_proj(mix_a.reshape(H, B * S, 128), 0, mix_b.reshape(H, B * S, 128), 0, w_out_b, xp,
                           g_post_mix, g_pre_ffn, tm=ROW_TM)
    y_p = _ffn(u2_p, w_gu_b, w_dn_b, x1_p, g_post_ffn, tm=ROW_TM, tf=FFN_TF)

    mix_s2 = mix_s.reshape(Bs * Ts, 2 * WIDTH)
    s_conv = conv_rows(proj_sm[COL_GQ:COL_GZ, :Bs * Ts].reshape(3 * H, Bs, Ts, 128)[:, :, Ts - (CONV_W - 1):, :])
    x1_s, u2_s = _out_proj(mix_s2, 0, mix_s2, 1, w_out_b, xs, g_post_mix, g_pre_ffn, tm=ROW_TM)
    y_s = _ffn(u2_s, w_gu_b, w_dn_b, x1_s, g_post_ffn, tm=ROW_TM, tf=FFN_TF)

    return (y_p.reshape(B, S, D), y_s.reshape(Bs, Ts, D),
            p_c[None], p_n[:, :, 0, :][None], p_m[:, :, 0, 0][None], p_s[None], p_conv[None],
            s_c.reshape(Bs, H, 128, 128)[None], s_n.reshape(Bs, H, 128)[None], s_m.reshape(Bs, H)[None],
            s_s.reshape(Bs, H, 128, 128)[None], s_conv[None])
```

```python
import functools

import jax
import jax.numpy as jnp
from jax import lax
from jax.experimental import pallas as pl
from jax.experimental.pallas import tpu as pltpu

F32 = jnp.float32
BF16 = jnp.bfloat16

HEAD_DIM = 128
N_HEADS = 8
WIDTH = N_HEADS * HEAD_DIM
CONV_W = 4
N_META = 16
D_FF = 5632
EPS = 1e-6
Q_SCALE = HEAD_DIM ** -0.5

COL_MQ, COL_MK, COL_MV, COL_MO = 0, 8, 16, 24
COL_GQ, COL_GK, COL_GV, COL_GZ = 32, 40, 48, 56
COL_GATES = 64
PROJ_COLS = 66 * 128
LANE_MI, LANE_MF, LANE_GA, LANE_GB = 0, 8, 16, 24

MLSTM_CHUNK = 256
MIXER_HEADS = 1
GDN_CHUNK = 64
GDN_PREP_CHUNKS = 32
SAMPLE_SEQS = 2

MXU_TILE = 256
VMEM_LIMIT = 56 * 1024 * 1024
PROJ_TM, PROJ_TN = 1024, 3 * MXU_TILE
ROW_TM = 512
FFN_TF = 2 * MXU_TILE


def _sigmoid(x):
    return 0.5 * jnp.tanh(0.5 * x) + 0.5


def _softplus(x):
    return jnp.maximum(x, 0.0) + jnp.log1p(jnp.exp(-jnp.abs(x)))


def _log_sigmoid(x):
    return -_softplus(-x)


def _rms_scale(x):
    return lax.rsqrt(jnp.mean(x * x, axis=-1, keepdims=True) + EPS)


def _l2_scale(x):
    return lax.rsqrt(jnp.sum(x * x, axis=-1, keepdims=True) + EPS)


def _dot(a, b):
    return jnp.dot(a, b, preferred_element_type=F32)


def _dot_nt(a, b):
    return lax.dot_general(a, b, (((1,), (1,)), ((), ())), preferred_element_type=F32)


def _dot_tn(a, b):
    return lax.dot_general(a, b, (((0,), (0,)), ((), ())), preferred_element_type=F32)


def _select_lane(x, lane_idx, target):
    return jnp.sum(jnp.where(lane_idx == target, x, 0.0), axis=1, keepdims=True)


REGROUP_TILE = 256


def _regroup_kernel(w_ref, mg_ref, gg_ref, o_ref):
    j = pl.program_id(0)
    wide_tiles = 2 * 4 * WIDTH // REGROUP_TILE

    @pl.when(j < wide_tiles)
    def _():
        o_ref[...] = w_ref[...].T.astype(BF16)

    @pl.when(j == wide_tiles)
    def _():
        pad = jnp.zeros((REGROUP_TILE - 4 * N_HEADS, w_ref.shape[1]), F32)
        o_ref[...] = jnp.concatenate([mg_ref[...], gg_ref[...], pad], axis=0).T.astype(BF16)


def _regroup_weights(w_t):
    n, d = w_t.shape
    wide = 4 * WIDTH
    narrow = 2 * N_HEADS
    assert n == 2 * (wide + narrow) and PROJ_COLS == 2 * wide + REGROUP_TILE
    wide_tiles = 2 * wide // REGROUP_TILE

    def src_row(j):
        j = jnp.minimum(j, wide_tiles - 1)
        return (j * (REGROUP_TILE // narrow) + jnp.where(j >= wide_tiles // 2, 1, 0)) * narrow

    return pl.pallas_call(
        _regroup_kernel,
        grid=(wide_tiles + 1,),
        in_specs=[pl.BlockSpec((pl.Element(REGROUP_TILE), pl.Element(d)), lambda j: (src_row(j), 0)),
                  pl.BlockSpec((narrow, d), lambda j: (wide // narrow, 0)),
                  pl.BlockSpec((narrow, d), lambda j: ((2 * wide + narrow) // narrow, 0))],
        out_specs=pl.BlockSpec((d, REGROUP_TILE), lambda j: (0, j)),
        out_shape=jax.ShapeDtypeStruct((d, PROJ_COLS), BF16),
        compiler_params=pltpu.CompilerParams(dimension_semantics=("parallel",), vmem_limit_bytes=VMEM_LIMIT),
        name="regroup_weights",
    )(w_t, w_t, w_t)


def _norm_matmul_kernel(x_ref, g_ref, w_ref, o_ref, u_ref):
    @pl.when(pl.program_id(1) == 0)
    def _():
        x = x_ref[...]
        u_ref[...] = (x * _rms_scale(x) * g_ref[...]).astype(BF16)

    res = _dot(u_ref[...], w_ref[...])
    for blk in range(o_ref.shape[0]):
        o_ref[blk] = res[:, blk * 128:(blk + 1) * 128]


def _norm_matmul(x, gain, w, *, tm, tn):
    m, d = x.shape
    n = w.shape[1]
    return pl.pallas_call(
        _norm_matmul_kernel,
        grid=(m // tm, n // tn),
        in_specs=[pl.BlockSpec((tm, d), lambda i, j: (i, 0)),
                  pl.BlockSpec((1, d), lambda i, j: (0, 0)),
                  pl.BlockSpec((d, tn), lambda i, j: (0, j))],
        out_specs=pl.BlockSpec((tn // 128, tm, 128), lambda i, j: (j, i, 0)),
        out_shape=jax.ShapeDtypeStruct((n // 128, m, 128), F32),
        scratch_shapes=[pltpu.VMEM((tm, d), BF16)],
        compiler_params=pltpu.CompilerParams(dimension_semantics=("parallel", "arbitrary"),
                                             vmem_limit_bytes=VMEM_LIMIT),
        name="norm_matmul",
    )(x, gain, w)


def _out_proj_kernel(ma_ref, mb_ref, wa_ref, wb_ref, x_ref, gpost_ref, gpre_ref, x1_ref, u2_ref):
    def rows(ref, sl):
        if len(ref.shape) == 2:
            return ref[sl, :].astype(BF16)
        return jnp.concatenate([ref[h, sl, :] for h in range(ref.shape[0])], axis=1).astype(BF16)

    tq = x_ref.shape[0] // 4
    quarters = [slice(i * tq, (i + 1) * tq) for i in range(4)]
    mos = [_dot(rows(ma_ref, sl), wa_ref[...]) + _dot(rows(mb_ref, sl), wb_ref[...]) for sl in quarters]
    for sl, mo in zip(quarters, mos):
        x1 = x_ref[sl, :] + mo * _rms_scale(mo) * gpost_ref[...]
        x1_ref[sl, :] = x1
        u2_ref[sl, :] = (x1 * _rms_scale(x1) * gpre_ref[...]).astype(BF16)


def _out_proj(mix_a, col_a, mix_b, col_b, w_out, x, g_post, g_pre, *, tm):
    m, d = x.shape

    def mix_spec(mix, col):
        if mix.ndim == 2:
            return pl.BlockSpec((tm, WIDTH), lambda i: (i, col))
        return pl.BlockSpec((N_HEADS, tm, 128), lambda i: (0, i, 0))

    return pl.pallas_call(
        _out_proj_kernel,
        grid=(m // tm,),
        in_specs=[mix_spec(mix_a, col_a),
                  mix_spec(mix_b, col_b),
                  pl.BlockSpec((WIDTH, d), lambda i: (0, 0)),
                  pl.BlockSpec((WIDTH, d), lambda i: (1, 0)),
                  pl.BlockSpec((tm, d), lambda i: (i, 0)),
                  pl.BlockSpec((1, d), lambda i: (0, 0)),
                  pl.BlockSpec((1, d), lambda i: (0, 0))],
        out_specs=[pl.BlockSpec((tm, d), lambda i: (i, 0)),
                   pl.BlockSpec((tm, d), lambda i: (i, 0))],
        out_shape=[jax.ShapeDtypeStruct((m, d), F32), jax.ShapeDtypeStruct((m, d), BF16)],
        compiler_params=pltpu.CompilerParams(dimension_semantics=("parallel",),
                                             vmem_limit_bytes=VMEM_LIMIT),
        name="out_proj",
    )(mix_a, mix_b, w_out, w_out, x, g_post, g_pre)


def _ffn_kernel(u_ref, wgu_ref, wd_ref, x1_ref, gain_ref, o_ref, acc_ref):
    f = pl.program_id(1)
    tf = wd_ref.shape[0]

    @pl.when(f == 0)
    def _():
        acc_ref[...] = jnp.zeros_like(acc_ref)

    def hidden_tile():
        gu = _dot(u_ref[...], wgu_ref[...])
        g, up = gu[:, :tf], gu[:, tf:]
        return (g * _sigmoid(g) * up).astype(BF16)

    last = f == pl.num_programs(1) - 1

    @pl.when(jnp.logical_not(last))
    def _():
        acc_ref[...] += _dot(hidden_tile(), wd_ref[...])

    @pl.when(last)
    def _():
        hidden = hidden_tile()
        tm = acc_ref.shape[0]
        halves = [slice(0, tm // 2), slice(tm // 2, tm)]
        for sl in halves:
            acc_ref[sl, :] += _dot(hidden[sl, :], wd_ref[...])
        for sl in halves:
            y = acc_ref[sl, :]
            o_ref[sl, :] = x1_ref[sl, :] + y * _rms_scale(y) * gain_ref[...]


def _ffn(u2, w_gate_up, w_down, x1, gain, *, tm, tf):
    m, d = x1.shape
    nf = D_FF // tf
    return pl.pallas_call(
        _ffn_kernel,
        grid=(m // tm, nf),
        in_specs=[pl.BlockSpec((tm, d), lambda i, f: (i, 0)),
                  pl.BlockSpec((d, 2 * tf), lambda i, f: (0, f)),
                  pl.BlockSpec((tf, d), lambda i, f: (f, 0)),
                  pl.BlockSpec((tm, d), lambda i, f: (i, 0)),
                  pl.BlockSpec((1, d), lambda i, f: (0, 0))],
        out_specs=pl.BlockSpec((tm, d), lambda i, f: (i, 0)),
        out_shape=jax.ShapeDtypeStruct((m, d), F32),
        scratch_shapes=[pltpu.VMEM((tm, d), F32)],
        compiler_params=pltpu.CompilerParams(dimension_semantics=("parallel", "arbitrary"),
                                             vmem_limit_bytes=VMEM_LIMIT),
        name="ffn",
    )(u2, w_gate_up, w_down, x1, gain)


def _gate_act_kernel(g_ref, gbias_ref, arow_ref, o_ref):
    g = g_ref[0] + gbias_ref[...]
    lane = lax.broadcasted_iota(jnp.int32, g.shape, 1)
    decay = -jnp.exp(arow_ref[...]) * _softplus(g)
    o_ref[...] = jnp.where(lane < LANE_MF, g,
                           jnp.where(lane < LANE_GA, _log_sigmoid(g), jnp.where(lane < LANE_GB, decay, _sigmoid(g))))


def _gate_act(proj, gbias, arow, *, tm):
    m = proj.shape[1]
    return pl.pallas_call(
        _gate_act_kernel,
        grid=(m // tm,),
        in_specs=[pl.BlockSpec((1, tm, 128), lambda i: (COL_GATES, i, 0)),
                  pl.BlockSpec((1, 128), lambda i: (0, 0)),
                  pl.BlockSpec((1, 128), lambda i: (0, 0))],
        out_specs=pl.BlockSpec((tm, 128), lambda i: (i, 0)),
        out_shape=jax.ShapeDtypeStruct((m, 128), F32),
        compiler_params=pltpu.CompilerParams(dimension_semantics=("parallel",), vmem_limit_bytes=VMEM_LIMIT),
        name="gate_act",
    )(proj, gbias, arow)


def _mlstm_init(q_ref, k_ref, v_ref, og_ref, gates_ref, mnorm_ref, c0_ref, n0_ref, m0_ref,
                h_ref, c_out_ref, n_out_ref, m_out_ref, c_s, n_s, m_s):
    shared = c0_ref.shape[0] == 1

    @pl.when(pl.program_id(1) == 0)
    def _():
        for hh in range(q_ref.shape[0]):
            for b in range(q_ref.shape[1]):
                b0 = 0 if shared else b
                c_s[b, hh] = c0_ref[b0, hh]
                n_s[b, hh] = n0_ref[b0, hh]
                m_s[b, hh] = m0_ref[b0, hh]


def _mlstm_chunk_kernel(q_ref, k_ref, v_ref, og_ref, gates_ref, mnorm_ref, c0_ref, n0_ref, m0_ref,
                        h_ref, c_out_ref, n_out_ref, m_out_ref, c_s, n_s, m_s, *, chunk):
    L = chunk
    nh, nb = q_ref.shape[0], q_ref.shape[1]
    chains = [(hh, b) for hh in range(nh) for b in range(nb)]
    heads = [pl.program_id(0) * nh + hh for hh, _ in chains]

    lane = lax.broadcasted_iota(jnp.int32, (L, 128), 1)
    rows = lax.broadcasted_iota(jnp.int32, (L, L), 0)
    cols = lax.broadcasted_iota(jnp.int32, (L, L), 1)
    tril = rows >= cols
    triu = rows <= cols
    eye = rows == cols

    seqs = range(len(chains))
    gates = [gates_ref[b] for _, b in chains]
    li_col = [_select_lane(g, lane, hd + LANE_MI) for g, hd in zip(gates, heads)]
    lf_col = [_select_lane(g, lane, hd + LANE_MF) for g, hd in zip(gates, heads)]
    lf_row = [jnp.sum(jnp.where(eye, x, 0.0), axis=0, keepdims=True) for x in lf_col]
    li_row = [jnp.sum(jnp.where(eye, x, 0.0), axis=0, keepdims=True) for x in li_col]
    b_col = [jnp.sum(jnp.where(tril, x, 0.0), axis=1, keepdims=True) for x in lf_row]
    b_row = [jnp.sum(jnp.where(triu, x, 0.0), axis=0, keepdims=True) for x in lf_col]

    qb = [(q_ref[hh, b] * Q_SCALE).astype(BF16) for hh, b in chains]
    kb = [k_ref[hh, b].astype(BF16) for hh, b in chains]
    vb = [v_ref[hh, b].astype(BF16) for hh, b in chains]
    qk = [_dot_nt(x, y) for x, y in zip(qb, kb)]
    qc = [_dot(x, c_s[b, hh].astype(BF16)) for (hh, b), x in zip(chains, qb)]

    m_prev = [m_s[b, hh, :, 0:1] for hh, b in chains]
    a_col = [x + y for x, y in zip(b_col, m_prev)]
    dm = [jnp.where(tril, bc - br + lr, -jnp.inf) for bc, br, lr in zip(b_col, b_row, li_row)]
    m_t = [jnp.maximum(a, jnp.max(d, axis=1, keepdims=True)) for a, d in zip(a_col, dm)]
    inter = [jnp.exp(a - m) for a, m in zip(a_col, m_t)]
    s = [x * jnp.exp(d - m) for x, d, m in zip(qk, dm, m_t)]
    sv = [_dot(x.astype(BF16), v) for x, v in zip(s, vb)]

    m_last = [m[L - 1:L, :] for m in m_t]
    b_last = [x[L - 1:L, :] for x in b_col]
    d_c = [jnp.exp(bl + mp - ml) for bl, mp, ml in zip(b_last, m_prev, m_last)]
    kw = [jnp.exp(bl - bc + li - ml) * k_ref[hh, b]
          for (hh, b), bl, bc, li, ml in zip(chains, b_last, b_col, li_col, m_last)]
    c_upd = [_dot_tn(x.astype(BF16), v) for x, v in zip(kw, vb)]

    for i, (hh, b) in enumerate(chains):
        q = q_ref[hh, b] * Q_SCALE
        num = sv[i] + inter[i] * qc[i]
        den = (jnp.sum(s[i], axis=1, keepdims=True)
               + inter[i] * jnp.sum(q * n_s[b, hh], axis=1, keepdims=True))
        hid = num / jnp.maximum(jnp.abs(den), jnp.exp(-m_t[i]))
        out = hid * _rms_scale(hid) * mnorm_ref[hh] * _sigmoid(og_ref[hh, b])
        h_ref[hh, b] = out.astype(h_ref.dtype)
        c_new = d_c[i] * c_s[b, hh] + c_upd[i]
        n_new = d_c[i] * n_s[b, hh] + jnp.sum(kw[i], axis=0, keepdims=True)
        m_new = jnp.broadcast_to(m_last[i], (1, 128))
        c_s[b, hh], n_s[b, hh], m_s[b, hh] = c_new, n_new, m_new
        c_out_ref[b, hh], n_out_ref[b, hh], m_out_ref[b, hh] = c_new, n_new, m_new


def _mlstm_chunked(proj, gates, mnorm, c0, n0, m0, *, chunk, heads):
    _, batch, seq, _ = proj.shape
    nb0 = c0.shape[0]
    hb = heads
    assert N_HEADS % hb == 0 and all(base % hb == 0 for base in (COL_MQ, COL_MK, COL_MV, COL_MO))

    def col(base):
        return pl.BlockSpec((hb, batch, chunk, 128), lambda h, c: (base // hb + h, 0, c, 0))

    def state(nb, shape):
        return pl.BlockSpec((nb, hb) + shape, lambda h, c: (0, h, 0, 0))

    return dict(
        init=_mlstm_init,
        body=functools.partial(_mlstm_chunk_kernel, chunk=chunk),
        grid=(N_HEADS // hb, seq // chunk),
        in_specs=[col(COL_MQ), col(COL_MK), col(COL_MV), col(COL_MO),
                  pl.BlockSpec((batch, chunk, 128), lambda h, c: (0, c, 0)),
                  pl.BlockSpec((hb, 1, 128), lambda h, c: (h, 0, 0)),
                  state(nb0, (128, 128)), state(nb0, (1, 128)), state(nb0, (1, 128))],
        out_specs=[pl.BlockSpec((hb, batch, chunk, 128), lambda h, c: (h, 0, c, 0)),
                   state(batch, (128, 128)), state(batch, (1, 128)), state(batch, (1, 128))],
        out_shape=[jax.ShapeDtypeStruct((N_HEADS, batch, seq, 128), BF16),
                   jax.ShapeDtypeStruct((batch, N_HEADS, 128, 128), F32),
                   jax.ShapeDtypeStruct((batch, N_HEADS, 1, 128), F32),
                   jax.ShapeDtypeStruct((batch, N_HEADS, 1, 128), F32)],
        scratch_shapes=[pltpu.VMEM((batch, hb, 128, 128), F32), pltpu.VMEM((batch, hb, 1, 128), F32),
                        pltpu.VMEM((batch, hb, 1, 128), F32)],
        args=(proj, proj, proj, proj, gates, mnorm, c0, n0, m0))


def _run_parts(parts, *, semantics, name):
    grid = parts[0]["grid"]
    assert all(p["grid"] == grid for p in parts)
    n_in = [len(p["in_specs"]) for p in parts]
    n_out = [len(p["out_specs"]) for p in parts]
    n_scr = [len(p["scratch_shapes"]) for p in parts]

    def body(*refs):
        ins, outs, scr = refs[:sum(n_in)], refs[sum(n_in):sum(n_in) + sum(n_out)], refs[sum(n_in) + sum(n_out):]
        part_refs = [ins[sum(n_in[:k]):sum(n_in[:k + 1])] + outs[sum(n_out[:k]):sum(n_out[:k + 1])]
                     + scr[sum(n_scr[:k]):sum(n_scr[:k + 1])] for k in range(len(parts))]
        for p, r in zip(parts, part_refs):
            if "init" in p:
                p["init"](*r)
        for p, r in zip(parts, part_refs):
            p["body"](*r)

    outs = pl.pallas_call(
        body,
        grid=grid,
        in_specs=[s for p in parts for s in p["in_specs"]],
        out_specs=[s for p in parts for s in p["out_specs"]],
        out_shape=[s for p in parts for s in p["out_shape"]],
        scratch_shapes=[s for p in parts for s in p["scratch_shapes"]],
        compiler_params=pltpu.CompilerParams(dimension_semantics=semantics, vmem_limit_bytes=VMEM_LIMIT),
        name=name,
    )(*[a for p in parts for a in p["args"]])
    return [outs[sum(n_out[:k]):sum(n_out[:k + 1])] for k in range(len(parts))]


def _unit_lower_inverse(mats, rows, cols, size):
    def same_block(shift):
        return (rows >> shift) == (cols >> shift)

    ident = jnp.where(rows == cols, 1.0, 0.0)
    d = [jnp.where(same_block(3), a, 0.0) for a in mats]
    d_b = [x.astype(BF16) for x in d]
    d2_b = [_dot(x, x).astype(BF16) for x in d_b]
    d4_b = [_dot(x, x).astype(BF16) for x in d2_b]
    t = [ident - x for x in d]
    t = [x + _dot(x.astype(BF16), y) for x, y in zip(t, d2_b)]
    t = [x + _dot(x.astype(BF16), y) for x, y in zip(t, d4_b)]
    a_hi = [x.astype(BF16) for x in mats]
    shift = 3
    while (1 << shift) < size:
        off = jnp.logical_and(same_block(shift + 1), jnp.logical_not(same_block(shift)))
        off_bf = jnp.where(off, 1.0, 0.0).astype(BF16)
        t_b = [x.astype(BF16) for x in t]
        te_b = [_dot(x, y * off_bf).astype(BF16) for x, y in zip(t_b, a_hi)]
        t = [x - _dot(y, z) for x, y, z in zip(t, te_b, t_b)]
        shift += 1
    return t


def _gdn_prep_kernel(q_ref, k_ref, v_ref, pq_ref, pk_ref, pv_ref, iq_ref, ik_ref, iv_ref, wq_ref, wk_ref, wv_ref,
                     gates_ref,
                     u_ref, w_ref, qe_ref, kd_ref, attn_ref, glast_ref, ext, *, chunk, nsub):
    L = chunk
    R = nsub * L
    head = pl.program_id(1)
    first = pl.program_id(2) == 0

    def conv(x_ref, prev_ref, init_ref, cw_ref):
        prev = jnp.where(first, init_ref[0], prev_ref[0, 0])
        ext[0:8, :] = prev
        ext[8:8 + R, :] = x_ref[0, 0]
        base = 8 - (CONV_W - 1)
        acc = ext[base:base + R, :] * cw_ref[0:1, :]
        for i in range(1, CONV_W):
            acc = acc + ext[base + i:base + i + R, :] * cw_ref[i:i + 1, :]
        return acc * _sigmoid(acc)

    q = conv(q_ref, pq_ref, iq_ref, wq_ref)
    k = conv(k_ref, pk_ref, ik_ref, wk_ref)
    v_all = conv(v_ref, pv_ref, iv_ref, wv_ref)
    qn_all = q * _l2_scale(q) * Q_SCALE
    kn_all = k * _l2_scale(k)

    gates = gates_ref[0]
    lane = lax.broadcasted_iota(jnp.int32, (R, 128), 1)
    lg_all = _select_lane(gates, lane, head + LANE_GA)
    beta_all = _select_lane(gates, lane, head + LANE_GB)

    U = min(R, 128)
    per_unit = U // L
    rows = lax.broadcasted_iota(jnp.int32, (U, U), 0)
    cols = lax.broadcasted_iota(jnp.int32, (U, U), 1)
    shift = L.bit_length() - 1
    same_chunk = (rows >> shift) == (cols >> shift)
    causal = jnp.logical_and(rows >= cols, same_chunk)
    strict = jnp.logical_and(rows > cols, same_chunk)
    upper = jnp.logical_and(rows <= cols, same_chunk)
    eye = rows == cols
    row_id = lax.broadcasted_iota(jnp.int32, (U, 1), 0)

    units = range(R // U)
    sls = [slice(j * U, (j + 1) * U) for j in units]
    g_cols, decays, a_mats, qks = [], [], [], []
    for sl in sls:
        lg_col = lg_all[sl]
        lg_row = jnp.sum(jnp.where(eye, lg_col, 0.0), axis=0, keepdims=True)
        g_col = jnp.sum(jnp.where(causal, lg_row, 0.0), axis=1, keepdims=True)
        g_row = jnp.sum(jnp.where(upper, lg_col, 0.0), axis=0, keepdims=True)
        decay = jnp.where(causal, jnp.exp(jnp.where(causal, g_col - g_row, 0.0)), 0.0)
        kb = kn_all[sl].astype(BF16)
        kk = _dot_nt(kb, kb)
        qks.append(_dot_nt(qn_all[sl].astype(BF16), kb))
        a_mats.append(jnp.where(strict, beta_all[sl] * kk * decay, 0.0))
        g_cols.append(g_col)
        decays.append(decay)

    t_mats = _unit_lower_inverse(a_mats, rows, cols, L)

    e_gs = [jnp.exp(g) for g in g_cols]
    rhs_s = [jnp.concatenate([beta_all[sl] * v_all[sl], (beta_all[sl] * e_g) * kn_all[sl]], axis=1).astype(BF16)
             for sl, e_g in zip(sls, e_gs)]
    uws = [_dot(t.astype(BF16), r) for t, r in zip(t_mats, rhs_s)]

    for j in units:
        sl, g_col = sls[j], g_cols[j]
        u_ref[0, 0, sl, :] = uws[j][:, 0:128]
        w_ref[0, 0, sl, :] = uws[j][:, 128:256].astype(BF16)
        qe_ref[0, 0, sl, :] = (qn_all[sl] * e_gs[j]).astype(BF16)
        attn = qks[j] * decays[j]
        g_last_col = jnp.zeros((U, 1), F32)
        for i in range(per_unit):
            g_last = g_col[(i + 1) * L - 1:(i + 1) * L, :]
            g_last_col = jnp.where((row_id >> shift) == i, g_last, g_last_col)
            glast_ref[0, 0, j * per_unit + i] = jnp.broadcast_to(jnp.exp(g_last), (1, 128))
            diag = attn[i * L:(i + 1) * L, :]
            if i > 0:
                diag = pltpu.roll(diag, U - i * L, axis=1)
            attn_ref[0, 0, j * per_unit + i] = diag[:, 0:L].astype(BF16)
        kd_ref[0, 0, sl, :] = (kn_all[sl] * jnp.exp(g_last_col - g_col)).astype(BF16)


def _gdn_prep(proj, gates, conv_init, conv_w, *, chunk, nsub):
    _, batch, seq, _ = proj.shape
    nc = seq // chunk
    R = nsub * chunk

    def col(base):
        return pl.BlockSpec((1, 1, R, 128), lambda b, h, c: (base + h, b, c, 0))

    def prev(base):
        return pl.BlockSpec((1, 1, 8, 128), lambda b, h, c: (base + h, b, jnp.maximum(c * (R // 8) - 1, 0), 0))

    def init(base):
        return pl.BlockSpec((1, 8, 128), lambda b, h, c: (base + h, 0, 0))

    def cw(base):
        return pl.BlockSpec((CONV_W, 128), lambda b, h, c: (0, base + h))

    def per_token(dtype):
        return (pl.BlockSpec((1, 1, R, 128), lambda b, h, c: (b, h, c, 0)),
                jax.ShapeDtypeStruct((batch, N_HEADS, seq, 128), dtype))

    outs = [per_token(F32), per_token(BF16), per_token(BF16), per_token(BF16),
            (pl.BlockSpec((1, 1, nsub, chunk, chunk), lambda b, h, c: (b, h, c, 0, 0)),
             jax.ShapeDtypeStruct((batch, N_HEADS, nc, chunk, chunk), BF16)),
            (pl.BlockSpec((1, 1, nsub, 1, 128), lambda b, h, c: (b, h, c, 0, 0)),
             jax.ShapeDtypeStruct((batch, N_HEADS, nc, 1, 128), F32))]
    return dict(
        body=functools.partial(_gdn_prep_kernel, chunk=chunk, nsub=nsub),
        grid=(batch, N_HEADS, nc // nsub),
        in_specs=[col(COL_GQ), col(COL_GK), col(COL_GV), prev(COL_GQ), prev(COL_GK), prev(COL_GV),
                  init(0), init(8), init(16), cw(0), cw(8), cw(16),
                  pl.BlockSpec((1, R, 128), lambda b, h, c: (b, c, 0))],
        out_specs=[o[0] for o in outs],
        out_shape=[o[1] for o in outs],
        scratch_shapes=[pltpu.VMEM((8 + R, 128), F32)],
        args=(proj, proj, proj, proj, proj, proj, conv_init, conv_init, conv_init, conv_w, conv_w, conv_w, gates))


def _cast_kernel(*refs, pair_tiles):
    for src, dst, tile in zip(refs[:len(refs) // 2], refs[len(refs) // 2:], pair_tiles):
        if tile is None:
            dst[...] = src[...].astype(dst.dtype)
        else:
            half = src.shape[1] // 2
            for t in range(half // tile):
                dst[:, 2 * t * tile:(2 * t + 1) * tile] = src[:, t * tile:(t + 1) * tile].astype(dst.dtype)
                dst[:, (2 * t + 1) * tile:(2 * t + 2) * tile] = (
                    src[:, half + t * tile:half + (t + 1) * tile].astype(dst.dtype))


def _cast_rows(weights, grid, pair_tiles):
    steps = 1
    for g in grid:
        steps *= g

    def slab(w):
        rows = w.shape[0] // steps
        assert rows * steps == w.shape[0] and rows % 16 == 0

        def index(*idx):
            flat = idx[0]
            for i, g in zip(idx[1:], grid[1:]):
                flat = flat * g + i
            return (flat, 0)

        return pl.BlockSpec((rows, w.shape[1]), index)

    return dict(body=functools.partial(_cast_kernel, pair_tiles=tuple(pair_tiles)), grid=grid,
                in_specs=[slab(w) for w in weights],
                out_specs=[slab(w) for w in weights],
                out_shape=[jax.ShapeDtypeStruct(w.shape, BF16) for w in weights], scratch_shapes=[],
                args=tuple(weights))


def _gdn_scan_init(u_ref, w_ref, qe_ref, kd_ref, attn_ref, glast_ref, gz_ref, gnorm_ref, s0_ref,
                   o_ref, s_out_ref, s_s):
    shared = s0_ref.shape[0] == 1

    @pl.when(pl.program_id(1) == 0)
    def _():
        for b in range(u_ref.shape[0]):
            for h in range(u_ref.shape[1]):
                s_s[b, h] = s0_ref[0 if shared else b, h]


def _gdn_scan_kernel(u_ref, w_ref, qe_ref, kd_ref, attn_ref, glast_ref, gz_ref, gnorm_ref, s0_ref,
                     o_ref, s_out_ref, s_s):
    nb, nh = u_ref.shape[0], u_ref.shape[1]
    L = attn_ref.shape[3]
    chains = [(b, h) for b in range(nb) for h in range(nh)]
    s_cur = [s_s[b, h] for b, h in chains]
    for j in range(attn_ref.shape[2]):
        sl = slice(j * L, (j + 1) * L)
        ws = [_dot(jnp.concatenate([w_ref[b, h, sl, :], qe_ref[b, h, sl, :]], axis=0), sp.astype(BF16))
              for (b, h), sp in zip(chains, s_cur)]
        vb = [(u_ref[b, h, sl, :] - r[0:L]).astype(BF16) for (b, h), r in zip(chains, ws)]
        intra = [_dot(attn_ref[b, h, j], v) for (b, h), v in zip(chains, vb)]
        upd = [_dot_tn(kd_ref[b, h, sl, :], v) for (b, h), v in zip(chains, vb)]
        s_cur = [glast_ref[b, h, j][:, 0:1] * sp + up for (b, h), sp, up in zip(chains, s_cur, upd)]
        for i, (b, h) in enumerate(chains):
            o = ws[i][L:2 * L] + intra[i]
            gz = gz_ref[h, b, sl, :]
            o_ref[h, b, sl, :] = (o * _rms_scale(o) * gnorm_ref[h] * (gz * _sigmoid(gz))).astype(o_ref.dtype)
    for (b, h), sp in zip(chains, s_cur):
        s_s[b, h] = sp
        s_out_ref[b, h] = sp


def _gdn_scan(prep, proj, gnorm, s0, *, chunk, heads, nsub):
    _, batch, seq, _ = proj.shape
    u, w, qe, kd, attn, glast = prep
    hb = heads
    assert COL_GZ % hb == 0 and N_HEADS % hb == 0
    tok = pl.BlockSpec((batch, hb, nsub * chunk, 128), lambda h, c: (0, h, c, 0))
    return dict(
        init=_gdn_scan_init,
        body=_gdn_scan_kernel,
        grid=(N_HEADS // hb, seq // (nsub * chunk)),
        in_specs=[tok, tok, tok, tok,
                  pl.BlockSpec((batch, hb, nsub, chunk, chunk), lambda h, c: (0, h, c, 0, 0)),
                  pl.BlockSpec((batch, hb, nsub, 1, 128), lambda h, c: (0, h, c, 0, 0)),
                  pl.BlockSpec((hb, batch, nsub * chunk, 128), lambda h, c: (COL_GZ // hb + h, 0, c, 0)),
                  pl.BlockSpec((hb, 1, 128), lambda h, c: (h, 0, 0)),
                  pl.BlockSpec((s0.shape[0], hb, 128, 128), lambda h, c: (0, h, 0, 0))],
        out_specs=[pl.BlockSpec((hb, batch, nsub * chunk, 128), lambda h, c: (h, 0, c, 0)),
                   pl.BlockSpec((batch, hb, 128, 128), lambda h, c: (0, h, 0, 0))],
        out_shape=[jax.ShapeDtypeStruct((N_HEADS, batch, seq, 128), BF16),
                   jax.ShapeDtypeStruct((batch, N_HEADS, 128, 128), F32)],
        scratch_shapes=[pltpu.VMEM((batch, hb, 128, 128), F32)],
        args=(u, w, qe, kd, attn, glast, proj, gnorm, s0))


def _sample_kernel(p_ref, gates_ref, c_ref, n_ref, m_ref, s_ref, cb_ref, cw_ref, mnorm_ref, gnorm_ref,
                   mix_ref, c_out_ref, n_out_ref, m_out_ref, s_out_ref, ext, xc_s, *, steps):
    T = steps
    NS = SAMPLE_SEQS
    R = NS * T
    H = N_HEADS
    nbuf = CONV_W - 1

    for s in range(NS):
        ext[8 * s:8 * s + nbuf, :] = cb_ref[0, s]
        for blk in range(COL_GZ - COL_GQ):
            ext[8 * s + nbuf:8 * s + nbuf + T, blk * 128:(blk + 1) * 128] = p_ref[COL_GQ + blk, 0, s * T:(s + 1) * T, :]
        acc = ext[8 * s:8 * s + T, :] * cw_ref[0:1, :]
        for i in range(1, CONV_W):
            acc = acc + ext[8 * s + i:8 * s + i + T, :] * cw_ref[i:i + 1, :]
        xc_s[s * T:(s + 1) * T, :] = acc * _sigmoid(acc)

    rowi = lax.broadcasted_iota(jnp.int32, (R, 1), 0)
    tpos = rowi & (T - 1)
    second = rowi >= T
    seq_rows = [jnp.logical_not(second), second]

    def pick(x, j):
        return jnp.where(second, x[T + j:T + j + 1, :], x[j:j + 1, :])

    def cumsum_t(x):
        acc = jnp.where(tpos >= 0, pick(x, 0), 0.0)
        for j in range(1, T):
            acc = acc + jnp.where(tpos >= j, pick(x, j), 0.0)
        return acc

    def pcol(block, h):
        return p_ref[block + h, 0]

    gates = gates_ref[0]
    lane = lax.broadcasted_iota(jnp.int32, (R, 128), 1)
    lf_all = lg_all = beta_all = gates
    m_in = m_ref[0]
    lane_s = lax.broadcasted_iota(jnp.int32, (NS, H), 1)
    row_s = lax.broadcasted_iota(jnp.int32, (NS, H), 0)
    m_acc = jnp.zeros((NS, H), F32)

    qc_all, gdn_in = [], []
    for h in range(H):
        qb = (pcol(COL_MQ, h) * Q_SCALE).astype(BF16)
        qc_all.append(jnp.where(second, _dot(qb, c_ref[0, 1, h].astype(BF16)),
                                _dot(qb, c_ref[0, 0, h].astype(BF16))))
        cq = xc_s[:, h * 128:(h + 1) * 128]
        ck = xc_s[:, WIDTH + h * 128:WIDTH + (h + 1) * 128]
        qn = cq * _l2_scale(cq) * Q_SCALE
        kn = ck * _l2_scale(ck)
        kq = jnp.concatenate([kn, qn], axis=0).astype(BF16)
        r0 = _dot(kq, s_ref[0, 0, h].astype(BF16))
        r1 = _dot(kq, s_ref[0, 1, h].astype(BF16))
        gdn_in.append((qn, kn, jnp.where(second, r1[0:R], r0[0:R]), jnp.where(second, r1[R:2 * R], r0[R:2 * R])))

    c_upd, s_upd = [], []
    for h in range(H):
        li = _select_lane(gates, lane, LANE_MI + h)
        lf = _select_lane(lf_all, lane, LANE_MF + h)
        mh = _select_lane(m_in, lane_s, h)
        m0 = jnp.where(second, mh[1:2, :], mh[0:1, :])
        b = cumsum_t(lf)
        a = b + m0
        dcol = [b - pick(b, j) + pick(li, j) for j in range(T)]
        m_t = a
        for j in range(T):
            m_t = jnp.maximum(m_t, jnp.where(tpos >= j, dcol[j], -jnp.inf))
        inter = jnp.exp(a - m_t)

        q = pcol(COL_MQ, h) * Q_SCALE
        k = pcol(COL_MK, h)
        v = pcol(COL_MV, h)
        num = jnp.zeros((R, 128), F32)
        den = jnp.zeros((R, 1), F32)
        for j in range(T):
            w_j = jnp.where(tpos >= j, jnp.exp(jnp.where(tpos >= j, dcol[j] - m_t, 0.0)), 0.0)
            s_j = jnp.sum(q * pick(k, j), axis=1, keepdims=True) * w_j
            num = num + s_j * pick(v, j)
            den = den + s_j
        qc = qc_all[h]
        n_sel =jnp.where(second, n_ref[0, 1, h:h + 1, :], n_ref[0, 0, h:h + 1, :])
        num = num + inter * qc
        den = den + inter * jnp.sum(q * n_sel, axis=1, keepdims=True)
        hh = num / jnp.maximum(jnp.abs(den), jnp.exp(-m_t))
        mix_ref[0, :, h * 128:(h + 1) * 128] = hh * _rms_scale(hh) * mnorm_ref[h:h + 1, :] * _sigmoid(pcol(COL_MO, h))

        m_l = pick(m_t, T - 1)
        b_l = pick(b, T - 1)
        kw = jnp.exp(b_l - b + li - m_l) * k
        d_c = jnp.exp(b_l + m0 - m_l)
        c_upd.append((kw, d_c, v.astype(BF16)))
        for s in range(NS):
            kws = jnp.where(seq_rows[s], kw, 0.0)
            n_out_ref[0, s, h:h + 1, :] = (d_c[s * T:s * T + 1, :] * n_ref[0, s, h:h + 1, :]
                                           + jnp.sum(kws, axis=0, keepdims=True))
            m_acc = jnp.where(jnp.logical_and(lane_s == h, row_s == s), m_l[s * T:s * T + 1, :], m_acc)

        lg = _select_lane(lg_all, lane, LANE_GA + h)
        beta = _select_lane(beta_all, lane, LANE_GB + h)
        g = cumsum_t(lg)
        e_g = jnp.exp(g)
        cv = xc_s[:, 2 * WIDTH + h * 128:2 * WIDTH + (h + 1) * 128]
        qn, kn, k_s, q_s = gdn_in[h]
        x = beta * (cv - e_g * k_s)
        rel = [jnp.exp(jnp.where(tpos >= j, g - pick(g, j), 0.0)) for j in range(T)]
        for j in range(T - 1):
            kk_j = jnp.sum(kn * pick(kn, j), axis=1, keepdims=True)
            a_j = jnp.where(tpos > j, beta * kk_j * rel[j], 0.0)
            x = x - a_j * pick(x, j)
        o = e_g * q_s
        for j in range(T):
            qk_j = jnp.sum(qn * pick(kn, j), axis=1, keepdims=True)
            o = o + jnp.where(tpos >= j, qk_j * rel[j], 0.0) * pick(x, j)
        gz = pcol(COL_GZ, h)
        mix_ref[0, :, WIDTH + h * 128:WIDTH + (h + 1) * 128] = (
            o * _rms_scale(o) * gnorm_ref[h:h + 1, :] * (gz * _sigmoid(gz)))

        g_l = pick(g, T - 1)
        s_upd.append((kn * jnp.exp(g_l - g), jnp.exp(g_l), x.astype(BF16)))
    m_out_ref[0] = m_acc

    for h in range(H):
        for (lhs, scale, rhs), st_ref, out_ref in ((c_upd[h], c_ref, c_out_ref), (s_upd[h], s_ref, s_out_ref)):
            for s in range(NS):
                rows_s = jnp.where(seq_rows[s], lhs, 0.0).astype(BF16)
                out_ref[0, s, h] = scale[s * T:s * T + 1, :] * st_ref[0, s, h] + _dot_tn(rows_s, rhs)


def _sample_step(proj, gates, c, n, m, s, conv_buf, conv_w, mnorm, gnorm, *, steps, grid):
    groups = c.shape[0]
    ns = SAMPLE_SEQS
    assert ns == 2 and ns * steps == 8 and steps >= CONV_W - 1 and steps & (steps - 1) == 0
    assert len(grid) == 2 and grid[0] * grid[1] == groups

    def per_group(*shape):
        return pl.BlockSpec((1,) + shape, lambda i, j: (i * grid[1] + j,) + (0,) * len(shape))

    def const(*shape):
        return pl.BlockSpec(shape, lambda i, j: (0,) * len(shape))

    state = (ns, N_HEADS, 128, 128)
    return dict(
        body=functools.partial(_sample_kernel, steps=steps),
        grid=grid,
        in_specs=[pl.BlockSpec((PROJ_COLS // 128, 1, ns * steps, 128), lambda i, j: (0, i * grid[1] + j, 0, 0)),
                  per_group(ns * steps, 128), per_group(*state), per_group(ns, N_HEADS, 128),
                  per_group(ns, N_HEADS), per_group(*state), per_group(ns, CONV_W - 1, 3 * WIDTH),
                  const(CONV_W, 3 * WIDTH), const(N_HEADS, 128), const(N_HEADS, 128)],
        out_specs=[per_group(ns * steps, 2 * WIDTH), per_group(*state), per_group(ns, N_HEADS, 128),
                   per_group(ns, N_HEADS), per_group(*state)],
        out_shape=[jax.ShapeDtypeStruct((groups, ns * steps, 2 * WIDTH), F32),
                   jax.ShapeDtypeStruct((groups,) + state, F32),
                   jax.ShapeDtypeStruct((groups, ns, N_HEADS, 128), F32),
                   jax.ShapeDtypeStruct((groups, ns, N_HEADS), F32),
                   jax.ShapeDtypeStruct((groups,) + state, F32)],
        scratch_shapes=[pltpu.VMEM((8 * ns, 3 * WIDTH), F32), pltpu.VMEM((ns * steps, 3 * WIDTH), F32)],
        args=(proj, gates, c, n, m, s, conv_buf, conv_w, mnorm, gnorm))


def kernel(x_prompt, x_sample, state_mlstm_C, state_mlstm_n, state_mlstm_m, state_gdn_S, state_gdn_conv, meta_tokens, norm_pre_mix, norm_post_mix, norm_pre_ffn, norm_post_ffn, w_in, mlstm_b_i, mlstm_b_f, mlstm_norm, gdn_conv_w, gdn_A_log, gdn_dt_bias, gdn_norm, w_out, w_gate_up, w_down):
    assert w_in.shape[0] == 1, "single-layer trunk"
    B, S, D = x_prompt.shape
    Bs, Ts, _ = x_sample.shape
    H = N_HEADS

    w_proj = _regroup_weights(jnp.transpose(w_in[0]))
    zeros_h = jnp.zeros((H,), F32)
    gbias = jnp.concatenate([mlstm_b_i[0], mlstm_b_f[0], gdn_dt_bias[0], zeros_h,
                             jnp.zeros((128 - 4 * H,), F32)]).reshape(1, 128)
    arow = jnp.concatenate([zeros_h, zeros_h, gdn_A_log[0], zeros_h,
                            jnp.zeros((128 - 4 * H,), F32)]).reshape(1, 128)
    g_pre_mix = norm_pre_mix[0].reshape(1, D)
    g_post_mix = norm_post_mix[0].reshape(1, D)
    g_pre_ffn = norm_pre_ffn[0].reshape(1, D)
    g_post_ffn = norm_post_ffn[0].reshape(1, D)
    mnorm = mlstm_norm[0]
    gnorm = gdn_norm[0]
    mnorm3 = mnorm.reshape(H, 1, 128)
    gnorm3 = gnorm.reshape(H, 1, 128)
    conv_w = gdn_conv_w[0]

    xp = x_prompt.reshape(B * S, D)
    xs = x_sample.reshape(Bs * Ts, D)

    nblk = PROJ_COLS // 128
    proj_p = _norm_matmul(xp, g_pre_mix, w_proj, tm=PROJ_TM, tn=PROJ_TN).reshape(nblk, B, S, 128)
    n_sm = Bs * Ts + N_META
    proj_sm = _norm_matmul(jnp.concatenate([xs, meta_tokens], axis=0), g_pre_mix, w_proj, tm=n_sm, tn=PROJ_TN)
    proj_m = proj_sm[:, Bs * Ts:].reshape(nblk, 1, N_META, 128)

    gates_p = _gate_act(proj_p.reshape(nblk, B * S, 128), gbias, arow, tm=PROJ_TM).reshape(B, S, 128)
    gates_sm = _gate_act(proj_sm, gbias, arow, tm=n_sm)
    gates_m = gates_sm[Bs * Ts:].reshape(1, N_META, 128)

    def conv_rows(blocks):
        return jnp.moveaxis(blocks, 0, -2).reshape(blocks.shape[1:-1] + (3 * WIDTH,))

    c0, n0, m0 = (jnp.zeros((1, H, 128, 128), F32), jnp.zeros((1, H, 1, 128), F32), jnp.zeros((1, H, 1, 128), F32))
    (_, c_l, n_l, m_l), = _run_parts([_mlstm_chunked(proj_m, gates_m, mnorm3, c0, n0, m0, chunk=N_META, heads=H)],
                                     semantics=("parallel", "arbitrary"), name="mlstm_lead")
    prep_l, = _run_parts([_gdn_prep(proj_m, gates_m, jnp.zeros((3 * H, 8, 128), F32), conv_w, chunk=N_META, nsub=1)],
                         semantics=("parallel", "parallel", "parallel"), name="gdn_prep_lead")
    (_, s_l), = _run_parts([_gdn_scan(prep_l, proj_m, gnorm3, c0, chunk=N_META, heads=N_HEADS, nsub=1)],
                           semantics=("parallel", "arbitrary"), name="gdn_scan_lead")

    conv_init = proj_m[COL_GQ:COL_GZ, 0, N_META - 8:, :]
    prep_parts = _gdn_prep(proj_p, gates_p, conv_init, conv_w, chunk=GDN_CHUNK, nsub=GDN_PREP_CHUNKS)
    prep_p, (w_out_b, w_gu_b, w_dn_b) = _run_parts(
        [prep_parts, _cast_rows([w_out[0], w_gate_up[0], w_down[0]], prep_parts["grid"], [None, FFN_TF, None])],
        semantics=("parallel", "parallel", "parallel"), name="gdn_prep_and_casts")

    ns = SAMPLE_SEQS
    grp = Bs // ns
    m_in = state_mlstm_m[0].reshape(grp, ns, H)
    mlstm_parts = _mlstm_chunked(proj_p, gates_p, mnorm3, c_l, n_l, m_l, chunk=MLSTM_CHUNK, heads=MIXER_HEADS)
    scan_parts = _gdn_scan(prep_p, proj_p, gnorm3, s_l, chunk=GDN_CHUNK, heads=MIXER_HEADS,
                           nsub=MLSTM_CHUNK // GDN_CHUNK)
    sample_parts = _sample_step(
        proj_sm.reshape(nblk, n_sm // (ns * Ts), ns * Ts, 128), gates_sm.reshape(n_sm // (ns * Ts), ns * Ts, 128),
        state_mlstm_C[0].reshape(grp, ns, H, 128, 128),
        state_mlstm_n[0].reshape(grp, ns, H, 128), m_in, state_gdn_S[0].reshape(grp, ns, H, 128, 128),
        state_gdn_conv[0].reshape(grp, ns, CONV_W - 1, 3 * WIDTH), conv_w, mnorm, gnorm, steps=Ts,
        grid=mlstm_parts["grid"])
    (mix_b, p_s), (mix_a, p_c, p_n, p_m), (mix_s, s_c, s_n, s_m, s_s) = _run_parts(
        [scan_parts, mlstm_parts, sample_parts], semantics=("parallel", "arbitrary"), name="scan_mlstm_sample")
    p_conv = conv_rows(proj_p[COL_GQ:COL_GZ, :, S - (CONV_W - 1):, :])

    x1_p, u2_p = _out_proj(mix_a.reshape(H, B * S, 128), 0, mix_b.reshape(H, B * S, 128), 0, w_out_b, xp,
                           g_post_mix, g_pre_ffn, tm=ROW_TM)
    y_p = _ffn(u2_p, w_gu_b, w_dn_b, x1_p, g_post_ffn, tm=ROW_TM, tf=FFN_TF)

    mix_s2 = mix_s.reshape(Bs * Ts, 2 * WIDTH)
    s_conv = conv_rows(proj_sm[COL_GQ:COL_GZ, :Bs * Ts].reshape(3 * H, Bs, Ts, 128)[:, :, Ts - (CONV_W - 1):, :])
    x1_s, u2_s = _out_proj(mix_s2, 0, mix_s2, 1, w_out_b, xs, g_post_mix, g_pre_ffn, tm=ROW_TM)
    y_s = _ffn(u2_s, w_gu_b, w_dn_b, x1_s, g_post_ffn, tm=ROW_TM, tf=FFN_TF)

    return (y_p.reshape(B, S, D), y_s.reshape(Bs, Ts, D),
            p_c[None], p_n[:, :, 0, :][None], p_m[:, :, 0, 0][None], p_s[None], p_conv[None],
            s_c.reshape(Bs, H, 128, 128)[None], s_n.reshape(Bs, H, 128)[None], s_m.reshape(Bs, H)[None],
            s_s.reshape(Bs, H, 128, 128)[None], s_conv[None])
```

```python
import functools

import jax
import jax.numpy as jnp
from jax import lax
from jax.experimental import pallas as pl
from jax.experimental.pallas import tpu as pltpu

F32 = jnp.float32
BF16 = jnp.bfloat16

HEAD_DIM = 128
N_HEADS = 8
WIDTH = N_HEADS * HEAD_DIM
CONV_W = 4
N_META = 16
D_FF = 5632
EPS = 1e-6
Q_SCALE = HEAD_DIM ** -0.5

COL_MQ, COL_MK, COL_MV, COL_MO = 0, 8, 16, 24
COL_GQ, COL_GK, COL_GV, COL_GZ = 32, 40, 48, 56
COL_GATES = 64
PROJ_COLS = 66 * 128
LANE_MI, LANE_MF, LANE_GA, LANE_GB = 0, 8, 16, 24

MLSTM_CHUNK = 256
MIXER_HEADS = 1
GDN_CHUNK = 64
GDN_PREP_CHUNKS = 32
SAMPLE_SEQS = 2

MXU_TILE = 256
VMEM_LIMIT = 56 * 1024 * 1024
PROJ_TM, PROJ_TN = 1024, 3 * MXU_TILE
ROW_TM = 512
FFN_TF = 2 * MXU_TILE


def _sigmoid(x):
    return 0.5 * jnp.tanh(0.5 * x) + 0.5


def _softplus(x):
    return jnp.maximum(x, 0.0) + jnp.log1p(jnp.exp(-jnp.abs(x)))


def _log_sigmoid(x):
    return -_softplus(-x)


def _rms_scale(x):
    return lax.rsqrt(jnp.mean(x * x, axis=-1, keepdims=True) + EPS)


def _l2_scale(x):
    return lax.rsqrt(jnp.sum(x * x, axis=-1, keepdims=True) + EPS)


def _dot(a, b):
    return jnp.dot(a, b, preferred_element_type=F32)


def _dot_nt(a, b):
    return lax.dot_general(a, b, (((1,), (1,)), ((), ())), preferred_element_type=F32)


def _dot_tn(a, b):
    return lax.dot_general(a, b, (((0,), (0,)), ((), ())), preferred_element_type=F32)


def _select_lane(x, lane_idx, target):
    return jnp.sum(jnp.where(lane_idx == target, x, 0.0), axis=1, keepdims=True)


REGROUP_TILE = 256


def _regroup_kernel(w_ref, mg_ref, gg_ref, o_ref):
    j = pl.program_id(0)
    wide_tiles = 2 * 4 * WIDTH // REGROUP_TILE

    @pl.when(j < wide_tiles)
    def _():
        o_ref[...] = w_ref[...].T.astype(BF16)

    @pl.when(j == wide_tiles)
    def _():
        pad = jnp.zeros((REGROUP_TILE - 4 * N_HEADS, w_ref.shape[1]), F32)
        o_ref[...] = jnp.concatenate([mg_ref[...], gg_ref[...], pad], axis=0).T.astype(BF16)


def _regroup_weights(w_t):
    n, d = w_t.shape
    wide = 4 * WIDTH
    narrow = 2 * N_HEADS
    assert n == 2 * (wide + narrow) and PROJ_COLS == 2 * wide + REGROUP_TILE
    wide_tiles = 2 * wide // REGROUP_TILE

    def src_row(j):
        j = jnp.minimum(j, wide_tiles - 1)
        return (j * (REGROUP_TILE // narrow) + jnp.where(j >= wide_tiles // 2, 1, 0)) * narrow

    return pl.pallas_call(
        _regroup_kernel,
        grid=(wide_tiles + 1,),
        in_specs=[pl.BlockSpec((pl.Element(REGROUP_TILE), pl.Element(d)), lambda j: (src_row(j), 0)),
                  pl.BlockSpec((narrow, d), lambda j: (wide // narrow, 0)),
                  pl.BlockSpec((narrow, d), lambda j: ((2 * wide + narrow) // narrow, 0))],
        out_specs=pl.BlockSpec((d, REGROUP_TILE), lambda j: (0, j)),
        out_shape=jax.ShapeDtypeStruct((d, PROJ_COLS), BF16),
        compiler_params=pltpu.CompilerParams(dimension_semantics=("parallel",), vmem_limit_bytes=VMEM_LIMIT),
        name="regroup_weights",
    )(w_t, w_t, w_t)


def _norm_matmul_kernel(x_ref, g_ref, w_ref, o_ref, u_ref):
    def store(res, sl):
        for blk in range(o_ref.shape[0]):
            o_ref[blk, sl, :] = res[:, blk * 128:(blk + 1) * 128]

    first = pl.program_id(1) == 0
    tm = x_ref.shape[0]

    @pl.when(first)
    def _():
        split = tm // 32 * 16
        for sl in (slice(0, split), slice(split, tm)):
            x = x_ref[sl, :]
            u = (x * _rms_scale(x) * g_ref[...]).astype(BF16)
            u_ref[sl, :] = u
            store(_dot(u, w_ref[...]), sl)

    @pl.when(jnp.logical_not(first))
    def _():
        store(_dot(u_ref[...], w_ref[...]), slice(0, tm))


def _norm_matmul(x, gain, w, *, tm, tn):
    m, d = x.shape
    n = w.shape[1]
    return pl.pallas_call(
        _norm_matmul_kernel,
        grid=(m // tm, n // tn),
        in_specs=[pl.BlockSpec((tm, d), lambda i, j: (i, 0)),
                  pl.BlockSpec((1, d), lambda i, j: (0, 0)),
                  pl.BlockSpec((d, tn), lambda i, j: (0, j))],
        out_specs=pl.BlockSpec((tn // 128, tm, 128), lambda i, j: (j, i, 0)),
        out_shape=jax.ShapeDtypeStruct((n // 128, m, 128), F32),
        scratch_shapes=[pltpu.VMEM((tm, d), BF16)],
        compiler_params=pltpu.CompilerParams(dimension_semantics=("parallel", "arbitrary"),
                                             vmem_limit_bytes=VMEM_LIMIT),
        name="norm_matmul",
    )(x, gain, w)


def _out_proj_kernel(ma_ref, mb_ref, wa_ref, wb_ref, x_ref, gpost_ref, gpre_ref, x1_ref, u2_ref):
    def rows(ref, sl):
        if len(ref.shape) == 2:
            return ref[sl, :].astype(BF16)
        return jnp.concatenate([ref[h, sl, :] for h in range(ref.shape[0])], axis=1).astype(BF16)

    tq = x_ref.shape[0] // 4
    quarters = [slice(i * tq, (i + 1) * tq) for i in range(4)]
    mos = [_dot(rows(ma_ref, sl), wa_ref[...]) + _dot(rows(mb_ref, sl), wb_ref[...]) for sl in quarters]
    for sl, mo in zip(quarters, mos):
        x1 = x_ref[sl, :] + mo * _rms_scale(mo) * gpost_ref[...]
        x1_ref[sl, :] = x1
        u2_ref[sl, :] = (x1 * _rms_scale(x1) * gpre_ref[...]).astype(BF16)


def _out_proj(mix_a, col_a, mix_b, col_b, w_out, x, g_post, g_pre, *, tm):
    m, d = x.shape

    def mix_spec(mix, col):
        if mix.ndim == 2:
            return pl.BlockSpec((tm, WIDTH), lambda i: (i, col))
        return pl.BlockSpec((N_HEADS, tm, 128), lambda i: (0, i, 0))

    return pl.pallas_call(
        _out_proj_kernel,
        grid=(m // tm,),
        in_specs=[mix_spec(mix_a, col_a),
                  mix_spec(mix_b, col_b),
                  pl.BlockSpec((WIDTH, d), lambda i: (0, 0)),
                  pl.BlockSpec((WIDTH, d), lambda i: (1, 0)),
                  pl.BlockSpec((tm, d), lambda i: (i, 0)),
                  pl.BlockSpec((1, d), lambda i: (0, 0)),
                  pl.BlockSpec((1, d), lambda i: (0, 0))],
        out_specs=[pl.BlockSpec((tm, d), lambda i: (i, 0)),
                   pl.BlockSpec((tm, d), lambda i: (i, 0))],
        out_shape=[jax.ShapeDtypeStruct((m, d), F32), jax.ShapeDtypeStruct((m, d), BF16)],
        compiler_params=pltpu.CompilerParams(dimension_semantics=("parallel",),
                                             vmem_limit_bytes=VMEM_LIMIT),
        name="out_proj",
    )(mix_a, mix_b, w_out, w_out, x, g_post, g_pre)


def _ffn_kernel(u_ref, wgu_ref, wd_ref, x1_ref, gain_ref, o_ref, acc_ref):
    f = pl.program_id(1)
    tf = wd_ref.shape[0]

    @pl.when(f == 0)
    def _():
        acc_ref[...] = jnp.zeros_like(acc_ref)

    def hidden_tile():
        gu = _dot(u_ref[...], wgu_ref[...])
        g, up = gu[:, :tf], gu[:, tf:]
        return (g * _sigmoid(g) * up).astype(BF16)

    last = f == pl.num_programs(1) - 1

    @pl.when(jnp.logical_not(last))
    def _():
        acc_ref[...] += _dot(hidden_tile(), wd_ref[...])

    @pl.when(last)
    def _():
        hidden = hidden_tile()
        tm = acc_ref.shape[0]
        halves = [slice(0, tm // 2), slice(tm // 2, tm)]
        for sl in halves:
            acc_ref[sl, :] += _dot(hidden[sl, :], wd_ref[...])
        for sl in halves:
            y = acc_ref[sl, :]
            o_ref[sl, :] = x1_ref[sl, :] + y * _rms_scale(y) * gain_ref[...]


def _ffn(u2, w_gate_up, w_down, x1, gain, *, tm, tf):
    m, d = x1.shape
    nf = D_FF // tf
    return pl.pallas_call(
        _ffn_kernel,
        grid=(m // tm, nf),
        in_specs=[pl.BlockSpec((tm, d), lambda i, f: (i, 0)),
                  pl.BlockSpec((d, 2 * tf), lambda i, f: (0, f)),
                  pl.BlockSpec((tf, d), lambda i, f: (f, 0)),
                  pl.BlockSpec((tm, d), lambda i, f: (i, 0)),
                  pl.BlockSpec((1, d), lambda i, f: (0, 0))],
        out_specs=pl.BlockSpec((tm, d), lambda i, f: (i, 0)),
        out_shape=jax.ShapeDtypeStruct((m, d), F32),
        scratch_shapes=[pltpu.VMEM((tm, d), F32)],
        compiler_params=pltpu.CompilerParams(dimension_semantics=("parallel", "arbitrary"),
                                             vmem_limit_bytes=VMEM_LIMIT),
        name="ffn",
    )(u2, w_gate_up, w_down, x1, gain)


def _gate_act_kernel(g_ref, gbias_ref, arow_ref, o_ref):
    g = g_ref[0] + gbias_ref[...]
    lane = lax.broadcasted_iota(jnp.int32, g.shape, 1)
    decay = -jnp.exp(arow_ref[...]) * _softplus(g)
    o_ref[...] = jnp.where(lane < LANE_MF, g,
                           jnp.where(lane < LANE_GA, _log_sigmoid(g), jnp.where(lane < LANE_GB, decay, _sigmoid(g))))


def _gate_act(proj, gbias, arow, *, tm):
    m = proj.shape[1]
    return pl.pallas_call(
        _gate_act_kernel,
        grid=(m // tm,),
        in_specs=[pl.BlockSpec((1, tm, 128), lambda i: (COL_GATES, i, 0)),
                  pl.BlockSpec((1, 128), lambda i: (0, 0)),
                  pl.BlockSpec((1, 128), lambda i: (0, 0))],
        out_specs=pl.BlockSpec((tm, 128), lambda i: (i, 0)),
        out_shape=jax.ShapeDtypeStruct((m, 128), F32),
        compiler_params=pltpu.CompilerParams(dimension_semantics=("parallel",), vmem_limit_bytes=VMEM_LIMIT),
        name="gate_act",
    )(proj, gbias, arow)


def _mlstm_init(q_ref, k_ref, v_ref, og_ref, gates_ref, mnorm_ref, c0_ref, n0_ref, m0_ref,
                h_ref, c_out_ref, n_out_ref, m_out_ref, c_s, n_s, m_s):
    shared = c0_ref.shape[0] == 1

    @pl.when(pl.program_id(1) == 0)
    def _():
        for hh in range(q_ref.shape[0]):
            for b in range(q_ref.shape[1]):
                b0 = 0 if shared else b
                c_s[b, hh] = c0_ref[b0, hh]
                n_s[b, hh] = n0_ref[b0, hh]
                m_s[b, hh] = m0_ref[b0, hh]


def _mlstm_chunk_kernel(q_ref, k_ref, v_ref, og_ref, gates_ref, mnorm_ref, c0_ref, n0_ref, m0_ref,
                        h_ref, c_out_ref, n_out_ref, m_out_ref, c_s, n_s, m_s, *, chunk):
    L = chunk
    nh, nb = q_ref.shape[0], q_ref.shape[1]
    chains = [(hh, b) for hh in range(nh) for b in range(nb)]
    heads = [pl.program_id(0) * nh + hh for hh, _ in chains]

    lane = lax.broadcasted_iota(jnp.int32, (L, 128), 1)
    rows = lax.broadcasted_iota(jnp.int32, (L, L), 0)
    cols = lax.broadcasted_iota(jnp.int32, (L, L), 1)
    tril = rows >= cols
    triu = rows <= cols
    eye = rows == cols

    seqs = range(len(chains))
    gates = [gates_ref[b] for _, b in chains]
    li_col = [_select_lane(g, lane, hd + LANE_MI) for g, hd in zip(gates, heads)]
    lf_col = [_select_lane(g, lane, hd + LANE_MF) for g, hd in zip(gates, heads)]
    lf_row = [jnp.sum(jnp.where(eye, x, 0.0), axis=0, keepdims=True) for x in lf_col]
    li_row = [jnp.sum(jnp.where(eye, x, 0.0), axis=0, keepdims=True) for x in li_col]
    b_col = [jnp.sum(jnp.where(tril, x, 0.0), axis=1, keepdims=True) for x in lf_row]
    b_row = [jnp.sum(jnp.where(triu, x, 0.0), axis=0, keepdims=True) for x in lf_col]

    qb = [(q_ref[hh, b] * Q_SCALE).astype(BF16) for hh, b in chains]
    kb = [k_ref[hh, b].astype(BF16) for hh, b in chains]
    vb = [v_ref[hh, b].astype(BF16) for hh, b in chains]
    qk = [_dot_nt(x, y) for x, y in zip(qb, kb)]
    qc = [_dot(x, c_s[b, hh].astype(BF16)) for (hh, b), x in zip(chains, qb)]

    m_prev = [m_s[b, hh, :, 0:1] for hh, b in chains]
    a_col = [x + y for x, y in zip(b_col, m_prev)]
    dm = [jnp.where(tril, bc - br + lr, -jnp.inf) for bc, br, lr in zip(b_col, b_row, li_row)]
    m_t = [jnp.maximum(a, jnp.max(d, axis=1, keepdims=True)) for a, d in zip(a_col, dm)]
    inter = [jnp.exp(a - m) for a, m in zip(a_col, m_t)]
    s = [x * jnp.exp(d - m) for x, d, m in zip(qk, dm, m_t)]
    sv = [_dot(x.astype(BF16), v) for x, v in zip(s, vb)]

    m_last = [m[L - 1:L, :] for m in m_t]
    b_last = [x[L - 1:L, :] for x in b_col]
    d_c = [jnp.exp(bl + mp - ml) for bl, mp, ml in zip(b_last, m_prev, m_last)]
    kw = [jnp.exp(bl - bc + li - ml) * k_ref[hh, b]
          for (hh, b), bl, bc, li, ml in zip(chains, b_last, b_col, li_col, m_last)]
    c_upd = [_dot_tn(x.astype(BF16), v) for x, v in zip(kw, vb)]

    for i, (hh, b) in enumerate(chains):
        q = q_ref[hh, b] * Q_SCALE
        num = sv[i] + inter[i] * qc[i]
        den = (jnp.sum(s[i], axis=1, keepdims=True)
               + inter[i] * jnp.sum(q * n_s[b, hh], axis=1, keepdims=True))
        hid = num / jnp.maximum(jnp.abs(den), jnp.exp(-m_t[i]))
        out = hid * _rms_scale(hid) * mnorm_ref[hh] * _sigmoid(og_ref[hh, b])
        h_ref[hh, b] = out.astype(h_ref.dtype)
        c_new = d_c[i] * c_s[b, hh] + c_upd[i]
        n_new = d_c[i] * n_s[b, hh] + jnp.sum(kw[i], axis=0, keepdims=True)
        m_new = jnp.broadcast_to(m_last[i], (1, 128))
        c_s[b, hh], n_s[b, hh], m_s[b, hh] = c_new, n_new, m_new
        c_out_ref[b, hh], n_out_ref[b, hh], m_out_ref[b, hh] = c_new, n_new, m_new


def _mlstm_chunked(proj, gates, mnorm, c0, n0, m0, *, chunk, heads):
    _, batch, seq, _ = proj.shape
    nb0 = c0.shape[0]
    hb = heads
    assert N_HEADS % hb == 0 and all(base % hb == 0 for base in (COL_MQ, COL_MK, COL_MV, COL_MO))

    def col(base):
        return pl.BlockSpec((hb, batch, chunk, 128), lambda h, c: (base // hb + h, 0, c, 0))

    def state(nb, shape):
        return pl.BlockSpec((nb, hb) + shape, lambda h, c: (0, h, 0, 0))

    return dict(
        init=_mlstm_init,
        body=functools.partial(_mlstm_chunk_kernel, chunk=chunk),
        grid=(N_HEADS // hb, seq // chunk),
        in_specs=[col(COL_MQ), col(COL_MK), col(COL_MV), col(COL_MO),
                  pl.BlockSpec((batch, chunk, 128), lambda h, c: (0, c, 0)),
                  pl.BlockSpec((hb, 1, 128), lambda h, c: (h, 0, 0)),
                  state(nb0, (128, 128)), state(nb0, (1, 128)), state(nb0, (1, 128))],
        out_specs=[pl.BlockSpec((hb, batch, chunk, 128), lambda h, c: (h, 0, c, 0)),
                   state(batch, (128, 128)), state(batch, (1, 128)), state(batch, (1, 128))],
        out_shape=[jax.ShapeDtypeStruct((N_HEADS, batch, seq, 128), BF16),
                   jax.ShapeDtypeStruct((batch, N_HEADS, 128, 128), F32),
                   jax.ShapeDtypeStruct((batch, N_HEADS, 1, 128), F32),
                   jax.ShapeDtypeStruct((batch, N_HEADS, 1, 128), F32)],
        scratch_shapes=[pltpu.VMEM((batch, hb, 128, 128), F32), pltpu.VMEM((batch, hb, 1, 128), F32),
                        pltpu.VMEM((batch, hb, 1, 128), F32)],
        args=(proj, proj, proj, proj, gates, mnorm, c0, n0, m0))


def _run_parts(parts, *, semantics, name):
    grid = parts[0]["grid"]
    assert all(p["grid"] == grid for p in parts)
    n_in = [len(p["in_specs"]) for p in parts]
    n_out = [len(p["out_specs"]) for p in parts]
    n_scr = [len(p["scratch_shapes"]) for p in parts]

    def body(*refs):
        ins, outs, scr = refs[:sum(n_in)], refs[sum(n_in):sum(n_in) + sum(n_out)], refs[sum(n_in) + sum(n_out):]
        part_refs = [ins[sum(n_in[:k]):sum(n_in[:k + 1])] + outs[sum(n_out[:k]):sum(n_out[:k + 1])]
                     + scr[sum(n_scr[:k]):sum(n_scr[:k + 1])] for k in range(len(parts))]
        for p, r in zip(parts, part_refs):
            if "init" in p:
                p["init"](*r)
        for p, r in zip(parts, part_refs):
            p["body"](*r)

    outs = pl.pallas_call(
        body,
        grid=grid,
        in_specs=[s for p in parts for s in p["in_specs"]],
        out_specs=[s for p in parts for s in p["out_specs"]],
        out_shape=[s for p in parts for s in p["out_shape"]],
        scratch_shapes=[s for p in parts for s in p["scratch_shapes"]],
        compiler_params=pltpu.CompilerParams(dimension_semantics=semantics, vmem_limit_bytes=VMEM_LIMIT),
        name=name,
    )(*[a for p in parts for a in p["args"]])
    return [outs[sum(n_out[:k]):sum(n_out[:k + 1])] for k in range(len(parts))]


def _unit_lower_inverse(mats, rows, cols, size):
    def same_block(shift):
        return (rows >> shift) == (cols >> shift)

    ident = jnp.where(rows == cols, 1.0, 0.0)
    d = [jnp.where(same_block(3), a, 0.0) for a in mats]
    d_b = [x.astype(BF16) for x in d]
    d2_b = [_dot(x, x).astype(BF16) for x in d_b]
    d4_b = [_dot(x, x).astype(BF16) for x in d2_b]
    t = [ident - x for x in d]
    t = [x + _dot(x.astype(BF16), y) for x, y in zip(t, d2_b)]
    t = [x + _dot(x.astype(BF16), y) for x, y in zip(t, d4_b)]
    a_hi = [x.astype(BF16) for x in mats]
    shift = 3
    while (1 << shift) < size:
        off = jnp.logical_and(same_block(shift + 1), jnp.logical_not(same_block(shift)))
        off_bf = jnp.where(off, 1.0, 0.0).astype(BF16)
        t_b = [x.astype(BF16) for x in t]
        te_b = [_dot(x, y * off_bf).astype(BF16) for x, y in zip(t_b, a_hi)]
        t = [x - _dot(y, z) for x, y, z in zip(t, te_b, t_b)]
        shift += 1
    return t


def _gdn_prep_kernel(q_ref, k_ref, v_ref, pq_ref, pk_ref, pv_ref, iq_ref, ik_ref, iv_ref, wq_ref, wk_ref, wv_ref,
                     gates_ref,
                     u_ref, w_ref, qe_ref, kd_ref, attn_ref, glast_ref, ext, *, chunk, nsub):
    L = chunk
    R = nsub * L
    head = pl.program_id(1)
    first = pl.program_id(2) == 0

    def conv(x_ref, prev_ref, init_ref, cw_ref):
        prev = jnp.where(first, init_ref[0], prev_ref[0, 0])
        ext[0:8, :] = prev
        ext[8:8 + R, :] = x_ref[0, 0]
        base = 8 - (CONV_W - 1)
        acc = ext[base:base + R, :] * cw_ref[0:1, :]
        for i in range(1, CONV_W):
            acc = acc + ext[base + i:base + i + R, :] * cw_ref[i:i + 1, :]
        return acc * _sigmoid(acc)

    q = conv(q_ref, pq_ref, iq_ref, wq_ref)
    k = conv(k_ref, pk_ref, ik_ref, wk_ref)
    v_all = conv(v_ref, pv_ref, iv_ref, wv_ref)
    qn_all = q * _l2_scale(q) * Q_SCALE
    kn_all = k * _l2_scale(k)

    gates = gates_ref[0]
    lane = lax.broadcasted_iota(jnp.int32, (R, 128), 1)
    lg_all = _select_lane(gates, lane, head + LANE_GA)
    beta_all = _select_lane(gates, lane, head + LANE_GB)

    U = min(R, 128)
    per_unit = U // L
    rows = lax.broadcasted_iota(jnp.int32, (U, U), 0)
    cols = lax.broadcasted_iota(jnp.int32, (U, U), 1)
    shift = L.bit_length() - 1
    same_chunk = (rows >> shift) == (cols >> shift)
    causal = jnp.logical_and(rows >= cols, same_chunk)
    strict = jnp.logical_and(rows > cols, same_chunk)
    upper = jnp.logical_and(rows <= cols, same_chunk)
    eye = rows == cols
    row_id = lax.broadcasted_iota(jnp.int32, (U, 1), 0)

    units = range(R // U)
    sls = [slice(j * U, (j + 1) * U) for j in units]
    g_cols, decays, a_mats, qks = [], [], [], []
    for sl in sls:
        lg_col = lg_all[sl]
        lg_row = jnp.sum(jnp.where(eye, lg_col, 0.0), axis=0, keepdims=True)
        g_col = jnp.sum(jnp.where(causal, lg_row, 0.0), axis=1, keepdims=True)
        g_row = jnp.sum(jnp.where(upper, lg_col, 0.0), axis=0, keepdims=True)
        decay = jnp.where(causal, jnp.exp(jnp.where(causal, g_col - g_row, 0.0)), 0.0)
        kb = kn_all[sl].astype(BF16)
        kk = _dot_nt(kb, kb)
        qks.append(_dot_nt(qn_all[sl].astype(BF16), kb))
        a_mats.append(jnp.where(strict, beta_all[sl] * kk * decay, 0.0))
        g_cols.append(g_col)
        decays.append(decay)

    t_mats = _unit_lower_inverse(a_mats, rows, cols, L)

    e_gs = [jnp.exp(g) for g in g_cols]
    rhs_s = [jnp.concatenate([beta_all[sl] * v_all[sl], (beta_all[sl] * e_g) * kn_all[sl]], axis=1).astype(BF16)
             for sl, e_g in zip(sls, e_gs)]
    uws = [_dot(t.astype(BF16), r) for t, r in zip(t_mats, rhs_s)]

    for j in units:
        sl, g_col = sls[j], g_cols[j]
        u_ref[0, 0, sl, :] = uws[j][:, 0:128]
        w_ref[0, 0, sl, :] = uws[j][:, 128:256].astype(BF16)
        qe_ref[0, 0, sl, :] = (qn_all[sl] * e_gs[j]).astype(BF16)
        attn = qks[j] * decays[j]
        g_last_col = jnp.zeros((U, 1), F32)
        for i in range(per_unit):
            g_last = g_col[(i + 1) * L - 1:(i + 1) * L, :]
            g_last_col = jnp.where((row_id >> shift) == i, g_last, g_last_col)
            glast_ref[0, 0, j * per_unit + i] = jnp.broadcast_to(jnp.exp(g_last), (1, 128))
            diag = attn[i * L:(i + 1) * L, :]
            if i > 0:
                diag = pltpu.roll(diag, U - i * L, axis=1)
            attn_ref[0, 0, j * per_unit + i] = diag[:, 0:L].astype(BF16)
        kd_ref[0, 0, sl, :] = (kn_all[sl] * jnp.exp(g_last_col - g_col)).astype(BF16)


def _gdn_prep(proj, gates, conv_init, conv_w, *, chunk, nsub):
    _, batch, seq, _ = proj.shape
    nc = seq // chunk
    R = nsub * chunk

    def col(base):
        return pl.BlockSpec((1, 1, R, 128), lambda b, h, c: (base + h, b, c, 0))

    def prev(base):
        return pl.BlockSpec((1, 1, 8, 128), lambda b, h, c: (base + h, b, jnp.maximum(c * (R // 8) - 1, 0), 0))

    def init(base):
        return pl.BlockSpec((1, 8, 128), lambda b, h, c: (base + h, 0, 0))

    def cw(base):
        return pl.BlockSpec((CONV_W, 128), lambda b, h, c: (0, base + h))

    def per_token(dtype):
        return (pl.BlockSpec((1, 1, R, 128), lambda b, h, c: (b, h, c, 0)),
                jax.ShapeDtypeStruct((batch, N_HEADS, seq, 128), dtype))

    outs = [per_token(F32), per_token(BF16), per_token(BF16), per_token(BF16),
            (pl.BlockSpec((1, 1, nsub, chunk, chunk), lambda b, h, c: (b, h, c, 0, 0)),
             jax.ShapeDtypeStruct((batch, N_HEADS, nc, chunk, chunk), BF16)),
            (pl.BlockSpec((1, 1, nsub, 1, 128), lambda b, h, c: (b, h, c, 0, 0)),
             jax.ShapeDtypeStruct((batch, N_HEADS, nc, 1, 128), F32))]
    return dict(
        body=functools.partial(_gdn_prep_kernel, chunk=chunk, nsub=nsub),
        grid=(batch, N_HEADS, nc // nsub),
        in_specs=[col(COL_GQ), col(COL_GK), col(COL_GV), prev(COL_GQ), prev(COL_GK), prev(COL_GV),
                  init(0), init(8), init(16), cw(0), cw(8), cw(16),
                  pl.BlockSpec((1, R, 128), lambda b, h, c: (b, c, 0))],
        out_specs=[o[0] for o in outs],
        out_shape=[o[1] for o in outs],
        scratch_shapes=[pltpu.VMEM((8 + R, 128), F32)],
        args=(proj, proj, proj, proj, proj, proj, conv_init, conv_init, conv_init, conv_w, conv_w, conv_w, gates))


def _cast_kernel(*refs, pair_tiles):
    for src, dst, tile in zip(refs[:len(refs) // 2], refs[len(refs) // 2:], pair_tiles):
        if tile is None:
            dst[...] = src[...].astype(dst.dtype)
        else:
            half = src.shape[1] // 2
            for t in range(half // tile):
                dst[:, 2 * t * tile:(2 * t + 1) * tile] = src[:, t * tile:(t + 1) * tile].astype(dst.dtype)
                dst[:, (2 * t + 1) * tile:(2 * t + 2) * tile] = (
                    src[:, half + t * tile:half + (t + 1) * tile].astype(dst.dtype))


def _cast_rows(weights, grid, pair_tiles):
    steps = 1
    for g in grid:
        steps *= g

    def slab(w):
        rows = w.shape[0] // steps
        assert rows * steps == w.shape[0] and rows % 16 == 0

        def index(*idx):
            flat = idx[0]
            for i, g in zip(idx[1:], grid[1:]):
                flat = flat * g + i
            return (flat, 0)

        return pl.BlockSpec((rows, w.shape[1]), index)

    return dict(body=functools.partial(_cast_kernel, pair_tiles=tuple(pair_tiles)), grid=grid,
                in_specs=[slab(w) for w in weights],
                out_specs=[slab(w) for w in weights],
                out_shape=[jax.ShapeDtypeStruct(w.shape, BF16) for w in weights], scratch_shapes=[],
                args=tuple(weights))


def _gdn_scan_init(u_ref, w_ref, qe_ref, kd_ref, attn_ref, glast_ref, gz_ref, gnorm_ref, s0_ref,
                   o_ref, s_out_ref, s_s):
    shared = s0_ref.shape[0] == 1

    @pl.when(pl.program_id(1) == 0)
    def _():
        for b in range(u_ref.shape[0]):
            for h in range(u_ref.shape[1]):
                s_s[b, h] = s0_ref[0 if shared else b, h]


def _gdn_scan_kernel(u_ref, w_ref, qe_ref, kd_ref, attn_ref, glast_ref, gz_ref, gnorm_ref, s0_ref,
                     o_ref, s_out_ref, s_s):
    nb, nh = u_ref.shape[0], u_ref.shape[1]
    L = attn_ref.shape[3]
    chains = [(b, h) for b in range(nb) for h in range(nh)]
    s_cur = [s_s[b, h] for b, h in chains]
    for j in range(attn_ref.shape[2]):
        sl = slice(j * L, (j + 1) * L)
        ws = [_dot(jnp.concatenate([w_ref[b, h, sl, :], qe_ref[b, h, sl, :]], axis=0), sp.astype(BF16))
              for (b, h), sp in zip(chains, s_cur)]
        vb = [(u_ref[b, h, sl, :] - r[0:L]).astype(BF16) for (b, h), r in zip(chains, ws)]
        intra = [_dot(attn_ref[b, h, j], v) for (b, h), v in zip(chains, vb)]
        upd = [_dot_tn(kd_ref[b, h, sl, :], v) for (b, h), v in zip(chains, vb)]
        s_cur = [glast_ref[b, h, j][:, 0:1] * sp + up for (b, h), sp, up in zip(chains, s_cur, upd)]
        for i, (b, h) in enumerate(chains):
            o = ws[i][L:2 * L] + intra[i]
            gz = gz_ref[h, b, sl, :]
            o_ref[h, b, sl, :] = (o * _rms_scale(o) * gnorm_ref[h] * (gz * _sigmoid(gz))).astype(o_ref.dtype)
    for (b, h), sp in zip(chains, s_cur):
        s_s[b, h] = sp
        s_out_ref[b, h] = sp


def _gdn_scan(prep, proj, gnorm, s0, *, chunk, heads, nsub):
    _, batch, seq, _ = proj.shape
    u, w, qe, kd, attn, glast = prep
    hb = heads
    assert COL_GZ % hb == 0 and N_HEADS % hb == 0
    tok = pl.BlockSpec((batch, hb, nsub * chunk, 128), lambda h, c: (0, h, c, 0))
    return dict(
        init=_gdn_scan_init,
        body=_gdn_scan_kernel,
        grid=(N_HEADS // hb, seq // (nsub * chunk)),
        in_specs=[tok, tok, tok, tok,
                  pl.BlockSpec((batch, hb, nsub, chunk, chunk), lambda h, c: (0, h, c, 0, 0)),
                  pl.BlockSpec((batch, hb, nsub, 1, 128), lambda h, c: (0, h, c, 0, 0)),
                  pl.BlockSpec((hb, batch, nsub * chunk, 128), lambda h, c: (COL_GZ // hb + h, 0, c, 0)),
                  pl.BlockSpec((hb, 1, 128), lambda h, c: (h, 0, 0)),
                  pl.BlockSpec((s0.shape[0], hb, 128, 128), lambda h, c: (0, h, 0, 0))],
        out_specs=[pl.BlockSpec((hb, batch, nsub * chunk, 128), lambda h, c: (h, 0, c, 0)),
                   pl.BlockSpec((batch, hb, 128, 128), lambda h, c: (0, h, 0, 0))],
        out_shape=[jax.ShapeDtypeStruct((N_HEADS, batch, seq, 128), BF16),
                   jax.ShapeDtypeStruct((batch, N_HEADS, 128, 128), F32)],
        scratch_shapes=[pltpu.VMEM((batch, hb, 128, 128), F32)],
        args=(u, w, qe, kd, attn, glast, proj, gnorm, s0))


def _sample_kernel(p_ref, gates_ref, c_ref, n_ref, m_ref, s_ref, cb_ref, cw_ref, mnorm_ref, gnorm_ref,
                   mix_ref, c_out_ref, n_out_ref, m_out_ref, s_out_ref, ext, xc_s, *, steps):
    T = steps
    NS = SAMPLE_SEQS
    R = NS * T
    H = N_HEADS
    nbuf = CONV_W - 1

    for s in range(NS):
        ext[8 * s:8 * s + nbuf, :] = cb_ref[0, s]
        for blk in range(COL_GZ - COL_GQ):
            ext[8 * s + nbuf:8 * s + nbuf + T, blk * 128:(blk + 1) * 128] = p_ref[COL_GQ + blk, 0, s * T:(s + 1) * T, :]
        acc = ext[8 * s:8 * s + T, :] * cw_ref[0:1, :]
        for i in range(1, CONV_W):
            acc = acc + ext[8 * s + i:8 * s + i + T, :] * cw_ref[i:i + 1, :]
        xc_s[s * T:(s + 1) * T, :] = acc * _sigmoid(acc)

    rowi = lax.broadcasted_iota(jnp.int32, (R, 1), 0)
    tpos = rowi & (T - 1)
    second = rowi >= T
    seq_rows = [jnp.logical_not(second), second]

    def pick(x, j):
        return jnp.where(second, x[T + j:T + j + 1, :], x[j:j + 1, :])

    def cumsum_t(x):
        acc = jnp.where(tpos >= 0, pick(x, 0), 0.0)
        for j in range(1, T):
            acc = acc + jnp.where(tpos >= j, pick(x, j), 0.0)
        return acc

    def pcol(block, h):
        return p_ref[block + h, 0]

    gates = gates_ref[0]
    lane = lax.broadcasted_iota(jnp.int32, (R, 128), 1)
    lf_all = lg_all = beta_all = gates
    m_in = m_ref[0]
    lane_s = lax.broadcasted_iota(jnp.int32, (NS, H), 1)
    row_s = lax.broadcasted_iota(jnp.int32, (NS, H), 0)
    m_acc = jnp.zeros((NS, H), F32)

    qc_all, gdn_in = [], []
    for h in range(H):
        qb = (pcol(COL_MQ, h) * Q_SCALE).astype(BF16)
        qc_all.append(jnp.where(second, _dot(qb, c_ref[0, 1, h].astype(BF16)),
                                _dot(qb, c_ref[0, 0, h].astype(BF16))))
        cq = xc_s[:, h * 128:(h + 1) * 128]
        ck = xc_s[:, WIDTH + h * 128:WIDTH + (h + 1) * 128]
        qn = cq * _l2_scale(cq) * Q_SCALE
        kn = ck * _l2_scale(ck)
        kq = jnp.concatenate([kn, qn], axis=0).astype(BF16)
        r0 = _dot(kq, s_ref[0, 0, h].astype(BF16))
        r1 = _dot(kq, s_ref[0, 1, h].astype(BF16))
        gdn_in.append((qn, kn, jnp.where(second, r1[0:R], r0[0:R]), jnp.where(second, r1[R:2 * R], r0[R:2 * R])))

    c_upd, s_upd = [], []
    for h in range(H):
        li = _select_lane(gates, lane, LANE_MI + h)
        lf = _select_lane(lf_all, lane, LANE_MF + h)
        mh = _select_lane(m_in, lane_s, h)
        m0 = jnp.where(second, mh[1:2, :], mh[0:1, :])
        b = cumsum_t(lf)
        a = b + m0
        dcol = [b - pick(b, j) + pick(li, j) for j in range(T)]
        m_t = a
        for j in range(T):
            m_t = jnp.maximum(m_t, jnp.where(tpos >= j, dcol[j], -jnp.inf))
        inter = jnp.exp(a - m_t)

        q = pcol(COL_MQ, h) * Q_SCALE
        k = pcol(COL_MK, h)
        v = pcol(COL_MV, h)
        num = jnp.zeros((R, 128), F32)
        den = jnp.zeros((R, 1), F32)
        for j in range(T):
            w_j = jnp.where(tpos >= j, jnp.exp(jnp.where(tpos >= j, dcol[j] - m_t, 0.0)), 0.0)
            s_j = jnp.sum(q * pick(k, j), axis=1, keepdims=True) * w_j
            num = num + s_j * pick(v, j)
            den = den + s_j
        qc = qc_all[h]
        n_sel =jnp.where(second, n_ref[0, 1, h:h + 1, :], n_ref[0, 0, h:h + 1, :])
        num = num + inter * qc
        den = den + inter * jnp.sum(q * n_sel, axis=1, keepdims=True)
        hh = num / jnp.maximum(jnp.abs(den), jnp.exp(-m_t))
        mix_ref[0, :, h * 128:(h + 1) * 128] = hh * _rms_scale(hh) * mnorm_ref[h:h + 1, :] * _sigmoid(pcol(COL_MO, h))

        m_l = pick(m_t, T - 1)
        b_l = pick(b, T - 1)
        kw = jnp.exp(b_l - b + li - m_l) * k
        d_c = jnp.exp(b_l + m0 - m_l)
        c_upd.append((kw, d_c, v.astype(BF16)))
        for s in range(NS):
            kws = jnp.where(seq_rows[s], kw, 0.0)
            n_out_ref[0, s, h:h + 1, :] = (d_c[s * T:s * T + 1, :] * n_ref[0, s, h:h + 1, :]
                                           + jnp.sum(kws, axis=0, keepdims=True))
            m_acc = jnp.where(jnp.logical_and(lane_s == h, row_s == s), m_l[s * T:s * T + 1, :], m_acc)

        lg = _select_lane(lg_all, lane, LANE_GA + h)
        beta = _select_lane(beta_all, lane, LANE_GB + h)
        g = cumsum_t(lg)
        e_g = jnp.exp(g)
        cv = xc_s[:, 2 * WIDTH + h * 128:2 * WIDTH + (h + 1) * 128]
        qn, kn, k_s, q_s = gdn_in[h]
        x = beta * (cv - e_g * k_s)
        rel = [jnp.exp(jnp.where(tpos >= j, g - pick(g, j), 0.0)) for j in range(T)]
        for j in range(T - 1):
            kk_j = jnp.sum(kn * pick(kn, j), axis=1, keepdims=True)
            a_j = jnp.where(tpos > j, beta * kk_j * rel[j], 0.0)
            x = x - a_j * pick(x, j)
        o = e_g * q_s
        for j in range(T):
            qk_j = jnp.sum(qn * pick(kn, j), axis=1, keepdims=True)
            o = o + jnp.where(tpos >= j, qk_j * rel[j], 0.0) * pick(x, j)
        gz = pcol(COL_GZ, h)
        mix_ref[0, :, WIDTH + h * 128:WIDTH + (h + 1) * 128] = (
            o * _rms_scale(o) * gnorm_ref[h:h + 1, :] * (gz * _sigmoid(gz)))

        g_l = pick(g, T - 1)
        s_upd.append((kn * jnp.exp(g_l - g), jnp.exp(g_l), x.astype(BF16)))
    m_out_ref[0] = m_acc

    for h in range(H):
        for (lhs, scale, rhs), st_ref, out_ref in ((c_upd[h], c_ref, c_out_ref), (s_upd[h], s_ref, s_out_ref)):
            for s in range(NS):
                rows_s = jnp.where(seq_rows[s], lhs, 0.0).astype(BF16)
                out_ref[0, s, h] = scale[s * T:s * T + 1, :] * st_ref[0, s, h] + _dot_tn(rows_s, rhs)


def _sample_step(proj, gates, c, n, m, s, conv_buf, conv_w, mnorm, gnorm, *, steps, grid):
    groups = c.shape[0]
    ns = SAMPLE_SEQS
    assert ns == 2 and ns * steps == 8 and steps >= CONV_W - 1 and steps & (steps - 1) == 0
    assert len(grid) == 2 and grid[0] * grid[1] == groups

    def per_group(*shape):
        return pl.BlockSpec((1,) + shape, lambda i, j: (i * grid[1] + j,) + (0,) * len(shape))

    def const(*shape):
        return pl.BlockSpec(shape, lambda i, j: (0,) * len(shape))

    state = (ns, N_HEADS, 128, 128)
    return dict(
        body=functools.partial(_sample_kernel, steps=steps),
        grid=grid,
        in_specs=[pl.BlockSpec((PROJ_COLS // 128, 1, ns * steps, 128), lambda i, j: (0, i * grid[1] + j, 0, 0)),
                  per_group(ns * steps, 128), per_group(*state), per_group(ns, N_HEADS, 128),
                  per_group(ns, N_HEADS), per_group(*state), per_group(ns, CONV_W - 1, 3 * WIDTH),
                  const(CONV_W, 3 * WIDTH), const(N_HEADS, 128), const(N_HEADS, 128)],
        out_specs=[per_group(ns * steps, 2 * WIDTH), per_group(*state), per_group(ns, N_HEADS, 128),
                   per_group(ns, N_HEADS), per_group(*state)],
        out_shape=[jax.ShapeDtypeStruct((groups, ns * steps, 2 * WIDTH), F32),
                   jax.ShapeDtypeStruct((groups,) + state, F32),
                   jax.ShapeDtypeStruct((groups, ns, N_HEADS, 128), F32),
                   jax.ShapeDtypeStruct((groups, ns, N_HEADS), F32),
                   jax.ShapeDtypeStruct((groups,) + state, F32)],
        scratch_shapes=[pltpu.VMEM((8 * ns, 3 * WIDTH), F32), pltpu.VMEM((ns * steps, 3 * WIDTH), F32)],
        args=(proj, gates, c, n, m, s, conv_buf, conv_w, mnorm, gnorm))


def kernel(x_prompt, x_sample, state_mlstm_C, state_mlstm_n, state_mlstm_m, state_gdn_S, state_gdn_conv, meta_tokens, norm_pre_mix, norm_post_mix, norm_pre_ffn, norm_post_ffn, w_in, mlstm_b_i, mlstm_b_f, mlstm_norm, gdn_conv_w, gdn_A_log, gdn_dt_bias, gdn_norm, w_out, w_gate_up, w_down):
    assert w_in.shape[0] == 1, "single-layer trunk"
    B, S, D = x_prompt.shape
    Bs, Ts, _ = x_sample.shape
    H = N_HEADS

    w_proj = _regroup_weights(jnp.transpose(w_in[0]))
    zeros_h = jnp.zeros((H,), F32)
    gbias = jnp.concatenate([mlstm_b_i[0], mlstm_b_f[0], gdn_dt_bias[0], zeros_h,
                             jnp.zeros((128 - 4 * H,), F32)]).reshape(1, 128)
    arow = jnp.concatenate([zeros_h, zeros_h, gdn_A_log[0], zeros_h,
                            jnp.zeros((128 - 4 * H,), F32)]).reshape(1, 128)
    g_pre_mix = norm_pre_mix[0].reshape(1, D)
    g_post_mix = norm_post_mix[0].reshape(1, D)
    g_pre_ffn = norm_pre_ffn[0].reshape(1, D)
    g_post_ffn = norm_post_ffn[0].reshape(1, D)
    mnorm = mlstm_norm[0]
    gnorm = gdn_norm[0]
    mnorm3 = mnorm.reshape(H, 1, 128)
    gnorm3 = gnorm.reshape(H, 1, 128)
    conv_w = gdn_conv_w[0]

    xp = x_prompt.reshape(B * S, D)
    xs = x_sample.reshape(Bs * Ts, D)

    nblk = PROJ_COLS // 128
    proj_p = _norm_matmul(xp, g_pre_mix, w_proj, tm=PROJ_TM, tn=PROJ_TN).reshape(nblk, B, S, 128)
    n_sm = Bs * Ts + N_META
    proj_sm = _norm_matmul(jnp.concatenate([xs, meta_tokens], axis=0), g_pre_mix, w_proj, tm=n_sm, tn=PROJ_TN)
    proj_m = proj_sm[:, Bs * Ts:].reshape(nblk, 1, N_META, 128)

    gates_p = _gate_act(proj_p.reshape(nblk, B * S, 128), gbias, arow, tm=PROJ_TM).reshape(B, S, 128)
    gates_sm = _gate_act(proj_sm, gbias, arow, tm=n_sm)
    gates_m = gates_sm[Bs * Ts:].reshape(1, N_META, 128)

    def conv_rows(blocks):
        return jnp.moveaxis(blocks, 0, -2).reshape(blocks.shape[1:-1] + (3 * WIDTH,))

    c0, n0, m0 = (jnp.zeros((1, H, 128, 128), F32), jnp.zeros((1, H, 1, 128), F32), jnp.zeros((1, H, 1, 128), F32))
    (_, c_l, n_l, m_l), = _run_parts([_mlstm_chunked(proj_m, gates_m, mnorm3, c0, n0, m0, chunk=N_META, heads=H)],
                                     semantics=("parallel", "arbitrary"), name="mlstm_lead")
    prep_l, = _run_parts([_gdn_prep(proj_m, gates_m, jnp.zeros((3 * H, 8, 128), F32), conv_w, chunk=N_META, nsub=1)],
                         semantics=("parallel", "parallel", "parallel"), name="gdn_prep_lead")
    (_, s_l), = _run_parts([_gdn_scan(prep_l, proj_m, gnorm3, c0, chunk=N_META, heads=N_HEADS, nsub=1)],
                           semantics=("parallel", "arbitrary"), name="gdn_scan_lead")

    conv_init = proj_m[COL_GQ:COL_GZ, 0, N_META - 8:, :]
    prep_parts = _gdn_prep(proj_p, gates_p, conv_init, conv_w, chunk=GDN_CHUNK, nsub=GDN_PREP_CHUNKS)
    prep_p, (w_out_b, w_gu_b, w_dn_b) = _run_parts(
        [prep_parts, _cast_rows([w_out[0], w_gate_up[0], w_down[0]], prep_parts["grid"], [None, FFN_TF, None])],
        semantics=("parallel", "parallel", "parallel"), name="gdn_prep_and_casts")

    ns = SAMPLE_SEQS
    grp = Bs // ns
    m_in = state_mlstm_m[0].reshape(grp, ns, H)
    mlstm_parts = _mlstm_chunked(proj_p, gates_p, mnorm3, c_l, n_l, m_l, chunk=MLSTM_CHUNK, heads=MIXER_HEADS)
    scan_parts = _gdn_scan(prep_p, proj_p, gnorm3, s_l, chunk=GDN_CHUNK, heads=MIXER_HEADS,
                           nsub=MLSTM_CHUNK // GDN_CHUNK)
    sample_parts = _sample_step(
        proj_sm.reshape(nblk, n_sm // (ns * Ts), ns * Ts, 128), gates_sm.reshape(n_sm // (ns * Ts), ns * Ts, 128),
        state_mlstm_C[0].reshape(grp, ns, H, 128, 128),
        state_mlstm_n[0].reshape(grp, ns, H, 128), m_in, state_gdn_S[0].reshape(grp, ns, H, 128, 128),
        state_gdn_conv[0].reshape(grp, ns, CONV_W - 1, 3 * WIDTH), conv_w, mnorm, gnorm, steps=Ts,
        grid=mlstm_parts["grid"])
    (mix_b, p_s), (mix_a, p_c, p_n, p_m), (mix_s, s_c, s_n, s_m, s_s) = _run_parts(
        [scan_parts, mlstm_parts, sample_parts], semantics=("parallel", "arbitrary"), name="scan_mlstm_sample")
    p_conv = conv_rows(proj_p[COL_GQ:COL_GZ, :, S - (CONV_W - 1):, :])

    x1_p, u2_p = _out_proj(mix_a.reshape(H, B * S, 128), 0, mix_b.reshape(H, B * S, 128), 0, w_out_b, xp,
                           g_post_mix, g_pre_ffn, tm=ROW_TM)
    y_p = _ffn(u2_p, w_gu_b, w_dn_b, x1_p, g_post_ffn, tm=ROW_TM, tf=FFN_TF)

    mix_s2 = mix_s.reshape(Bs * Ts, 2 * WIDTH)
    s_conv = conv_rows(proj_sm[COL_GQ:COL_GZ, :Bs * Ts].reshape(3 * H, Bs, Ts, 128)[:, :, Ts - (CONV_W - 1):, :])
    x1_s, u2_s = _out_proj(mix_s2, 0, mix_s2, 1, w_out_b, xs, g_post_mix, g_pre_ffn, tm=ROW_TM)
    y_s = _ffn(u2_s, w_gu_b, w_dn_b, x1_s, g_post_ffn, tm=ROW_TM, tf=FFN_TF)

    return (y_p.reshape(B, S, D), y_s.reshape(Bs, Ts, D),
            p_c[None], p_n[:, :, 0, :][None], p_m[:, :, 0, 0][None], p_s[None], p_conv[None],
            s_c.reshape(Bs, H, 128, 128)[None], s_n.reshape(Bs, H, 128)[None], s_m.reshape(Bs, H)[None],
            s_s.reshape(Bs, H, 128, 128)[None], s_conv[None])
```
